```python
import jax
import jax.numpy as jnp
from jax import lax
import numpy as np

D_MODEL = 1024
BATCH = 8
SEQ = 2048
DEPTH = 2

N_HEADS = 8
HEAD_DIM = 128
IDX_HEADS = 8
IDX_DIM = 64
IDX_ROPE_DIM = 32
INDEX_TOPK = 256
Q_BLOCK = 128
ROPE_THETA = 10000.0
GMLP_WIDTH = 1024
GMLP_GROUPS = 8
GMLP_CHUNK = 128
FFN_DIM = 3584
N_EXPERTS = 8
TOP_K = 2
NORM_EPS = 1e-6
DSA_IN_SIZES = (N_HEADS * HEAD_DIM, HEAD_DIM, HEAD_DIM, IDX_HEADS * IDX_DIM, IDX_DIM, IDX_HEADS)
DSA_IN_DIM = N_HEADS * HEAD_DIM + 2 * HEAD_DIM + IDX_HEADS * IDX_DIM + IDX_DIM + IDX_HEADS

kernel_name = 'hybrid_dsa_gmlp_moe_trunk'


def rms_norm(x, g):
    x32 = x.astype(jnp.float32)
    y = x32 * lax.rsqrt(jnp.mean(x32 * x32, axis=-1, keepdims=True) + NORM_EPS)
    return (y * g.astype(jnp.float32)).astype(x.dtype)


def layer_norm(x, g, b):
    x32 = x.astype(jnp.float32)
    mu = jnp.mean(x32, axis=-1, keepdims=True)
    var = jnp.mean(jnp.square(x32 - mu), axis=-1, keepdims=True)
    y = (x32 - mu) * lax.rsqrt(var + NORM_EPS)
    return (y * g.astype(jnp.float32) + b.astype(jnp.float32)).astype(x.dtype)


def rope_tables(length, dim):
    inv_freq = ROPE_THETA ** (-jnp.arange(0, dim, 2, dtype=jnp.float32) / dim)
    ang = jnp.arange(length, dtype=jnp.float32)[:, None] * inv_freq[None, :]
    return jnp.cos(ang), jnp.sin(ang)


def apply_rotary(x, cos, sin):
    lead = (1,) * (x.ndim - 3)
    c = cos.reshape(cos.shape[0], *lead, cos.shape[1])
    s = sin.reshape(sin.shape[0], *lead, sin.shape[1])
    x1, x2 = jnp.split(x.astype(jnp.float32), 2, axis=-1)
    return jnp.concatenate([x1 * c - x2 * s, x2 * c + x1 * s], axis=-1).astype(x.dtype)


def dsa_attention(h, w_in, idx_k_gain, w_out):
    b, l, _ = h.shape
    split_at = np.cumsum(DSA_IN_SIZES)[:-1].tolist()
    q, k, v, qi, ki, wi = jnp.split(h @ w_in, split_at, axis=-1)
    cos, sin = rope_tables(l, HEAD_DIM)
    q = apply_rotary(q.reshape(b, l, N_HEADS, HEAD_DIM), cos, sin)
    k = apply_rotary(k, cos, sin)
    kv = jnp.concatenate([k, v], axis=-1)
    ci, si = rope_tables(l, IDX_ROPE_DIM)
    qi = qi.reshape(b, l, IDX_HEADS, IDX_DIM)
    qi = jnp.concatenate([apply_rotary(qi[..., :IDX_ROPE_DIM], ci, si), qi[..., IDX_ROPE_DIM:]], axis=-1)
    ki = rms_norm(ki, idx_k_gain)
    ki = jnp.concatenate([apply_rotary(ki[..., :IDX_ROPE_DIM], ci, si), ki[..., IDX_ROPE_DIM:]], axis=-1)
    ki32 = ki.astype(jnp.float32)
    wi = wi.astype(jnp.float32) * IDX_HEADS ** -0.5
    n_sel = min(INDEX_TOPK, l // 4)
    n_blk = l // Q_BLOCK
    key_pos = jnp.arange(l)

    def to_blocks(t):
        return t.reshape(b, n_blk, Q_BLOCK, *t.shape[2:]).swapaxes(0, 1)

    def block(args):
        qb, qib, wib, start = args
        q_pos = start + jnp.arange(Q_BLOCK)
        causal = key_pos[None, :] <= q_pos[:, None]
        rel = jax.nn.relu(jnp.einsum('bqhd,bsd->bqhs', qib.astype(jnp.float32), ki32))
        score = jnp.einsum('bqh,bqhs->bqs', wib, rel) * IDX_DIM ** -0.5
        score = jnp.where(causal[None], score, -jnp.inf)
        _, sel = lax.top_k(score, n_sel)
        valid = sel <= q_pos[None, :, None]
        kv_sel = jax.vmap(lambda kv_b, sel_b: kv_b[sel_b])(kv, sel).astype(jnp.float32)
        k_sel, v_sel = jnp.split(kv_sel, 2, axis=-1)
        logits = jnp.einsum('bqhd,bqkd->bhqk', qb.astype(jnp.float32), k_sel) * HEAD_DIM ** -0.5
        logits = jnp.where(valid[:, None], logits, -jnp.inf)
        p = jax.nn.softmax(logits, axis=-1)
        return jnp.einsum('bhqk,bqkd->bqhd', p, v_sel).astype(h.dtype)

    starts = jnp.arange(n_blk) * Q_BLOCK
    o = lax.map(block, (to_blocks(q), to_blocks(qi), to_blocks(wi), starts))
    o = o.swapaxes(0, 1).reshape(b, l, N_HEADS * HEAD_DIM)
    return o @ w_out


def chunked_gmlp(h, w_in, ln_g, ln_b, w_spatial, b_spatial, w_out):
    b, l, _ = h.shape
    u, v = jnp.split(jax.nn.gelu(h @ w_in), 2, axis=-1)
    v = layer_norm(v, ln_g, ln_b).reshape(
        b, l // GMLP_CHUNK, GMLP_CHUNK, GMLP_GROUPS, GMLP_WIDTH // GMLP_GROUPS)
    tril = jnp.tril(jnp.ones((GMLP_CHUNK, GMLP_CHUNK), dtype=bool))
    w_s = jnp.where(tril[None], w_spatial, jnp.zeros((), w_spatial.dtype))
    s = jnp.einsum('gij,bcjge->bcige', w_s, v) + b_spatial.T[:, :, None]
    return (u * s.reshape(b, l, GMLP_WIDTH)) @ w_out


def swiglu(h, w_gu, w_down):
    g, u = jnp.split(h @ w_gu, 2, axis=-1)
    return (jax.nn.silu(g) * u) @ w_down


def moe_swiglu(h, w_router, w_gu, w_down):
    logits = (h @ w_router).astype(jnp.float32)
    top_v, top_i = lax.top_k(logits, TOP_K)
    gates = jax.nn.softmax(top_v, axis=-1)
    combine = jnp.sum(jax.nn.one_hot(top_i, N_EXPERTS, dtype=jnp.float32) * gates[..., None], axis=-2)
    out = jnp.zeros_like(h)
    for e in range(N_EXPERTS):
        out = out + combine[..., e:e + 1].astype(h.dtype) * swiglu(h, w_gu[e], w_down[e])
    return out


def setup_inputs(seed: int = 0) -> dict:
    key = jax.random.key(seed)
    ks = jax.random.split(key, 18)
    na, nb = (DEPTH + 1) // 2, DEPTH // 2
    f32 = jnp.float32

    def dense(k, shape, fan_in):
        return jax.random.normal(k, shape, f32) * (fan_in ** -0.5)

    def gain(k, shape):
        return 1.0 + 0.02 * jax.random.normal(k, shape, f32)

    return {
        'x': jax.random.normal(ks[0], (BATCH, SEQ, D_MODEL), f32),
        'norm_mix': gain(ks[1], (DEPTH, D_MODEL)),
        'norm_ffn': gain(ks[2], (DEPTH, D_MODEL)),
        'dsa_w_in': dense(ks[3], (na, D_MODEL, DSA_IN_DIM), D_MODEL),
        'dsa_idx_k_gain': gain(ks[4], (na, IDX_DIM)),
        'dsa_w_out': dense(ks[5], (na, N_HEADS * HEAD_DIM, D_MODEL), N_HEADS * HEAD_DIM),
        'ffn_w_gu': dense(ks[6], (na, D_MODEL, 2 * FFN_DIM), D_MODEL),
        'ffn_w_down': dense(ks[7], (na, FFN_DIM, D_MODEL), FFN_DIM),
        'gmlp_w_in': dense(ks[8], (nb, D_MODEL, 2 * GMLP_WIDTH), D_MODEL),
        'gmlp_ln_gain': gain(ks[9], (nb, GMLP_WIDTH)),
        'gmlp_ln_bias': 0.02 * jax.random.normal(ks[10], (nb, GMLP_WIDTH), f32),
        'gmlp_w_spatial': dense(ks[11], (nb, GMLP_GROUPS, GMLP_CHUNK, GMLP_CHUNK), GMLP_CHUNK),
        'gmlp_b_spatial': gain(ks[12], (nb, GMLP_GROUPS, GMLP_CHUNK)),
        'gmlp_w_out': dense(ks[13], (nb, GMLP_WIDTH, D_MODEL), GMLP_WIDTH),
        'moe_w_router': dense(ks[14], (nb, D_MODEL, N_EXPERTS), D_MODEL),
        'moe_w_gu': dense(ks[15], (nb, N_EXPERTS, D_MODEL, 2 * FFN_DIM), D_MODEL),
        'moe_w_down': dense(ks[16], (nb, N_EXPERTS, FFN_DIM, D_MODEL), FFN_DIM),
        'final_norm': gain(ks[17], (D_MODEL,)),
    }


def reference(x, norm_mix, norm_ffn, dsa_w_in, dsa_idx_k_gain, dsa_w_out, ffn_w_gu, ffn_w_down,
              gmlp_w_in, gmlp_ln_gain, gmlp_ln_bias, gmlp_w_spatial, gmlp_b_spatial, gmlp_w_out,
              moe_w_router, moe_w_gu, moe_w_down, final_norm):
    for i in range(DEPTH):
        j = i // 2
        h = rms_norm(x, norm_mix[i])
        if i % 2 == 0:
            x = x + dsa_attention(h, dsa_w_in[j], dsa_idx_k_gain[j], dsa_w_out[j])
            x = x + swiglu(rms_norm(x, norm_ffn[i]), ffn_w_gu[j], ffn_w_down[j])
        else:
            x = x + chunked_gmlp(h, gmlp_w_in[j], gmlp_ln_gain[j], gmlp_ln_bias[j],
                                 gmlp_w_spatial[j], gmlp_b_spatial[j], gmlp_w_out[j])
            x = x + moe_swiglu(rms_norm(x, norm_ffn[i]), moe_w_router[j], moe_w_gu[j], moe_w_down[j])
    return rms_norm(x, final_norm)
```

```python
import functools

import numpy as np
import jax
import jax.numpy as jnp
from jax import lax
from jax.experimental import pallas as pl
from jax.experimental.pallas import tpu as pltpu

F32 = jnp.float32
BF16 = jnp.bfloat16

D_MODEL = 1024
BATCH = 8
SEQ = 2048
N_TOK = BATCH * SEQ
N_HEADS = 8
HEAD_DIM = 128
IDX_HEADS = 8
IDX_DIM = 64
IDX_ROPE_DIM = 32
INDEX_TOPK = 256
Q_BLOCK = 128
ROPE_THETA = 10000.0
GMLP_WIDTH = 1024
GMLP_GROUPS = 8
GMLP_CHUNK = 128
FFN_DIM = 3584
N_EXPERTS = 8
NORM_EPS = 1e-6

LANES = 128
SUBLANES = 8
VMEM_LIMIT = 56 * 1024 * 1024

DSA_COLS = 1920
COL_K = N_HEADS * HEAD_DIM
COL_V = COL_K + HEAD_DIM
COL_QI = COL_V + HEAD_DIM
COL_KI = COL_QI + IDX_HEADS * IDX_DIM

NT_DIMS = (((1,), (1,)), ((), ()))
MASK_BIAS = -1e30
SCORE_MASKED = -3.0e38


def _rmsnorm(x, g):
    ms = jnp.mean(x * x, axis=-1, keepdims=True)
    return x * lax.rsqrt(ms + NORM_EPS) * g


def _dsa_proj_kernel(x_ref, g_ref, w_ref, wwi_ref, kig_ref, cq_ref, sq_ref, ci_ref, sia_ref, sib_ref,
                     q_ref, k_ref, vt_ref, kia_ref, kib_ref, qi_ref, wt_ref):
    h = _rmsnorm(x_ref[...], g_ref[...]).astype(BF16)
    y = jnp.dot(h, w_ref[...], preferred_element_type=F32)
    cq = cq_ref[...]
    sq = sq_ref[...]
    ci = ci_ref[...]
    sia = sia_ref[...]
    sib = sib_ref[...]

    def rope_head(t):
        return t * cq + pltpu.roll(t, HEAD_DIM // 2, 1) * sq

    def rope_idx(t):
        half = IDX_ROPE_DIM // 2
        return t * ci + pltpu.roll(t, LANES - half, 1) * sia + pltpu.roll(t, half, 1) * sib

    q_scale = HEAD_DIM ** -0.5
    for hh in range(N_HEADS):
        q_ref[hh] = (rope_head(y[:, hh * HEAD_DIM:(hh + 1) * HEAD_DIM]) * q_scale).astype(BF16)
    k_ref[...] = rope_head(y[:, COL_K:COL_V]).astype(BF16)
    vt_ref[...] = y[:, COL_V:COL_QI].T.astype(BF16)
    for p in range(IDX_HEADS * IDX_DIM // LANES):
        qi_ref[:, p * LANES:(p + 1) * LANES] = rope_idx(
            y[:, COL_QI + p * LANES:COL_QI + (p + 1) * LANES]).astype(BF16)

    last = y[:, COL_KI:COL_KI + LANES]
    lane = lax.broadcasted_iota(jnp.int32, last.shape, 1)
    kraw = jnp.where(lane < IDX_DIM, last, 0.0)
    kms = jnp.sum(kraw * kraw, axis=-1, keepdims=True) * (1.0 / IDX_DIM)
    kn = kraw * lax.rsqrt(kms + NORM_EPS) * kig_ref[...]
    kr = rope_idx(kn)
    kia_ref[...] = kr.astype(BF16)
    kib_ref[...] = pltpu.roll(kr, IDX_DIM, 1).astype(BF16)

    w_scale = IDX_HEADS ** -0.5 * IDX_DIM ** -0.5
    wt_ref[...] = lax.dot_general(wwi_ref[...], h, NT_DIMS, preferred_element_type=F32) * w_scale


def _dsa_proj(x2d, g, w_all, wwi_t, kig, tabs, tm=512):
    n = x2d.shape[0]
    nblk_seq = SEQ // tm
    tab_spec = pl.BlockSpec((tm, LANES), lambda i: (i % nblk_seq, 0))
    full = lambda shape: pl.BlockSpec(shape, lambda i: (0,) * len(shape))
    return pl.pallas_call(
        _dsa_proj_kernel,
        grid=(n // tm,),
        in_specs=[
            pl.BlockSpec((tm, D_MODEL), lambda i: (i, 0)),
            full((1, D_MODEL)),
            full((D_MODEL, DSA_COLS)),
            full((IDX_HEADS, D_MODEL)),
            full((1, LANES)),
            tab_spec, tab_spec, tab_spec, tab_spec, tab_spec,
        ],
        out_specs=[
            pl.BlockSpec((N_HEADS, tm, HEAD_DIM), lambda i: (0, i, 0)),
            pl.BlockSpec((tm, HEAD_DIM), lambda i: (i, 0)),
            pl.BlockSpec((HEAD_DIM, tm), lambda i: (0, i)),
            pl.BlockSpec((tm, LANES), lambda i: (i, 0)),
            pl.BlockSpec((tm, LANES), lambda i: (i, 0)),
            pl.BlockSpec((tm, IDX_HEADS * IDX_DIM), lambda i: (i, 0)),
            pl.BlockSpec((IDX_HEADS, tm), lambda i: (0, i)),
        ],
        out_shape=[
            jax.ShapeDtypeStruct((N_HEADS, n, HEAD_DIM), BF16),
            jax.ShapeDtypeStruct((n, HEAD_DIM), BF16),
            jax.ShapeDtypeStruct((HEAD_DIM, n), BF16),
            jax.ShapeDtypeStruct((n, LANES), BF16),
            jax.ShapeDtypeStruct((n, LANES), BF16),
            jax.ShapeDtypeStruct((n, IDX_HEADS * IDX_DIM), BF16),
            jax.ShapeDtypeStruct((IDX_HEADS, n), F32),
        ],
        compiler_params=pltpu.CompilerParams(
            dimension_semantics=("parallel",), vmem_limit_bytes=VMEM_LIMIT),
        name="dsa_proj",
    )(x2d, g, w_all, wwi_t, kig, *tabs)


def _ordered_bits_to_float(u):
    key = u ^ jnp.int32(-2 ** 31)
    key = jnp.maximum(key, jnp.int32(-2139095041))
    bits = jnp.where(key >= 0, key, key ^ jnp.int32(0x7FFFFFFF))
    return lax.bitcast_convert_type(bits, F32)


def _dsa_attn_kernel(q_ref, k_ref, vt_ref, kia_ref, kib_ref, qi_ref, wt_ref, o_ref, s_ref):
    j = pl.program_id(1)
    n_chunks = SEQ // Q_BLOCK

    for p in range(IDX_HEADS // 2):
        qp = qi_ref[:, p * LANES:(p + 1) * LANES]
        ra = lax.dot_general(kia_ref[...], qp, NT_DIMS, preferred_element_type=F32)
        rb = lax.dot_general(kib_ref[...], qp, NT_DIMS, preferred_element_type=F32)
        contrib = (jnp.maximum(ra, 0.0) * wt_ref[2 * p:2 * p + 1, :]
                   + jnp.maximum(rb, 0.0) * wt_ref[2 * p + 1:2 * p + 2, :])
        if p == 0:
            s_ref[...] = contrib
        else:
            s_ref[...] += contrib

    qpos = j * Q_BLOCK + lax.broadcasted_iota(jnp.int32, (Q_BLOCK, Q_BLOCK), 1)
    krow = lax.broadcasted_iota(jnp.int32, (Q_BLOCK, Q_BLOCK), 0)

    def causal_chunk(c):
        return (krow + c * Q_BLOCK) <= qpos

    for c in range(n_chunks):
        sl = pl.ds(c * Q_BLOCK, Q_BLOCK)
        s_ref[sl, :] = jnp.where(causal_chunk(c), s_ref[sl, :], SCORE_MASKED)

    def count_ge(cand):
        acc = jnp.zeros((SUBLANES, LANES), F32)
        for c in range(n_chunks):
            sc = s_ref[pl.ds(c * Q_BLOCK, Q_BLOCK), :]
            ind = jnp.where(sc >= cand, 1.0, 0.0)
            acc = acc + ind.reshape(Q_BLOCK // SUBLANES, SUBLANES, LANES).sum(axis=0)
        return jnp.sum(acc, axis=0, keepdims=True)

    def bit_body(i, carry):
        prefix, cbest = carry
        cu = prefix | lax.shift_left(jnp.int32(1), 31 - i)
        cnt = count_ge(_ordered_bits_to_float(cu))
        ok = cnt >= float(INDEX_TOPK)
        return jnp.where(ok, cu, prefix), jnp.where(ok, cnt, cbest)

    prefix, cbest = lax.fori_loop(
        0, 32, bit_body,
        (jnp.zeros((1, LANES), jnp.int32), jnp.full((1, LANES), float(SEQ), F32)))
    tau = _ordered_bits_to_float(prefix)
    drop = cbest - float(INDEX_TOPK)

    upper = jnp.where(krow < lax.broadcasted_iota(jnp.int32, (Q_BLOCK, Q_BLOCK), 1), 1.0, 0.0).astype(BF16)
    later_ties = jnp.zeros((1, LANES), F32)
    for c in reversed(range(n_chunks)):
        sl = pl.ds(c * Q_BLOCK, Q_BLOCK)
        sc = s_ref[sl, :]
        eq = jnp.where(sc == tau, 1.0, 0.0)
        ties_after = jnp.dot(upper, eq.astype(BF16), preferred_element_type=F32) + later_ties
        later_ties = later_ties + jnp.sum(eq, axis=0, keepdims=True)
        tie_bias = jnp.where(ties_after >= drop, 0.0, MASK_BIAS)
        bias = jnp.where(sc > tau, 0.0, jnp.where(sc == tau, tie_bias, MASK_BIAS))
        s_ref[sl, :] = jnp.where(causal_chunk(c), bias, MASK_BIAS)

    def head_body(h, _):
        lt = lax.dot_general(k_ref[...], q_ref[h], NT_DIMS, preferred_element_type=F32) + s_ref[...]
        m = jnp.max(lt, axis=0, keepdims=True)
        p = jnp.exp(lt - m)
        denom = jnp.sum(p, axis=0, keepdims=True)
        ot = jnp.dot(vt_ref[...], p.astype(BF16), preferred_element_type=F32)
        o_ref[h] = (ot * (1.0 / denom)).T.astype(BF16)
        return 0

    lax.fori_loop(0, N_HEADS, head_body, 0)


def _dsa_attn(q3, k, vt, kia, kib, qi, wt):
    nqb = SEQ // Q_BLOCK
    return pl.pallas_call(
        _dsa_attn_kernel,
        grid=(BATCH, nqb),
        in_specs=[
            pl.BlockSpec((N_HEADS, Q_BLOCK, HEAD_DIM), lambda b, j: (0, b * nqb + j, 0)),
            pl.BlockSpec((SEQ, HEAD_DIM), lambda b, j: (b, 0)),
            pl.BlockSpec((HEAD_DIM, SEQ), lambda b, j: (0, b)),
            pl.BlockSpec((SEQ, LANES), lambda b, j: (b, 0)),
            pl.BlockSpec((SEQ, LANES), lambda b, j: (b, 0)),
            pl.BlockSpec((Q_BLOCK, IDX_HEADS * IDX_DIM), lambda b, j: (b * nqb + j, 0)),
            pl.BlockSpec((IDX_HEADS, Q_BLOCK), lambda b, j: (0, b * nqb + j)),
        ],
        out_specs=pl.BlockSpec((N_HEADS, Q_BLOCK, HEAD_DIM), lambda b, j: (0, b * nqb + j, 0)),
        out_shape=jax.ShapeDtypeStruct((N_HEADS, N_TOK, HEAD_DIM), BF16),
        scratch_shapes=[pltpu.VMEM((SEQ, Q_BLOCK), F32)],
        compiler_params=pltpu.CompilerParams(
            dimension_semantics=("parallel", "arbitrary"), vmem_limit_bytes=VMEM_LIMIT),
        name="dsa_attn",
    )(q3, k, vt, kia, kib, qi, wt)


def _resid_proj_kernel(x_ref, o_ref, w_ref, out_ref):
    o = jnp.concatenate([o_ref[h] for h in range(N_HEADS)], axis=1)
    out_ref[...] = x_ref[...] + jnp.dot(o, w_ref[...], preferred_element_type=F32)


def _resid_proj(x2d, o3, w_out, tm=512):
    n = x2d.shape[0]
    return pl.pallas_call(
        _resid_proj_kernel,
        grid=(n // tm,),
        in_specs=[
            pl.BlockSpec((tm, D_MODEL), lambda i: (i, 0)),
            pl.BlockSpec((N_HEADS, tm, HEAD_DIM), lambda i: (0, i, 0)),
            pl.BlockSpec((D_MODEL, D_MODEL), lambda i: (0, 0)),
        ],
        out_specs=pl.BlockSpec((tm, D_MODEL), lambda i: (i, 0)),
        out_shape=jax.ShapeDtypeStruct((n, D_MODEL), F32),
        compiler_params=pltpu.CompilerParams(
            dimension_semantics=("parallel",), vmem_limit_bytes=VMEM_LIMIT),
        name="dsa_out_proj",
    )(x2d, o3, w_out)


def _ffn_kernel(x_ref, g_ref, wg_ref, wu_ref, wd_ref, out_ref, h_ref, acc_ref):
    c = pl.program_id(1)

    @pl.when(c == 0)
    def _():
        h_ref[...] = _rmsnorm(x_ref[...], g_ref[...]).astype(BF16)
        acc_ref[...] = jnp.zeros_like(acc_ref)

    h = h_ref[...]
    gate = jnp.dot(h, wg_ref[...], preferred_element_type=F32)
    up = jnp.dot(h, wu_ref[...], preferred_element_type=F32)
    a = (gate * jax.nn.sigmoid(gate) * up).astype(BF16)
    acc_ref[...] += jnp.dot(a, wd_ref[...], preferred_element_type=F32)

    @pl.when(c == pl.num_programs(1) - 1)
    def _():
        out_ref[...] = x_ref[...] + acc_ref[...]


def _ffn(x2d, g, w_gu, w_down, tm=1024, fc=512):
    n = x2d.shape[0]
    nc = FFN_DIM // fc
    return pl.pallas_call(
        _ffn_kernel,
        grid=(n // tm, nc),
        in_specs=[
            pl.BlockSpec((tm, D_MODEL), lambda i, c: (i, 0)),
            pl.BlockSpec((1, D_MODEL), lambda i, c: (0, 0)),
            pl.BlockSpec((D_MODEL, fc), lambda i, c: (0, c)),
            pl.BlockSpec((D_MODEL, fc), lambda i, c: (0, c + nc)),
            pl.BlockSpec((fc, D_MODEL), lambda i, c: (c, 0)),
        ],
        out_specs=pl.BlockSpec((tm, D_MODEL), lambda i, c: (i, 0)),
        out_shape=jax.ShapeDtypeStruct((n, D_MODEL), F32),
        scratch_shapes=[pltpu.VMEM((tm, D_MODEL), BF16), pltpu.VMEM((tm, D_MODEL), F32)],
        compiler_params=pltpu.CompilerParams(
            dimension_semantics=("parallel", "arbitrary"), vmem_limit_bytes=VMEM_LIMIT),
        name="swiglu_ffn",
    )(x2d, g, w_gu, w_gu, w_down)


def _gelu_tanh(x):
    return 0.5 * x * (1.0 + jnp.tanh(np.sqrt(2.0 / np.pi) * (x + 0.044715 * (x * x * x))))


def _gmlp_kernel(x_ref, g_ref, win_ref, lng_ref, lnb_ref, ws_ref, bs_ref, wout_ref, out_ref):
    x = x_ref[...]
    tm = x.shape[0]
    h = _rmsnorm(x, g_ref[...]).astype(BF16)
    y = _gelu_tanh(jnp.dot(h, win_ref[...], preferred_element_type=F32))
    u = y[:, :GMLP_WIDTH]
    v = y[:, GMLP_WIDTH:]
    mu = jnp.mean(v, axis=-1, keepdims=True)
    vc = v - mu
    var = jnp.mean(vc * vc, axis=-1, keepdims=True)
    vb = (vc * lax.rsqrt(var + NORM_EPS) * lng_ref[...] + lnb_ref[...]).astype(BF16)

    gw = GMLP_WIDTH // GMLP_GROUPS
    row = lax.broadcasted_iota(jnp.int32, (GMLP_CHUNK, GMLP_CHUNK), 0)
    col = lax.broadcasted_iota(jnp.int32, (GMLP_CHUNK, GMLP_CHUNK), 1)
    w_tril = [jnp.where(row >= col, ws_ref[gi], 0.0).astype(BF16) for gi in range(GMLP_GROUPS)]
    rows = []
    for c in range(tm // GMLP_CHUNK):
        parts = [
            jnp.dot(w_tril[gi], vb[c * GMLP_CHUNK:(c + 1) * GMLP_CHUNK, gi * gw:(gi + 1) * gw],
                    preferred_element_type=F32)
            for gi in range(GMLP_GROUPS)
        ]
        rows.append(jnp.concatenate(parts, axis=1) + bs_ref[...])
    s = jnp.concatenate(rows, axis=0)
    z = (u * s).astype(BF16)
    out_ref[...] = x + jnp.dot(z, wout_ref[...], preferred_element_type=F32)


def _gmlp(x2d, g, w_in, ln_g, ln_b, w_sp, b_sp_full, w_out, tm=512):
    n = x2d.shape[0]
    full = lambda shape: pl.BlockSpec(shape, lambda i: (0,) * len(shape))
    return pl.pallas_call(
        _gmlp_kernel,
        grid=(n // tm,),
        in_specs=[
            pl.BlockSpec((tm, D_MODEL), lambda i: (i, 0)),
            full((1, D_MODEL)),
            full((D_MODEL, 2 * GMLP_WIDTH)),
            full((1, GMLP_WIDTH)),
            full((1, GMLP_WIDTH)),
            full((GMLP_GROUPS, GMLP_CHUNK, GMLP_CHUNK)),
            full((GMLP_CHUNK, GMLP_WIDTH)),
            full((GMLP_WIDTH, D_MODEL)),
        ],
        out_specs=pl.BlockSpec((tm, D_MODEL), lambda i: (i, 0)),
        out_shape=jax.ShapeDtypeStruct((n, D_MODEL), F32),
        compiler_params=pltpu.CompilerParams(
            dimension_semantics=("parallel",), vmem_limit_bytes=VMEM_LIMIT),
        name="gmlp",
    )(x2d, g, w_in, ln_g, ln_b, w_sp, b_sp_full, w_out)


def _moe_kernel(x_ref, g_ref, wr_ref, wg_ref, wu_ref, wd_ref, gf_ref, out_ref, h_ref, cw_ref, acc_ref):
    e = pl.program_id(1)
    c = pl.program_id(2)
    first = jnp.logical_and(e == 0, c == 0)
    last = jnp.logical_and(e == pl.num_programs(1) - 1, c == pl.num_programs(2) - 1)

    @pl.when(first)
    def _():
        hf = _rmsnorm(x_ref[...], g_ref[...])
        h_ref[...] = hf.astype(BF16)
        acc_ref[...] = jnp.zeros_like(acc_ref)
        logits = jnp.dot(hf, wr_ref[...], preferred_element_type=F32, precision=lax.Precision.HIGHEST)
        lane = lax.broadcasted_iota(jnp.int32, logits.shape, 1).astype(F32)
        neg_inf = -jnp.inf
        lg = jnp.where(lane < N_EXPERTS, logits, neg_inf)
        m1 = jnp.max(lg, axis=-1, keepdims=True)
        i1 = jnp.min(jnp.where(lg == m1, lane, float(LANES)), axis=-1, keepdims=True)
        lg2 = jnp.where(lane == i1, neg_inf, lg)
        m2 = jnp.max(lg2, axis=-1, keepdims=True)
        i2 = jnp.min(jnp.where(lg2 == m2, lane, float(LANES)), axis=-1, keepdims=True)
        t = jnp.exp(m2 - m1)
        inv = 1.0 / (1.0 + t)
        cw_ref[...] = jnp.where(lane == i1, inv, 0.0) + jnp.where(lane == i2, t * inv, 0.0)

    lane = lax.broadcasted_iota(jnp.int32, cw_ref.shape, 1)
    cw_e = jnp.sum(jnp.where(lane == e, cw_ref[...], 0.0), axis=-1, keepdims=True)
    h = h_ref[...]
    gate = jnp.dot(h, wg_ref[0], preferred_element_type=F32)
    up = jnp.dot(h, wu_ref[0], preferred_element_type=F32)
    a = (gate * jax.nn.sigmoid(gate) * up * cw_e).astype(BF16)
    acc_ref[...] += jnp.dot(a, wd_ref[0], preferred_element_type=F32)

    @pl.when(last)
    def _():
        out_ref[...] = _rmsnorm(x_ref[...] + acc_ref[...], gf_ref[...])


def _moe(x2d, g, w_router, w_gu, w_down, g_final, tm=1024, fc=512):
    n = x2d.shape[0]
    nc = FFN_DIM // fc
    return pl.pallas_call(
        _moe_kernel,
        grid=(n // tm, N_EXPERTS, nc),
        in_specs=[
            pl.BlockSpec((tm, D_MODEL), lambda i, e, c: (i, 0)),
            pl.BlockSpec((1, D_MODEL), lambda i, e, c: (0, 0)),
            pl.BlockSpec((D_MODEL, LANES), lambda i, e, c: (0, 0)),
            pl.BlockSpec((1, D_MODEL, fc), lambda i, e, c: (e, 0, c)),
            pl.BlockSpec((1, D_MODEL, fc), lambda i, e, c: (e, 0, c + nc)),
            pl.BlockSpec((1, fc, D_MODEL), lambda i, e, c: (e, c, 0)),
            pl.BlockSpec((1, D_MODEL), lambda i, e, c: (0, 0)),
        ],
        out_specs=pl.BlockSpec((tm, D_MODEL), lambda i, e, c: (i, 0)),
        out_shape=jax.ShapeDtypeStruct((n, D_MODEL), F32),
        scratch_shapes=[
            pltpu.VMEM((tm, D_MODEL), BF16),
            pltpu.VMEM((tm, LANES), F32),
            pltpu.VMEM((tm, D_MODEL), F32),
        ],
        compiler_params=pltpu.CompilerParams(
            dimension_semantics=("parallel", "arbitrary", "arbitrary"), vmem_limit_bytes=VMEM_LIMIT),
        name="moe",
    )(x2d, g, w_router, w_gu, w_gu, w_down, g_final)


def _rope_tables():
    def tables(dim):
        inv_freq = ROPE_THETA ** (-jnp.arange(0, dim, 2, dtype=F32) / dim)
        ang = jnp.arange(SEQ, dtype=F32)[:, None] * inv_freq[None, :]
        return jnp.cos(ang), jnp.sin(ang)

    c, s = tables(HEAD_DIM)
    cq = jnp.concatenate([c, c], axis=1)
    sq = jnp.concatenate([-s, s], axis=1)
    c, s = tables(IDX_ROPE_DIM)
    ones = jnp.ones((SEQ, IDX_DIM - IDX_ROPE_DIM), F32)
    zeros = jnp.zeros_like(ones)
    z16 = jnp.zeros_like(s)
    ci = jnp.concatenate([c, c, ones], axis=1)
    sia = jnp.concatenate([-s, z16, zeros], axis=1)
    sib = jnp.concatenate([z16, s, zeros], axis=1)
    rep = LANES // IDX_DIM
    return cq, sq, jnp.tile(ci, (1, rep)), jnp.tile(sia, (1, rep)), jnp.tile(sib, (1, rep))


def kernel(x, norm_mix, norm_ffn, dsa_w_in, dsa_idx_k_gain, dsa_w_out, ffn_w_gu, ffn_w_down, gmlp_w_in,
           gmlp_ln_gain, gmlp_ln_bias, gmlp_w_spatial, gmlp_b_spatial, gmlp_w_out, moe_w_router, moe_w_gu,
           moe_w_down, final_norm):
    x2d = x.reshape(N_TOK, D_MODEL)
    row = lambda v: v.reshape(1, -1)

    w_in = dsa_w_in[0]
    w_all = jnp.pad(w_in, ((0, 0), (0, DSA_COLS - w_in.shape[1]))).astype(BF16)
    wwi_t = w_in[:, COL_KI + IDX_DIM:COL_KI + IDX_DIM + IDX_HEADS].T.astype(BF16)
    kig = jnp.pad(dsa_idx_k_gain[0], (0, LANES - IDX_DIM)).reshape(1, LANES)
    q3, k, vt, kia, kib, qi, wt = _dsa_proj(x2d, row(norm_mix[0]), w_all, wwi_t, kig, _rope_tables())
    o3 = _dsa_attn(q3, k, vt, kia, kib, qi, wt)
    x2d = _resid_proj(x2d, o3, dsa_w_out[0].astype(BF16))
    x2d = _ffn(x2d, row(norm_ffn[0]), ffn_w_gu[0].astype(BF16), ffn_w_down[0].astype(BF16))

    gw = GMLP_WIDTH // GMLP_GROUPS
    b_sp_full = jnp.repeat(gmlp_b_spatial[0].T, gw, axis=1)
    x2d = _gmlp(x2d, row(norm_mix[1]), gmlp_w_in[0].astype(BF16), row(gmlp_ln_gain[0]), row(gmlp_ln_bias[0]),
                gmlp_w_spatial[0], b_sp_full, gmlp_w_out[0].astype(BF16))
    w_router = jnp.pad(moe_w_router[0], ((0, 0), (0, LANES - N_EXPERTS)))
    out = _moe(x2d, row(norm_ffn[1]), w_router, moe_w_gu[0].astype(BF16), moe_w_down[0].astype(BF16),
               row(final_norm))
    return out.reshape(BATCH, SEQ, D_MODEL)
```

```python
import functools

import numpy as np
import jax
import jax.numpy as jnp
from jax import lax
from jax.experimental import pallas as pl
from jax.experimental.pallas import tpu as pltpu

F32 = jnp.float32
BF16 = jnp.bfloat16

D_MODEL = 1024
BATCH = 8
SEQ = 2048
N_TOK = BATCH * SEQ
N_HEADS = 8
HEAD_DIM = 128
IDX_HEADS = 8
IDX_DIM = 64
IDX_ROPE_DIM = 32
INDEX_TOPK = 256
Q_BLOCK = 128
KEY_SUPER = 512
ROPE_THETA = 10000.0
GMLP_WIDTH = 1024
GMLP_GROUPS = 8
GMLP_CHUNK = 128
FFN_DIM = 3584
N_EXPERTS = 8
NORM_EPS = 1e-6

LANES = 128
SUBLANES = 8
VMEM_LIMIT = 56 * 1024 * 1024

DSA_COLS = 1920
COL_K = N_HEADS * HEAD_DIM
COL_V = COL_K + HEAD_DIM
COL_QI = COL_V + HEAD_DIM
COL_KI = COL_QI + IDX_HEADS * IDX_DIM

NT_DIMS = (((1,), (1,)), ((), ()))
MASK_BIAS = -1e30
SCORE_MASKED = -3.0e38


def _rmsnorm(x, g):
    ms = jnp.mean(x * x, axis=-1, keepdims=True)
    return x * lax.rsqrt(ms + NORM_EPS) * g


def _dsa_proj_kernel(x_ref, g_ref, w_ref, wwi_ref, kig_ref, cq_ref, sq_ref, ci_ref, sia_ref, sib_ref,
                     q_ref, k_ref, vt_ref, kia_ref, kib_ref, qi_ref, wt_ref):
    h = _rmsnorm(x_ref[...], g_ref[...]).astype(BF16)
    y = jnp.dot(h, w_ref[...], preferred_element_type=F32)
    cq = cq_ref[...]
    sq = sq_ref[...]
    ci = ci_ref[...]
    sia = sia_ref[...]
    sib = sib_ref[...]

    def rope_head(t):
        return t * cq + pltpu.roll(t, HEAD_DIM // 2, 1) * sq

    def rope_idx(t):
        half = IDX_ROPE_DIM // 2
        return t * ci + pltpu.roll(t, LANES - half, 1) * sia + pltpu.roll(t, half, 1) * sib

    q_scale = HEAD_DIM ** -0.5 * np.log2(np.e)
    for hh in range(N_HEADS):
        q_ref[hh] = (rope_head(y[:, hh * HEAD_DIM:(hh + 1) * HEAD_DIM]) * q_scale).astype(BF16)
    k_ref[...] = rope_head(y[:, COL_K:COL_V]).astype(BF16)
    vt_ref[0] = y[:, COL_V:COL_QI].T.astype(BF16)
    for p in range(IDX_HEADS * IDX_DIM // LANES):
        qi_ref[:, p * LANES:(p + 1) * LANES] = rope_idx(
            y[:, COL_QI + p * LANES:COL_QI + (p + 1) * LANES]).astype(BF16)

    last = y[:, COL_KI:COL_KI + LANES]
    lane = lax.broadcasted_iota(jnp.int32, last.shape, 1)
    kraw = jnp.where(lane < IDX_DIM, last, 0.0)
    kms = jnp.sum(kraw * kraw, axis=-1, keepdims=True) * (1.0 / IDX_DIM)
    kn = kraw * lax.rsqrt(kms + NORM_EPS) * kig_ref[...]
    kr = rope_idx(kn)
    kia_ref[...] = kr.astype(BF16)
    kib_ref[...] = pltpu.roll(kr, IDX_DIM, 1).astype(BF16)

    w_scale = IDX_HEADS ** -0.5 * IDX_DIM ** -0.5
    wt_ref[...] = lax.dot_general(wwi_ref[...], h, NT_DIMS, preferred_element_type=F32) * w_scale


def _dsa_proj(x2d, g, w_all, wwi_t, kig, tabs, tm=KEY_SUPER):
    n = x2d.shape[0]
    nblk_seq = SEQ // tm
    tab_spec = pl.BlockSpec((tm, LANES), lambda i: (i % nblk_seq, 0))
    full = lambda shape: pl.BlockSpec(shape, lambda i: (0,) * len(shape))
    return pl.pallas_call(
        _dsa_proj_kernel,
        grid=(n // tm,),
        in_specs=[
            pl.BlockSpec((tm, D_MODEL), lambda i: (i, 0)),
            full((1, D_MODEL)),
            full((D_MODEL, DSA_COLS)),
            full((IDX_HEADS, D_MODEL)),
            full((1, LANES)),
            tab_spec, tab_spec, tab_spec, tab_spec, tab_spec,
        ],
        out_specs=[
            pl.BlockSpec((N_HEADS, tm, HEAD_DIM), lambda i: (0, i, 0)),
            pl.BlockSpec((tm, HEAD_DIM), lambda i: (i, 0)),
            pl.BlockSpec((1, HEAD_DIM, tm), lambda i: (i, 0, 0)),
            pl.BlockSpec((tm, LANES), lambda i: (i, 0)),
            pl.BlockSpec((tm, LANES), lambda i: (i, 0)),
            pl.BlockSpec((tm, IDX_HEADS * IDX_DIM), lambda i: (i, 0)),
            pl.BlockSpec((IDX_HEADS, tm), lambda i: (0, i)),
        ],
        out_shape=[
            jax.ShapeDtypeStruct((N_HEADS, n, HEAD_DIM), BF16),
            jax.ShapeDtypeStruct((n, HEAD_DIM), BF16),
            jax.ShapeDtypeStruct((n // tm, HEAD_DIM, tm), BF16),
            jax.ShapeDtypeStruct((n, LANES), BF16),
            jax.ShapeDtypeStruct((n, LANES), BF16),
            jax.ShapeDtypeStruct((n, IDX_HEADS * IDX_DIM), BF16),
            jax.ShapeDtypeStruct((IDX_HEADS, n), F32),
        ],
        compiler_params=pltpu.CompilerParams(
            dimension_semantics=("parallel",), vmem_limit_bytes=VMEM_LIMIT),
        name="dsa_proj",
    )(x2d, g, w_all, wwi_t, kig, *tabs)


def _ordered_bits_to_float(u):
    key = u ^ jnp.int32(-2 ** 31)
    key = jnp.maximum(key, jnp.int32(-2139095041))
    bits = jnp.where(key >= 0, key, key ^ jnp.int32(0x7FFFFFFF))
    return lax.bitcast_convert_type(bits, F32)


def _dsa_attn_kernel(q_ref, k_ref, vt_ref, kia_ref, kib_ref, qi_ref, wt_ref, o_ref,
                     s_ref, l_ref, p_ref, acc_ref):
    j = pl.program_id(1)
    per_super = KEY_SUPER // Q_BLOCK
    n_super = j // per_super + 1
    hq = N_HEADS * Q_BLOCK

    qpos = j * Q_BLOCK + lax.broadcasted_iota(jnp.int32, (Q_BLOCK, Q_BLOCK), 1)
    krow = lax.broadcasted_iota(jnp.int32, (Q_BLOCK, Q_BLOCK), 0)

    def rows_of(sc, c4):
        return pl.ds(pl.multiple_of(sc * KEY_SUPER, KEY_SUPER) + c4 * Q_BLOCK, Q_BLOCK)

    def super_rows(sc):
        return pl.ds(pl.multiple_of(sc * KEY_SUPER, KEY_SUPER), KEY_SUPER)

    def causal(sc, c4):
        return (krow + (sc * KEY_SUPER + c4 * Q_BLOCK)) <= qpos

    def fold8(t):
        return t.reshape(t.shape[0] // SUBLANES, SUBLANES, t.shape[1])

    def score_body(sc, _):
        srows = super_rows(sc)
        ka = kia_ref[srows, :]
        kb = kib_ref[srows, :]
        n_pairs = IDX_HEADS // 2
        for p in range(n_pairs):
            qp = qi_ref[:, p * LANES:(p + 1) * LANES]
            ra = lax.dot_general(ka, qp, NT_DIMS, preferred_element_type=F32)
            rb = lax.dot_general(kb, qp, NT_DIMS, preferred_element_type=F32)
            term = (jnp.maximum(ra, 0.0) * wt_ref[2 * p:2 * p + 1, :]
                    + jnp.maximum(rb, 0.0) * wt_ref[2 * p + 1:2 * p + 2, :])
            if p == 0:
                s_ref[srows, :] = term
            elif p < n_pairs - 1:
                s_ref[srows, :] += term
            else:
                kpos = sc * KEY_SUPER + lax.broadcasted_iota(jnp.int32, (KEY_SUPER, Q_BLOCK), 0)
                qp_s = j * Q_BLOCK + lax.broadcasted_iota(jnp.int32, (KEY_SUPER, Q_BLOCK), 1)
                s_ref[srows, :] = jnp.where(kpos <= qp_s, s_ref[srows, :] + term, SCORE_MASKED)
        return 0

    lax.fori_loop(0, n_super, score_body, 0)

    def count_ge(cand):
        def body(sc, acc):
            for c4 in range(per_super):
                ind = jnp.where(s_ref[rows_of(sc, c4), :] >= cand, 1.0, 0.0)
                acc = acc + fold8(ind).sum(axis=0)
            return acc
        acc = lax.fori_loop(0, n_super, body, jnp.zeros((SUBLANES, LANES), F32))
        return jnp.sum(acc, axis=0, keepdims=True)

    def bit_body(i, carry):
        prefix, cbest = carry
        cu = prefix | lax.shift_left(jnp.int32(1), 31 - i)
        cnt = count_ge(_ordered_bits_to_float(cu))
        ok = cnt >= float(INDEX_TOPK)
        return jnp.where(ok, cu, prefix), jnp.where(ok, cnt, cbest)

    n_keys = (n_super * KEY_SUPER).astype(F32)
    prefix, cbest = lax.fori_loop(
        0, 32, bit_body,
        (jnp.zeros((1, LANES), jnp.int32), jnp.zeros((1, LANES), F32) + n_keys))
    tau = _ordered_bits_to_float(prefix)
    drop = cbest - float(INDEX_TOPK)

    upper = jnp.where(krow < lax.broadcasted_iota(jnp.int32, (Q_BLOCK, Q_BLOCK), 1), 1.0, 0.0).astype(BF16)

    def mask_body(i, later_ties):
        sc = n_super - 1 - i
        for c4 in reversed(range(per_super)):
            rows = rows_of(sc, c4)
            s = s_ref[rows, :]
            eq = jnp.where(s == tau, 1.0, 0.0)
            ties_after = jnp.dot(upper, eq.astype(BF16), preferred_element_type=F32) + later_ties
            later_ties = later_ties + jnp.sum(eq, axis=0, keepdims=True)
            tie_bias = jnp.where(ties_after >= drop, 0.0, MASK_BIAS)
            bias = jnp.where(s > tau, 0.0, jnp.where(s == tau, tie_bias, MASK_BIAS))
            s_ref[rows, :] = jnp.where(causal(sc, c4), bias, MASK_BIAS)
        return later_ties

    lax.fori_loop(0, n_super, mask_body, jnp.zeros((1, LANES), F32))

    pair_w = 2 * Q_BLOCK

    def logit_body(sc, m_acc):
        srows = super_rows(sc)
        kc = k_ref[srows, :]
        bias = s_ref[srows, :]
        bias2 = jnp.concatenate([bias, bias], axis=1)
        parts = []
        for pr in range(N_HEADS // 2):
            qpair = q_ref[2 * pr:2 * pr + 2].reshape(pair_w, HEAD_DIM)
            lg = lax.dot_general(kc, qpair, NT_DIMS, preferred_element_type=F32) + bias2
            l_ref[srows, pr * pair_w:(pr + 1) * pair_w] = lg
            parts.append(fold8(lg).max(axis=0))
        return jnp.maximum(m_acc, jnp.concatenate(parts, axis=1))

    m_acc = lax.fori_loop(0, n_super, logit_body, jnp.full((SUBLANES, hq), MASK_BIAS, F32))
    m = jnp.max(m_acc, axis=0, keepdims=True)

    acc_ref[...] = jnp.zeros_like(acc_ref)

    def pv_body(sc, d_acc):
        for c4 in range(per_super):
            rows = rows_of(sc, c4)
            parts = []
            for pr in range(N_HEADS // 2):
                cols = slice(pr * pair_w, (pr + 1) * pair_w)
                p = jnp.exp2(l_ref[rows, cols] - m[:, cols])
                p_ref[rows, cols] = p.astype(BF16)
                parts.append(fold8(p).sum(axis=0))
            d_acc = d_acc + jnp.concatenate(parts, axis=1)
        acc_ref[...] += jnp.dot(vt_ref[sc], p_ref[super_rows(sc), :], preferred_element_type=F32)
        return d_acc

    d_acc = lax.fori_loop(0, n_super, pv_body, jnp.zeros((SUBLANES, hq), F32))
    inv = 1.0 / jnp.sum(d_acc, axis=0, keepdims=True)
    for h in range(N_HEADS):
        cols = slice(h * Q_BLOCK, (h + 1) * Q_BLOCK)
        o_ref[h] = (acc_ref[:, cols] * inv[:, cols]).T.astype(BF16)


def _dsa_attn(q3, k, vt, kia, kib, qi, wt):
    nqb = SEQ // Q_BLOCK
    n_sup = SEQ // KEY_SUPER
    hq = N_HEADS * Q_BLOCK
    return pl.pallas_call(
        _dsa_attn_kernel,
        grid=(BATCH, nqb),
        in_specs=[
            pl.BlockSpec((N_HEADS, Q_BLOCK, HEAD_DIM), lambda b, j: (0, b * nqb + j, 0)),
            pl.BlockSpec((SEQ, HEAD_DIM), lambda b, j: (b, 0)),
            pl.BlockSpec((n_sup, HEAD_DIM, KEY_SUPER), lambda b, j: (b, 0, 0)),
            pl.BlockSpec((SEQ, LANES), lambda b, j: (b, 0)),
            pl.BlockSpec((SEQ, LANES), lambda b, j: (b, 0)),
            pl.BlockSpec((Q_BLOCK, IDX_HEADS * IDX_DIM), lambda b, j: (b * nqb + j, 0)),
            pl.BlockSpec((IDX_HEADS, Q_BLOCK), lambda b, j: (0, b * nqb + j)),
        ],
        out_specs=pl.BlockSpec((N_HEADS, Q_BLOCK, HEAD_DIM), lambda b, j: (0, b * nqb + j, 0)),
        out_shape=jax.ShapeDtypeStruct((N_HEADS, N_TOK, HEAD_DIM), BF16),
        scratch_shapes=[
            pltpu.VMEM((SEQ, Q_BLOCK), F32),
            pltpu.VMEM((SEQ, hq), F32),
            pltpu.VMEM((SEQ, hq), BF16),
            pltpu.VMEM((HEAD_DIM, hq), F32),
        ],
        compiler_params=pltpu.CompilerParams(
            dimension_semantics=("parallel", "arbitrary"), vmem_limit_bytes=VMEM_LIMIT),
        name="dsa_attn",
    )(q3, k, vt, kia, kib, qi, wt)


def _resid_proj_kernel(x_ref, o_ref, w_ref, out_ref):
    o = jnp.concatenate([o_ref[h] for h in range(N_HEADS)], axis=1)
    out_ref[...] = x_ref[...] + jnp.dot(o, w_ref[...], preferred_element_type=F32)


def _resid_proj(x2d, o3, w_out, tm=512):
    n = x2d.shape[0]
    return pl.pallas_call(
        _resid_proj_kernel,
        grid=(n // tm,),
        in_specs=[
            pl.BlockSpec((tm, D_MODEL), lambda i: (i, 0)),
            pl.BlockSpec((N_HEADS, tm, HEAD_DIM), lambda i: (0, i, 0)),
            pl.BlockSpec((D_MODEL, D_MODEL), lambda i: (0, 0)),
        ],
        out_specs=pl.BlockSpec((tm, D_MODEL), lambda i: (i, 0)),
        out_shape=jax.ShapeDtypeStruct((n, D_MODEL), F32),
        compiler_params=pltpu.CompilerParams(
            dimension_semantics=("parallel",), vmem_limit_bytes=VMEM_LIMIT),
        name="dsa_out_proj",
    )(x2d, o3, w_out)


def _ffn_kernel(x_ref, g_ref, wg_ref, wu_ref, wd_ref, out_ref, h_ref, acc_ref):
    c = pl.program_id(1)

    @pl.when(c == 0)
    def _():
        h_ref[...] = _rmsnorm(x_ref[...], g_ref[...]).astype(BF16)
        acc_ref[...] = jnp.zeros_like(acc_ref)

    h = h_ref[...]
    gate = jnp.dot(h, wg_ref[...], preferred_element_type=F32)
    up = jnp.dot(h, wu_ref[...], preferred_element_type=F32)
    a = (gate * jax.nn.sigmoid(gate) * up).astype(BF16)
    acc_ref[...] += jnp.dot(a, wd_ref[...], preferred_element_type=F32)

    @pl.when(c == pl.num_programs(1) - 1)
    def _():
        out_ref[...] = x_ref[...] + acc_ref[...]


def _ffn(x2d, g, w_gu, w_down, tm=1024, fc=512):
    n = x2d.shape[0]
    nc = FFN_DIM // fc
    return pl.pallas_call(
        _ffn_kernel,
        grid=(n // tm, nc),
        in_specs=[
            pl.BlockSpec((tm, D_MODEL), lambda i, c: (i, 0)),
            pl.BlockSpec((1, D_MODEL), lambda i, c: (0, 0)),
            pl.BlockSpec((D_MODEL, fc), lambda i, c: (0, c)),
            pl.BlockSpec((D_MODEL, fc), lambda i, c: (0, c + nc)),
            pl.BlockSpec((fc, D_MODEL), lambda i, c: (c, 0)),
        ],
        out_specs=pl.BlockSpec((tm, D_MODEL), lambda i, c: (i, 0)),
        out_shape=jax.ShapeDtypeStruct((n, D_MODEL), F32),
        scratch_shapes=[pltpu.VMEM((tm, D_MODEL), BF16), pltpu.VMEM((tm, D_MODEL), F32)],
        compiler_params=pltpu.CompilerParams(
            dimension_semantics=("parallel", "arbitrary"), vmem_limit_bytes=VMEM_LIMIT),
        name="swiglu_ffn",
    )(x2d, g, w_gu, w_gu, w_down)


def _gelu_tanh(x):
    return 0.5 * x * (1.0 + jnp.tanh(np.sqrt(2.0 / np.pi) * (x + 0.044715 * (x * x * x))))


def _gmlp_kernel(x_ref, g_ref, win_ref, lng_ref, lnb_ref, ws_ref, bs_ref, wout_ref, out_ref):
    x = x_ref[...]
    tm = x.shape[0]
    h = _rmsnorm(x, g_ref[...]).astype(BF16)
    y = _gelu_tanh(jnp.dot(h, win_ref[...], preferred_element_type=F32))
    u = y[:, :GMLP_WIDTH]
    v = y[:, GMLP_WIDTH:]
    mu = jnp.mean(v, axis=-1, keepdims=True)
    vc = v - mu
    var = jnp.mean(vc * vc, axis=-1, keepdims=True)
    vb = (vc * lax.rsqrt(var + NORM_EPS) * lng_ref[...] + lnb_ref[...]).astype(BF16)

    gw = GMLP_WIDTH // GMLP_GROUPS
    row = lax.broadcasted_iota(jnp.int32, (GMLP_CHUNK, GMLP_CHUNK), 0)
    col = lax.broadcasted_iota(jnp.int32, (GMLP_CHUNK, GMLP_CHUNK), 1)
    w_tril = [jnp.where(row >= col, ws_ref[gi], 0.0).astype(BF16) for gi in range(GMLP_GROUPS)]
    rows = []
    for c in range(tm // GMLP_CHUNK):
        parts = [
            jnp.dot(w_tril[gi], vb[c * GMLP_CHUNK:(c + 1) * GMLP_CHUNK, gi * gw:(gi + 1) * gw],
                    preferred_element_type=F32)
            for gi in range(GMLP_GROUPS)
        ]
        rows.append(jnp.concatenate(parts, axis=1) + bs_ref[...])
    s = jnp.concatenate(rows, axis=0)
    z = (u * s).astype(BF16)
    out_ref[...] = x + jnp.dot(z, wout_ref[...], preferred_element_type=F32)


def _gmlp(x2d, g, w_in, ln_g, ln_b, w_sp, b_sp_full, w_out, tm=512):
    n = x2d.shape[0]
    full = lambda shape: pl.BlockSpec(shape, lambda i: (0,) * len(shape))
    return pl.pallas_call(
        _gmlp_kernel,
        grid=(n // tm,),
        in_specs=[
            pl.BlockSpec((tm, D_MODEL), lambda i: (i, 0)),
            full((1, D_MODEL)),
            full((D_MODEL, 2 * GMLP_WIDTH)),
            full((1, GMLP_WIDTH)),
            full((1, GMLP_WIDTH)),
            full((GMLP_GROUPS, GMLP_CHUNK, GMLP_CHUNK)),
            full((GMLP_CHUNK, GMLP_WIDTH)),
            full((GMLP_WIDTH, D_MODEL)),
        ],
        out_specs=pl.BlockSpec((tm, D_MODEL), lambda i: (i, 0)),
        out_shape=jax.ShapeDtypeStruct((n, D_MODEL), F32),
        compiler_params=pltpu.CompilerParams(
            dimension_semantics=("parallel",), vmem_limit_bytes=VMEM_LIMIT),
        name="gmlp",
    )(x2d, g, w_in, ln_g, ln_b, w_sp, b_sp_full, w_out)


def _moe_kernel(x_ref, g_ref, wr_ref, wg_ref, wu_ref, wd_ref, gf_ref, out_ref, h_ref, cw_ref, acc_ref):
    e = pl.program_id(1)
    c = pl.program_id(2)
    first = jnp.logical_and(e == 0, c == 0)
    last = jnp.logical_and(e == pl.num_programs(1) - 1, c == pl.num_programs(2) - 1)

    @pl.when(first)
    def _():
        hf = _rmsnorm(x_ref[...], g_ref[...])
        h_ref[...] = hf.astype(BF16)
        acc_ref[...] = jnp.zeros_like(acc_ref)
        logits = jnp.dot(hf, wr_ref[...], preferred_element_type=F32, precision=lax.Precision.HIGHEST)
        lane = lax.broadcasted_iota(jnp.int32, logits.shape, 1).astype(F32)
        neg_inf = -jnp.inf
        lg = jnp.where(lane < N_EXPERTS, logits, neg_inf)
        m1 = jnp.max(lg, axis=-1, keepdims=True)
        i1 = jnp.min(jnp.where(lg == m1, lane, float(LANES)), axis=-1, keepdims=True)
        lg2 = jnp.where(lane == i1, neg_inf, lg)
        m2 = jnp.max(lg2, axis=-1, keepdims=True)
        i2 = jnp.min(jnp.where(lg2 == m2, lane, float(LANES)), axis=-1, keepdims=True)
        t = jnp.exp(m2 - m1)
        inv = 1.0 / (1.0 + t)
        cw_ref[...] = jnp.where(lane == i1, inv, 0.0) + jnp.where(lane == i2, t * inv, 0.0)

    lane = lax.broadcasted_iota(jnp.int32, cw_ref.shape, 1)
    cw_e = jnp.sum(jnp.where(lane == e, cw_ref[...], 0.0), axis=-1, keepdims=True)
    h = h_ref[...]
    gate = jnp.dot(h, wg_ref[0], preferred_element_type=F32)
    up = jnp.dot(h, wu_ref[0], preferred_element_type=F32)
    a = (gate * jax.nn.sigmoid(gate) * up * cw_e).astype(BF16)
    acc_ref[...] += jnp.dot(a, wd_ref[0], preferred_element_type=F32)

    @pl.when(last)
    def _():
        out_ref[...] = _rmsnorm(x_ref[...] + acc_ref[...], gf_ref[...])


def _moe(x2d, g, w_router, w_gu, w_down, g_final, tm=1024, fc=512):
    n = x2d.shape[0]
    nc = FFN_DIM // fc
    return pl.pallas_call(
        _moe_kernel,
        grid=(n // tm, N_EXPERTS, nc),
        in_specs=[
            pl.BlockSpec((tm, D_MODEL), lambda i, e, c: (i, 0)),
            pl.BlockSpec((1, D_MODEL), lambda i, e, c: (0, 0)),
            pl.BlockSpec((D_MODEL, LANES), lambda i, e, c: (0, 0)),
            pl.BlockSpec((1, D_MODEL, fc), lambda i, e, c: (e, 0, c)),
            pl.BlockSpec((1, D_MODEL, fc), lambda i, e, c: (e, 0, c + nc)),
            pl.BlockSpec((1, fc, D_MODEL), lambda i, e, c: (e, c, 0)),
            pl.BlockSpec((1, D_MODEL), lambda i, e, c: (0, 0)),
        ],
        out_specs=pl.BlockSpec((tm, D_MODEL), lambda i, e, c: (i, 0)),
        out_shape=jax.ShapeDtypeStruct((n, D_MODEL), F32),
        scratch_shapes=[
            pltpu.VMEM((tm, D_MODEL), BF16),
            pltpu.VMEM((tm, LANES), F32),
            pltpu.VMEM((tm, D_MODEL), F32),
        ],
        compiler_params=pltpu.CompilerParams(
            dimension_semantics=("parallel", "arbitrary", "arbitrary"), vmem_limit_bytes=VMEM_LIMIT),
        name="moe",
    )(x2d, g, w_router, w_gu, w_gu, w_down, g_final)


def _rope_tables():
    def tables(dim):
        inv_freq = ROPE_THETA ** (-jnp.arange(0, dim, 2, dtype=F32) / dim)
        ang = jnp.arange(SEQ, dtype=F32)[:, None] * inv_freq[None, :]
        return jnp.cos(ang), jnp.sin(ang)

    c, s = tables(HEAD_DIM)
    cq = jnp.concatenate([c, c], axis=1)
    sq = jnp.concatenate([-s, s], axis=1)
    c, s = tables(IDX_ROPE_DIM)
    ones = jnp.ones((SEQ, IDX_DIM - IDX_ROPE_DIM), F32)
    zeros = jnp.zeros_like(ones)
    z16 = jnp.zeros_like(s)
    ci = jnp.concatenate([c, c, ones], axis=1)
    sia = jnp.concatenate([-s, z16, zeros], axis=1)
    sib = jnp.concatenate([z16, s, zeros], axis=1)
    rep = LANES // IDX_DIM
    return cq, sq, jnp.tile(ci, (1, rep)), jnp.tile(sia, (1, rep)), jnp.tile(sib, (1, rep))


def kernel(x, norm_mix, norm_ffn, dsa_w_in, dsa_idx_k_gain, dsa_w_out, ffn_w_gu, ffn_w_down, gmlp_w_in,
           gmlp_ln_gain, gmlp_ln_bias, gmlp_w_spatial, gmlp_b_spatial, gmlp_w_out, moe_w_router, moe_w_gu,
           moe_w_down, final_norm):
    x2d = x.reshape(N_TOK, D_MODEL)
    row = lambda v: v.reshape(1, -1)

    w_in = dsa_w_in[0]
    w_all = jnp.pad(w_in, ((0, 0), (0, DSA_COLS - w_in.shape[1]))).astype(BF16)
    wwi_t = w_in[:, COL_KI + IDX_DIM:COL_KI + IDX_DIM + IDX_HEADS].T.astype(BF16)
    kig = jnp.pad(dsa_idx_k_gain[0], (0, LANES - IDX_DIM)).reshape(1, LANES)
    q3, k, vt, kia, kib, qi, wt = _dsa_proj(x2d, row(norm_mix[0]), w_all, wwi_t, kig, _rope_tables())
    o3 = _dsa_attn(q3, k, vt, kia, kib, qi, wt)
    x2d = _resid_proj(x2d, o3, dsa_w_out[0].astype(BF16))
    x2d = _ffn(x2d, row(norm_ffn[0]), ffn_w_gu[0].astype(BF16), ffn_w_down[0].astype(BF16))

    gw = GMLP_WIDTH // GMLP_GROUPS
    b_sp_full = jnp.repeat(gmlp_b_spatial[0].T, gw, axis=1)
    x2d = _gmlp(x2d, row(norm_mix[1]), gmlp_w_in[0].astype(BF16), row(gmlp_ln_gain[0]), row(gmlp_ln_bias[0]),
                gmlp_w_spatial[0], b_sp_full, gmlp_w_out[0].astype(BF16))
    w_router = jnp.pad(moe_w_router[0], ((0, 0), (0, LANES - N_EXPERTS)))
    out = _moe(x2d, row(norm_ffn[1]), w_router, moe_w_gu[0].astype(BF16), moe_w_down[0].astype(BF16),
               row(final_norm))
    return out.reshape(BATCH, SEQ, D_MODEL)
```

```python
import functools

import numpy as np
import jax
import jax.numpy as jnp
from jax import lax
from jax.experimental import pallas as pl
from jax.experimental.pallas import tpu as pltpu

F32 = jnp.float32
BF16 = jnp.bfloat16

D_MODEL = 1024
BATCH = 8
SEQ = 2048
N_TOK = BATCH * SEQ
N_HEADS = 8
HEAD_DIM = 128
IDX_HEADS = 8
IDX_DIM = 64
IDX_ROPE_DIM = 32
INDEX_TOPK = 256
Q_BLOCK = 128
KEY_SUPER = 512
ROPE_THETA = 10000.0
GMLP_WIDTH = 1024
GMLP_GROUPS = 8
GMLP_CHUNK = 128
FFN_DIM = 3584
N_EXPERTS = 8
NORM_EPS = 1e-6

LANES = 128
SUBLANES = 8
VMEM_LIMIT = 56 * 1024 * 1024

DSA_COLS = 1920
COL_K = N_HEADS * HEAD_DIM
COL_V = COL_K + HEAD_DIM
COL_QI = COL_V + HEAD_DIM
COL_KI = COL_QI + IDX_HEADS * IDX_DIM

NT_DIMS = (((1,), (1,)), ((), ()))
MASK_BIAS = -1e30
SCORE_MASKED = -3.0e38


def _rmsnorm(x, g):
    ms = jnp.mean(x * x, axis=-1, keepdims=True)
    return x * lax.rsqrt(ms + NORM_EPS) * g


def _dsa_proj_kernel(x_ref, g_ref, w_ref, wwi_ref, kig_ref, cq_ref, sq_ref, ci_ref, sia_ref, sib_ref,
                     q_ref, k_ref, vt_ref, kia_ref, kib_ref, qi_ref, wt_ref):
    h = _rmsnorm(x_ref[...], g_ref[...]).astype(BF16)
    y = jnp.dot(h, w_ref[...], preferred_element_type=F32)
    cq = cq_ref[...]
    sq = sq_ref[...]
    ci = ci_ref[...]
    sia = sia_ref[...]
    sib = sib_ref[...]

    def rope_head(t):
        return t * cq + pltpu.roll(t, HEAD_DIM // 2, 1) * sq

    def rope_idx(t):
        half = IDX_ROPE_DIM // 2
        return t * ci + pltpu.roll(t, LANES - half, 1) * sia + pltpu.roll(t, half, 1) * sib

    q_scale = HEAD_DIM ** -0.5 * np.log2(np.e)
    for hh in range(N_HEADS):
        q_ref[hh] = (rope_head(y[:, hh * HEAD_DIM:(hh + 1) * HEAD_DIM]) * q_scale).astype(BF16)
    k_ref[...] = rope_head(y[:, COL_K:COL_V]).astype(BF16)
    vt_ref[0] = y[:, COL_V:COL_QI].T.astype(BF16)
    for p in range(IDX_HEADS * IDX_DIM // LANES):
        qi_ref[:, p * LANES:(p + 1) * LANES] = rope_idx(
            y[:, COL_QI + p * LANES:COL_QI + (p + 1) * LANES]).astype(BF16)

    last = y[:, COL_KI:COL_KI + LANES]
    lane = lax.broadcasted_iota(jnp.int32, last.shape, 1)
    kraw = jnp.where(lane < IDX_DIM, last, 0.0)
    kms = jnp.sum(kraw * kraw, axis=-1, keepdims=True) * (1.0 / IDX_DIM)
    kn = kraw * lax.rsqrt(kms + NORM_EPS) * kig_ref[...]
    kr = rope_idx(kn)
    kia_ref[...] = kr.astype(BF16)
    kib_ref[...] = pltpu.roll(kr, IDX_DIM, 1).astype(BF16)

    w_scale = IDX_HEADS ** -0.5 * IDX_DIM ** -0.5
    wt_ref[...] = lax.dot_general(wwi_ref[...], h, NT_DIMS, preferred_element_type=F32) * w_scale


def _dsa_proj(x2d, g, w_all, wwi_t, kig, tabs, tm=KEY_SUPER):
    n = x2d.shape[0]
    nblk_seq = SEQ // tm
    tab_spec = pl.BlockSpec((tm, LANES), lambda i: (i % nblk_seq, 0))
    full = lambda shape: pl.BlockSpec(shape, lambda i: (0,) * len(shape))
    return pl.pallas_call(
        _dsa_proj_kernel,
        grid=(n // tm,),
        in_specs=[
            pl.BlockSpec((tm, D_MODEL), lambda i: (i, 0)),
            full((1, D_MODEL)),
            full((D_MODEL, DSA_COLS)),
            full((IDX_HEADS, D_MODEL)),
            full((1, LANES)),
            tab_spec, tab_spec, tab_spec, tab_spec, tab_spec,
        ],
        out_specs=[
            pl.BlockSpec((N_HEADS, tm, HEAD_DIM), lambda i: (0, i, 0)),
            pl.BlockSpec((tm, HEAD_DIM), lambda i: (i, 0)),
            pl.BlockSpec((1, HEAD_DIM, tm), lambda i: (i, 0, 0)),
            pl.BlockSpec((tm, LANES), lambda i: (i, 0)),
            pl.BlockSpec((tm, LANES), lambda i: (i, 0)),
            pl.BlockSpec((tm, IDX_HEADS * IDX_DIM), lambda i: (i, 0)),
            pl.BlockSpec((IDX_HEADS, tm), lambda i: (0, i)),
        ],
        out_shape=[
            jax.ShapeDtypeStruct((N_HEADS, n, HEAD_DIM), BF16),
            jax.ShapeDtypeStruct((n, HEAD_DIM), BF16),
            jax.ShapeDtypeStruct((n // tm, HEAD_DIM, tm), BF16),
            jax.ShapeDtypeStruct((n, LANES), BF16),
            jax.ShapeDtypeStruct((n, LANES), BF16),
            jax.ShapeDtypeStruct((n, IDX_HEADS * IDX_DIM), BF16),
            jax.ShapeDtypeStruct((IDX_HEADS, n), F32),
        ],
        compiler_params=pltpu.CompilerParams(
            dimension_semantics=("parallel",), vmem_limit_bytes=VMEM_LIMIT),
        name="dsa_proj",
    )(x2d, g, w_all, wwi_t, kig, *tabs)


def _ordered_bits_to_float(u):
    key = u ^ jnp.int32(-2 ** 31)
    key = jnp.maximum(key, jnp.int32(-2139095041))
    bits = jnp.where(key >= 0, key, key ^ jnp.int32(0x7FFFFFFF))
    return lax.bitcast_convert_type(bits, F32)


def _dsa_attn_kernel(q_ref, k_ref, vt_ref, kia_ref, kib_ref, qi_ref, wt_ref, o_ref,
                     s_ref, l_ref, p_ref, acc_ref):
    j = pl.program_id(1)
    per_super = KEY_SUPER // Q_BLOCK
    n_super = j // per_super + 1
    hq = N_HEADS * Q_BLOCK

    qpos = j * Q_BLOCK + lax.broadcasted_iota(jnp.int32, (Q_BLOCK, Q_BLOCK), 1)
    krow = lax.broadcasted_iota(jnp.int32, (Q_BLOCK, Q_BLOCK), 0)

    def rows_of(sc, c4):
        return pl.ds(pl.multiple_of(sc * KEY_SUPER, KEY_SUPER) + c4 * Q_BLOCK, Q_BLOCK)

    def super_rows(sc):
        return pl.ds(pl.multiple_of(sc * KEY_SUPER, KEY_SUPER), KEY_SUPER)

    def causal(sc, c4):
        return (krow + (sc * KEY_SUPER + c4 * Q_BLOCK)) <= qpos

    def fold8(t):
        return t.reshape(t.shape[0] // SUBLANES, SUBLANES, t.shape[1])

    def score_body(sc, _):
        srows = super_rows(sc)
        ka = kia_ref[srows, :]
        kb = kib_ref[srows, :]
        n_pairs = IDX_HEADS // 2
        for p in range(n_pairs):
            qp = qi_ref[:, p * LANES:(p + 1) * LANES]
            ra = lax.dot_general(ka, qp, NT_DIMS, preferred_element_type=F32)
            rb = lax.dot_general(kb, qp, NT_DIMS, preferred_element_type=F32)
            term = (jnp.maximum(ra, 0.0) * wt_ref[2 * p:2 * p + 1, :]
                    + jnp.maximum(rb, 0.0) * wt_ref[2 * p + 1:2 * p + 2, :])
            if p == 0:
                s_ref[srows, :] = term
            elif p < n_pairs - 1:
                s_ref[srows, :] += term
            else:
                kpos = sc * KEY_SUPER + lax.broadcasted_iota(jnp.int32, (KEY_SUPER, Q_BLOCK), 0)
                qp_s = j * Q_BLOCK + lax.broadcasted_iota(jnp.int32, (KEY_SUPER, Q_BLOCK), 1)
                s_ref[srows, :] = jnp.where(kpos <= qp_s, s_ref[srows, :] + term, SCORE_MASKED)
        return 0

    lax.fori_loop(0, n_super, score_body, 0)

    def count_ge(cand):
        def body(sc, acc):
            for c4 in range(per_super):
                ind = jnp.where(s_ref[rows_of(sc, c4), :] >= cand, 1.0, 0.0)
                acc = acc + fold8(ind).sum(axis=0)
            return acc
        acc = lax.fori_loop(0, n_super, body, jnp.zeros((SUBLANES, LANES), F32))
        return jnp.sum(acc, axis=0, keepdims=True)

    def bit_body(i, carry):
        prefix, cbest = carry
        cu = prefix | lax.shift_left(jnp.int32(1), 31 - i)
        cnt = count_ge(_ordered_bits_to_float(cu))
        ok = cnt >= float(INDEX_TOPK)
        return jnp.where(ok, cu, prefix), jnp.where(ok, cnt, cbest)

    n_keys = (n_super * KEY_SUPER).astype(F32)
    prefix, cbest = lax.fori_loop(
        0, 32, bit_body,
        (jnp.zeros((1, LANES), jnp.int32), jnp.zeros((1, LANES), F32) + n_keys))
    tau = _ordered_bits_to_float(prefix)
    drop = cbest - float(INDEX_TOPK)

    upper = jnp.where(krow < lax.broadcasted_iota(jnp.int32, (Q_BLOCK, Q_BLOCK), 1), 1.0, 0.0).astype(BF16)

    def mask_body(i, later_ties):
        sc = n_super - 1 - i
        for c4 in reversed(range(per_super)):
            rows = rows_of(sc, c4)
            s = s_ref[rows, :]
            eq = jnp.where(s == tau, 1.0, 0.0)
            ties_after = jnp.dot(upper, eq.astype(BF16), preferred_element_type=F32) + later_ties
            later_ties = later_ties + jnp.sum(eq, axis=0, keepdims=True)
            tie_bias = jnp.where(ties_after >= drop, 0.0, MASK_BIAS)
            bias = jnp.where(s > tau, 0.0, jnp.where(s == tau, tie_bias, MASK_BIAS))
            s_ref[rows, :] = jnp.where(causal(sc, c4), bias, MASK_BIAS)
        return later_ties

    lax.fori_loop(0, n_super, mask_body, jnp.zeros((1, LANES), F32))

    pair_w = 2 * Q_BLOCK

    def logit_body(sc, m_acc):
        srows = super_rows(sc)
        kc = k_ref[srows, :]
        bias = s_ref[srows, :]
        bias2 = jnp.concatenate([bias, bias], axis=1)
        parts = []
        for pr in range(N_HEADS // 2):
            qpair = q_ref[2 * pr:2 * pr + 2].reshape(pair_w, HEAD_DIM)
            lg = lax.dot_general(kc, qpair, NT_DIMS, preferred_element_type=F32) + bias2
            l_ref[srows, pr * pair_w:(pr + 1) * pair_w] = lg
            parts.append(fold8(lg).max(axis=0))
        return jnp.maximum(m_acc, jnp.concatenate(parts, axis=1))

    m_acc = lax.fori_loop(0, n_super, logit_body, jnp.full((SUBLANES, hq), MASK_BIAS, F32))
    m = jnp.max(m_acc, axis=0, keepdims=True)

    acc_ref[...] = jnp.zeros_like(acc_ref)

    def pv_body(sc, d_acc):
        for c4 in range(per_super):
            rows = rows_of(sc, c4)
            parts = []
            for pr in range(N_HEADS // 2):
                cols = slice(pr * pair_w, (pr + 1) * pair_w)
                p = jnp.exp2(l_ref[rows, cols] - m[:, cols])
                p_ref[rows, cols] = p.astype(BF16)
                parts.append(fold8(p).sum(axis=0))
            d_acc = d_acc + jnp.concatenate(parts, axis=1)
        acc_ref[...] += jnp.dot(vt_ref[sc], p_ref[super_rows(sc), :], preferred_element_type=F32)
        return d_acc

    d_acc = lax.fori_loop(0, n_super, pv_body, jnp.zeros((SUBLANES, hq), F32))
    inv = 1.0 / jnp.sum(d_acc, axis=0, keepdims=True)
    for h in range(N_HEADS):
        cols = slice(h * Q_BLOCK, (h + 1) * Q_BLOCK)
        o_ref[h] = (acc_ref[:, cols] * inv[:, cols]).T.astype(BF16)


def _dsa_attn(q3, k, vt, kia, kib, qi, wt):
    nqb = SEQ // Q_BLOCK
    n_sup = SEQ // KEY_SUPER
    hq = N_HEADS * Q_BLOCK
    return pl.pallas_call(
        _dsa_attn_kernel,
        grid=(BATCH, nqb),
        in_specs=[
            pl.BlockSpec((N_HEADS, Q_BLOCK, HEAD_DIM), lambda b, j: (0, b * nqb + j, 0)),
            pl.BlockSpec((SEQ, HEAD_DIM), lambda b, j: (b, 0)),
            pl.BlockSpec((n_sup, HEAD_DIM, KEY_SUPER), lambda b, j: (b, 0, 0)),
            pl.BlockSpec((SEQ, LANES), lambda b, j: (b, 0)),
            pl.BlockSpec((SEQ, LANES), lambda b, j: (b, 0)),
            pl.BlockSpec((Q_BLOCK, IDX_HEADS * IDX_DIM), lambda b, j: (b * nqb + j, 0)),
            pl.BlockSpec((IDX_HEADS, Q_BLOCK), lambda b, j: (0, b * nqb + j)),
        ],
        out_specs=pl.BlockSpec((N_HEADS, Q_BLOCK, HEAD_DIM), lambda b, j: (0, b * nqb + j, 0)),
        out_shape=jax.ShapeDtypeStruct((N_HEADS, N_TOK, HEAD_DIM), BF16),
        scratch_shapes=[
            pltpu.VMEM((SEQ, Q_BLOCK), F32),
            pltpu.VMEM((SEQ, hq), F32),
            pltpu.VMEM((SEQ, hq), BF16),
            pltpu.VMEM((HEAD_DIM, hq), F32),
        ],
        compiler_params=pltpu.CompilerParams(
            dimension_semantics=("parallel", "arbitrary"), vmem_limit_bytes=VMEM_LIMIT),
        name="dsa_attn",
    )(q3, k, vt, kia, kib, qi, wt)


def _resid_proj_kernel(x_ref, o_ref, w_ref, out_ref):
    o = jnp.concatenate([o_ref[h] for h in range(N_HEADS)], axis=1)
    out_ref[...] = x_ref[...] + jnp.dot(o, w_ref[...], preferred_element_type=F32)


def _resid_proj(x2d, o3, w_out, tm=512):
    n = x2d.shape[0]
    return pl.pallas_call(
        _resid_proj_kernel,
        grid=(n // tm,),
        in_specs=[
            pl.BlockSpec((tm, D_MODEL), lambda i: (i, 0)),
            pl.BlockSpec((N_HEADS, tm, HEAD_DIM), lambda i: (0, i, 0)),
            pl.BlockSpec((D_MODEL, D_MODEL), lambda i: (0, 0)),
        ],
        out_specs=pl.BlockSpec((tm, D_MODEL), lambda i: (i, 0)),
        out_shape=jax.ShapeDtypeStruct((n, D_MODEL), F32),
        compiler_params=pltpu.CompilerParams(
            dimension_semantics=("parallel",), vmem_limit_bytes=VMEM_LIMIT),
        name="dsa_out_proj",
    )(x2d, o3, w_out)


def _ffn_kernel(x_ref, g_ref, wg_ref, wu_ref, wd_ref, out_ref, h_ref, acc_ref):
    c = pl.program_id(1)

    @pl.when(c == 0)
    def _():
        h_ref[...] = _rmsnorm(x_ref[...], g_ref[...]).astype(BF16)
        acc_ref[...] = jnp.zeros_like(acc_ref)

    h = h_ref[...]
    gate = jnp.dot(h, wg_ref[...], preferred_element_type=F32)
    up = jnp.dot(h, wu_ref[...], preferred_element_type=F32)
    a = (gate * jax.nn.sigmoid(gate) * up).astype(BF16)
    acc_ref[...] += jnp.dot(a, wd_ref[...], preferred_element_type=F32)

    @pl.when(c == pl.num_programs(1) - 1)
    def _():
        out_ref[...] = x_ref[...] + acc_ref[...]


def _ffn(x2d, g, w_gu, w_down, tm=1024, fc=512):
    n = x2d.shape[0]
    nc = FFN_DIM // fc
    return pl.pallas_call(
        _ffn_kernel,
        grid=(n // tm, nc),
        in_specs=[
            pl.BlockSpec((tm, D_MODEL), lambda i, c: (i, 0)),
            pl.BlockSpec((1, D_MODEL), lambda i, c: (0, 0)),
            pl.BlockSpec((D_MODEL, fc), lambda i, c: (0, c)),
            pl.BlockSpec((D_MODEL, fc), lambda i, c: (0, c + nc)),
            pl.BlockSpec((fc, D_MODEL), lambda i, c: (c, 0)),
        ],
        out_specs=pl.BlockSpec((tm, D_MODEL), lambda i, c: (i, 0)),
        out_shape=jax.ShapeDtypeStruct((n, D_MODEL), F32),
        scratch_shapes=[pltpu.VMEM((tm, D_MODEL), BF16), pltpu.VMEM((tm, D_MODEL), F32)],
        compiler_params=pltpu.CompilerParams(
            dimension_semantics=("parallel", "arbitrary"), vmem_limit_bytes=VMEM_LIMIT),
        name="swiglu_ffn",
    )(x2d, g, w_gu, w_gu, w_down)


def _gelu_tanh(x):
    return 0.5 * x * (1.0 + jnp.tanh(np.sqrt(2.0 / np.pi) * (x + 0.044715 * (x * x * x))))


def _gmlp_kernel(x_ref, g_ref, win_ref, lng_ref, lnb_ref, ws_ref, bs_ref, wout_ref, out_ref):
    x = x_ref[...]
    tm = x.shape[0]
    h = _rmsnorm(x, g_ref[...]).astype(BF16)
    y = _gelu_tanh(jnp.dot(h, win_ref[...], preferred_element_type=F32))
    u = y[:, :GMLP_WIDTH]
    v = y[:, GMLP_WIDTH:]
    mu = jnp.mean(v, axis=-1, keepdims=True)
    vc = v - mu
    var = jnp.mean(vc * vc, axis=-1, keepdims=True)
    vb = (vc * lax.rsqrt(var + NORM_EPS) * lng_ref[...] + lnb_ref[...]).astype(BF16)

    gw = GMLP_WIDTH // GMLP_GROUPS
    row = lax.broadcasted_iota(jnp.int32, (GMLP_CHUNK, GMLP_CHUNK), 0)
    col = lax.broadcasted_iota(jnp.int32, (GMLP_CHUNK, GMLP_CHUNK), 1)
    w_tril = [jnp.where(row >= col, ws_ref[gi], 0.0).astype(BF16) for gi in range(GMLP_GROUPS)]
    rows = []
    for c in range(tm // GMLP_CHUNK):
        parts = [
            jnp.dot(w_tril[gi], vb[c * GMLP_CHUNK:(c + 1) * GMLP_CHUNK, gi * gw:(gi + 1) * gw],
                    preferred_element_type=F32)
            for gi in range(GMLP_GROUPS)
        ]
        rows.append(jnp.concatenate(parts, axis=1) + bs_ref[...])
    s = jnp.concatenate(rows, axis=0)
    z = (u * s).astype(BF16)
    out_ref[...] = x + jnp.dot(z, wout_ref[...], preferred_element_type=F32)


def _gmlp(x2d, g, w_in, ln_g, ln_b, w_sp, b_sp_full, w_out, tm=512):
    n = x2d.shape[0]
    full = lambda shape: pl.BlockSpec(shape, lambda i: (0,) * len(shape))
    return pl.pallas_call(
        _gmlp_kernel,
        grid=(n // tm,),
        in_specs=[
            pl.BlockSpec((tm, D_MODEL), lambda i: (i, 0)),
            full((1, D_MODEL)),
            full((D_MODEL, 2 * GMLP_WIDTH)),
            full((1, GMLP_WIDTH)),
            full((1, GMLP_WIDTH)),
            full((GMLP_GROUPS, GMLP_CHUNK, GMLP_CHUNK)),
            full((GMLP_CHUNK, GMLP_WIDTH)),
            full((GMLP_WIDTH, D_MODEL)),
        ],
        out_specs=pl.BlockSpec((tm, D_MODEL), lambda i: (i, 0)),
        out_shape=jax.ShapeDtypeStruct((n, D_MODEL), F32),
        compiler_params=pltpu.CompilerParams(
            dimension_semantics=("parallel",), vmem_limit_bytes=VMEM_LIMIT),
        name="gmlp",
    )(x2d, g, w_in, ln_g, ln_b, w_sp, b_sp_full, w_out)


MOE_BLOCK = 512
MOE_GRAN = 16
MOE_TILE = 512
MOE_LOCAL_ROWS = 2 * MOE_BLOCK + N_EXPERTS * MOE_GRAN
N_MOE_BLOCKS = N_TOK // MOE_BLOCK
MOE_MAX_TILES = (2 * N_TOK + N_MOE_BLOCKS * N_EXPERTS * (MOE_GRAN - 1)) // MOE_TILE + N_EXPERTS
MOE_ROWS = MOE_MAX_TILES * MOE_TILE
ROUTE_D1, ROUTE_D2, ROUTE_G1, ROUTE_G2 = 0, 1, 2, 3
TAB_LOFF, TAB_PC, TAB_GOFF = 0, N_EXPERTS, 2 * N_EXPERTS


def _lane_scalar(vec, lane_idx, e):
    return jnp.sum(jnp.where(lane_idx == e, vec, 0.0)).astype(jnp.int32)


def _moe_route_kernel(x_ref, g_ref, wr_ref, route_ref, meta_ref):
    b = pl.program_id(0)
    t_blk = MOE_BLOCK
    hf = _rmsnorm(x_ref[...], g_ref[...])
    logits = jnp.dot(hf, wr_ref[...], preferred_element_type=F32, precision=lax.Precision.HIGHEST)
    lane = lax.broadcasted_iota(jnp.int32, logits.shape, 1).astype(F32)
    neg_inf = -jnp.inf
    lg = jnp.where(lane < N_EXPERTS, logits, neg_inf)
    m1 = jnp.max(lg, axis=-1, keepdims=True)
    i1 = jnp.min(jnp.where(lg == m1, lane, float(LANES)), axis=-1, keepdims=True)
    lg2 = jnp.where(lane == i1, neg_inf, lg)
    m2 = jnp.max(lg2, axis=-1, keepdims=True)
    i2 = jnp.min(jnp.where(lg2 == m2, lane, float(LANES)), axis=-1, keepdims=True)
    t = jnp.exp(m2 - m1)
    g1 = 1.0 / (1.0 + t)
    g2 = t * g1

    memb = jnp.where(jnp.logical_or(lane == i1, lane == i2), 1.0, 0.0)
    tr = lax.broadcasted_iota(jnp.int32, (t_blk, t_blk), 0)
    tc = lax.broadcasted_iota(jnp.int32, (t_blk, t_blk), 1)
    earlier = jnp.where(tc < tr, 1.0, 0.0).astype(BF16)
    rank = jnp.dot(earlier, memb.astype(BF16), preferred_element_type=F32)
    cnt = jnp.sum(memb, axis=0, keepdims=True)
    pc = jnp.floor((cnt + float(MOE_GRAN - 1)) * (1.0 / MOE_GRAN)) * float(MOE_GRAN)
    er = lax.broadcasted_iota(jnp.int32, (LANES, LANES), 0)
    ec = lax.broadcasted_iota(jnp.int32, (LANES, LANES), 1)
    before = jnp.where(er < ec, 1.0, 0.0)
    loff = jnp.dot(jnp.broadcast_to(pc, (SUBLANES, LANES)), before, preferred_element_type=F32,
                   precision=lax.Precision.HIGHEST)[0:1, :]
    dest = loff + rank
    d1 = jnp.sum(jnp.where(lane == i1, dest, 0.0), axis=-1, keepdims=True)
    d2 = jnp.sum(jnp.where(lane == i2, dest, 0.0), axis=-1, keepdims=True)
    route = jnp.where(lane == ROUTE_D1, d1,
                      jnp.where(lane == ROUTE_D2, d2,
                                jnp.where(lane == ROUTE_G1, g1, jnp.where(lane == ROUTE_G2, g2, 0.0))))
    route_ref[...] = route
    lane_i = lax.broadcasted_iota(jnp.int32, (1, LANES), 1)
    for e in range(N_EXPERTS):
        meta_ref[b, TAB_LOFF + e] = _lane_scalar(loff, lane_i, e)
        meta_ref[b, TAB_PC + e] = _lane_scalar(pc, lane_i, e)


def _moe_route(x2d, g, w_router):
    return pl.pallas_call(
        _moe_route_kernel,
        grid=(N_MOE_BLOCKS,),
        in_specs=[
            pl.BlockSpec((MOE_BLOCK, D_MODEL), lambda b: (b, 0)),
            pl.BlockSpec((1, D_MODEL), lambda b: (0, 0)),
            pl.BlockSpec((D_MODEL, LANES), lambda b: (0, 0)),
        ],
        out_specs=[
            pl.BlockSpec((MOE_BLOCK, LANES), lambda b: (b, 0)),
            pl.BlockSpec(memory_space=pltpu.SMEM),
        ],
        out_shape=[
            jax.ShapeDtypeStruct((N_TOK, LANES), F32),
            jax.ShapeDtypeStruct((N_MOE_BLOCKS, 2 * N_EXPERTS), jnp.int32),
        ],
        compiler_params=pltpu.CompilerParams(
            dimension_semantics=("arbitrary",), vmem_limit_bytes=VMEM_LIMIT),
        name="moe_route",
    )(x2d, g, w_router)


def _moe_plan(meta):
    pc = meta[:, TAB_PC:TAB_PC + N_EXPERTS]
    totals = jnp.sum(pc, axis=0)
    per_expert = (totals + (MOE_TILE - 1)) // MOE_TILE
    ends = jnp.cumsum(per_expert)
    seg_start = (ends - per_expert) * MOE_TILE
    goff = seg_start[None, :] + jnp.cumsum(pc, axis=0) - pc
    block_tab = jnp.concatenate([meta, goff], axis=1).astype(jnp.int32)
    n_tiles = ends[-1]
    t = jnp.minimum(jnp.arange(MOE_MAX_TILES, dtype=jnp.int32), n_tiles - 1)
    expert = jnp.sum((t[:, None] >= ends[None, :]).astype(jnp.int32), axis=1)
    tile_tab = jnp.stack([expert, jnp.full_like(t, n_tiles)]).astype(jnp.int32)
    fill_tab = jnp.stack([seg_start + totals, ends * MOE_TILE]).astype(jnp.int32)
    return block_tab, tile_tab, fill_tab


def _moe_dispatch_kernel(tab_ref, tile_ref, fill_ref, x_ref, g_ref, route_ref, hs_hbm, hloc_ref, zero_ref, sem):
    b = pl.program_id(0)
    hb = _rmsnorm(x_ref[...], g_ref[...]).astype(BF16)

    route_t = route_ref[...].T
    r_iota = lax.broadcasted_iota(jnp.int32, (MOE_LOCAL_ROWS, MOE_BLOCK), 0).astype(F32)
    hit = jnp.logical_or(r_iota == route_t[ROUTE_D1:ROUTE_D1 + 1, :], r_iota == route_t[ROUTE_D2:ROUTE_D2 + 1, :])
    perm = jnp.where(hit, 1.0, 0.0).astype(BF16)
    hloc_ref[...] = jnp.dot(perm, hb, preferred_element_type=F32).astype(BF16)

    def granule(ref, row):
        return ref.at[pl.ds(pl.multiple_of(row, MOE_GRAN), MOE_GRAN)]

    n_started = 0
    for e in range(N_EXPERTS):
        lo = tab_ref[b, TAB_LOFF + e]
        goff = tab_ref[b, TAB_GOFF + e]
        n_gran = tab_ref[b, TAB_PC + e] // MOE_GRAN

        def start(gi, _, lo=lo, goff=goff):
            pltpu.make_async_copy(granule(hloc_ref, lo + gi * MOE_GRAN),
                                  granule(hs_hbm, goff + gi * MOE_GRAN), sem).start()
            return 0

        lax.fori_loop(0, n_gran, start, 0)
        n_started = n_started + n_gran

    def wait_one(gi, _):
        pltpu.make_async_copy(granule(hloc_ref, 0), granule(hs_hbm, 0), sem).wait()
        return 0

    lax.fori_loop(0, n_started, wait_one, 0)

    @pl.when(b == pl.num_programs(0) - 1)
    def _():
        zero_ref[...] = jnp.zeros_like(zero_ref)
        n_fill = 0
        for e in range(N_EXPERTS):
            first = fill_ref[0, e]
            n_e = (fill_ref[1, e] - first) // MOE_GRAN

            def fill(gi, _, first=first):
                pltpu.make_async_copy(granule(zero_ref, 0), granule(hs_hbm, first + gi * MOE_GRAN), sem).start()
                return 0

            lax.fori_loop(0, n_e, fill, 0)
            n_fill = n_fill + n_e

        def wait_fill(gi, _):
            pltpu.make_async_copy(granule(zero_ref, 0), granule(hs_hbm, 0), sem).wait()
            return 0

        lax.fori_loop(0, n_fill, wait_fill, 0)

        def tile_at(ti):
            return hs_hbm.at[pl.ds(pl.multiple_of(ti * MOE_TILE, MOE_TILE), MOE_TILE)]

        n_tiles = tile_ref[1, 0]

        def fill_tile(ti, _):
            pltpu.make_async_copy(zero_ref, tile_at(ti), sem).start()
            return 0

        def wait_tile(ti, _):
            pltpu.make_async_copy(zero_ref, tile_at(0), sem).wait()
            return 0

        lax.fori_loop(n_tiles, MOE_MAX_TILES, fill_tile, 0)
        lax.fori_loop(n_tiles, MOE_MAX_TILES, wait_tile, 0)


def _moe_dispatch(block_tab, tile_tab, fill_tab, x2d, g, route):
    grid_spec = pltpu.PrefetchScalarGridSpec(
        num_scalar_prefetch=3,
        grid=(N_MOE_BLOCKS,),
        in_specs=[
            pl.BlockSpec((MOE_BLOCK, D_MODEL), lambda b, *_: (b, 0)),
            pl.BlockSpec((1, D_MODEL), lambda b, *_: (0, 0)),
            pl.BlockSpec((MOE_BLOCK, LANES), lambda b, *_: (b, 0)),
        ],
        out_specs=pl.BlockSpec(memory_space=pl.ANY),
        scratch_shapes=[
            pltpu.VMEM((MOE_LOCAL_ROWS, D_MODEL), BF16),
            pltpu.VMEM((MOE_TILE, D_MODEL), BF16),
            pltpu.SemaphoreType.DMA(()),
        ],
    )
    return pl.pallas_call(
        _moe_dispatch_kernel,
        grid_spec=grid_spec,
        out_shape=jax.ShapeDtypeStruct((MOE_ROWS, D_MODEL), BF16),
        compiler_params=pltpu.CompilerParams(
            dimension_semantics=("arbitrary",), vmem_limit_bytes=VMEM_LIMIT),
        name="moe_dispatch",
    )(block_tab, tile_tab, fill_tab, x2d, g, route)


def _moe_expert_kernel(tile_ref, hs_ref, wg_ref, wu_ref, wd_ref, y_ref, acc_ref):
    t = pl.program_id(0)
    c = pl.program_id(1)
    active = t < tile_ref[1, 0]

    @pl.when(jnp.logical_and(jnp.logical_not(active), c == 0))
    def _():
        y_ref[...] = jnp.zeros_like(y_ref)

    @pl.when(active)
    def _():
        @pl.when(c == 0)
        def _():
            acc_ref[...] = jnp.zeros_like(acc_ref)

        h = hs_ref[...]
        gate = jnp.dot(h, wg_ref[0], preferred_element_type=F32)
        up = jnp.dot(h, wu_ref[0], preferred_element_type=F32)
        a = (gate * jax.nn.sigmoid(gate) * up).astype(BF16)
        acc_ref[...] += jnp.dot(a, wd_ref[0], preferred_element_type=F32)

        @pl.when(c == pl.num_programs(1) - 1)
        def _():
            y_ref[...] = acc_ref[...].astype(BF16)


def _moe_experts(tile_tab, hs, w_gu, w_down, fc=512):
    nc = FFN_DIM // fc

    def tile_of(t, tr):
        return jnp.minimum(t, tr[1, 0] - 1)

    def chunk_of(t, c, tr):
        return jnp.where(t < tr[1, 0], c, nc - 1)

    grid_spec = pltpu.PrefetchScalarGridSpec(
        num_scalar_prefetch=1,
        grid=(MOE_MAX_TILES, nc),
        in_specs=[
            pl.BlockSpec((MOE_TILE, D_MODEL), lambda t, c, tr: (tile_of(t, tr), 0)),
            pl.BlockSpec((1, D_MODEL, fc), lambda t, c, tr: (tr[0, t], 0, chunk_of(t, c, tr))),
            pl.BlockSpec((1, D_MODEL, fc), lambda t, c, tr: (tr[0, t], 0, chunk_of(t, c, tr) + nc)),
            pl.BlockSpec((1, fc, D_MODEL), lambda t, c, tr: (tr[0, t], chunk_of(t, c, tr), 0)),
        ],
        out_specs=pl.BlockSpec((MOE_TILE, D_MODEL), lambda t, c, tr: (t, 0)),
        scratch_shapes=[pltpu.VMEM((MOE_TILE, D_MODEL), F32)],
    )
    return pl.pallas_call(
        _moe_expert_kernel,
        grid_spec=grid_spec,
        out_shape=jax.ShapeDtypeStruct((MOE_ROWS, D_MODEL), BF16),
        compiler_params=pltpu.CompilerParams(
            dimension_semantics=("arbitrary", "arbitrary"), vmem_limit_bytes=VMEM_LIMIT),
        name="moe_experts",
    )(tile_tab, hs, w_gu, w_gu, w_down)


def _moe_combine_kernel(meta_ref, x_ref, route_ref, y_hbm, gf_ref, out_ref, yloc_ref, sem):
    b = pl.program_id(0)

    @pl.when(b == 0)
    def _():
        yloc_ref[...] = jnp.zeros_like(yloc_ref)

    def granule(ref, row):
        return ref.at[pl.ds(pl.multiple_of(row, MOE_GRAN), MOE_GRAN)]

    n_started = 0
    for e in range(N_EXPERTS):
        lo = meta_ref[b, TAB_LOFF + e]
        goff = meta_ref[b, TAB_GOFF + e]
        n_gran = meta_ref[b, TAB_PC + e] // MOE_GRAN

        def start(gi, _, lo=lo, goff=goff):
            pltpu.make_async_copy(granule(y_hbm, goff + gi * MOE_GRAN),
                                  granule(yloc_ref, lo + gi * MOE_GRAN), sem).start()
            return 0

        lax.fori_loop(0, n_gran, start, 0)
        n_started = n_started + n_gran

    def wait_one(gi, _):
        pltpu.make_async_copy(granule(y_hbm, 0), granule(yloc_ref, 0), sem).wait()
        return 0

    lax.fori_loop(0, n_started, wait_one, 0)

    route = route_ref[...]
    col = lax.broadcasted_iota(jnp.int32, (MOE_BLOCK, MOE_LOCAL_ROWS), 1).astype(F32)
    pick1 = jnp.where(col == route[:, ROUTE_D1:ROUTE_D1 + 1], 1.0, 0.0).astype(BF16)
    pick2 = jnp.where(col == route[:, ROUTE_D2:ROUTE_D2 + 1], 1.0, 0.0).astype(BF16)
    yl = yloc_ref[...]
    moe = (route[:, ROUTE_G1:ROUTE_G1 + 1] * jnp.dot(pick1, yl, preferred_element_type=F32)
           + route[:, ROUTE_G2:ROUTE_G2 + 1] * jnp.dot(pick2, yl, preferred_element_type=F32))
    out_ref[...] = _rmsnorm(x_ref[...] + moe, gf_ref[...])


def _moe_combine(meta, x2d, route, y, g_final):
    grid_spec = pltpu.PrefetchScalarGridSpec(
        num_scalar_prefetch=1,
        grid=(N_MOE_BLOCKS,),
        in_specs=[
            pl.BlockSpec((MOE_BLOCK, D_MODEL), lambda b, m: (b, 0)),
            pl.BlockSpec((MOE_BLOCK, LANES), lambda b, m: (b, 0)),
            pl.BlockSpec(memory_space=pl.ANY),
            pl.BlockSpec((1, D_MODEL), lambda b, m: (0, 0)),
        ],
        out_specs=pl.BlockSpec((MOE_BLOCK, D_MODEL), lambda b, m: (b, 0)),
        scratch_shapes=[
            pltpu.VMEM((MOE_LOCAL_ROWS, D_MODEL), BF16),
            pltpu.SemaphoreType.DMA(()),
        ],
    )
    return pl.pallas_call(
        _moe_combine_kernel,
        grid_spec=grid_spec,
        out_shape=jax.ShapeDtypeStruct((N_TOK, D_MODEL), F32),
        compiler_params=pltpu.CompilerParams(
            dimension_semantics=("arbitrary",), vmem_limit_bytes=VMEM_LIMIT),
        name="moe_combine",
    )(meta, x2d, route, y, g_final)


def _moe(x2d, g, w_router, w_gu, w_down, g_final):
    route, meta = _moe_route(x2d, g, w_router)
    block_tab, tile_tab, fill_tab = _moe_plan(meta)
    hs = _moe_dispatch(block_tab, tile_tab, fill_tab, x2d, g, route)
    y = _moe_experts(tile_tab, hs, w_gu, w_down)
    return _moe_combine(block_tab, x2d, route, y, g_final)


def _rope_tables():
    def tables(dim):
        inv_freq = ROPE_THETA ** (-jnp.arange(0, dim, 2, dtype=F32) / dim)
        ang = jnp.arange(SEQ, dtype=F32)[:, None] * inv_freq[None, :]
        return jnp.cos(ang), jnp.sin(ang)

    c, s = tables(HEAD_DIM)
    cq = jnp.concatenate([c, c], axis=1)
    sq = jnp.concatenate([-s, s], axis=1)
    c, s = tables(IDX_ROPE_DIM)
    ones = jnp.ones((SEQ, IDX_DIM - IDX_ROPE_DIM), F32)
    zeros = jnp.zeros_like(ones)
    z16 = jnp.zeros_like(s)
    ci = jnp.concatenate([c, c, ones], axis=1)
    sia = jnp.concatenate([-s, z16, zeros], axis=1)
    sib = jnp.concatenate([z16, s, zeros], axis=1)
    rep = LANES // IDX_DIM
    return cq, sq, jnp.tile(ci, (1, rep)), jnp.tile(sia, (1, rep)), jnp.tile(sib, (1, rep))


def kernel(x, norm_mix, norm_ffn, dsa_w_in, dsa_idx_k_gain, dsa_w_out, ffn_w_gu, ffn_w_down, gmlp_w_in,
           gmlp_ln_gain, gmlp_ln_bias, gmlp_w_spatial, gmlp_b_spatial, gmlp_w_out, moe_w_router, moe_w_gu,
           moe_w_down, final_norm):
    x2d = x.reshape(N_TOK, D_MODEL)
    row = lambda v: v.reshape(1, -1)

    w_in = dsa_w_in[0]
    w_all = jnp.pad(w_in, ((0, 0), (0, DSA_COLS - w_in.shape[1]))).astype(BF16)
    wwi_t = w_in[:, COL_KI + IDX_DIM:COL_KI + IDX_DIM + IDX_HEADS].T.astype(BF16)
    kig = jnp.pad(dsa_idx_k_gain[0], (0, LANES - IDX_DIM)).reshape(1, LANES)
    q3, k, vt, kia, kib, qi, wt = _dsa_proj(x2d, row(norm_mix[0]), w_all, wwi_t, kig, _rope_tables())
    o3 = _dsa_attn(q3, k, vt, kia, kib, qi, wt)
    x2d = _resid_proj(x2d, o3, dsa_w_out[0].astype(BF16))
    x2d = _ffn(x2d, row(norm_ffn[0]), ffn_w_gu[0].astype(BF16), ffn_w_down[0].astype(BF16))

    gw = GMLP_WIDTH // GMLP_GROUPS
    b_sp_full = jnp.repeat(gmlp_b_spatial[0].T, gw, axis=1)
    x2d = _gmlp(x2d, row(norm_mix[1]), gmlp_w_in[0].astype(BF16), row(gmlp_ln_gain[0]), row(gmlp_ln_bias[0]),
                gmlp_w_spatial[0], b_sp_full, gmlp_w_out[0].astype(BF16))
    w_router = jnp.pad(moe_w_router[0], ((0, 0), (0, LANES - N_EXPERTS)))
    out = _moe(x2d, row(norm_ffn[1]), w_router, moe_w_gu[0].astype(BF16), moe_w_down[0].astype(BF16),
               row(final_norm))
    return out.reshape(BATCH, SEQ, D_MODEL)
```

```python
import functools

import numpy as np
import jax
import jax.numpy as jnp
from jax import lax
from jax.experimental import pallas as pl
from jax.experimental.pallas import tpu as pltpu

F32 = jnp.float32
BF16 = jnp.bfloat16

D_MODEL = 1024
BATCH = 8
SEQ = 2048
N_TOK = BATCH * SEQ
N_HEADS = 8
HEAD_DIM = 128
IDX_HEADS = 8
IDX_DIM = 64
IDX_ROPE_DIM = 32
INDEX_TOPK = 256
Q_BLOCK = 128
KEY_SUPER = 512
ROPE_THETA = 10000.0
GMLP_WIDTH = 1024
GMLP_GROUPS = 8
GMLP_CHUNK = 128
FFN_DIM = 3584
N_EXPERTS = 8
NORM_EPS = 1e-6

LANES = 128
SUBLANES = 8
VMEM_LIMIT = 56 * 1024 * 1024

DSA_COLS = 1920
COL_K = N_HEADS * HEAD_DIM
COL_V = COL_K + HEAD_DIM
COL_QI = COL_V + HEAD_DIM
COL_KI = COL_QI + IDX_HEADS * IDX_DIM

NT_DIMS = (((1,), (1,)), ((), ()))
MASK_BIAS = -1e30
SCORE_MASKED = -3.0e38


def _rmsnorm(x, g):
    ms = jnp.mean(x * x, axis=-1, keepdims=True)
    return x * lax.rsqrt(ms + NORM_EPS) * g


def _cast_once(src_ref, dst_ref):
    @pl.when(pl.program_id(0) == 0)
    def _():
        dst_ref[...] = src_ref[...].astype(BF16)


def _resident(shape):
    return pl.BlockSpec(shape, lambda i: (0,) * len(shape), pipeline_mode=pl.Buffered(1))


def _dsa_proj_kernel(x_ref, g_ref, w_ref, wwi_ref, kig_ref, cq_ref, sq_ref, ci_ref, sia_ref, sib_ref,
                     q_ref, k_ref, vt_ref, kia_ref, kib_ref, qi_ref, wt_ref, wbf_ref):
    _cast_once(w_ref, wbf_ref)
    h = _rmsnorm(x_ref[...], g_ref[...]).astype(BF16)
    y = jnp.dot(h, wbf_ref[...], preferred_element_type=F32)
    cq = cq_ref[...]
    sq = sq_ref[...]
    ci = ci_ref[...]
    sia = sia_ref[...]
    sib = sib_ref[...]

    def rope_head(t):
        return t * cq + pltpu.roll(t, HEAD_DIM // 2, 1) * sq

    def rope_idx(t):
        half = IDX_ROPE_DIM // 2
        return t * ci + pltpu.roll(t, LANES - half, 1) * sia + pltpu.roll(t, half, 1) * sib

    q_scale = HEAD_DIM ** -0.5 * np.log2(np.e)
    for hh in range(N_HEADS):
        q_ref[hh] = (rope_head(y[:, hh * HEAD_DIM:(hh + 1) * HEAD_DIM]) * q_scale).astype(BF16)
    k_ref[...] = rope_head(y[:, COL_K:COL_V]).astype(BF16)
    vt_ref[0] = y[:, COL_V:COL_QI].T.astype(BF16)
    for p in range(IDX_HEADS * IDX_DIM // LANES):
        qi_ref[:, p * LANES:(p + 1) * LANES] = rope_idx(
            y[:, COL_QI + p * LANES:COL_QI + (p + 1) * LANES]).astype(BF16)

    last = y[:, COL_KI:COL_KI + LANES]
    lane = lax.broadcasted_iota(jnp.int32, last.shape, 1)
    kraw = jnp.where(lane < IDX_DIM, last, 0.0)
    kms = jnp.sum(kraw * kraw, axis=-1, keepdims=True) * (1.0 / IDX_DIM)
    kn = kraw * lax.rsqrt(kms + NORM_EPS) * kig_ref[...]
    kr = rope_idx(kn)
    kia_ref[...] = kr.astype(BF16)
    kib_ref[...] = pltpu.roll(kr, IDX_DIM, 1).astype(BF16)

    w_scale = IDX_HEADS ** -0.5 * IDX_DIM ** -0.5
    wt_ref[...] = lax.dot_general(wwi_ref[...].astype(BF16), h, NT_DIMS, preferred_element_type=F32) * w_scale


def _dsa_proj(x2d, g, w_all, wwi_t, kig, tabs, tm=KEY_SUPER):
    n = x2d.shape[0]
    nblk_seq = SEQ // tm
    tab_spec = pl.BlockSpec((tm, LANES), lambda i: (i % nblk_seq, 0))
    full = lambda shape: pl.BlockSpec(shape, lambda i: (0,) * len(shape))
    return pl.pallas_call(
        _dsa_proj_kernel,
        grid=(n // tm,),
        in_specs=[
            pl.BlockSpec((tm, D_MODEL), lambda i: (i, 0)),
            full((1, D_MODEL)),
            _resident((D_MODEL, DSA_COLS)),
            full((IDX_HEADS, D_MODEL)),
            full((1, LANES)),
            tab_spec, tab_spec, tab_spec, tab_spec, tab_spec,
        ],
        out_specs=[
            pl.BlockSpec((N_HEADS, tm, HEAD_DIM), lambda i: (0, i, 0)),
            pl.BlockSpec((tm, HEAD_DIM), lambda i: (i, 0)),
            pl.BlockSpec((1, HEAD_DIM, tm), lambda i: (i, 0, 0)),
            pl.BlockSpec((tm, LANES), lambda i: (i, 0)),
            pl.BlockSpec((tm, LANES), lambda i: (i, 0)),
            pl.BlockSpec((tm, IDX_HEADS * IDX_DIM), lambda i: (i, 0)),
            pl.BlockSpec((IDX_HEADS, tm), lambda i: (0, i)),
        ],
        out_shape=[
            jax.ShapeDtypeStruct((N_HEADS, n, HEAD_DIM), BF16),
            jax.ShapeDtypeStruct((n, HEAD_DIM), BF16),
            jax.ShapeDtypeStruct((n // tm, HEAD_DIM, tm), BF16),
            jax.ShapeDtypeStruct((n, LANES), BF16),
            jax.ShapeDtypeStruct((n, LANES), BF16),
            jax.ShapeDtypeStruct((n, IDX_HEADS * IDX_DIM), BF16),
            jax.ShapeDtypeStruct((IDX_HEADS, n), F32),
        ],
        scratch_shapes=[pltpu.VMEM((D_MODEL, DSA_COLS), BF16)],
        compiler_params=pltpu.CompilerParams(
            dimension_semantics=("arbitrary",), vmem_limit_bytes=VMEM_LIMIT),
        name="dsa_proj",
    )(x2d, g, w_all, wwi_t, kig, *tabs)


def _ordered_bits_to_float(u):
    key = u ^ jnp.int32(-2 ** 31)
    key = jnp.maximum(key, jnp.int32(-2139095041))
    bits = jnp.where(key >= 0, key, key ^ jnp.int32(0x7FFFFFFF))
    return lax.bitcast_convert_type(bits, F32)


def _dsa_attn_kernel(q_ref, k_ref, vt_ref, kia_ref, kib_ref, qi_ref, wt_ref, o_ref,
                     s_ref, l_ref, p_ref, acc_ref):
    j = pl.program_id(1)
    per_super = KEY_SUPER // Q_BLOCK
    n_super = j // per_super + 1
    hq = N_HEADS * Q_BLOCK

    qpos = j * Q_BLOCK + lax.broadcasted_iota(jnp.int32, (Q_BLOCK, Q_BLOCK), 1)
    krow = lax.broadcasted_iota(jnp.int32, (Q_BLOCK, Q_BLOCK), 0)

    def rows_of(sc, c4):
        return pl.ds(pl.multiple_of(sc * KEY_SUPER, KEY_SUPER) + c4 * Q_BLOCK, Q_BLOCK)

    def super_rows(sc):
        return pl.ds(pl.multiple_of(sc * KEY_SUPER, KEY_SUPER), KEY_SUPER)

    def causal(sc, c4):
        return (krow + (sc * KEY_SUPER + c4 * Q_BLOCK)) <= qpos

    def fold8(t):
        return t.reshape(t.shape[0] // SUBLANES, SUBLANES, t.shape[1])

    def score_body(sc, _):
        srows = super_rows(sc)
        ka = kia_ref[srows, :]
        kb = kib_ref[srows, :]
        n_pairs = IDX_HEADS // 2
        for p in range(n_pairs):
            qp = qi_ref[:, p * LANES:(p + 1) * LANES]
            ra = lax.dot_general(ka, qp, NT_DIMS, preferred_element_type=F32)
            rb = lax.dot_general(kb, qp, NT_DIMS, preferred_element_type=F32)
            term = (jnp.maximum(ra, 0.0) * wt_ref[2 * p:2 * p + 1, :]
                    + jnp.maximum(rb, 0.0) * wt_ref[2 * p + 1:2 * p + 2, :])
            if p == 0:
                s_ref[srows, :] = term
            elif p < n_pairs - 1:
                s_ref[srows, :] += term
            else:
                kpos = sc * KEY_SUPER + lax.broadcasted_iota(jnp.int32, (KEY_SUPER, Q_BLOCK), 0)
                qp_s = j * Q_BLOCK + lax.broadcasted_iota(jnp.int32, (KEY_SUPER, Q_BLOCK), 1)
                s_ref[srows, :] = jnp.where(kpos <= qp_s, s_ref[srows, :] + term, SCORE_MASKED)
        return 0

    lax.fori_loop(0, n_super, score_body, 0)

    def count_ge(cand):
        def body(sc, acc):
            for c4 in range(per_super):
                ind = jnp.where(s_ref[rows_of(sc, c4), :] >= cand, 1.0, 0.0)
                acc = acc + fold8(ind).sum(axis=0)
            return acc
        acc = lax.fori_loop(0, n_super, body, jnp.zeros((SUBLANES, LANES), F32))
        return jnp.sum(acc, axis=0, keepdims=True)

    def bit_body(i, carry):
        prefix, cbest = carry
        cu = prefix | lax.shift_left(jnp.int32(1), 31 - i)
        cnt = count_ge(_ordered_bits_to_float(cu))
        ok = cnt >= float(INDEX_TOPK)
        return jnp.where(ok, cu, prefix), jnp.where(ok, cnt, cbest)

    n_keys = (n_super * KEY_SUPER).astype(F32)
    prefix, cbest = lax.fori_loop(
        0, 32, bit_body,
        (jnp.zeros((1, LANES), jnp.int32), jnp.zeros((1, LANES), F32) + n_keys))
    tau = _ordered_bits_to_float(prefix)
    drop = cbest - float(INDEX_TOPK)

    upper = jnp.where(krow < lax.broadcasted_iota(jnp.int32, (Q_BLOCK, Q_BLOCK), 1), 1.0, 0.0).astype(BF16)

    def mask_body(i, later_ties):
        sc = n_super - 1 - i
        for c4 in reversed(range(per_super)):
            rows = rows_of(sc, c4)
            s = s_ref[rows, :]
            eq = jnp.where(s == tau, 1.0, 0.0)
            ties_after = jnp.dot(upper, eq.astype(BF16), preferred_element_type=F32) + later_ties
            later_ties = later_ties + jnp.sum(eq, axis=0, keepdims=True)
            tie_bias = jnp.where(ties_after >= drop, 0.0, MASK_BIAS)
            bias = jnp.where(s > tau, 0.0, jnp.where(s == tau, tie_bias, MASK_BIAS))
            s_ref[rows, :] = jnp.where(causal(sc, c4), bias, MASK_BIAS)
        return later_ties

    lax.fori_loop(0, n_super, mask_body, jnp.zeros((1, LANES), F32))

    pair_w = 2 * Q_BLOCK

    def logit_body(sc, m_acc):
        srows = super_rows(sc)
        kc = k_ref[srows, :]
        bias = s_ref[srows, :]
        bias2 = jnp.concatenate([bias, bias], axis=1)
        parts = []
        for pr in range(N_HEADS // 2):
            qpair = q_ref[2 * pr:2 * pr + 2].reshape(pair_w, HEAD_DIM)
            lg = lax.dot_general(kc, qpair, NT_DIMS, preferred_element_type=F32) + bias2
            l_ref[srows, pr * pair_w:(pr + 1) * pair_w] = lg
            parts.append(fold8(lg).max(axis=0))
        return jnp.maximum(m_acc, jnp.concatenate(parts, axis=1))

    m_acc = lax.fori_loop(0, n_super, logit_body, jnp.full((SUBLANES, hq), MASK_BIAS, F32))
    m = jnp.max(m_acc, axis=0, keepdims=True)

    acc_ref[...] = jnp.zeros_like(acc_ref)

    def pv_body(sc, d_acc):
        for c4 in range(per_super):
            rows = rows_of(sc, c4)
            parts = []
            for pr in range(N_HEADS // 2):
                cols = slice(pr * pair_w, (pr + 1) * pair_w)
                p = jnp.exp2(l_ref[rows, cols] - m[:, cols])
                p_ref[rows, cols] = p.astype(BF16)
                parts.append(fold8(p).sum(axis=0))
            d_acc = d_acc + jnp.concatenate(parts, axis=1)
        acc_ref[...] += jnp.dot(vt_ref[sc], p_ref[super_rows(sc), :], preferred_element_type=F32)
        return d_acc

    d_acc = lax.fori_loop(0, n_super, pv_body, jnp.zeros((SUBLANES, hq), F32))
    inv = 1.0 / jnp.sum(d_acc, axis=0, keepdims=True)
    for h in range(N_HEADS):
        cols = slice(h * Q_BLOCK, (h + 1) * Q_BLOCK)
        o_ref[h] = (acc_ref[:, cols] * inv[:, cols]).T.astype(BF16)


def _dsa_attn(q3, k, vt, kia, kib, qi, wt):
    nqb = SEQ // Q_BLOCK
    n_sup = SEQ // KEY_SUPER
    hq = N_HEADS * Q_BLOCK
    return pl.pallas_call(
        _dsa_attn_kernel,
        grid=(BATCH, nqb),
        in_specs=[
            pl.BlockSpec((N_HEADS, Q_BLOCK, HEAD_DIM), lambda b, j: (0, b * nqb + j, 0)),
            pl.BlockSpec((SEQ, HEAD_DIM), lambda b, j: (b, 0)),
            pl.BlockSpec((n_sup, HEAD_DIM, KEY_SUPER), lambda b, j: (b, 0, 0)),
            pl.BlockSpec((SEQ, LANES), lambda b, j: (b, 0)),
            pl.BlockSpec((SEQ, LANES), lambda b, j: (b, 0)),
            pl.BlockSpec((Q_BLOCK, IDX_HEADS * IDX_DIM), lambda b, j: (b * nqb + j, 0)),
            pl.BlockSpec((IDX_HEADS, Q_BLOCK), lambda b, j: (0, b * nqb + j)),
        ],
        out_specs=pl.BlockSpec((N_HEADS, Q_BLOCK, HEAD_DIM), lambda b, j: (0, b * nqb + j, 0)),
        out_shape=jax.ShapeDtypeStruct((N_HEADS, N_TOK, HEAD_DIM), BF16),
        scratch_shapes=[
            pltpu.VMEM((SEQ, Q_BLOCK), F32),
            pltpu.VMEM((SEQ, hq), F32),
            pltpu.VMEM((SEQ, hq), BF16),
            pltpu.VMEM((HEAD_DIM, hq), F32),
        ],
        compiler_params=pltpu.CompilerParams(
            dimension_semantics=("parallel", "arbitrary"), vmem_limit_bytes=VMEM_LIMIT),
        name="dsa_attn",
    )(q3, k, vt, kia, kib, qi, wt)


def _resid_proj_kernel(x_ref, o_ref, w_ref, out_ref, wbf_ref):
    _cast_once(w_ref, wbf_ref)
    o = jnp.concatenate([o_ref[h] for h in range(N_HEADS)], axis=1)
    out_ref[...] = x_ref[...] + jnp.dot(o, wbf_ref[...], preferred_element_type=F32)


def _resid_proj(x2d, o3, w_out, tm=512):
    n = x2d.shape[0]
    return pl.pallas_call(
        _resid_proj_kernel,
        grid=(n // tm,),
        in_specs=[
            pl.BlockSpec((tm, D_MODEL), lambda i: (i, 0)),
            pl.BlockSpec((N_HEADS, tm, HEAD_DIM), lambda i: (0, i, 0)),
            _resident((D_MODEL, D_MODEL)),
        ],
        out_specs=pl.BlockSpec((tm, D_MODEL), lambda i: (i, 0)),
        out_shape=jax.ShapeDtypeStruct((n, D_MODEL), F32),
        scratch_shapes=[pltpu.VMEM((D_MODEL, D_MODEL), BF16)],
        compiler_params=pltpu.CompilerParams(
            dimension_semantics=("arbitrary",), vmem_limit_bytes=VMEM_LIMIT),
        name="dsa_out_proj",
    )(x2d, o3, w_out)


def _ffn_kernel(x_ref, g_ref, wg_ref, wu_ref, wd_ref, out_ref, h_ref, acc_ref):
    c = pl.program_id(1)

    @pl.when(c == 0)
    def _():
        h_ref[...] = _rmsnorm(x_ref[...], g_ref[...]).astype(BF16)
        acc_ref[...] = jnp.zeros_like(acc_ref)

    h = h_ref[...]
    gate = jnp.dot(h, wg_ref[...].astype(BF16), preferred_element_type=F32)
    up = jnp.dot(h, wu_ref[...].astype(BF16), preferred_element_type=F32)
    a = (gate * jax.nn.sigmoid(gate) * up).astype(BF16)
    acc_ref[...] += jnp.dot(a, wd_ref[...].astype(BF16), preferred_element_type=F32)

    @pl.when(c == pl.num_programs(1) - 1)
    def _():
        out_ref[...] = x_ref[...] + acc_ref[...]


def _ffn(x2d, g, w_gu, w_down, tm=1024, fc=512):
    n = x2d.shape[0]
    nc = FFN_DIM // fc
    return pl.pallas_call(
        _ffn_kernel,
        grid=(n // tm, nc),
        in_specs=[
            pl.BlockSpec((tm, D_MODEL), lambda i, c: (i, 0)),
            pl.BlockSpec((1, D_MODEL), lambda i, c: (0, 0)),
            pl.BlockSpec((D_MODEL, fc), lambda i, c: (0, c)),
            pl.BlockSpec((D_MODEL, fc), lambda i, c: (0, c + nc)),
            pl.BlockSpec((fc, D_MODEL), lambda i, c: (c, 0)),
        ],
        out_specs=pl.BlockSpec((tm, D_MODEL), lambda i, c: (i, 0)),
        out_shape=jax.ShapeDtypeStruct((n, D_MODEL), F32),
        scratch_shapes=[pltpu.VMEM((tm, D_MODEL), BF16), pltpu.VMEM((tm, D_MODEL), F32)],
        compiler_params=pltpu.CompilerParams(
            dimension_semantics=("parallel", "arbitrary"), vmem_limit_bytes=VMEM_LIMIT),
        name="swiglu_ffn",
    )(x2d, g, w_gu, w_gu, w_down)


def _gelu_tanh(x):
    return 0.5 * x * (1.0 + jnp.tanh(np.sqrt(2.0 / np.pi) * (x + 0.044715 * (x * x * x))))


def _gmlp_kernel(x_ref, g_ref, win_ref, lng_ref, lnb_ref, ws_ref, bs_ref, wout_ref, out_ref,
                 winbf_ref, woutbf_ref):
    _cast_once(win_ref, winbf_ref)
    _cast_once(wout_ref, woutbf_ref)
    x = x_ref[...]
    tm = x.shape[0]
    h = _rmsnorm(x, g_ref[...]).astype(BF16)
    y = _gelu_tanh(jnp.dot(h, winbf_ref[...], preferred_element_type=F32))
    u = y[:, :GMLP_WIDTH]
    v = y[:, GMLP_WIDTH:]
    mu = jnp.mean(v, axis=-1, keepdims=True)
    vc = v - mu
    var = jnp.mean(vc * vc, axis=-1, keepdims=True)
    vb = (vc * lax.rsqrt(var + NORM_EPS) * lng_ref[...] + lnb_ref[...]).astype(BF16)

    gw = GMLP_WIDTH // GMLP_GROUPS
    row = lax.broadcasted_iota(jnp.int32, (GMLP_CHUNK, GMLP_CHUNK), 0)
    col = lax.broadcasted_iota(jnp.int32, (GMLP_CHUNK, GMLP_CHUNK), 1)
    w_tril = [jnp.where(row >= col, ws_ref[gi], 0.0).astype(BF16) for gi in range(GMLP_GROUPS)]
    rows = []
    for c in range(tm // GMLP_CHUNK):
        parts = [
            jnp.dot(w_tril[gi], vb[c * GMLP_CHUNK:(c + 1) * GMLP_CHUNK, gi * gw:(gi + 1) * gw],
                    preferred_element_type=F32)
            for gi in range(GMLP_GROUPS)
        ]
        rows.append(jnp.concatenate(parts, axis=1) + bs_ref[...])
    s = jnp.concatenate(rows, axis=0)
    z = (u * s).astype(BF16)
    out_ref[...] = x + jnp.dot(z, woutbf_ref[...], preferred_element_type=F32)


def _gmlp(x2d, g, w_in, ln_g, ln_b, w_sp, b_sp_full, w_out, tm=512):
    n = x2d.shape[0]
    full = lambda shape: pl.BlockSpec(shape, lambda i: (0,) * len(shape))
    return pl.pallas_call(
        _gmlp_kernel,
        grid=(n // tm,),
        in_specs=[
            pl.BlockSpec((tm, D_MODEL), lambda i: (i, 0)),
            full((1, D_MODEL)),
            _resident((D_MODEL, 2 * GMLP_WIDTH)),
            full((1, GMLP_WIDTH)),
            full((1, GMLP_WIDTH)),
            full((GMLP_GROUPS, GMLP_CHUNK, GMLP_CHUNK)),
            full((GMLP_CHUNK, GMLP_WIDTH)),
            _resident((GMLP_WIDTH, D_MODEL)),
        ],
        out_specs=pl.BlockSpec((tm, D_MODEL), lambda i: (i, 0)),
        out_shape=jax.ShapeDtypeStruct((n, D_MODEL), F32),
        scratch_shapes=[pltpu.VMEM((D_MODEL, 2 * GMLP_WIDTH), BF16), pltpu.VMEM((GMLP_WIDTH, D_MODEL), BF16)],
        compiler_params=pltpu.CompilerParams(
            dimension_semantics=("arbitrary",), vmem_limit_bytes=VMEM_LIMIT),
        name="gmlp",
    )(x2d, g, w_in, ln_g, ln_b, w_sp, b_sp_full, w_out)


MOE_BLOCK = 512
MOE_GRAN = 16
MOE_TILE = 1024
MOE_LOCAL_ROWS = 2 * MOE_BLOCK + N_EXPERTS * MOE_GRAN
N_MOE_BLOCKS = N_TOK // MOE_BLOCK
MOE_MAX_TILES = (2 * N_TOK + N_MOE_BLOCKS * N_EXPERTS * (MOE_GRAN - 1)) // MOE_TILE + N_EXPERTS
MOE_ROWS = MOE_MAX_TILES * MOE_TILE
ROUTE_D1, ROUTE_D2, ROUTE_G1, ROUTE_G2 = 0, 1, 2, 3
TAB_LOFF, TAB_PC, TAB_GOFF = 0, N_EXPERTS, 2 * N_EXPERTS


def _lane_scalar(vec, lane_idx, e):
    return jnp.sum(jnp.where(lane_idx == e, vec, 0.0)).astype(jnp.int32)


def _moe_route_kernel(x_ref, g_ref, wr_ref, route_ref, meta_ref):
    b = pl.program_id(0)
    t_blk = MOE_BLOCK
    hf = _rmsnorm(x_ref[...], g_ref[...])
    logits = jnp.dot(hf, wr_ref[...], preferred_element_type=F32, precision=lax.Precision.HIGHEST)
    lane = lax.broadcasted_iota(jnp.int32, logits.shape, 1).astype(F32)
    neg_inf = -jnp.inf
    lg = jnp.where(lane < N_EXPERTS, logits, neg_inf)
    m1 = jnp.max(lg, axis=-1, keepdims=True)
    i1 = jnp.min(jnp.where(lg == m1, lane, float(LANES)), axis=-1, keepdims=True)
    lg2 = jnp.where(lane == i1, neg_inf, lg)
    m2 = jnp.max(lg2, axis=-1, keepdims=True)
    i2 = jnp.min(jnp.where(lg2 == m2, lane, float(LANES)), axis=-1, keepdims=True)
    t = jnp.exp(m2 - m1)
    g1 = 1.0 / (1.0 + t)
    g2 = t * g1

    memb = jnp.where(jnp.logical_or(lane == i1, lane == i2), 1.0, 0.0)
    tr = lax.broadcasted_iota(jnp.int32, (t_blk, t_blk), 0)
    tc = lax.broadcasted_iota(jnp.int32, (t_blk, t_blk), 1)
    earlier = jnp.where(tc < tr, 1.0, 0.0).astype(BF16)
    rank = jnp.dot(earlier, memb.astype(BF16), preferred_element_type=F32)
    cnt = jnp.sum(memb, axis=0, keepdims=True)
    pc = jnp.floor((cnt + float(MOE_GRAN - 1)) * (1.0 / MOE_GRAN)) * float(MOE_GRAN)
    er = lax.broadcasted_iota(jnp.int32, (LANES, LANES), 0)
    ec = lax.broadcasted_iota(jnp.int32, (LANES, LANES), 1)
    before = jnp.where(er < ec, 1.0, 0.0)
    loff = jnp.dot(jnp.broadcast_to(pc, (SUBLANES, LANES)), before, preferred_element_type=F32,
                   precision=lax.Precision.HIGHEST)[0:1, :]
    dest = loff + rank
    d1 = jnp.sum(jnp.where(lane == i1, dest, 0.0), axis=-1, keepdims=True)
    d2 = jnp.sum(jnp.where(lane == i2, dest, 0.0), axis=-1, keepdims=True)
    route = jnp.where(lane == ROUTE_D1, d1,
                      jnp.where(lane == ROUTE_D2, d2,
                                jnp.where(lane == ROUTE_G1, g1, jnp.where(lane == ROUTE_G2, g2, 0.0))))
    route_ref[...] = route
    lane_i = lax.broadcasted_iota(jnp.int32, (1, LANES), 1)
    for e in range(N_EXPERTS):
        meta_ref[b, TAB_LOFF + e] = _lane_scalar(loff, lane_i, e)
        meta_ref[b, TAB_PC + e] = _lane_scalar(pc, lane_i, e)


def _moe_route(x2d, g, w_router):
    return pl.pallas_call(
        _moe_route_kernel,
        grid=(N_MOE_BLOCKS,),
        in_specs=[
            pl.BlockSpec((MOE_BLOCK, D_MODEL), lambda b: (b, 0)),
            pl.BlockSpec((1, D_MODEL), lambda b: (0, 0)),
            pl.BlockSpec((D_MODEL, LANES), lambda b: (0, 0)),
        ],
        out_specs=[
            pl.BlockSpec((MOE_BLOCK, LANES), lambda b: (b, 0)),
            pl.BlockSpec(memory_space=pltpu.SMEM),
        ],
        out_shape=[
            jax.ShapeDtypeStruct((N_TOK, LANES), F32),
            jax.ShapeDtypeStruct((N_MOE_BLOCKS, 2 * N_EXPERTS), jnp.int32),
        ],
        compiler_params=pltpu.CompilerParams(
            dimension_semantics=("arbitrary",), vmem_limit_bytes=VMEM_LIMIT),
        name="moe_route",
    )(x2d, g, w_router)


def _moe_plan(meta):
    pc = meta[:, TAB_PC:TAB_PC + N_EXPERTS]
    totals = jnp.sum(pc, axis=0)
    per_expert = (totals + (MOE_TILE - 1)) // MOE_TILE
    ends = jnp.cumsum(per_expert)
    seg_start = (ends - per_expert) * MOE_TILE
    goff = seg_start[None, :] + jnp.cumsum(pc, axis=0) - pc
    block_tab = jnp.concatenate([meta, goff], axis=1).astype(jnp.int32)
    n_tiles = ends[-1]
    t = jnp.minimum(jnp.arange(MOE_MAX_TILES, dtype=jnp.int32), n_tiles - 1)
    expert = jnp.sum((t[:, None] >= ends[None, :]).astype(jnp.int32), axis=1)
    tile_tab = jnp.stack([expert, jnp.full_like(t, n_tiles)]).astype(jnp.int32)
    fill_tab = jnp.stack([seg_start + totals, ends * MOE_TILE]).astype(jnp.int32)
    return block_tab, tile_tab, fill_tab


def _moe_dispatch_kernel(tab_ref, tile_ref, fill_ref, x_ref, g_ref, route_ref, hs_hbm, hloc_ref, zero_ref, sem):
    b = pl.program_id(0)
    hb = _rmsnorm(x_ref[...], g_ref[...]).astype(BF16)

    route_t = route_ref[...].T
    r_iota = lax.broadcasted_iota(jnp.int32, (MOE_LOCAL_ROWS, MOE_BLOCK), 0).astype(F32)
    hit = jnp.logical_or(r_iota == route_t[ROUTE_D1:ROUTE_D1 + 1, :], r_iota == route_t[ROUTE_D2:ROUTE_D2 + 1, :])
    perm = jnp.where(hit, 1.0, 0.0).astype(BF16)
    hloc_ref[...] = jnp.dot(perm, hb, preferred_element_type=F32).astype(BF16)

    def granule(ref, row):
        return ref.at[pl.ds(pl.multiple_of(row, MOE_GRAN), MOE_GRAN)]

    n_started = 0
    for e in range(N_EXPERTS):
        lo = tab_ref[b, TAB_LOFF + e]
        goff = tab_ref[b, TAB_GOFF + e]
        n_gran = tab_ref[b, TAB_PC + e] // MOE_GRAN

        def start(gi, _, lo=lo, goff=goff):
            pltpu.make_async_copy(granule(hloc_ref, lo + gi * MOE_GRAN),
                                  granule(hs_hbm, goff + gi * MOE_GRAN), sem).start()
            return 0

        lax.fori_loop(0, n_gran, start, 0)
        n_started = n_started + n_gran

    def wait_one(gi, _):
        pltpu.make_async_copy(granule(hloc_ref, 0), granule(hs_hbm, 0), sem).wait()
        return 0

    lax.fori_loop(0, n_started, wait_one, 0)

    @pl.when(b == pl.num_programs(0) - 1)
    def _():
        zero_ref[...] = jnp.zeros_like(zero_ref)
        n_fill = 0
        for e in range(N_EXPERTS):
            first = fill_ref[0, e]
            n_e = (fill_ref[1, e] - first) // MOE_GRAN

            def fill(gi, _, first=first):
                pltpu.make_async_copy(granule(zero_ref, 0), granule(hs_hbm, first + gi * MOE_GRAN), sem).start()
                return 0

            lax.fori_loop(0, n_e, fill, 0)
            n_fill = n_fill + n_e

        def wait_fill(gi, _):
            pltpu.make_async_copy(granule(zero_ref, 0), granule(hs_hbm, 0), sem).wait()
            return 0

        lax.fori_loop(0, n_fill, wait_fill, 0)

        def tile_at(ti):
            return hs_hbm.at[pl.ds(pl.multiple_of(ti * MOE_TILE, MOE_TILE), MOE_TILE)]

        n_tiles = tile_ref[1, 0]

        def fill_tile(ti, _):
            pltpu.make_async_copy(zero_ref, tile_at(ti), sem).start()
            return 0

        def wait_tile(ti, _):
            pltpu.make_async_copy(zero_ref, tile_at(0), sem).wait()
            return 0

        lax.fori_loop(n_tiles, MOE_MAX_TILES, fill_tile, 0)
        lax.fori_loop(n_tiles, MOE_MAX_TILES, wait_tile, 0)


def _moe_dispatch(block_tab, tile_tab, fill_tab, x2d, g, route):
    grid_spec = pltpu.PrefetchScalarGridSpec(
        num_scalar_prefetch=3,
        grid=(N_MOE_BLOCKS,),
        in_specs=[
            pl.BlockSpec((MOE_BLOCK, D_MODEL), lambda b, *_: (b, 0)),
            pl.BlockSpec((1, D_MODEL), lambda b, *_: (0, 0)),
            pl.BlockSpec((MOE_BLOCK, LANES), lambda b, *_: (b, 0)),
        ],
        out_specs=pl.BlockSpec(memory_space=pl.ANY),
        scratch_shapes=[
            pltpu.VMEM((MOE_LOCAL_ROWS, D_MODEL), BF16),
            pltpu.VMEM((MOE_TILE, D_MODEL), BF16),
            pltpu.SemaphoreType.DMA(()),
        ],
    )
    return pl.pallas_call(
        _moe_dispatch_kernel,
        grid_spec=grid_spec,
        out_shape=jax.ShapeDtypeStruct((MOE_ROWS, D_MODEL), BF16),
        compiler_params=pltpu.CompilerParams(
            dimension_semantics=("arbitrary",), vmem_limit_bytes=VMEM_LIMIT),
        name="moe_dispatch",
    )(block_tab, tile_tab, fill_tab, x2d, g, route)


def _moe_expert_kernel(tile_ref, hs_ref, wg_ref, wu_ref, wd_ref, y_ref, acc_ref):
    t = pl.program_id(0)
    c = pl.program_id(1)
    active = t < tile_ref[1, 0]

    @pl.when(jnp.logical_and(jnp.logical_not(active), c == 0))
    def _():
        y_ref[...] = jnp.zeros_like(y_ref)

    @pl.when(active)
    def _():
        @pl.when(c == 0)
        def _():
            acc_ref[...] = jnp.zeros_like(acc_ref)

        h = hs_ref[...]
        gate = jnp.dot(h, wg_ref[0].astype(BF16), preferred_element_type=F32)
        up = jnp.dot(h, wu_ref[0].astype(BF16), preferred_element_type=F32)
        a = (gate * jax.nn.sigmoid(gate) * up).astype(BF16)
        acc_ref[...] += jnp.dot(a, wd_ref[0].astype(BF16), preferred_element_type=F32)

        @pl.when(c == pl.num_programs(1) - 1)
        def _():
            y_ref[...] = acc_ref[...].astype(BF16)


def _moe_experts(tile_tab, hs, w_gu, w_down, fc=512):
    nc = FFN_DIM // fc

    def tile_of(t, tr):
        return jnp.minimum(t, tr[1, 0] - 1)

    def chunk_of(t, c, tr):
        return jnp.where(t < tr[1, 0], c, nc - 1)

    grid_spec = pltpu.PrefetchScalarGridSpec(
        num_scalar_prefetch=1,
        grid=(MOE_MAX_TILES, nc),
        in_specs=[
            pl.BlockSpec((MOE_TILE, D_MODEL), lambda t, c, tr: (tile_of(t, tr), 0)),
            pl.BlockSpec((1, D_MODEL, fc), lambda t, c, tr: (tr[0, t], 0, chunk_of(t, c, tr))),
            pl.BlockSpec((1, D_MODEL, fc), lambda t, c, tr: (tr[0, t], 0, chunk_of(t, c, tr) + nc)),
            pl.BlockSpec((1, fc, D_MODEL), lambda t, c, tr: (tr[0, t], chunk_of(t, c, tr), 0)),
        ],
        out_specs=pl.BlockSpec((MOE_TILE, D_MODEL), lambda t, c, tr: (t, 0)),
        scratch_shapes=[pltpu.VMEM((MOE_TILE, D_MODEL), F32)],
    )
    return pl.pallas_call(
        _moe_expert_kernel,
        grid_spec=grid_spec,
        out_shape=jax.ShapeDtypeStruct((MOE_ROWS, D_MODEL), BF16),
        compiler_params=pltpu.CompilerParams(
            dimension_semantics=("arbitrary", "arbitrary"), vmem_limit_bytes=VMEM_LIMIT),
        name="moe_experts",
    )(tile_tab, hs, w_gu, w_gu, w_down)


def _moe_combine_kernel(meta_ref, x_ref, route_ref, y_hbm, gf_ref, out_ref, yloc_ref, sem):
    b = pl.program_id(0)

    @pl.when(b == 0)
    def _():
        yloc_ref[...] = jnp.zeros_like(yloc_ref)

    def granule(ref, row):
        return ref.at[pl.ds(pl.multiple_of(row, MOE_GRAN), MOE_GRAN)]

    n_started = 0
    for e in range(N_EXPERTS):
        lo = meta_ref[b, TAB_LOFF + e]
        goff = meta_ref[b, TAB_GOFF + e]
        n_gran = meta_ref[b, TAB_PC + e] // MOE_GRAN

        def start(gi, _, lo=lo, goff=goff):
            pltpu.make_async_copy(granule(y_hbm, goff + gi * MOE_GRAN),
                                  granule(yloc_ref, lo + gi * MOE_GRAN), sem).start()
            return 0

        lax.fori_loop(0, n_gran, start, 0)
        n_started = n_started + n_gran

    def wait_one(gi, _):
        pltpu.make_async_copy(granule(y_hbm, 0), granule(yloc_ref, 0), sem).wait()
        return 0

    lax.fori_loop(0, n_started, wait_one, 0)

    route = route_ref[...]
    col = lax.broadcasted_iota(jnp.int32, (MOE_BLOCK, MOE_LOCAL_ROWS), 1).astype(F32)
    pick1 = jnp.where(col == route[:, ROUTE_D1:ROUTE_D1 + 1], 1.0, 0.0).astype(BF16)
    pick2 = jnp.where(col == route[:, ROUTE_D2:ROUTE_D2 + 1], 1.0, 0.0).astype(BF16)
    yl = yloc_ref[...]
    moe = (route[:, ROUTE_G1:ROUTE_G1 + 1] * jnp.dot(pick1, yl, preferred_element_type=F32)
           + route[:, ROUTE_G2:ROUTE_G2 + 1] * jnp.dot(pick2, yl, preferred_element_type=F32))
    out_ref[...] = _rmsnorm(x_ref[...] + moe, gf_ref[...])


def _moe_combine(meta, x2d, route, y, g_final):
    grid_spec = pltpu.PrefetchScalarGridSpec(
        num_scalar_prefetch=1,
        grid=(N_MOE_BLOCKS,),
        in_specs=[
            pl.BlockSpec((MOE_BLOCK, D_MODEL), lambda b, m: (b, 0)),
            pl.BlockSpec((MOE_BLOCK, LANES), lambda b, m: (b, 0)),
            pl.BlockSpec(memory_space=pl.ANY),
            pl.BlockSpec((1, D_MODEL), lambda b, m: (0, 0)),
        ],
        out_specs=pl.BlockSpec((MOE_BLOCK, D_MODEL), lambda b, m: (b, 0)),
        scratch_shapes=[
            pltpu.VMEM((MOE_LOCAL_ROWS, D_MODEL), BF16),
            pltpu.SemaphoreType.DMA(()),
        ],
    )
    return pl.pallas_call(
        _moe_combine_kernel,
        grid_spec=grid_spec,
        out_shape=jax.ShapeDtypeStruct((N_TOK, D_MODEL), F32),
        compiler_params=pltpu.CompilerParams(
            dimension_semantics=("arbitrary",), vmem_limit_bytes=VMEM_LIMIT),
        name="moe_combine",
    )(meta, x2d, route, y, g_final)


def _moe(x2d, g, w_router, w_gu, w_down, g_final):
    route, meta = _moe_route(x2d, g, w_router)
    block_tab, tile_tab, fill_tab = _moe_plan(meta)
    hs = _moe_dispatch(block_tab, tile_tab, fill_tab, x2d, g, route)
    y = _moe_experts(tile_tab, hs, w_gu, w_down)
    return _moe_combine(block_tab, x2d, route, y, g_final)


def _rope_tables():
    def tables(dim):
        inv_freq = ROPE_THETA ** (-jnp.arange(0, dim, 2, dtype=F32) / dim)
        ang = jnp.arange(SEQ, dtype=F32)[:, None] * inv_freq[None, :]
        return jnp.cos(ang), jnp.sin(ang)

    c, s = tables(HEAD_DIM)
    cq = jnp.concatenate([c, c], axis=1)
    sq = jnp.concatenate([-s, s], axis=1)
    c, s = tables(IDX_ROPE_DIM)
    ones = jnp.ones((SEQ, IDX_DIM - IDX_ROPE_DIM), F32)
    zeros = jnp.zeros_like(ones)
    z16 = jnp.zeros_like(s)
    ci = jnp.concatenate([c, c, ones], axis=1)
    sia = jnp.concatenate([-s, z16, zeros], axis=1)
    sib = jnp.concatenate([z16, s, zeros], axis=1)
    rep = LANES // IDX_DIM
    return cq, sq, jnp.tile(ci, (1, rep)), jnp.tile(sia, (1, rep)), jnp.tile(sib, (1, rep))


def kernel(x, norm_mix, norm_ffn, dsa_w_in, dsa_idx_k_gain, dsa_w_out, ffn_w_gu, ffn_w_down, gmlp_w_in,
           gmlp_ln_gain, gmlp_ln_bias, gmlp_w_spatial, gmlp_b_spatial, gmlp_w_out, moe_w_router, moe_w_gu,
           moe_w_down, final_norm):
    x2d = x.reshape(N_TOK, D_MODEL)
    row = lambda v: v.reshape(1, -1)

    w_in = dsa_w_in[0]
    w_all = jnp.pad(w_in, ((0, 0), (0, DSA_COLS - w_in.shape[1])))
    wwi_t = w_in[:, COL_KI + IDX_DIM:COL_KI + IDX_DIM + IDX_HEADS].T
    kig = jnp.pad(dsa_idx_k_gain[0], (0, LANES - IDX_DIM)).reshape(1, LANES)
    q3, k, vt, kia, kib, qi, wt = _dsa_proj(x2d, row(norm_mix[0]), w_all, wwi_t, kig, _rope_tables())
    o3 = _dsa_attn(q3, k, vt, kia, kib, qi, wt)
    x2d = _resid_proj(x2d, o3, dsa_w_out[0])
    x2d = _ffn(x2d, row(norm_ffn[0]), ffn_w_gu[0], ffn_w_down[0])

    gw = GMLP_WIDTH // GMLP_GROUPS
    b_sp_full = jnp.repeat(gmlp_b_spatial[0].T, gw, axis=1)
    x2d = _gmlp(x2d, row(norm_mix[1]), gmlp_w_in[0], row(gmlp_ln_gain[0]), row(gmlp_ln_bias[0]),
                gmlp_w_spatial[0], b_sp_full, gmlp_w_out[0])
    w_router = jnp.pad(moe_w_router[0], ((0, 0), (0, LANES - N_EXPERTS)))
    out = _moe(x2d, row(norm_ffn[1]), w_router, moe_w_gu[0], moe_w_down[0], row(final_norm))
    return out.reshape(BATCH, SEQ, D_MODEL)
```

```python
import functools

import numpy as np
import jax
import jax.numpy as jnp
from jax import lax
from jax.experimental import pallas as pl
from jax.experimental.pallas import tpu as pltpu

F32 = jnp.float32
BF16 = jnp.bfloat16

D_MODEL = 1024
BATCH = 8
SEQ = 2048
N_TOK = BATCH * SEQ
N_HEADS = 8
HEAD_DIM = 128
IDX_HEADS = 8
IDX_DIM = 64
IDX_ROPE_DIM = 32
INDEX_TOPK = 256
Q_BLOCK = 128
KEY_SUPER = 512
ROPE_THETA = 10000.0
GMLP_WIDTH = 1024
GMLP_GROUPS = 8
GMLP_CHUNK = 128
FFN_DIM = 3584
N_EXPERTS = 8
NORM_EPS = 1e-6

LANES = 128
SUBLANES = 8
VMEM_LIMIT = 56 * 1024 * 1024

DSA_COLS = 1920
COL_K = N_HEADS * HEAD_DIM
COL_V = COL_K + HEAD_DIM
COL_QI = COL_V + HEAD_DIM
COL_KI = COL_QI + IDX_HEADS * IDX_DIM

NT_DIMS = (((1,), (1,)), ((), ()))
MASK_BIAS = -1e30
SCORE_MASKED = -3.0e38


def _rmsnorm(x, g):
    ms = jnp.mean(x * x, axis=-1, keepdims=True)
    return x * lax.rsqrt(ms + NORM_EPS) * g


def _cast_once(src_ref, dst_ref):
    @pl.when(pl.program_id(0) == 0)
    def _():
        dst_ref[...] = src_ref[...].astype(BF16)


def _resident(shape):
    return pl.BlockSpec(shape, lambda i: (0,) * len(shape), pipeline_mode=pl.Buffered(1))


def _dsa_proj_kernel(x_ref, g_ref, w_ref, wwi_ref, kig_ref, cq_ref, sq_ref, ci_ref, sia_ref, sib_ref,
                     q_ref, k_ref, vt_ref, kia_ref, kib_ref, qi_ref, wt_ref, wbf_ref):
    _cast_once(w_ref, wbf_ref)
    h = _rmsnorm(x_ref[...], g_ref[...]).astype(BF16)
    y = jnp.dot(h, wbf_ref[...], preferred_element_type=F32)
    cq = cq_ref[...]
    sq = sq_ref[...]
    ci = ci_ref[...]
    sia = sia_ref[...]
    sib = sib_ref[...]

    def rope_head(t):
        return t * cq + pltpu.roll(t, HEAD_DIM // 2, 1) * sq

    def rope_idx(t):
        half = IDX_ROPE_DIM // 2
        return t * ci + pltpu.roll(t, LANES - half, 1) * sia + pltpu.roll(t, half, 1) * sib

    q_scale = HEAD_DIM ** -0.5 * np.log2(np.e)
    for hh in range(N_HEADS):
        q_ref[hh] = (rope_head(y[:, hh * HEAD_DIM:(hh + 1) * HEAD_DIM]) * q_scale).astype(BF16)
    k_ref[...] = rope_head(y[:, COL_K:COL_V]).astype(BF16)
    vt_ref[0] = y[:, COL_V:COL_QI].T.astype(BF16)
    for p in range(IDX_HEADS * IDX_DIM // LANES):
        qi_ref[:, p * LANES:(p + 1) * LANES] = rope_idx(
            y[:, COL_QI + p * LANES:COL_QI + (p + 1) * LANES]).astype(BF16)

    last = y[:, COL_KI:COL_KI + LANES]
    lane = lax.broadcasted_iota(jnp.int32, last.shape, 1)
    kraw = jnp.where(lane < IDX_DIM, last, 0.0)
    kms = jnp.sum(kraw * kraw, axis=-1, keepdims=True) * (1.0 / IDX_DIM)
    kn = kraw * lax.rsqrt(kms + NORM_EPS) * kig_ref[...]
    kr = rope_idx(kn)
    kia_ref[...] = kr.astype(BF16)
    kib_ref[...] = pltpu.roll(kr, IDX_DIM, 1).astype(BF16)

    w_scale = IDX_HEADS ** -0.5 * IDX_DIM ** -0.5
    wt_ref[...] = lax.dot_general(wwi_ref[...].astype(BF16), h, NT_DIMS, preferred_element_type=F32) * w_scale


def _dsa_proj(x2d, g, w_all, wwi_t, kig, tabs, tm=KEY_SUPER):
    n = x2d.shape[0]
    nblk_seq = SEQ // tm
    tab_spec = pl.BlockSpec((tm, LANES), lambda i: (i % nblk_seq, 0))
    full = lambda shape: pl.BlockSpec(shape, lambda i: (0,) * len(shape))
    return pl.pallas_call(
        _dsa_proj_kernel,
        grid=(n // tm,),
        in_specs=[
            pl.BlockSpec((tm, D_MODEL), lambda i: (i, 0)),
            full((1, D_MODEL)),
            _resident((D_MODEL, DSA_COLS)),
            full((IDX_HEADS, D_MODEL)),
            full((1, LANES)),
            tab_spec, tab_spec, tab_spec, tab_spec, tab_spec,
        ],
        out_specs=[
            pl.BlockSpec((N_HEADS, tm, HEAD_DIM), lambda i: (0, i, 0)),
            pl.BlockSpec((tm, HEAD_DIM), lambda i: (i, 0)),
            pl.BlockSpec((1, HEAD_DIM, tm), lambda i: (i, 0, 0)),
            pl.BlockSpec((tm, LANES), lambda i: (i, 0)),
            pl.BlockSpec((tm, LANES), lambda i: (i, 0)),
            pl.BlockSpec((tm, IDX_HEADS * IDX_DIM), lambda i: (i, 0)),
            pl.BlockSpec((IDX_HEADS, tm), lambda i: (0, i)),
        ],
        out_shape=[
            jax.ShapeDtypeStruct((N_HEADS, n, HEAD_DIM), BF16),
            jax.ShapeDtypeStruct((n, HEAD_DIM), BF16),
            jax.ShapeDtypeStruct((n // tm, HEAD_DIM, tm), BF16),
            jax.ShapeDtypeStruct((n, LANES), BF16),
            jax.ShapeDtypeStruct((n, LANES), BF16),
            jax.ShapeDtypeStruct((n, IDX_HEADS * IDX_DIM), BF16),
            jax.ShapeDtypeStruct((IDX_HEADS, n), F32),
        ],
        scratch_shapes=[pltpu.VMEM((D_MODEL, DSA_COLS), BF16)],
        compiler_params=pltpu.CompilerParams(
            dimension_semantics=("arbitrary",), vmem_limit_bytes=VMEM_LIMIT),
        name="dsa_proj",
    )(x2d, g, w_all, wwi_t, kig, *tabs)


def _ordered_bits_to_float(u):
    key = u ^ jnp.int32(-2 ** 31)
    key = jnp.maximum(key, jnp.int32(-2139095041))
    bits = jnp.where(key >= 0, key, key ^ jnp.int32(0x7FFFFFFF))
    return lax.bitcast_convert_type(bits, F32)


def _dsa_attn_kernel(q_ref, k_ref, vt_ref, kia_ref, kib_ref, qi_ref, wt_ref, o_ref,
                     s_ref, l_ref, p_ref, acc_ref):
    j = pl.program_id(1)
    per_super = KEY_SUPER // Q_BLOCK
    n_super = j // per_super + 1
    hq = N_HEADS * Q_BLOCK

    qpos = j * Q_BLOCK + lax.broadcasted_iota(jnp.int32, (Q_BLOCK, Q_BLOCK), 1)
    krow = lax.broadcasted_iota(jnp.int32, (Q_BLOCK, Q_BLOCK), 0)

    def rows_of(sc, c4):
        return pl.ds(pl.multiple_of(sc * KEY_SUPER, KEY_SUPER) + c4 * Q_BLOCK, Q_BLOCK)

    def super_rows(sc):
        return pl.ds(pl.multiple_of(sc * KEY_SUPER, KEY_SUPER), KEY_SUPER)

    def causal(sc, c4):
        return (krow + (sc * KEY_SUPER + c4 * Q_BLOCK)) <= qpos

    def fold8(t):
        return t.reshape(t.shape[0] // SUBLANES, SUBLANES, t.shape[1])

    def score_body(sc, _):
        srows = super_rows(sc)
        ka = kia_ref[srows, :]
        kb = kib_ref[srows, :]
        for half in range(IDX_HEADS // 4):
            p0 = 2 * half
            qpp = jnp.concatenate(
                [qi_ref[:, p0 * LANES:(p0 + 1) * LANES], qi_ref[:, (p0 + 1) * LANES:(p0 + 2) * LANES]], axis=0)
            ra = lax.dot_general(ka, qpp, NT_DIMS, preferred_element_type=F32)
            rb = lax.dot_general(kb, qpp, NT_DIMS, preferred_element_type=F32)
            term = None
            for sub in range(2):
                p = p0 + sub
                cols = slice(sub * Q_BLOCK, (sub + 1) * Q_BLOCK)
                t2 = (jnp.maximum(ra[:, cols], 0.0) * wt_ref[2 * p:2 * p + 1, :]
                      + jnp.maximum(rb[:, cols], 0.0) * wt_ref[2 * p + 1:2 * p + 2, :])
                term = t2 if term is None else term + t2
            if half == 0:
                s_ref[srows, :] = term
            else:
                kpos = sc * KEY_SUPER + lax.broadcasted_iota(jnp.int32, (KEY_SUPER, Q_BLOCK), 0)
                qp_s = j * Q_BLOCK + lax.broadcasted_iota(jnp.int32, (KEY_SUPER, Q_BLOCK), 1)
                s_ref[srows, :] = jnp.where(kpos <= qp_s, s_ref[srows, :] + term, SCORE_MASKED)
        return 0

    lax.fori_loop(0, n_super, score_body, 0)

    def count_ge(cand):
        def body(sc, acc):
            for c4 in range(per_super):
                ind = jnp.where(s_ref[rows_of(sc, c4), :] >= cand, 1.0, 0.0)
                acc = acc + fold8(ind).sum(axis=0)
            return acc
        acc = lax.fori_loop(0, n_super, body, jnp.zeros((SUBLANES, LANES), F32))
        return jnp.sum(acc, axis=0, keepdims=True)

    def bit_body(i, carry):
        prefix, cbest = carry
        cu = prefix | lax.shift_left(jnp.int32(1), 31 - i)
        cnt = count_ge(_ordered_bits_to_float(cu))
        ok = cnt >= float(INDEX_TOPK)
        return jnp.where(ok, cu, prefix), jnp.where(ok, cnt, cbest)

    n_keys = (n_super * KEY_SUPER).astype(F32)
    prefix, cbest = lax.fori_loop(
        0, 32, bit_body,
        (jnp.zeros((1, LANES), jnp.int32), jnp.zeros((1, LANES), F32) + n_keys))
    tau = _ordered_bits_to_float(prefix)
    drop = cbest - float(INDEX_TOPK)

    upper = jnp.where(krow < lax.broadcasted_iota(jnp.int32, (Q_BLOCK, Q_BLOCK), 1), 1.0, 0.0).astype(BF16)
    pair_w = 2 * Q_BLOCK

    def select_logit_body(i, carry):
        later_ties, m_acc = carry
        sc = n_super - 1 - i
        biases = [None] * per_super
        for c4 in reversed(range(per_super)):
            s = s_ref[rows_of(sc, c4), :]
            eq = jnp.where(s == tau, 1.0, 0.0)
            ties_after = jnp.dot(upper, eq.astype(BF16), preferred_element_type=F32) + later_ties
            later_ties = later_ties + jnp.sum(eq, axis=0, keepdims=True)
            tie_bias = jnp.where(ties_after >= drop, 0.0, MASK_BIAS)
            bias = jnp.where(s > tau, 0.0, jnp.where(s == tau, tie_bias, MASK_BIAS))
            biases[c4] = jnp.where(causal(sc, c4), bias, MASK_BIAS)
        bias = jnp.concatenate(biases, axis=0)
        bias2 = jnp.concatenate([bias, bias], axis=1)
        srows = super_rows(sc)
        kc = k_ref[srows, :]
        parts = []
        for pr in range(N_HEADS // 2):
            qpair = q_ref[2 * pr:2 * pr + 2].reshape(pair_w, HEAD_DIM)
            lg = lax.dot_general(kc, qpair, NT_DIMS, preferred_element_type=F32) + bias2
            l_ref[srows, pr * pair_w:(pr + 1) * pair_w] = lg
            parts.append(fold8(lg).max(axis=0))
        return later_ties, jnp.maximum(m_acc, jnp.concatenate(parts, axis=1))

    _, m_acc = lax.fori_loop(
        0, n_super, select_logit_body,
        (jnp.zeros((1, LANES), F32), jnp.full((SUBLANES, hq), MASK_BIAS, F32)))
    m = jnp.max(m_acc, axis=0, keepdims=True)

    def prob_chunk(sc, d_acc):
        for c4 in range(per_super):
            rows = rows_of(sc, c4)
            parts = []
            for pr in range(N_HEADS // 2):
                cols = slice(pr * pair_w, (pr + 1) * pair_w)
                p = jnp.exp2(l_ref[rows, cols] - m[:, cols])
                p_ref[rows, cols] = p.astype(BF16)
                parts.append(fold8(p).sum(axis=0))
            d_acc = d_acc + jnp.concatenate(parts, axis=1)
        return d_acc

    def pv_chunk(sc):
        acc_ref[...] += jnp.dot(vt_ref[sc], p_ref[super_rows(sc), :], preferred_element_type=F32)

    def prob_pv_body(sc, d_acc):
        pv_chunk(sc - 1)
        return prob_chunk(sc, d_acc)

    acc_ref[...] = jnp.zeros_like(acc_ref)
    d_acc = prob_chunk(0, jnp.zeros((SUBLANES, hq), F32))
    d_acc = lax.fori_loop(1, n_super, prob_pv_body, d_acc)
    pv_chunk(n_super - 1)
    inv = 1.0 / jnp.sum(d_acc, axis=0, keepdims=True)
    for h in range(N_HEADS):
        cols = slice(h * Q_BLOCK, (h + 1) * Q_BLOCK)
        o_ref[h] = (acc_ref[:, cols] * inv[:, cols]).T.astype(BF16)


def _dsa_attn(q3, k, vt, kia, kib, qi, wt):
    nqb = SEQ // Q_BLOCK
    n_sup = SEQ // KEY_SUPER
    hq = N_HEADS * Q_BLOCK
    return pl.pallas_call(
        _dsa_attn_kernel,
        grid=(BATCH, nqb),
        in_specs=[
            pl.BlockSpec((N_HEADS, Q_BLOCK, HEAD_DIM), lambda b, j: (0, b * nqb + j, 0)),
            pl.BlockSpec((SEQ, HEAD_DIM), lambda b, j: (b, 0)),
            pl.BlockSpec((n_sup, HEAD_DIM, KEY_SUPER), lambda b, j: (b, 0, 0)),
            pl.BlockSpec((SEQ, LANES), lambda b, j: (b, 0)),
            pl.BlockSpec((SEQ, LANES), lambda b, j: (b, 0)),
            pl.BlockSpec((Q_BLOCK, IDX_HEADS * IDX_DIM), lambda b, j: (b * nqb + j, 0)),
            pl.BlockSpec((IDX_HEADS, Q_BLOCK), lambda b, j: (0, b * nqb + j)),
        ],
        out_specs=pl.BlockSpec((N_HEADS, Q_BLOCK, HEAD_DIM), lambda b, j: (0, b * nqb + j, 0)),
        out_shape=jax.ShapeDtypeStruct((N_HEADS, N_TOK, HEAD_DIM), BF16),
        scratch_shapes=[
            pltpu.VMEM((SEQ, Q_BLOCK), F32),
            pltpu.VMEM((SEQ, hq), F32),
            pltpu.VMEM((SEQ, hq), BF16),
            pltpu.VMEM((HEAD_DIM, hq), F32),
        ],
        compiler_params=pltpu.CompilerParams(
            dimension_semantics=("parallel", "arbitrary"), vmem_limit_bytes=VMEM_LIMIT),
        name="dsa_attn",
    )(q3, k, vt, kia, kib, qi, wt)


def _resid_proj_kernel(x_ref, o_ref, w_ref, out_ref, wbf_ref):
    _cast_once(w_ref, wbf_ref)
    o = jnp.concatenate([o_ref[h] for h in range(N_HEADS)], axis=1)
    out_ref[...] = x_ref[...] + jnp.dot(o, wbf_ref[...], preferred_element_type=F32)


def _resid_proj(x2d, o3, w_out, tm=512):
    n = x2d.shape[0]
    return pl.pallas_call(
        _resid_proj_kernel,
        grid=(n // tm,),
        in_specs=[
            pl.BlockSpec((tm, D_MODEL), lambda i: (i, 0)),
            pl.BlockSpec((N_HEADS, tm, HEAD_DIM), lambda i: (0, i, 0)),
            _resident((D_MODEL, D_MODEL)),
        ],
        out_specs=pl.BlockSpec((tm, D_MODEL), lambda i: (i, 0)),
        out_shape=jax.ShapeDtypeStruct((n, D_MODEL), F32),
        scratch_shapes=[pltpu.VMEM((D_MODEL, D_MODEL), BF16)],
        compiler_params=pltpu.CompilerParams(
            dimension_semantics=("arbitrary",), vmem_limit_bytes=VMEM_LIMIT),
        name="dsa_out_proj",
    )(x2d, o3, w_out)


def _ffn_kernel(x_ref, g_ref, wg_ref, wu_ref, wd_ref, out_ref, h_ref, acc_ref):
    c = pl.program_id(1)

    @pl.when(c == 0)
    def _():
        h_ref[...] = _rmsnorm(x_ref[...], g_ref[...]).astype(BF16)
        acc_ref[...] = jnp.zeros_like(acc_ref)

    h = h_ref[...]
    gate = jnp.dot(h, wg_ref[...].astype(BF16), preferred_element_type=F32)
    up = jnp.dot(h, wu_ref[...].astype(BF16), preferred_element_type=F32)
    a = (gate * jax.nn.sigmoid(gate) * up).astype(BF16)
    acc_ref[...] += jnp.dot(a, wd_ref[...].astype(BF16), preferred_element_type=F32)

    @pl.when(c == pl.num_programs(1) - 1)
    def _():
        out_ref[...] = x_ref[...] + acc_ref[...]


def _ffn(x2d, g, w_gu, w_down, tm=1024, fc=512):
    n = x2d.shape[0]
    nc = FFN_DIM // fc
    return pl.pallas_call(
        _ffn_kernel,
        grid=(n // tm, nc),
        in_specs=[
            pl.BlockSpec((tm, D_MODEL), lambda i, c: (i, 0)),
            pl.BlockSpec((1, D_MODEL), lambda i, c: (0, 0)),
            pl.BlockSpec((D_MODEL, fc), lambda i, c: (0, c)),
            pl.BlockSpec((D_MODEL, fc), lambda i, c: (0, c + nc)),
            pl.BlockSpec((fc, D_MODEL), lambda i, c: (c, 0)),
        ],
        out_specs=pl.BlockSpec((tm, D_MODEL), lambda i, c: (i, 0)),
        out_shape=jax.ShapeDtypeStruct((n, D_MODEL), F32),
        scratch_shapes=[pltpu.VMEM((tm, D_MODEL), BF16), pltpu.VMEM((tm, D_MODEL), F32)],
        compiler_params=pltpu.CompilerParams(
            dimension_semantics=("parallel", "arbitrary"), vmem_limit_bytes=VMEM_LIMIT),
        name="swiglu_ffn",
    )(x2d, g, w_gu, w_gu, w_down)


def _gelu_tanh(x):
    return 0.5 * x * (1.0 + jnp.tanh(np.sqrt(2.0 / np.pi) * (x + 0.044715 * (x * x * x))))


def _gmlp_kernel(x_ref, g_ref, win_ref, lng_ref, lnb_ref, ws_ref, bs_ref, wout_ref, out_ref,
                 winbf_ref, woutbf_ref):
    _cast_once(win_ref, winbf_ref)
    _cast_once(wout_ref, woutbf_ref)
    x = x_ref[...]
    tm = x.shape[0]
    h = _rmsnorm(x, g_ref[...]).astype(BF16)
    y = _gelu_tanh(jnp.dot(h, winbf_ref[...], preferred_element_type=F32))
    u = y[:, :GMLP_WIDTH]
    v = y[:, GMLP_WIDTH:]
    mu = jnp.mean(v, axis=-1, keepdims=True)
    vc = v - mu
    var = jnp.mean(vc * vc, axis=-1, keepdims=True)
    vb = (vc * lax.rsqrt(var + NORM_EPS) * lng_ref[...] + lnb_ref[...]).astype(BF16)

    gw = GMLP_WIDTH // GMLP_GROUPS
    row = lax.broadcasted_iota(jnp.int32, (GMLP_CHUNK, GMLP_CHUNK), 0)
    col = lax.broadcasted_iota(jnp.int32, (GMLP_CHUNK, GMLP_CHUNK), 1)
    w_tril = [jnp.where(row >= col, ws_ref[gi], 0.0).astype(BF16) for gi in range(GMLP_GROUPS)]
    rows = []
    for c in range(tm // GMLP_CHUNK):
        parts = [
            jnp.dot(w_tril[gi], vb[c * GMLP_CHUNK:(c + 1) * GMLP_CHUNK, gi * gw:(gi + 1) * gw],
                    preferred_element_type=F32)
            for gi in range(GMLP_GROUPS)
        ]
        rows.append(jnp.concatenate(parts, axis=1) + bs_ref[...])
    s = jnp.concatenate(rows, axis=0)
    z = (u * s).astype(BF16)
    out_ref[...] = x + jnp.dot(z, woutbf_ref[...], preferred_element_type=F32)


def _gmlp(x2d, g, w_in, ln_g, ln_b, w_sp, b_sp_full, w_out, tm=512):
    n = x2d.shape[0]
    full = lambda shape: pl.BlockSpec(shape, lambda i: (0,) * len(shape))
    return pl.pallas_call(
        _gmlp_kernel,
        grid=(n // tm,),
        in_specs=[
            pl.BlockSpec((tm, D_MODEL), lambda i: (i, 0)),
            full((1, D_MODEL)),
            _resident((D_MODEL, 2 * GMLP_WIDTH)),
            full((1, GMLP_WIDTH)),
            full((1, GMLP_WIDTH)),
            full((GMLP_GROUPS, GMLP_CHUNK, GMLP_CHUNK)),
            full((GMLP_CHUNK, GMLP_WIDTH)),
            _resident((GMLP_WIDTH, D_MODEL)),
        ],
        out_specs=pl.BlockSpec((tm, D_MODEL), lambda i: (i, 0)),
        out_shape=jax.ShapeDtypeStruct((n, D_MODEL), F32),
        scratch_shapes=[pltpu.VMEM((D_MODEL, 2 * GMLP_WIDTH), BF16), pltpu.VMEM((GMLP_WIDTH, D_MODEL), BF16)],
        compiler_params=pltpu.CompilerParams(
            dimension_semantics=("arbitrary",), vmem_limit_bytes=VMEM_LIMIT),
        name="gmlp",
    )(x2d, g, w_in, ln_g, ln_b, w_sp, b_sp_full, w_out)


MOE_BLOCK = 512
MOE_GRAN = 16
MOE_TILE = 1024
MOE_LOCAL_ROWS = 2 * MOE_BLOCK + N_EXPERTS * MOE_GRAN
N_MOE_BLOCKS = N_TOK // MOE_BLOCK
MOE_MAX_TILES = (2 * N_TOK + N_MOE_BLOCKS * N_EXPERTS * (MOE_GRAN - 1)) // MOE_TILE + N_EXPERTS
MOE_ROWS = MOE_MAX_TILES * MOE_TILE
ROUTE_D1, ROUTE_D2, ROUTE_G1, ROUTE_G2 = 0, 1, 2, 3
TAB_LOFF, TAB_PC, TAB_GOFF = 0, N_EXPERTS, 2 * N_EXPERTS


def _lane_scalar(vec, lane_idx, e):
    return jnp.sum(jnp.where(lane_idx == e, vec, 0.0)).astype(jnp.int32)


def _moe_route_kernel(x_ref, g_ref, wr_ref, route_ref, meta_ref):
    b = pl.program_id(0)
    t_blk = MOE_BLOCK
    hf = _rmsnorm(x_ref[...], g_ref[...])
    logits = jnp.dot(hf, wr_ref[...], preferred_element_type=F32, precision=lax.Precision.HIGHEST)
    lane = lax.broadcasted_iota(jnp.int32, logits.shape, 1).astype(F32)
    neg_inf = -jnp.inf
    lg = jnp.where(lane < N_EXPERTS, logits, neg_inf)
    m1 = jnp.max(lg, axis=-1, keepdims=True)
    i1 = jnp.min(jnp.where(lg == m1, lane, float(LANES)), axis=-1, keepdims=True)
    lg2 = jnp.where(lane == i1, neg_inf, lg)
    m2 = jnp.max(lg2, axis=-1, keepdims=True)
    i2 = jnp.min(jnp.where(lg2 == m2, lane, float(LANES)), axis=-1, keepdims=True)
    t = jnp.exp(m2 - m1)
    g1 = 1.0 / (1.0 + t)
    g2 = t * g1

    memb = jnp.where(jnp.logical_or(lane == i1, lane == i2), 1.0, 0.0)
    tr = lax.broadcasted_iota(jnp.int32, (t_blk, t_blk), 0)
    tc = lax.broadcasted_iota(jnp.int32, (t_blk, t_blk), 1)
    earlier = jnp.where(tc < tr, 1.0, 0.0).astype(BF16)
    rank = jnp.dot(earlier, memb.astype(BF16), preferred_element_type=F32)
    cnt = jnp.sum(memb, axis=0, keepdims=True)
    pc = jnp.floor((cnt + float(MOE_GRAN - 1)) * (1.0 / MOE_GRAN)) * float(MOE_GRAN)
    er = lax.broadcasted_iota(jnp.int32, (LANES, LANES), 0)
    ec = lax.broadcasted_iota(jnp.int32, (LANES, LANES), 1)
    before = jnp.where(er < ec, 1.0, 0.0)
    loff = jnp.dot(jnp.broadcast_to(pc, (SUBLANES, LANES)), before, preferred_element_type=F32,
                   precision=lax.Precision.HIGHEST)[0:1, :]
    dest = loff + rank
    d1 = jnp.sum(jnp.where(lane == i1, dest, 0.0), axis=-1, keepdims=True)
    d2 = jnp.sum(jnp.where(lane == i2, dest, 0.0), axis=-1, keepdims=True)
    route = jnp.where(lane == ROUTE_D1, d1,
                      jnp.where(lane == ROUTE_D2, d2,
                                jnp.where(lane == ROUTE_G1, g1, jnp.where(lane == ROUTE_G2, g2, 0.0))))
    route_ref[...] = route
    lane_i = lax.broadcasted_iota(jnp.int32, (1, LANES), 1)
    for e in range(N_EXPERTS):
        meta_ref[b, TAB_LOFF + e] = _lane_scalar(loff, lane_i, e)
        meta_ref[b, TAB_PC + e] = _lane_scalar(pc, lane_i, e)


def _moe_route(x2d, g, w_router):
    return pl.pallas_call(
        _moe_route_kernel,
        grid=(N_MOE_BLOCKS,),
        in_specs=[
            pl.BlockSpec((MOE_BLOCK, D_MODEL), lambda b: (b, 0)),
            pl.BlockSpec((1, D_MODEL), lambda b: (0, 0)),
            pl.BlockSpec((D_MODEL, LANES), lambda b: (0, 0)),
        ],
        out_specs=[
            pl.BlockSpec((MOE_BLOCK, LANES), lambda b: (b, 0)),
            pl.BlockSpec(memory_space=pltpu.SMEM),
        ],
        out_shape=[
            jax.ShapeDtypeStruct((N_TOK, LANES), F32),
            jax.ShapeDtypeStruct((N_MOE_BLOCKS, 2 * N_EXPERTS), jnp.int32),
        ],
        compiler_params=pltpu.CompilerParams(
            dimension_semantics=("arbitrary",), vmem_limit_bytes=VMEM_LIMIT),
        name="moe_route",
    )(x2d, g, w_router)


def _moe_plan(meta):
    pc = meta[:, TAB_PC:TAB_PC + N_EXPERTS]
    totals = jnp.sum(pc, axis=0)
    per_expert = (totals + (MOE_TILE - 1)) // MOE_TILE
    ends = jnp.cumsum(per_expert)
    seg_start = (ends - per_expert) * MOE_TILE
    goff = seg_start[None, :] + jnp.cumsum(pc, axis=0) - pc
    block_tab = jnp.concatenate([meta, goff], axis=1).astype(jnp.int32)
    n_tiles = ends[-1]
    t = jnp.minimum(jnp.arange(MOE_MAX_TILES, dtype=jnp.int32), n_tiles - 1)
    expert = jnp.sum((t[:, None] >= ends[None, :]).astype(jnp.int32), axis=1)
    tile_tab = jnp.stack([expert, jnp.full_like(t, n_tiles)]).astype(jnp.int32)
    fill_tab = jnp.stack([seg_start + totals, ends * MOE_TILE]).astype(jnp.int32)
    return block_tab, tile_tab, fill_tab


def _moe_dispatch_kernel(tab_ref, tile_ref, fill_ref, x_ref, g_ref, route_ref, hs_hbm, hloc_ref, zero_ref, sem):
    b = pl.program_id(0)
    slot = b % 2
    hloc = hloc_ref.at[slot]
    hb = _rmsnorm(x_ref[...], g_ref[...]).astype(BF16)

    route_t = route_ref[...].T
    r_iota = lax.broadcasted_iota(jnp.int32, (MOE_LOCAL_ROWS, MOE_BLOCK), 0).astype(F32)
    hit = jnp.logical_or(r_iota == route_t[ROUTE_D1:ROUTE_D1 + 1, :], r_iota == route_t[ROUTE_D2:ROUTE_D2 + 1, :])
    perm = jnp.where(hit, 1.0, 0.0).astype(BF16)
    hloc_ref[slot] = jnp.dot(perm, hb, preferred_element_type=F32).astype(BF16)

    def granule(ref, row):
        return ref.at[pl.ds(pl.multiple_of(row, MOE_GRAN), MOE_GRAN)]

    for e in range(N_EXPERTS):
        lo = tab_ref[b, TAB_LOFF + e]
        goff = tab_ref[b, TAB_GOFF + e]

        def start(gi, _, lo=lo, goff=goff):
            pltpu.make_async_copy(granule(hloc, lo + gi * MOE_GRAN),
                                  granule(hs_hbm, goff + gi * MOE_GRAN), sem.at[slot]).start()
            return 0

        lax.fori_loop(0, tab_ref[b, TAB_PC + e] // MOE_GRAN, start, 0)

    def wait_block(blk, slot_):
        n = 0
        for e in range(N_EXPERTS):
            n = n + tab_ref[blk, TAB_PC + e] // MOE_GRAN

        def wait_one(gi, _):
            pltpu.make_async_copy(granule(hloc_ref.at[slot_], 0), granule(hs_hbm, 0), sem.at[slot_]).wait()
            return 0

        lax.fori_loop(0, n, wait_one, 0)

    @pl.when(b > 0)
    def _():
        wait_block(jnp.maximum(b - 1, 0), 1 - slot)

    @pl.when(b == pl.num_programs(0) - 1)
    def _():
        wait_block(b, slot)
        zero_ref[...] = jnp.zeros_like(zero_ref)
        n_fill = 0
        for e in range(N_EXPERTS):
            first = fill_ref[0, e]
            n_e = (fill_ref[1, e] - first) // MOE_GRAN

            def fill(gi, _, first=first):
                pltpu.make_async_copy(granule(zero_ref, 0), granule(hs_hbm, first + gi * MOE_GRAN),
                                      sem.at[slot]).start()
                return 0

            lax.fori_loop(0, n_e, fill, 0)
            n_fill = n_fill + n_e

        def wait_fill(gi, _):
            pltpu.make_async_copy(granule(zero_ref, 0), granule(hs_hbm, 0), sem.at[slot]).wait()
            return 0

        lax.fori_loop(0, n_fill, wait_fill, 0)

        def tile_at(ti):
            return hs_hbm.at[pl.ds(pl.multiple_of(ti * MOE_TILE, MOE_TILE), MOE_TILE)]

        n_tiles = tile_ref[1, 0]

        def fill_tile(ti, _):
            pltpu.make_async_copy(zero_ref, tile_at(ti), sem.at[slot]).start()
            return 0

        def wait_tile(ti, _):
            pltpu.make_async_copy(zero_ref, tile_at(0), sem.at[slot]).wait()
            return 0

        lax.fori_loop(n_tiles, MOE_MAX_TILES, fill_tile, 0)
        lax.fori_loop(n_tiles, MOE_MAX_TILES, wait_tile, 0)


def _moe_dispatch(block_tab, tile_tab, fill_tab, x2d, g, route):
    grid_spec = pltpu.PrefetchScalarGridSpec(
        num_scalar_prefetch=3,
        grid=(N_MOE_BLOCKS,),
        in_specs=[
            pl.BlockSpec((MOE_BLOCK, D_MODEL), lambda b, *_: (b, 0)),
            pl.BlockSpec((1, D_MODEL), lambda b, *_: (0, 0)),
            pl.BlockSpec((MOE_BLOCK, LANES), lambda b, *_: (b, 0)),
        ],
        out_specs=pl.BlockSpec(memory_space=pl.ANY),
        scratch_shapes=[
            pltpu.VMEM((2, MOE_LOCAL_ROWS, D_MODEL), BF16),
            pltpu.VMEM((MOE_TILE, D_MODEL), BF16),
            pltpu.SemaphoreType.DMA((2,)),
        ],
    )
    return pl.pallas_call(
        _moe_dispatch_kernel,
        grid_spec=grid_spec,
        out_shape=jax.ShapeDtypeStruct((MOE_ROWS, D_MODEL), BF16),
        compiler_params=pltpu.CompilerParams(
            dimension_semantics=("arbitrary",), vmem_limit_bytes=VMEM_LIMIT),
        name="moe_dispatch",
    )(block_tab, tile_tab, fill_tab, x2d, g, route)


def _moe_expert_kernel(tile_ref, hs_ref, wg_ref, wu_ref, wd_ref, y_ref, acc_ref):
    t = pl.program_id(0)
    c = pl.program_id(1)
    active = t < tile_ref[1, 0]

    @pl.when(jnp.logical_and(jnp.logical_not(active), c == 0))
    def _():
        y_ref[...] = jnp.zeros_like(y_ref)

    @pl.when(active)
    def _():
        @pl.when(c == 0)
        def _():
            acc_ref[...] = jnp.zeros_like(acc_ref)

        h = hs_ref[...]
        gate = jnp.dot(h, wg_ref[0].astype(BF16), preferred_element_type=F32)
        up = jnp.dot(h, wu_ref[0].astype(BF16), preferred_element_type=F32)
        a = (gate * jax.nn.sigmoid(gate) * up).astype(BF16)
        acc_ref[...] += jnp.dot(a, wd_ref[0].astype(BF16), preferred_element_type=F32)

        @pl.when(c == pl.num_programs(1) - 1)
        def _():
            y_ref[...] = acc_ref[...].astype(BF16)


def _moe_experts(tile_tab, hs, w_gu, w_down, fc=512):
    nc = FFN_DIM // fc

    def tile_of(t, tr):
        return jnp.maximum(jnp.minimum(t, tr[1, 0] - 1), 0)

    def chunk_of(t, c, tr):
        return jnp.where(t < tr[1, 0], c, nc - 1)

    grid_spec = pltpu.PrefetchScalarGridSpec(
        num_scalar_prefetch=1,
        grid=(MOE_MAX_TILES, nc),
        in_specs=[
            pl.BlockSpec((MOE_TILE, D_MODEL), lambda t, c, tr: (tile_of(t, tr), 0)),
            pl.BlockSpec((1, D_MODEL, fc), lambda t, c, tr: (tr[0, t], 0, chunk_of(t, c, tr))),
            pl.BlockSpec((1, D_MODEL, fc), lambda t, c, tr: (tr[0, t], 0, chunk_of(t, c, tr) + nc)),
            pl.BlockSpec((1, fc, D_MODEL), lambda t, c, tr: (tr[0, t], chunk_of(t, c, tr), 0)),
        ],
        out_specs=pl.BlockSpec((MOE_TILE, D_MODEL), lambda t, c, tr: (t, 0)),
        scratch_shapes=[pltpu.VMEM((MOE_TILE, D_MODEL), F32)],
    )
    return pl.pallas_call(
        _moe_expert_kernel,
        grid_spec=grid_spec,
        out_shape=jax.ShapeDtypeStruct((MOE_ROWS, D_MODEL), BF16),
        compiler_params=pltpu.CompilerParams(
            dimension_semantics=("arbitrary", "arbitrary"), vmem_limit_bytes=VMEM_LIMIT),
        name="moe_experts",
    )(tile_tab, hs, w_gu, w_gu, w_down)


def _moe_combine_kernel(meta_ref, x_ref, route_ref, y_hbm, gf_ref, out_ref, yloc_ref, sem):
    b = pl.program_id(0)
    slot = b % 2

    def granule(ref, row):
        return ref.at[pl.ds(pl.multiple_of(row, MOE_GRAN), MOE_GRAN)]

    def start_block(blk, slot_):
        for e in range(N_EXPERTS):
            lo = meta_ref[blk, TAB_LOFF + e]
            goff = meta_ref[blk, TAB_GOFF + e]

            def start(gi, _, lo=lo, goff=goff):
                pltpu.make_async_copy(granule(y_hbm, goff + gi * MOE_GRAN),
                                      granule(yloc_ref.at[slot_], lo + gi * MOE_GRAN), sem.at[slot_]).start()
                return 0

            lax.fori_loop(0, meta_ref[blk, TAB_PC + e] // MOE_GRAN, start, 0)

    @pl.when(b == 0)
    def _():
        yloc_ref[...] = jnp.zeros_like(yloc_ref)
        start_block(0, 0)

    @pl.when(b + 1 < pl.num_programs(0))
    def _():
        start_block(jnp.minimum(b + 1, pl.num_programs(0) - 1), 1 - slot)

    n_mine = 0
    for e in range(N_EXPERTS):
        n_mine = n_mine + meta_ref[b, TAB_PC + e] // MOE_GRAN

    def wait_one(gi, _):
        pltpu.make_async_copy(granule(y_hbm, 0), granule(yloc_ref.at[slot], 0), sem.at[slot]).wait()
        return 0

    lax.fori_loop(0, n_mine, wait_one, 0)

    route = route_ref[...]
    col = lax.broadcasted_iota(jnp.int32, (MOE_BLOCK, MOE_LOCAL_ROWS), 1).astype(F32)
    pick1 = jnp.where(col == route[:, ROUTE_D1:ROUTE_D1 + 1], 1.0, 0.0).astype(BF16)
    pick2 = jnp.where(col == route[:, ROUTE_D2:ROUTE_D2 + 1], 1.0, 0.0).astype(BF16)
    yl = yloc_ref[slot]
    moe = (route[:, ROUTE_G1:ROUTE_G1 + 1] * jnp.dot(pick1, yl, preferred_element_type=F32)
           + route[:, ROUTE_G2:ROUTE_G2 + 1] * jnp.dot(pick2, yl, preferred_element_type=F32))
    out_ref[...] = _rmsnorm(x_ref[...] + moe, gf_ref[...])


def _moe_combine(meta, x2d, route, y, g_final):
    grid_spec = pltpu.PrefetchScalarGridSpec(
        num_scalar_prefetch=1,
        grid=(N_MOE_BLOCKS,),
        in_specs=[
            pl.BlockSpec((MOE_BLOCK, D_MODEL), lambda b, m: (b, 0)),
            pl.BlockSpec((MOE_BLOCK, LANES), lambda b, m: (b, 0)),
            pl.BlockSpec(memory_space=pl.ANY),
            pl.BlockSpec((1, D_MODEL), lambda b, m: (0, 0)),
        ],
        out_specs=pl.BlockSpec((MOE_BLOCK, D_MODEL), lambda b, m: (b, 0)),
        scratch_shapes=[
            pltpu.VMEM((2, MOE_LOCAL_ROWS, D_MODEL), BF16),
            pltpu.SemaphoreType.DMA((2,)),
        ],
    )
    return pl.pallas_call(
        _moe_combine_kernel,
        grid_spec=grid_spec,
        out_shape=jax.ShapeDtypeStruct((N_TOK, D_MODEL), F32),
        compiler_params=pltpu.CompilerParams(
            dimension_semantics=("arbitrary",), vmem_limit_bytes=VMEM_LIMIT),
        name="moe_combine",
    )(meta, x2d, route, y, g_final)


def _moe(x2d, g, w_router, w_gu, w_down, g_final):
    route, meta = _moe_route(x2d, g, w_router)
    block_tab, tile_tab, fill_tab = _moe_plan(meta)
    hs = _moe_dispatch(block_tab, tile_tab, fill_tab, x2d, g, route)
    y = _moe_experts(tile_tab, hs, w_gu, w_down)
    return _moe_combine(block_tab, x2d, route, y, g_final)


def _rope_tables():
    def tables(dim):
        inv_freq = ROPE_THETA ** (-jnp.arange(0, dim, 2, dtype=F32) / dim)
        ang = jnp.arange(SEQ, dtype=F32)[:, None] * inv_freq[None, :]
        return jnp.cos(ang), jnp.sin(ang)

    c, s = tables(HEAD_DIM)
    cq = jnp.concatenate([c, c], axis=1)
    sq = jnp.concatenate([-s, s], axis=1)
    c, s = tables(IDX_ROPE_DIM)
    ones = jnp.ones((SEQ, IDX_DIM - IDX_ROPE_DIM), F32)
    zeros = jnp.zeros_like(ones)
    z16 = jnp.zeros_like(s)
    ci = jnp.concatenate([c, c, ones], axis=1)
    sia = jnp.concatenate([-s, z16, zeros], axis=1)
    sib = jnp.concatenate([z16, s, zeros], axis=1)
    rep = LANES // IDX_DIM
    return cq, sq, jnp.tile(ci, (1, rep)), jnp.tile(sia, (1, rep)), jnp.tile(sib, (1, rep))


def kernel(x, norm_mix, norm_ffn, dsa_w_in, dsa_idx_k_gain, dsa_w_out, ffn_w_gu, ffn_w_down, gmlp_w_in,
           gmlp_ln_gain, gmlp_ln_bias, gmlp_w_spatial, gmlp_b_spatial, gmlp_w_out, moe_w_router, moe_w_gu,
           moe_w_down, final_norm):
    x2d = x.reshape(N_TOK, D_MODEL)
    row = lambda v: v.reshape(1, -1)

    w_in = dsa_w_in[0]
    w_all = jnp.pad(w_in, ((0, 0), (0, DSA_COLS - w_in.shape[1])))
    wwi_t = w_in[:, COL_KI + IDX_DIM:COL_KI + IDX_DIM + IDX_HEADS].T
    kig = jnp.pad(dsa_idx_k_gain[0], (0, LANES - IDX_DIM)).reshape(1, LANES)
    q3, k, vt, kia, kib, qi, wt = _dsa_proj(x2d, row(norm_mix[0]), w_all, wwi_t, kig, _rope_tables())
    o3 = _dsa_attn(q3, k, vt, kia, kib, qi, wt)
    x2d = _resid_proj(x2d, o3, dsa_w_out[0])
    x2d = _ffn(x2d, row(norm_ffn[0]), ffn_w_gu[0], ffn_w_down[0])

    gw = GMLP_WIDTH // GMLP_GROUPS
    b_sp_full = jnp.repeat(gmlp_b_spatial[0].T, gw, axis=1)
    x2d = _gmlp(x2d, row(norm_mix[1]), gmlp_w_in[0], row(gmlp_ln_gain[0]), row(gmlp_ln_bias[0]),
                gmlp_w_spatial[0], b_sp_full, gmlp_w_out[0])
    w_router = jnp.pad(moe_w_router[0], ((0, 0), (0, LANES - N_EXPERTS)))
    out = _moe(x2d, row(norm_ffn[1]), w_router, moe_w_gu[0], moe_w_down[0], row(final_norm))
    return out.reshape(BATCH, SEQ, D_MODEL)
```

```python
import functools

import numpy as np
import jax
import jax.numpy as jnp
from jax import lax
from jax.experimental import pallas as pl
from jax.experimental.pallas import tpu as pltpu

F32 = jnp.float32
BF16 = jnp.bfloat16

D_MODEL = 1024
BATCH = 8
SEQ = 2048
N_TOK = BATCH * SEQ
N_HEADS = 8
HEAD_DIM = 128
IDX_HEADS = 8
IDX_DIM = 64
IDX_ROPE_DIM = 32
INDEX_TOPK = 256
Q_BLOCK = 128
KEY_SUPER = 512
ROPE_THETA = 10000.0
GMLP_WIDTH = 1024
GMLP_GROUPS = 8
GMLP_CHUNK = 128
FFN_DIM = 3584
N_EXPERTS = 8
NORM_EPS = 1e-6

LANES = 128
SUBLANES = 8
VMEM_LIMIT = 56 * 1024 * 1024

DSA_COLS = 1920
COL_K = N_HEADS * HEAD_DIM
COL_V = COL_K + HEAD_DIM
COL_QI = COL_V + HEAD_DIM
COL_KI = COL_QI + IDX_HEADS * IDX_DIM

NT_DIMS = (((1,), (1,)), ((), ()))
MASK_BIAS = -1e30
SCORE_MASKED = -3.0e38


def _rmsnorm(x, g):
    ms = jnp.mean(x * x, axis=-1, keepdims=True)
    return x * lax.rsqrt(ms + NORM_EPS) * g


def _cast_once(src_ref, dst_ref):
    @pl.when(pl.program_id(0) == 0)
    def _():
        dst_ref[...] = src_ref[...].astype(BF16)


def _resident(shape):
    return pl.BlockSpec(shape, lambda i: (0,) * len(shape), pipeline_mode=pl.Buffered(1))


def _dsa_proj_kernel(x_ref, g_ref, w_ref, wwi_ref, kig_ref, cq_ref, sq_ref, ci_ref, sia_ref, sib_ref,
                     q_ref, k_ref, vt_ref, kia_ref, kib_ref, qi_ref, wt_ref, wbf_ref):
    _cast_once(w_ref, wbf_ref)
    h = _rmsnorm(x_ref[...], g_ref[...]).astype(BF16)
    y = jnp.dot(h, wbf_ref[...], preferred_element_type=F32)
    cq = cq_ref[...]
    sq = sq_ref[...]
    ci = ci_ref[...]
    sia = sia_ref[...]
    sib = sib_ref[...]

    def rope_head(t):
        return t * cq + pltpu.roll(t, HEAD_DIM // 2, 1) * sq

    def rope_idx(t):
        half = IDX_ROPE_DIM // 2
        return t * ci + pltpu.roll(t, LANES - half, 1) * sia + pltpu.roll(t, half, 1) * sib

    q_scale = HEAD_DIM ** -0.5 * np.log2(np.e)
    for hh in range(N_HEADS):
        q_ref[hh] = (rope_head(y[:, hh * HEAD_DIM:(hh + 1) * HEAD_DIM]) * q_scale).astype(BF16)
    k_ref[...] = rope_head(y[:, COL_K:COL_V]).astype(BF16)
    vt_ref[0] = y[:, COL_V:COL_QI].T.astype(BF16)
    for p in range(IDX_HEADS * IDX_DIM // LANES):
        qi_ref[:, p * LANES:(p + 1) * LANES] = rope_idx(
            y[:, COL_QI + p * LANES:COL_QI + (p + 1) * LANES]).astype(BF16)

    last = y[:, COL_KI:COL_KI + LANES]
    lane = lax.broadcasted_iota(jnp.int32, last.shape, 1)
    kraw = jnp.where(lane < IDX_DIM, last, 0.0)
    kms = jnp.sum(kraw * kraw, axis=-1, keepdims=True) * (1.0 / IDX_DIM)
    kn = kraw * lax.rsqrt(kms + NORM_EPS) * kig_ref[...]
    kr = rope_idx(kn)
    kia_ref[...] = kr.astype(BF16)
    kib_ref[...] = pltpu.roll(kr, IDX_DIM, 1).astype(BF16)

    w_scale = IDX_HEADS ** -0.5 * IDX_DIM ** -0.5
    wt_ref[...] = lax.dot_general(wwi_ref[...].astype(BF16), h, NT_DIMS, preferred_element_type=F32) * w_scale


def _dsa_proj(x2d, g, w_all, wwi_t, kig, tabs, tm=KEY_SUPER):
    n = x2d.shape[0]
    nblk_seq = SEQ // tm
    tab_spec = pl.BlockSpec((tm, LANES), lambda i: (i % nblk_seq, 0))
    full = lambda shape: pl.BlockSpec(shape, lambda i: (0,) * len(shape))
    return pl.pallas_call(
        _dsa_proj_kernel,
        grid=(n // tm,),
        in_specs=[
            pl.BlockSpec((tm, D_MODEL), lambda i: (i, 0)),
            full((1, D_MODEL)),
            _resident((D_MODEL, DSA_COLS)),
            full((IDX_HEADS, D_MODEL)),
            full((1, LANES)),
            tab_spec, tab_spec, tab_spec, tab_spec, tab_spec,
        ],
        out_specs=[
            pl.BlockSpec((N_HEADS, tm, HEAD_DIM), lambda i: (0, i, 0)),
            pl.BlockSpec((tm, HEAD_DIM), lambda i: (i, 0)),
            pl.BlockSpec((1, HEAD_DIM, tm), lambda i: (i, 0, 0)),
            pl.BlockSpec((tm, LANES), lambda i: (i, 0)),
            pl.BlockSpec((tm, LANES), lambda i: (i, 0)),
            pl.BlockSpec((tm, IDX_HEADS * IDX_DIM), lambda i: (i, 0)),
            pl.BlockSpec((IDX_HEADS, tm), lambda i: (0, i)),
        ],
        out_shape=[
            jax.ShapeDtypeStruct((N_HEADS, n, HEAD_DIM), BF16),
            jax.ShapeDtypeStruct((n, HEAD_DIM), BF16),
            jax.ShapeDtypeStruct((n // tm, HEAD_DIM, tm), BF16),
            jax.ShapeDtypeStruct((n, LANES), BF16),
            jax.ShapeDtypeStruct((n, LANES), BF16),
            jax.ShapeDtypeStruct((n, IDX_HEADS * IDX_DIM), BF16),
            jax.ShapeDtypeStruct((IDX_HEADS, n), F32),
        ],
        scratch_shapes=[pltpu.VMEM((D_MODEL, DSA_COLS), BF16)],
        compiler_params=pltpu.CompilerParams(
            dimension_semantics=("arbitrary",), vmem_limit_bytes=VMEM_LIMIT),
        name="dsa_proj",
    )(x2d, g, w_all, wwi_t, kig, *tabs)


def _ordered_bits_to_float(u):
    key = u ^ jnp.int32(-2 ** 31)
    key = jnp.maximum(key, jnp.int32(-2139095041))
    bits = jnp.where(key >= 0, key, key ^ jnp.int32(0x7FFFFFFF))
    return lax.bitcast_convert_type(bits, F32)


def _dsa_attn_kernel(q_ref, k_ref, vt_ref, kia_ref, kib_ref, qi_ref, wt_ref, o_ref,
                     s_ref, l_ref, p_ref, acc_ref):
    j = pl.program_id(1)
    per_super = KEY_SUPER // Q_BLOCK
    n_super = j // per_super + 1
    hq = N_HEADS * Q_BLOCK

    qpos = j * Q_BLOCK + lax.broadcasted_iota(jnp.int32, (Q_BLOCK, Q_BLOCK), 1)
    krow = lax.broadcasted_iota(jnp.int32, (Q_BLOCK, Q_BLOCK), 0)

    def rows_of(sc, c4):
        return pl.ds(pl.multiple_of(sc * KEY_SUPER, KEY_SUPER) + c4 * Q_BLOCK, Q_BLOCK)

    def super_rows(sc):
        return pl.ds(pl.multiple_of(sc * KEY_SUPER, KEY_SUPER), KEY_SUPER)

    def causal(sc, c4):
        return (krow + (sc * KEY_SUPER + c4 * Q_BLOCK)) <= qpos

    def fold8(t):
        return t.reshape(t.shape[0] // SUBLANES, SUBLANES, t.shape[1])

    def score_body(sc, _):
        srows = super_rows(sc)
        ka = kia_ref[srows, :]
        kb = kib_ref[srows, :]
        for half in range(IDX_HEADS // 4):
            p0 = 2 * half
            qpp = jnp.concatenate(
                [qi_ref[:, p0 * LANES:(p0 + 1) * LANES], qi_ref[:, (p0 + 1) * LANES:(p0 + 2) * LANES]], axis=0)
            ra = lax.dot_general(ka, qpp, NT_DIMS, preferred_element_type=F32)
            rb = lax.dot_general(kb, qpp, NT_DIMS, preferred_element_type=F32)
            term = None
            for sub in range(2):
                p = p0 + sub
                cols = slice(sub * Q_BLOCK, (sub + 1) * Q_BLOCK)
                t2 = (jnp.maximum(ra[:, cols], 0.0) * wt_ref[2 * p:2 * p + 1, :]
                      + jnp.maximum(rb[:, cols], 0.0) * wt_ref[2 * p + 1:2 * p + 2, :])
                term = t2 if term is None else term + t2
            if half == 0:
                s_ref[srows, :] = term
            else:
                kpos = sc * KEY_SUPER + lax.broadcasted_iota(jnp.int32, (KEY_SUPER, Q_BLOCK), 0)
                qp_s = j * Q_BLOCK + lax.broadcasted_iota(jnp.int32, (KEY_SUPER, Q_BLOCK), 1)
                s_ref[srows, :] = jnp.where(kpos <= qp_s, s_ref[srows, :] + term, SCORE_MASKED)
        return 0

    lax.fori_loop(0, n_super, score_body, 0)

    def count_ge(cand):
        def body(sc, acc):
            for c4 in range(per_super):
                ind = jnp.where(s_ref[rows_of(sc, c4), :] >= cand, 1.0, 0.0)
                acc = acc + fold8(ind).sum(axis=0)
            return acc
        acc = lax.fori_loop(0, n_super, body, jnp.zeros((SUBLANES, LANES), F32))
        return jnp.sum(acc, axis=0, keepdims=True)

    def bit_body(i, carry):
        prefix, cbest = carry
        cu = prefix | lax.shift_left(jnp.int32(1), 31 - i)
        cnt = count_ge(_ordered_bits_to_float(cu))
        ok = cnt >= float(INDEX_TOPK)
        return jnp.where(ok, cu, prefix), jnp.where(ok, cnt, cbest)

    n_keys = (n_super * KEY_SUPER).astype(F32)
    prefix, cbest = lax.fori_loop(
        0, 32, bit_body,
        (jnp.zeros((1, LANES), jnp.int32), jnp.zeros((1, LANES), F32) + n_keys))
    tau = _ordered_bits_to_float(prefix)
    drop = cbest - float(INDEX_TOPK)

    upper = jnp.where(krow < lax.broadcasted_iota(jnp.int32, (Q_BLOCK, Q_BLOCK), 1), 1.0, 0.0).astype(BF16)
    pair_w = 2 * Q_BLOCK

    def select_logit_body(i, carry):
        later_ties, m_acc = carry
        sc = n_super - 1 - i
        biases = [None] * per_super
        for c4 in reversed(range(per_super)):
            s = s_ref[rows_of(sc, c4), :]
            eq = jnp.where(s == tau, 1.0, 0.0)
            ties_after = jnp.dot(upper, eq.astype(BF16), preferred_element_type=F32) + later_ties
            later_ties = later_ties + jnp.sum(eq, axis=0, keepdims=True)
            tie_bias = jnp.where(ties_after >= drop, 0.0, MASK_BIAS)
            bias = jnp.where(s > tau, 0.0, jnp.where(s == tau, tie_bias, MASK_BIAS))
            biases[c4] = jnp.where(causal(sc, c4), bias, MASK_BIAS)
        bias = jnp.concatenate(biases, axis=0)
        bias2 = jnp.concatenate([bias, bias], axis=1)
        srows = super_rows(sc)
        kc = k_ref[srows, :]
        parts = []
        for pr in range(N_HEADS // 2):
            qpair = q_ref[2 * pr:2 * pr + 2].reshape(pair_w, HEAD_DIM)
            lg = lax.dot_general(kc, qpair, NT_DIMS, preferred_element_type=F32) + bias2
            l_ref[srows, pr * pair_w:(pr + 1) * pair_w] = lg
            parts.append(fold8(lg).max(axis=0))
        return later_ties, jnp.maximum(m_acc, jnp.concatenate(parts, axis=1))

    _, m_acc = lax.fori_loop(
        0, n_super, select_logit_body,
        (jnp.zeros((1, LANES), F32), jnp.full((SUBLANES, hq), MASK_BIAS, F32)))
    m = jnp.max(m_acc, axis=0, keepdims=True)

    def prob_chunk(sc, d_acc):
        for c4 in range(per_super):
            rows = rows_of(sc, c4)
            parts = []
            for pr in range(N_HEADS // 2):
                cols = slice(pr * pair_w, (pr + 1) * pair_w)
                p = jnp.exp2(l_ref[rows, cols] - m[:, cols])
                p_ref[rows, cols] = p.astype(BF16)
                parts.append(fold8(p).sum(axis=0))
            d_acc = d_acc + jnp.concatenate(parts, axis=1)
        return d_acc

    def pv_chunk(sc):
        acc_ref[...] += jnp.dot(vt_ref[sc], p_ref[super_rows(sc), :], preferred_element_type=F32)

    def prob_pv_body(sc, d_acc):
        pv_chunk(sc - 1)
        return prob_chunk(sc, d_acc)

    acc_ref[...] = jnp.zeros_like(acc_ref)
    d_acc = prob_chunk(0, jnp.zeros((SUBLANES, hq), F32))
    d_acc = lax.fori_loop(1, n_super, prob_pv_body, d_acc)
    pv_chunk(n_super - 1)
    inv = 1.0 / jnp.sum(d_acc, axis=0, keepdims=True)
    for h in range(N_HEADS):
        cols = slice(h * Q_BLOCK, (h + 1) * Q_BLOCK)
        o_ref[h] = (acc_ref[:, cols] * inv[:, cols]).T.astype(BF16)


def _dsa_attn(q3, k, vt, kia, kib, qi, wt):
    nqb = SEQ // Q_BLOCK
    n_sup = SEQ // KEY_SUPER
    hq = N_HEADS * Q_BLOCK
    return pl.pallas_call(
        _dsa_attn_kernel,
        grid=(BATCH, nqb),
        in_specs=[
            pl.BlockSpec((N_HEADS, Q_BLOCK, HEAD_DIM), lambda b, j: (0, b * nqb + j, 0)),
            pl.BlockSpec((SEQ, HEAD_DIM), lambda b, j: (b, 0)),
            pl.BlockSpec((n_sup, HEAD_DIM, KEY_SUPER), lambda b, j: (b, 0, 0)),
            pl.BlockSpec((SEQ, LANES), lambda b, j: (b, 0)),
            pl.BlockSpec((SEQ, LANES), lambda b, j: (b, 0)),
            pl.BlockSpec((Q_BLOCK, IDX_HEADS * IDX_DIM), lambda b, j: (b * nqb + j, 0)),
            pl.BlockSpec((IDX_HEADS, Q_BLOCK), lambda b, j: (0, b * nqb + j)),
        ],
        out_specs=pl.BlockSpec((N_HEADS, Q_BLOCK, HEAD_DIM), lambda b, j: (0, b * nqb + j, 0)),
        out_shape=jax.ShapeDtypeStruct((N_HEADS, N_TOK, HEAD_DIM), BF16),
        scratch_shapes=[
            pltpu.VMEM((SEQ, Q_BLOCK), F32),
            pltpu.VMEM((SEQ, hq), F32),
            pltpu.VMEM((SEQ, hq), BF16),
            pltpu.VMEM((HEAD_DIM, hq), F32),
        ],
        compiler_params=pltpu.CompilerParams(
            dimension_semantics=("parallel", "arbitrary"), vmem_limit_bytes=VMEM_LIMIT),
        name="dsa_attn",
    )(q3, k, vt, kia, kib, qi, wt)


def _resid_proj_kernel(x_ref, o_ref, w_ref, out_ref, wbf_ref):
    _cast_once(w_ref, wbf_ref)
    o = jnp.concatenate([o_ref[h] for h in range(N_HEADS)], axis=1)
    out_ref[...] = x_ref[...] + jnp.dot(o, wbf_ref[...], preferred_element_type=F32)


def _resid_proj(x2d, o3, w_out, tm=512):
    n = x2d.shape[0]
    return pl.pallas_call(
        _resid_proj_kernel,
        grid=(n // tm,),
        in_specs=[
            pl.BlockSpec((tm, D_MODEL), lambda i: (i, 0)),
            pl.BlockSpec((N_HEADS, tm, HEAD_DIM), lambda i: (0, i, 0)),
            _resident((D_MODEL, D_MODEL)),
        ],
        out_specs=pl.BlockSpec((tm, D_MODEL), lambda i: (i, 0)),
        out_shape=jax.ShapeDtypeStruct((n, D_MODEL), F32),
        scratch_shapes=[pltpu.VMEM((D_MODEL, D_MODEL), BF16)],
        compiler_params=pltpu.CompilerParams(
            dimension_semantics=("arbitrary",), vmem_limit_bytes=VMEM_LIMIT),
        name="dsa_out_proj",
    )(x2d, o3, w_out)


def _ffn_kernel(x_ref, g_ref, wg_ref, wu_ref, wd_ref, out_ref, h_ref, acc_ref):
    c = pl.program_id(1)

    @pl.when(c == 0)
    def _():
        h_ref[...] = _rmsnorm(x_ref[...], g_ref[...]).astype(BF16)
        acc_ref[...] = jnp.zeros_like(acc_ref)

    h = h_ref[...]
    gate = jnp.dot(h, wg_ref[...].astype(BF16), preferred_element_type=F32)
    up = jnp.dot(h, wu_ref[...].astype(BF16), preferred_element_type=F32)
    a = (gate * jax.nn.sigmoid(gate) * up).astype(BF16)
    acc_ref[...] += jnp.dot(a, wd_ref[...].astype(BF16), preferred_element_type=F32)

    @pl.when(c == pl.num_programs(1) - 1)
    def _():
        out_ref[...] = x_ref[...] + acc_ref[...]


def _ffn(x2d, g, w_gu, w_down, tm=1024, fc=512):
    n = x2d.shape[0]
    nc = FFN_DIM // fc
    return pl.pallas_call(
        _ffn_kernel,
        grid=(n // tm, nc),
        in_specs=[
            pl.BlockSpec((tm, D_MODEL), lambda i, c: (i, 0)),
            pl.BlockSpec((1, D_MODEL), lambda i, c: (0, 0)),
            pl.BlockSpec((D_MODEL, fc), lambda i, c: (0, c)),
            pl.BlockSpec((D_MODEL, fc), lambda i, c: (0, c + nc)),
            pl.BlockSpec((fc, D_MODEL), lambda i, c: (c, 0)),
        ],
        out_specs=pl.BlockSpec((tm, D_MODEL), lambda i, c: (i, 0)),
        out_shape=jax.ShapeDtypeStruct((n, D_MODEL), F32),
        scratch_shapes=[pltpu.VMEM((tm, D_MODEL), BF16), pltpu.VMEM((tm, D_MODEL), F32)],
        compiler_params=pltpu.CompilerParams(
            dimension_semantics=("parallel", "arbitrary"), vmem_limit_bytes=VMEM_LIMIT),
        name="swiglu_ffn",
    )(x2d, g, w_gu, w_gu, w_down)


def _gelu_tanh(x):
    return 0.5 * x * (1.0 + jnp.tanh(np.sqrt(2.0 / np.pi) * (x + 0.044715 * (x * x * x))))


def _gmlp_kernel(x_ref, g_ref, win_ref, lng_ref, lnb_ref, ws_ref, bs_ref, wout_ref, out_ref,
                 winbf_ref, woutbf_ref):
    _cast_once(win_ref, winbf_ref)
    _cast_once(wout_ref, woutbf_ref)
    x = x_ref[...]
    tm = x.shape[0]
    h = _rmsnorm(x, g_ref[...]).astype(BF16)
    y = _gelu_tanh(jnp.dot(h, winbf_ref[...], preferred_element_type=F32))
    u = y[:, :GMLP_WIDTH]
    v = y[:, GMLP_WIDTH:]
    mu = jnp.mean(v, axis=-1, keepdims=True)
    vc = v - mu
    var = jnp.mean(vc * vc, axis=-1, keepdims=True)
    vb = (vc * lax.rsqrt(var + NORM_EPS) * lng_ref[...] + lnb_ref[...]).astype(BF16)

    gw = GMLP_WIDTH // GMLP_GROUPS
    row = lax.broadcasted_iota(jnp.int32, (GMLP_CHUNK, GMLP_CHUNK), 0)
    col = lax.broadcasted_iota(jnp.int32, (GMLP_CHUNK, GMLP_CHUNK), 1)
    w_tril = [jnp.where(row >= col, ws_ref[gi], 0.0).astype(BF16) for gi in range(GMLP_GROUPS)]
    rows = []
    for c in range(tm // GMLP_CHUNK):
        parts = [
            jnp.dot(w_tril[gi], vb[c * GMLP_CHUNK:(c + 1) * GMLP_CHUNK, gi * gw:(gi + 1) * gw],
                    preferred_element_type=F32)
            for gi in range(GMLP_GROUPS)
        ]
        rows.append(jnp.concatenate(parts, axis=1) + bs_ref[...])
    s = jnp.concatenate(rows, axis=0)
    z = (u * s).astype(BF16)
    out_ref[...] = x + jnp.dot(z, woutbf_ref[...], preferred_element_type=F32)


def _gmlp(x2d, g, w_in, ln_g, ln_b, w_sp, b_sp_full, w_out, tm=512):
    n = x2d.shape[0]
    full = lambda shape: pl.BlockSpec(shape, lambda i: (0,) * len(shape))
    return pl.pallas_call(
        _gmlp_kernel,
        grid=(n // tm,),
        in_specs=[
            pl.BlockSpec((tm, D_MODEL), lambda i: (i, 0)),
            full((1, D_MODEL)),
            _resident((D_MODEL, 2 * GMLP_WIDTH)),
            full((1, GMLP_WIDTH)),
            full((1, GMLP_WIDTH)),
            full((GMLP_GROUPS, GMLP_CHUNK, GMLP_CHUNK)),
            full((GMLP_CHUNK, GMLP_WIDTH)),
            _resident((GMLP_WIDTH, D_MODEL)),
        ],
        out_specs=pl.BlockSpec((tm, D_MODEL), lambda i: (i, 0)),
        out_shape=jax.ShapeDtypeStruct((n, D_MODEL), F32),
        scratch_shapes=[pltpu.VMEM((D_MODEL, 2 * GMLP_WIDTH), BF16), pltpu.VMEM((GMLP_WIDTH, D_MODEL), BF16)],
        compiler_params=pltpu.CompilerParams(
            dimension_semantics=("arbitrary",), vmem_limit_bytes=VMEM_LIMIT),
        name="gmlp",
    )(x2d, g, w_in, ln_g, ln_b, w_sp, b_sp_full, w_out)


MOE_BLOCK = 512
MOE_GRAN = 16
MOE_TILE = 1024
MOE_LOCAL_ROWS = 2 * MOE_BLOCK + N_EXPERTS * MOE_GRAN
N_MOE_BLOCKS = N_TOK // MOE_BLOCK
MOE_MAX_TILES = (2 * N_TOK + N_MOE_BLOCKS * N_EXPERTS * (MOE_GRAN - 1)) // MOE_TILE + N_EXPERTS
MOE_ROWS = MOE_MAX_TILES * MOE_TILE
ROUTE_D1, ROUTE_D2, ROUTE_G1, ROUTE_G2 = 0, 1, 2, 3
ROUTE_ROWS = N_EXPERTS
TAB_LOFF, TAB_PC, TAB_GOFF = 0, N_EXPERTS, 2 * N_EXPERTS


def _moe_route_kernel(x_ref, g_ref, wr_ref, route_ref, meta_ref):
    b = pl.program_id(0)
    t_blk = MOE_BLOCK
    hf = _rmsnorm(x_ref[...], g_ref[...])
    wr = wr_ref[...]
    h_hi = hf.astype(BF16)
    h_lo = (hf - h_hi.astype(F32)).astype(BF16)
    w_hi = wr.astype(BF16)
    w_lo = (wr - w_hi.astype(F32)).astype(BF16)
    logits = (jnp.dot(h_hi, w_hi, preferred_element_type=F32) + jnp.dot(h_hi, w_lo, preferred_element_type=F32)
              + jnp.dot(h_lo, w_hi, preferred_element_type=F32))

    lg = logits.T[0:N_EXPERTS, :]
    e_idx = lax.broadcasted_iota(jnp.int32, lg.shape, 0).astype(F32)
    neg_inf = -jnp.inf
    m1 = jnp.max(lg, axis=0, keepdims=True)
    i1 = jnp.min(jnp.where(lg == m1, e_idx, float(N_EXPERTS)), axis=0, keepdims=True)
    lg2 = jnp.where(e_idx == i1, neg_inf, lg)
    m2 = jnp.max(lg2, axis=0, keepdims=True)
    i2 = jnp.min(jnp.where(lg2 == m2, e_idx, float(N_EXPERTS)), axis=0, keepdims=True)
    t = jnp.exp(m2 - m1)
    g1 = 1.0 / (1.0 + t)
    g2 = t * g1

    memb = jnp.where(jnp.logical_or(e_idx == i1, e_idx == i2), 1.0, 0.0)
    tr = lax.broadcasted_iota(jnp.int32, (t_blk, t_blk), 0)
    tc = lax.broadcasted_iota(jnp.int32, (t_blk, t_blk), 1)
    earlier = jnp.where(tr < tc, 1.0, 0.0).astype(BF16)
    rank = jnp.dot(memb.astype(BF16), earlier, preferred_element_type=F32)
    loff_col = jnp.zeros((N_EXPERTS, 1), F32)
    e_col = lax.broadcasted_iota(jnp.int32, (N_EXPERTS, 1), 0)
    lo = jnp.int32(0)
    for e in range(N_EXPERTS):
        cnt = jnp.sum(memb[e:e + 1, :]).astype(jnp.int32)
        pce = (cnt + (MOE_GRAN - 1)) // MOE_GRAN * MOE_GRAN
        meta_ref[b, TAB_LOFF + e] = lo
        meta_ref[b, TAB_PC + e] = pce
        loff_col = jnp.where(e_col == e, lo.astype(F32), loff_col)
        lo = lo + pce
    dest = loff_col + rank
    d1 = jnp.sum(jnp.where(e_idx == i1, dest, 0.0), axis=0, keepdims=True)
    d2 = jnp.sum(jnp.where(e_idx == i2, dest, 0.0), axis=0, keepdims=True)
    r_idx = lax.broadcasted_iota(jnp.int32, lg.shape, 0)
    route_ref[...] = jnp.where(r_idx == ROUTE_D1, d1,
                               jnp.where(r_idx == ROUTE_D2, d2,
                                         jnp.where(r_idx == ROUTE_G1, g1, jnp.where(r_idx == ROUTE_G2, g2, 0.0))))


def _moe_route(x2d, g, w_router):
    return pl.pallas_call(
        _moe_route_kernel,
        grid=(N_MOE_BLOCKS,),
        in_specs=[
            pl.BlockSpec((MOE_BLOCK, D_MODEL), lambda b: (b, 0)),
            pl.BlockSpec((1, D_MODEL), lambda b: (0, 0)),
            pl.BlockSpec((D_MODEL, LANES), lambda b: (0, 0)),
        ],
        out_specs=[
            pl.BlockSpec((ROUTE_ROWS, MOE_BLOCK), lambda b: (0, b)),
            pl.BlockSpec(memory_space=pltpu.SMEM),
        ],
        out_shape=[
            jax.ShapeDtypeStruct((ROUTE_ROWS, N_TOK), F32),
            jax.ShapeDtypeStruct((N_MOE_BLOCKS, 2 * N_EXPERTS), jnp.int32),
        ],
        compiler_params=pltpu.CompilerParams(
            dimension_semantics=("arbitrary",), vmem_limit_bytes=VMEM_LIMIT),
        name="moe_route",
    )(x2d, g, w_router)


def _moe_plan(meta):
    pc = meta[:, TAB_PC:TAB_PC + N_EXPERTS]
    totals = jnp.sum(pc, axis=0)
    per_expert = (totals + (MOE_TILE - 1)) // MOE_TILE
    ends = jnp.cumsum(per_expert)
    seg_start = (ends - per_expert) * MOE_TILE
    goff = seg_start[None, :] + jnp.cumsum(pc, axis=0) - pc
    block_tab = jnp.concatenate([meta, goff], axis=1).astype(jnp.int32)
    n_tiles = ends[-1]
    t = jnp.minimum(jnp.arange(MOE_MAX_TILES, dtype=jnp.int32), n_tiles - 1)
    expert = jnp.sum((t[:, None] >= ends[None, :]).astype(jnp.int32), axis=1)
    used = jnp.clip(totals[expert] - (t - (ends - per_expert)[expert]) * MOE_TILE, 0, MOE_TILE)
    tile_tab = jnp.stack([expert, jnp.full_like(t, n_tiles), used]).astype(jnp.int32)
    fill_tab = jnp.stack([seg_start + totals, ends * MOE_TILE]).astype(jnp.int32)
    return block_tab, tile_tab, fill_tab


def _moe_dispatch_kernel(tab_ref, tile_ref, fill_ref, x_ref, g_ref, route_ref, hs_hbm, hloc_ref, zero_ref, sem):
    b = pl.program_id(0)
    slot = b % 2
    hloc = hloc_ref.at[slot]
    hb = _rmsnorm(x_ref[...], g_ref[...]).astype(BF16)

    route_t = route_ref[...]
    r_iota = lax.broadcasted_iota(jnp.int32, (MOE_LOCAL_ROWS, MOE_BLOCK), 0).astype(F32)
    hit = jnp.logical_or(r_iota == route_t[ROUTE_D1:ROUTE_D1 + 1, :], r_iota == route_t[ROUTE_D2:ROUTE_D2 + 1, :])
    perm = jnp.where(hit, 1.0, 0.0).astype(BF16)
    hloc_ref[slot] = jnp.dot(perm, hb, preferred_element_type=F32).astype(BF16)

    def granule(ref, row):
        return ref.at[pl.ds(pl.multiple_of(row, MOE_GRAN), MOE_GRAN)]

    for e in range(N_EXPERTS):
        lo = tab_ref[b, TAB_LOFF + e]
        goff = tab_ref[b, TAB_GOFF + e]

        def start(gi, _, lo=lo, goff=goff):
            pltpu.make_async_copy(granule(hloc, lo + gi * MOE_GRAN),
                                  granule(hs_hbm, goff + gi * MOE_GRAN), sem.at[slot]).start()
            return 0

        lax.fori_loop(0, tab_ref[b, TAB_PC + e] // MOE_GRAN, start, 0)

    def wait_block(blk, slot_):
        n = 0
        for e in range(N_EXPERTS):
            n = n + tab_ref[blk, TAB_PC + e] // MOE_GRAN

        def wait_one(gi, _):
            pltpu.make_async_copy(granule(hloc_ref.at[slot_], 0), granule(hs_hbm, 0), sem.at[slot_]).wait()
            return 0

        lax.fori_loop(0, n, wait_one, 0)

    @pl.when(b > 0)
    def _():
        wait_block(jnp.maximum(b - 1, 0), 1 - slot)

    @pl.when(b == pl.num_programs(0) - 1)
    def _():
        wait_block(b, slot)
        zero_ref[...] = jnp.zeros_like(zero_ref)
        n_fill = 0
        for e in range(N_EXPERTS):
            first = fill_ref[0, e]
            n_e = (fill_ref[1, e] - first) // MOE_GRAN

            def fill(gi, _, first=first):
                pltpu.make_async_copy(granule(zero_ref, 0), granule(hs_hbm, first + gi * MOE_GRAN),
                                      sem.at[slot]).start()
                return 0

            lax.fori_loop(0, n_e, fill, 0)
            n_fill = n_fill + n_e

        def wait_fill(gi, _):
            pltpu.make_async_copy(granule(zero_ref, 0), granule(hs_hbm, 0), sem.at[slot]).wait()
            return 0

        lax.fori_loop(0, n_fill, wait_fill, 0)

        def tile_at(ti):
            return hs_hbm.at[pl.ds(pl.multiple_of(ti * MOE_TILE, MOE_TILE), MOE_TILE)]

        n_tiles = tile_ref[1, 0]

        def fill_tile(ti, _):
            pltpu.make_async_copy(zero_ref, tile_at(ti), sem.at[slot]).start()
            return 0

        def wait_tile(ti, _):
            pltpu.make_async_copy(zero_ref, tile_at(0), sem.at[slot]).wait()
            return 0

        lax.fori_loop(n_tiles, MOE_MAX_TILES, fill_tile, 0)
        lax.fori_loop(n_tiles, MOE_MAX_TILES, wait_tile, 0)


def _moe_dispatch(block_tab, tile_tab, fill_tab, x2d, g, route):
    grid_spec = pltpu.PrefetchScalarGridSpec(
        num_scalar_prefetch=3,
        grid=(N_MOE_BLOCKS,),
        in_specs=[
            pl.BlockSpec((MOE_BLOCK, D_MODEL), lambda b, *_: (b, 0)),
            pl.BlockSpec((1, D_MODEL), lambda b, *_: (0, 0)),
            pl.BlockSpec((ROUTE_ROWS, MOE_BLOCK), lambda b, *_: (0, b)),
        ],
        out_specs=pl.BlockSpec(memory_space=pl.ANY),
        scratch_shapes=[
            pltpu.VMEM((2, MOE_LOCAL_ROWS, D_MODEL), BF16),
            pltpu.VMEM((MOE_TILE, D_MODEL), BF16),
            pltpu.SemaphoreType.DMA((2,)),
        ],
    )
    return pl.pallas_call(
        _moe_dispatch_kernel,
        grid_spec=grid_spec,
        out_shape=jax.ShapeDtypeStruct((MOE_ROWS, D_MODEL), BF16),
        compiler_params=pltpu.CompilerParams(
            dimension_semantics=("arbitrary",), vmem_limit_bytes=VMEM_LIMIT),
        name="moe_dispatch",
    )(block_tab, tile_tab, fill_tab, x2d, g, route)


def _moe_expert_kernel(tile_ref, hs_ref, wg_ref, wu_ref, wd_ref, y_ref, acc_ref):
    t = pl.program_id(0)
    c = pl.program_id(1)
    active = t < tile_ref[1, 0]

    @pl.when(jnp.logical_and(jnp.logical_not(active), c == 0))
    def _():
        y_ref[...] = jnp.zeros_like(y_ref)

    @pl.when(active)
    def _():
        @pl.when(c == 0)
        def _():
            acc_ref[...] = jnp.zeros_like(acc_ref)

        def swiglu_rows(n_rows):
            h = hs_ref[0:n_rows, :]
            gate = jnp.dot(h, wg_ref[0].astype(BF16), preferred_element_type=F32)
            up = jnp.dot(h, wu_ref[0].astype(BF16), preferred_element_type=F32)
            a = (gate * jax.nn.sigmoid(gate) * up).astype(BF16)
            acc_ref[0:n_rows, :] += jnp.dot(a, wd_ref[0].astype(BF16), preferred_element_type=F32)

        half_only = tile_ref[2, t] <= MOE_TILE // 2

        @pl.when(half_only)
        def _():
            swiglu_rows(MOE_TILE // 2)

        @pl.when(jnp.logical_not(half_only))
        def _():
            swiglu_rows(MOE_TILE)

        @pl.when(c == pl.num_programs(1) - 1)
        def _():
            y_ref[...] = acc_ref[...].astype(BF16)


def _moe_experts(tile_tab, hs, w_gu, w_down, fc=512):
    nc = FFN_DIM // fc

    def tile_of(t, tr):
        return jnp.maximum(jnp.minimum(t, tr[1, 0] - 1), 0)

    def chunk_of(t, c, tr):
        return jnp.where(t < tr[1, 0], c, nc - 1)

    grid_spec = pltpu.PrefetchScalarGridSpec(
        num_scalar_prefetch=1,
        grid=(MOE_MAX_TILES, nc),
        in_specs=[
            pl.BlockSpec((MOE_TILE, D_MODEL), lambda t, c, tr: (tile_of(t, tr), 0)),
            pl.BlockSpec((1, D_MODEL, fc), lambda t, c, tr: (tr[0, t], 0, chunk_of(t, c, tr))),
            pl.BlockSpec((1, D_MODEL, fc), lambda t, c, tr: (tr[0, t], 0, chunk_of(t, c, tr) + nc)),
            pl.BlockSpec((1, fc, D_MODEL), lambda t, c, tr: (tr[0, t], chunk_of(t, c, tr), 0)),
        ],
        out_specs=pl.BlockSpec((MOE_TILE, D_MODEL), lambda t, c, tr: (t, 0)),
        scratch_shapes=[pltpu.VMEM((MOE_TILE, D_MODEL), F32)],
    )
    return pl.pallas_call(
        _moe_expert_kernel,
        grid_spec=grid_spec,
        out_shape=jax.ShapeDtypeStruct((MOE_ROWS, D_MODEL), BF16),
        compiler_params=pltpu.CompilerParams(
            dimension_semantics=("arbitrary", "arbitrary"), vmem_limit_bytes=VMEM_LIMIT),
        name="moe_experts",
    )(tile_tab, hs, w_gu, w_gu, w_down)


def _moe_combine_kernel(meta_ref, x_ref, route_ref, y_hbm, gf_ref, out_ref, yloc_ref, sem):
    b = pl.program_id(0)
    slot = b % 2

    def granule(ref, row):
        return ref.at[pl.ds(pl.multiple_of(row, MOE_GRAN), MOE_GRAN)]

    def start_block(blk, slot_):
        for e in range(N_EXPERTS):
            lo = meta_ref[blk, TAB_LOFF + e]
            goff = meta_ref[blk, TAB_GOFF + e]

            def start(gi, _, lo=lo, goff=goff):
                pltpu.make_async_copy(granule(y_hbm, goff + gi * MOE_GRAN),
                                      granule(yloc_ref.at[slot_], lo + gi * MOE_GRAN), sem.at[slot_]).start()
                return 0

            lax.fori_loop(0, meta_ref[blk, TAB_PC + e] // MOE_GRAN, start, 0)

    @pl.when(b == 0)
    def _():
        yloc_ref[...] = jnp.zeros_like(yloc_ref)
        start_block(0, 0)

    @pl.when(b + 1 < pl.num_programs(0))
    def _():
        start_block(jnp.minimum(b + 1, pl.num_programs(0) - 1), 1 - slot)

    n_mine = 0
    for e in range(N_EXPERTS):
        n_mine = n_mine + meta_ref[b, TAB_PC + e] // MOE_GRAN

    def wait_one(gi, _):
        pltpu.make_async_copy(granule(y_hbm, 0), granule(yloc_ref.at[slot], 0), sem.at[slot]).wait()
        return 0

    lax.fori_loop(0, n_mine, wait_one, 0)

    route_t = route_ref[...]
    route = route_t.T
    col = lax.broadcasted_iota(jnp.int32, (MOE_BLOCK, MOE_LOCAL_ROWS), 1).astype(F32)
    pick1 = jnp.where(col == route[:, ROUTE_D1:ROUTE_D1 + 1], 1.0, 0.0).astype(BF16)
    pick2 = jnp.where(col == route[:, ROUTE_D2:ROUTE_D2 + 1], 1.0, 0.0).astype(BF16)
    yl = yloc_ref[slot]
    moe = (route[:, ROUTE_G1:ROUTE_G1 + 1] * jnp.dot(pick1, yl, preferred_element_type=F32)
           + route[:, ROUTE_G2:ROUTE_G2 + 1] * jnp.dot(pick2, yl, preferred_element_type=F32))
    out_ref[...] = _rmsnorm(x_ref[...] + moe, gf_ref[...])


def _moe_combine(meta, x2d, route, y, g_final):
    grid_spec = pltpu.PrefetchScalarGridSpec(
        num_scalar_prefetch=1,
        grid=(N_MOE_BLOCKS,),
        in_specs=[
            pl.BlockSpec((MOE_BLOCK, D_MODEL), lambda b, m: (b, 0)),
            pl.BlockSpec((ROUTE_ROWS, MOE_BLOCK), lambda b, m: (0, b)),
            pl.BlockSpec(memory_space=pl.ANY),
            pl.BlockSpec((1, D_MODEL), lambda b, m: (0, 0)),
        ],
        out_specs=pl.BlockSpec((MOE_BLOCK, D_MODEL), lambda b, m: (b, 0)),
        scratch_shapes=[
            pltpu.VMEM((2, MOE_LOCAL_ROWS, D_MODEL), BF16),
            pltpu.SemaphoreType.DMA((2,)),
        ],
    )
    return pl.pallas_call(
        _moe_combine_kernel,
        grid_spec=grid_spec,
        out_shape=jax.ShapeDtypeStruct((N_TOK, D_MODEL), F32),
        compiler_params=pltpu.CompilerParams(
            dimension_semantics=("arbitrary",), vmem_limit_bytes=VMEM_LIMIT),
        name="moe_combine",
    )(meta, x2d, route, y, g_final)


def _moe(x2d, g, w_router, w_gu, w_down, g_final):
    route, meta = _moe_route(x2d, g, w_router)
    block_tab, tile_tab, fill_tab = _moe_plan(meta)
    hs = _moe_dispatch(block_tab, tile_tab, fill_tab, x2d, g, route)
    y = _moe_experts(tile_tab, hs, w_gu, w_down)
    return _moe_combine(block_tab, x2d, route, y, g_final)


def _rope_tables():
    def tables(dim):
        inv_freq = ROPE_THETA ** (-jnp.arange(0, dim, 2, dtype=F32) / dim)
        ang = jnp.arange(SEQ, dtype=F32)[:, None] * inv_freq[None, :]
        return jnp.cos(ang), jnp.sin(ang)

    c, s = tables(HEAD_DIM)
    cq = jnp.concatenate([c, c], axis=1)
    sq = jnp.concatenate([-s, s], axis=1)
    c, s = tables(IDX_ROPE_DIM)
    ones = jnp.ones((SEQ, IDX_DIM - IDX_ROPE_DIM), F32)
    zeros = jnp.zeros_like(ones)
    z16 = jnp.zeros_like(s)
    ci = jnp.concatenate([c, c, ones], axis=1)
    sia = jnp.concatenate([-s, z16, zeros], axis=1)
    sib = jnp.concatenate([z16, s, zeros], axis=1)
    rep = LANES // IDX_DIM
    return cq, sq, jnp.tile(ci, (1, rep)), jnp.tile(sia, (1, rep)), jnp.tile(sib, (1, rep))


def kernel(x, norm_mix, norm_ffn, dsa_w_in, dsa_idx_k_gain, dsa_w_out, ffn_w_gu, ffn_w_down, gmlp_w_in,
           gmlp_ln_gain, gmlp_ln_bias, gmlp_w_spatial, gmlp_b_spatial, gmlp_w_out, moe_w_router, moe_w_gu,
           moe_w_down, final_norm):
    x2d = x.reshape(N_TOK, D_MODEL)
    row = lambda v: v.reshape(1, -1)

    w_in = dsa_w_in[0]
    w_all = jnp.pad(w_in, ((0, 0), (0, DSA_COLS - w_in.shape[1])))
    wwi_t = w_in[:, COL_KI + IDX_DIM:COL_KI + IDX_DIM + IDX_HEADS].T
    kig = jnp.pad(dsa_idx_k_gain[0], (0, LANES - IDX_DIM)).reshape(1, LANES)
    q3, k, vt, kia, kib, qi, wt = _dsa_proj(x2d, row(norm_mix[0]), w_all, wwi_t, kig, _rope_tables())
    o3 = _dsa_attn(q3, k, vt, kia, kib, qi, wt)
    x2d = _resid_proj(x2d, o3, dsa_w_out[0])
    x2d = _ffn(x2d, row(norm_ffn[0]), ffn_w_gu[0], ffn_w_down[0])

    gw = GMLP_WIDTH // GMLP_GROUPS
    b_sp_full = jnp.repeat(gmlp_b_spatial[0].T, gw, axis=1)
    x2d = _gmlp(x2d, row(norm_mix[1]), gmlp_w_in[0], row(gmlp_ln_gain[0]), row(gmlp_ln_bias[0]),
                gmlp_w_spatial[0], b_sp_full, gmlp_w_out[0])
    w_router = jnp.pad(moe_w_router[0], ((0, 0), (0, LANES - N_EXPERTS)))
    out = _moe(x2d, row(norm_ffn[1]), w_router, moe_w_gu[0], moe_w_down[0], row(final_norm))
    return out.reshape(BATCH, SEQ, D_MODEL)
```

```python
import functools

import numpy as np
import jax
import jax.numpy as jnp
from jax import lax
from jax.experimental import pallas as pl
from jax.experimental.pallas import tpu as pltpu

F32 = jnp.float32
BF16 = jnp.bfloat16

D_MODEL = 1024
BATCH = 8
SEQ = 2048
N_TOK = BATCH * SEQ
N_HEADS = 8
HEAD_DIM = 128
IDX_HEADS = 8
IDX_DIM = 64
IDX_ROPE_DIM = 32
INDEX_TOPK = 256
Q_BLOCK = 128
KEY_SUPER = 512
ROPE_THETA = 10000.0
GMLP_WIDTH = 1024
GMLP_GROUPS = 8
GMLP_CHUNK = 128
FFN_DIM = 3584
N_EXPERTS = 8
NORM_EPS = 1e-6

LANES = 128
SUBLANES = 8
VMEM_LIMIT = 56 * 1024 * 1024

DSA_COLS = 1920
COL_K = N_HEADS * HEAD_DIM
COL_V = COL_K + HEAD_DIM
COL_QI = COL_V + HEAD_DIM
COL_KI = COL_QI + IDX_HEADS * IDX_DIM

NT_DIMS = (((1,), (1,)), ((), ()))
MASK_BIAS = -1e30
SCORE_MASKED = -3.0e38


def _rmsnorm(x, g):
    ms = jnp.mean(x * x, axis=-1, keepdims=True)
    return x * lax.rsqrt(ms + NORM_EPS) * g


def _cast_once(src_ref, dst_ref):
    @pl.when(pl.program_id(0) == 0)
    def _():
        dst_ref[...] = src_ref[...].astype(BF16)


def _resident(shape):
    return pl.BlockSpec(shape, lambda i: (0,) * len(shape), pipeline_mode=pl.Buffered(1))


def _dsa_proj_kernel(x_ref, g_ref, w_ref, wwi_ref, kig_ref, cq_ref, sq_ref, ci_ref, sia_ref, sib_ref,
                     q_ref, k_ref, vt_ref, kia_ref, kib_ref, qi_ref, wt_ref, wbf_ref):
    _cast_once(w_ref, wbf_ref)
    h = _rmsnorm(x_ref[...], g_ref[...]).astype(BF16)
    y = jnp.dot(h, wbf_ref[...], preferred_element_type=F32)
    cq = cq_ref[...]
    sq = sq_ref[...]
    ci = ci_ref[...]
    sia = sia_ref[...]
    sib = sib_ref[...]

    def rope_head(t):
        return t * cq + pltpu.roll(t, HEAD_DIM // 2, 1) * sq

    def rope_idx(t):
        half = IDX_ROPE_DIM // 2
        return t * ci + pltpu.roll(t, LANES - half, 1) * sia + pltpu.roll(t, half, 1) * sib

    q_scale = HEAD_DIM ** -0.5 * np.log2(np.e)
    for hh in range(N_HEADS):
        q_ref[hh] = (rope_head(y[:, hh * HEAD_DIM:(hh + 1) * HEAD_DIM]) * q_scale).astype(BF16)
    k_ref[...] = rope_head(y[:, COL_K:COL_V]).astype(BF16)
    vt_ref[0] = y[:, COL_V:COL_QI].T.astype(BF16)
    for p in range(IDX_HEADS * IDX_DIM // LANES):
        qi_ref[:, p * LANES:(p + 1) * LANES] = rope_idx(
            y[:, COL_QI + p * LANES:COL_QI + (p + 1) * LANES]).astype(BF16)

    last = y[:, COL_KI:COL_KI + LANES]
    lane = lax.broadcasted_iota(jnp.int32, last.shape, 1)
    kraw = jnp.where(lane < IDX_DIM, last, 0.0)
    kms = jnp.sum(kraw * kraw, axis=-1, keepdims=True) * (1.0 / IDX_DIM)
    kn = kraw * lax.rsqrt(kms + NORM_EPS) * kig_ref[...]
    kr = rope_idx(kn)
    kia_ref[...] = kr.astype(BF16)
    kib_ref[...] = pltpu.roll(kr, IDX_DIM, 1).astype(BF16)

    w_scale = IDX_HEADS ** -0.5 * IDX_DIM ** -0.5
    wt_ref[...] = lax.dot_general(wwi_ref[...].astype(BF16), h, NT_DIMS, preferred_element_type=F32) * w_scale


def _dsa_proj(x2d, g, w_all, wwi_t, kig, tabs, tm=KEY_SUPER):
    n = x2d.shape[0]
    nblk_seq = SEQ // tm
    tab_spec = pl.BlockSpec((tm, LANES), lambda i: (i % nblk_seq, 0))
    full = lambda shape: pl.BlockSpec(shape, lambda i: (0,) * len(shape))
    return pl.pallas_call(
        _dsa_proj_kernel,
        grid=(n // tm,),
        in_specs=[
            pl.BlockSpec((tm, D_MODEL), lambda i: (i, 0)),
            full((1, D_MODEL)),
            _resident((D_MODEL, DSA_COLS)),
            full((IDX_HEADS, D_MODEL)),
            full((1, LANES)),
            tab_spec, tab_spec, tab_spec, tab_spec, tab_spec,
        ],
        out_specs=[
            pl.BlockSpec((N_HEADS, tm, HEAD_DIM), lambda i: (0, i, 0)),
            pl.BlockSpec((tm, HEAD_DIM), lambda i: (i, 0)),
            pl.BlockSpec((1, HEAD_DIM, tm), lambda i: (i, 0, 0)),
            pl.BlockSpec((tm, LANES), lambda i: (i, 0)),
            pl.BlockSpec((tm, LANES), lambda i: (i, 0)),
            pl.BlockSpec((tm, IDX_HEADS * IDX_DIM), lambda i: (i, 0)),
            pl.BlockSpec((IDX_HEADS, tm), lambda i: (0, i)),
        ],
        out_shape=[
            jax.ShapeDtypeStruct((N_HEADS, n, HEAD_DIM), BF16),
            jax.ShapeDtypeStruct((n, HEAD_DIM), BF16),
            jax.ShapeDtypeStruct((n // tm, HEAD_DIM, tm), BF16),
            jax.ShapeDtypeStruct((n, LANES), BF16),
            jax.ShapeDtypeStruct((n, LANES), BF16),
            jax.ShapeDtypeStruct((n, IDX_HEADS * IDX_DIM), BF16),
            jax.ShapeDtypeStruct((IDX_HEADS, n), F32),
        ],
        scratch_shapes=[pltpu.VMEM((D_MODEL, DSA_COLS), BF16)],
        compiler_params=pltpu.CompilerParams(
            dimension_semantics=("arbitrary",), vmem_limit_bytes=VMEM_LIMIT),
        name="dsa_proj",
    )(x2d, g, w_all, wwi_t, kig, *tabs)


def _ordered_bits_to_float(u):
    key = u ^ jnp.int32(-2 ** 31)
    key = jnp.maximum(key, jnp.int32(-2139095041))
    bits = jnp.where(key >= 0, key, key ^ jnp.int32(0x7FFFFFFF))
    return lax.bitcast_convert_type(bits, F32)


def _dsa_attn_kernel(q_ref, k_ref, vt_ref, kia_ref, kib_ref, qi_ref, wt_ref, o_ref,
                     s_ref, sel_ref, l_ref, p_ref, acc_ref):
    j = pl.program_id(1)
    per_super = KEY_SUPER // Q_BLOCK
    n_super = j // per_super + 1
    hq = N_HEADS * Q_BLOCK

    qpos = j * Q_BLOCK + lax.broadcasted_iota(jnp.int32, (Q_BLOCK, Q_BLOCK), 1)
    krow = lax.broadcasted_iota(jnp.int32, (Q_BLOCK, Q_BLOCK), 0)

    def rows_of(sc, c4):
        return pl.ds(pl.multiple_of(sc * KEY_SUPER, KEY_SUPER) + c4 * Q_BLOCK, Q_BLOCK)

    def super_rows(sc):
        return pl.ds(pl.multiple_of(sc * KEY_SUPER, KEY_SUPER), KEY_SUPER)

    def causal(sc, c4):
        return (krow + (sc * KEY_SUPER + c4 * Q_BLOCK)) <= qpos

    def fold8(t):
        return t.reshape(t.shape[0] // SUBLANES, SUBLANES, t.shape[1])

    def score_body(sc, _):
        srows = super_rows(sc)
        ka = kia_ref[srows, :]
        kb = kib_ref[srows, :]
        for half in range(IDX_HEADS // 4):
            p0 = 2 * half
            qpp = jnp.concatenate(
                [qi_ref[:, p0 * LANES:(p0 + 1) * LANES], qi_ref[:, (p0 + 1) * LANES:(p0 + 2) * LANES]], axis=0)
            ra = lax.dot_general(ka, qpp, NT_DIMS, preferred_element_type=F32)
            rb = lax.dot_general(kb, qpp, NT_DIMS, preferred_element_type=F32)
            term = None
            for sub in range(2):
                p = p0 + sub
                cols = slice(sub * Q_BLOCK, (sub + 1) * Q_BLOCK)
                t2 = (jnp.maximum(ra[:, cols], 0.0) * wt_ref[2 * p:2 * p + 1, :]
                      + jnp.maximum(rb[:, cols], 0.0) * wt_ref[2 * p + 1:2 * p + 2, :])
                term = t2 if term is None else term + t2
            if half == 0:
                s_ref[srows, :] = term
            else:
                kpos = sc * KEY_SUPER + lax.broadcasted_iota(jnp.int32, (KEY_SUPER, Q_BLOCK), 0)
                qp_s = j * Q_BLOCK + lax.broadcasted_iota(jnp.int32, (KEY_SUPER, Q_BLOCK), 1)
                s_ref[srows, :] = jnp.where(kpos <= qp_s, s_ref[srows, :] + term, SCORE_MASKED)
        return 0

    lax.fori_loop(0, n_super, score_body, 0)

    def search(n_static):
        def count_ge(cand):
            accs = [jnp.zeros((SUBLANES, LANES), F32), jnp.zeros((SUBLANES, LANES), F32)]
            for c in range(n_static * per_super):
                ind = jnp.where(s_ref[pl.ds(c * Q_BLOCK, Q_BLOCK), :] >= cand, 1.0, 0.0)
                accs[c % 2] = accs[c % 2] + fold8(ind).sum(axis=0)
            return jnp.sum(accs[0] + accs[1], axis=0, keepdims=True)

        def bit_body(i, carry):
            prefix, cbest = carry
            cu = prefix | lax.shift_left(jnp.int32(1), 31 - i)
            cnt = count_ge(_ordered_bits_to_float(cu))
            ok = cnt >= float(INDEX_TOPK)
            return jnp.where(ok, cu, prefix), jnp.where(ok, cnt, cbest)

        prefix, cbest = lax.fori_loop(
            0, 32, bit_body,
            (jnp.zeros((1, LANES), jnp.int32), jnp.full((1, LANES), float(n_static * KEY_SUPER), F32)))
        sel_ref[0:1, :] = _ordered_bits_to_float(prefix)
        sel_ref[1:2, :] = cbest

    for n_static in range(1, SEQ // KEY_SUPER + 1):
        pl.when(n_super == n_static)(functools.partial(search, n_static))
    tau = sel_ref[0:1, :]
    drop = sel_ref[1:2, :] - float(INDEX_TOPK)

    upper = jnp.where(krow < lax.broadcasted_iota(jnp.int32, (Q_BLOCK, Q_BLOCK), 1), 1.0, 0.0).astype(BF16)
    pair_w = 2 * Q_BLOCK

    def select_logit_body(i, carry):
        later_ties, m_acc = carry
        sc = n_super - 1 - i
        biases = [None] * per_super
        for c4 in reversed(range(per_super)):
            s = s_ref[rows_of(sc, c4), :]
            eq = jnp.where(s == tau, 1.0, 0.0)
            ties_after = jnp.dot(upper, eq.astype(BF16), preferred_element_type=F32) + later_ties
            later_ties = later_ties + jnp.sum(eq, axis=0, keepdims=True)
            tie_bias = jnp.where(ties_after >= drop, 0.0, MASK_BIAS)
            bias = jnp.where(s > tau, 0.0, jnp.where(s == tau, tie_bias, MASK_BIAS))
            biases[c4] = jnp.where(causal(sc, c4), bias, MASK_BIAS)
        bias = jnp.concatenate(biases, axis=0)
        bias2 = jnp.concatenate([bias, bias], axis=1)
        srows = super_rows(sc)
        kc = k_ref[srows, :]
        parts = []
        for pr in range(N_HEADS // 2):
            qpair = q_ref[2 * pr:2 * pr + 2].reshape(pair_w, HEAD_DIM)
            lg = lax.dot_general(kc, qpair, NT_DIMS, preferred_element_type=F32) + bias2
            l_ref[srows, pr * pair_w:(pr + 1) * pair_w] = lg
            parts.append(fold8(lg).max(axis=0))
        return later_ties, jnp.maximum(m_acc, jnp.concatenate(parts, axis=1))

    _, m_acc = lax.fori_loop(
        0, n_super, select_logit_body,
        (jnp.zeros((1, LANES), F32), jnp.full((SUBLANES, hq), MASK_BIAS, F32)))
    m = jnp.max(m_acc, axis=0, keepdims=True)

    def prob_chunk(sc, d_acc):
        for c4 in range(per_super):
            rows = rows_of(sc, c4)
            parts = []
            for pr in range(N_HEADS // 2):
                cols = slice(pr * pair_w, (pr + 1) * pair_w)
                p = jnp.exp2(l_ref[rows, cols] - m[:, cols])
                p_ref[rows, cols] = p.astype(BF16)
                parts.append(fold8(p).sum(axis=0))
            d_acc = d_acc + jnp.concatenate(parts, axis=1)
        return d_acc

    def pv_chunk(sc):
        acc_ref[...] += jnp.dot(vt_ref[sc], p_ref[super_rows(sc), :], preferred_element_type=F32)

    def prob_pv_body(sc, d_acc):
        pv_chunk(sc - 1)
        return prob_chunk(sc, d_acc)

    acc_ref[...] = jnp.zeros_like(acc_ref)
    d_acc = prob_chunk(0, jnp.zeros((SUBLANES, hq), F32))
    d_acc = lax.fori_loop(1, n_super, prob_pv_body, d_acc)
    pv_chunk(n_super - 1)
    inv = 1.0 / jnp.sum(d_acc, axis=0, keepdims=True)
    for h in range(N_HEADS):
        cols = slice(h * Q_BLOCK, (h + 1) * Q_BLOCK)
        o_ref[h] = (acc_ref[:, cols] * inv[:, cols]).T.astype(BF16)


def _dsa_attn(q3, k, vt, kia, kib, qi, wt):
    nqb = SEQ // Q_BLOCK
    n_sup = SEQ // KEY_SUPER
    hq = N_HEADS * Q_BLOCK
    return pl.pallas_call(
        _dsa_attn_kernel,
        grid=(BATCH, nqb),
        in_specs=[
            pl.BlockSpec((N_HEADS, Q_BLOCK, HEAD_DIM), lambda b, j: (0, b * nqb + j, 0)),
            pl.BlockSpec((SEQ, HEAD_DIM), lambda b, j: (b, 0)),
            pl.BlockSpec((n_sup, HEAD_DIM, KEY_SUPER), lambda b, j: (b, 0, 0)),
            pl.BlockSpec((SEQ, LANES), lambda b, j: (b, 0)),
            pl.BlockSpec((SEQ, LANES), lambda b, j: (b, 0)),
            pl.BlockSpec((Q_BLOCK, IDX_HEADS * IDX_DIM), lambda b, j: (b * nqb + j, 0)),
            pl.BlockSpec((IDX_HEADS, Q_BLOCK), lambda b, j: (0, b * nqb + j)),
        ],
        out_specs=pl.BlockSpec((N_HEADS, Q_BLOCK, HEAD_DIM), lambda b, j: (0, b * nqb + j, 0)),
        out_shape=jax.ShapeDtypeStruct((N_HEADS, N_TOK, HEAD_DIM), BF16),
        scratch_shapes=[
            pltpu.VMEM((SEQ, Q_BLOCK), F32),
            pltpu.VMEM((SUBLANES, Q_BLOCK), F32),
            pltpu.VMEM((SEQ, hq), F32),
            pltpu.VMEM((SEQ, hq), BF16),
            pltpu.VMEM((HEAD_DIM, hq), F32),
        ],
        compiler_params=pltpu.CompilerParams(
            dimension_semantics=("parallel", "arbitrary"), vmem_limit_bytes=VMEM_LIMIT),
        name="dsa_attn",
    )(q3, k, vt, kia, kib, qi, wt)


def _resid_proj_kernel(x_ref, o_ref, w_ref, out_ref, wbf_ref):
    _cast_once(w_ref, wbf_ref)
    o = jnp.concatenate([o_ref[h] for h in range(N_HEADS)], axis=1)
    out_ref[...] = x_ref[...] + jnp.dot(o, wbf_ref[...], preferred_element_type=F32)


def _resid_proj(x2d, o3, w_out, tm=1024):
    n = x2d.shape[0]
    return pl.pallas_call(
        _resid_proj_kernel,
        grid=(n // tm,),
        in_specs=[
            pl.BlockSpec((tm, D_MODEL), lambda i: (i, 0)),
            pl.BlockSpec((N_HEADS, tm, HEAD_DIM), lambda i: (0, i, 0)),
            _resident((D_MODEL, D_MODEL)),
        ],
        out_specs=pl.BlockSpec((tm, D_MODEL), lambda i: (i, 0)),
        out_shape=jax.ShapeDtypeStruct((n, D_MODEL), F32),
        scratch_shapes=[pltpu.VMEM((D_MODEL, D_MODEL), BF16)],
        compiler_params=pltpu.CompilerParams(
            dimension_semantics=("arbitrary",), vmem_limit_bytes=VMEM_LIMIT),
        name="dsa_out_proj",
    )(x2d, o3, w_out)


def _ffn_kernel(x_ref, g_ref, wg_ref, wu_ref, wd_ref, out_ref, h_ref, acc_ref):
    c = pl.program_id(1)

    @pl.when(c == 0)
    def _():
        h_ref[...] = _rmsnorm(x_ref[...], g_ref[...]).astype(BF16)
        acc_ref[...] = jnp.zeros_like(acc_ref)

    h = h_ref[...]
    gate = jnp.dot(h, wg_ref[...].astype(BF16), preferred_element_type=F32)
    up = jnp.dot(h, wu_ref[...].astype(BF16), preferred_element_type=F32)
    a = (gate * jax.nn.sigmoid(gate) * up).astype(BF16)
    acc_ref[...] += jnp.dot(a, wd_ref[...].astype(BF16), preferred_element_type=F32)

    @pl.when(c == pl.num_programs(1) - 1)
    def _():
        out_ref[...] = x_ref[...] + acc_ref[...]


def _ffn(x2d, g, w_gu, w_down, tm=1024, fc=512):
    n = x2d.shape[0]
    nc = FFN_DIM // fc
    return pl.pallas_call(
        _ffn_kernel,
        grid=(n // tm, nc),
        in_specs=[
            pl.BlockSpec((tm, D_MODEL), lambda i, c: (i, 0)),
            pl.BlockSpec((1, D_MODEL), lambda i, c: (0, 0)),
            pl.BlockSpec((D_MODEL, fc), lambda i, c: (0, c)),
            pl.BlockSpec((D_MODEL, fc), lambda i, c: (0, c + nc)),
            pl.BlockSpec((fc, D_MODEL), lambda i, c: (c, 0)),
        ],
        out_specs=pl.BlockSpec((tm, D_MODEL), lambda i, c: (i, 0)),
        out_shape=jax.ShapeDtypeStruct((n, D_MODEL), F32),
        scratch_shapes=[pltpu.VMEM((tm, D_MODEL), BF16), pltpu.VMEM((tm, D_MODEL), F32)],
        compiler_params=pltpu.CompilerParams(
            dimension_semantics=("parallel", "arbitrary"), vmem_limit_bytes=VMEM_LIMIT),
        name="swiglu_ffn",
    )(x2d, g, w_gu, w_gu, w_down)


def _gelu_tanh(x):
    return 0.5 * x * (1.0 + jnp.tanh(np.sqrt(2.0 / np.pi) * (x + 0.044715 * (x * x * x))))


def _gmlp_kernel(x_ref, g_ref, win_ref, lng_ref, lnb_ref, ws_ref, bs_ref, wout_ref, out_ref,
                 winbf_ref, woutbf_ref):
    _cast_once(win_ref, winbf_ref)
    _cast_once(wout_ref, woutbf_ref)
    x = x_ref[...]
    tm = x.shape[0]
    h = _rmsnorm(x, g_ref[...]).astype(BF16)
    y = _gelu_tanh(jnp.dot(h, winbf_ref[...], preferred_element_type=F32))
    u = y[:, :GMLP_WIDTH]
    v = y[:, GMLP_WIDTH:]
    mu = jnp.mean(v, axis=-1, keepdims=True)
    vc = v - mu
    var = jnp.mean(vc * vc, axis=-1, keepdims=True)
    vb = (vc * lax.rsqrt(var + NORM_EPS) * lng_ref[...] + lnb_ref[...]).astype(BF16)

    gw = GMLP_WIDTH // GMLP_GROUPS
    row = lax.broadcasted_iota(jnp.int32, (GMLP_CHUNK, GMLP_CHUNK), 0)
    col = lax.broadcasted_iota(jnp.int32, (GMLP_CHUNK, GMLP_CHUNK), 1)
    w_tril = [jnp.where(row >= col, ws_ref[gi], 0.0).astype(BF16) for gi in range(GMLP_GROUPS)]
    rows = []
    for c in range(tm // GMLP_CHUNK):
        parts = [
            jnp.dot(w_tril[gi], vb[c * GMLP_CHUNK:(c + 1) * GMLP_CHUNK, gi * gw:(gi + 1) * gw],
                    preferred_element_type=F32)
            for gi in range(GMLP_GROUPS)
        ]
        rows.append(jnp.concatenate(parts, axis=1) + bs_ref[...])
    s = jnp.concatenate(rows, axis=0)
    z = (u * s).astype(BF16)
    out_ref[...] = x + jnp.dot(z, woutbf_ref[...], preferred_element_type=F32)


def _gmlp(x2d, g, w_in, ln_g, ln_b, w_sp, b_sp_full, w_out, tm=512):
    n = x2d.shape[0]
    full = lambda shape: pl.BlockSpec(shape, lambda i: (0,) * len(shape))
    return pl.pallas_call(
        _gmlp_kernel,
        grid=(n // tm,),
        in_specs=[
            pl.BlockSpec((tm, D_MODEL), lambda i: (i, 0)),
            full((1, D_MODEL)),
            _resident((D_MODEL, 2 * GMLP_WIDTH)),
            full((1, GMLP_WIDTH)),
            full((1, GMLP_WIDTH)),
            full((GMLP_GROUPS, GMLP_CHUNK, GMLP_CHUNK)),
            full((GMLP_CHUNK, GMLP_WIDTH)),
            _resident((GMLP_WIDTH, D_MODEL)),
        ],
        out_specs=pl.BlockSpec((tm, D_MODEL), lambda i: (i, 0)),
        out_shape=jax.ShapeDtypeStruct((n, D_MODEL), F32),
        scratch_shapes=[pltpu.VMEM((D_MODEL, 2 * GMLP_WIDTH), BF16), pltpu.VMEM((GMLP_WIDTH, D_MODEL), BF16)],
        compiler_params=pltpu.CompilerParams(
            dimension_semantics=("arbitrary",), vmem_limit_bytes=VMEM_LIMIT),
        name="gmlp",
    )(x2d, g, w_in, ln_g, ln_b, w_sp, b_sp_full, w_out)


MOE_BLOCK = 512
MOE_GRAN = 16
MOE_TILE = 1024
MOE_LOCAL_ROWS = 2 * MOE_BLOCK + N_EXPERTS * MOE_GRAN
N_MOE_BLOCKS = N_TOK // MOE_BLOCK
MOE_MAX_TILES = (2 * N_TOK + N_MOE_BLOCKS * N_EXPERTS * (MOE_GRAN - 1)) // MOE_TILE + N_EXPERTS
MOE_ROWS = MOE_MAX_TILES * MOE_TILE
ROUTE_D1, ROUTE_D2, ROUTE_G1, ROUTE_G2 = 0, 1, 2, 3
ROUTE_ROWS = N_EXPERTS
TAB_LOFF, TAB_PC, TAB_GOFF = 0, N_EXPERTS, 2 * N_EXPERTS


def _moe_route_kernel(x_ref, g_ref, wr_ref, route_ref, meta_ref):
    b = pl.program_id(0)
    t_blk = MOE_BLOCK
    hf = _rmsnorm(x_ref[...], g_ref[...])
    wr = wr_ref[...]
    h_hi = hf.astype(BF16)
    h_lo = (hf - h_hi.astype(F32)).astype(BF16)
    w_hi = wr.astype(BF16)
    w_lo = (wr - w_hi.astype(F32)).astype(BF16)
    logits = (jnp.dot(h_hi, w_hi, preferred_element_type=F32) + jnp.dot(h_hi, w_lo, preferred_element_type=F32)
              + jnp.dot(h_lo, w_hi, preferred_element_type=F32))

    lg = logits.T[0:N_EXPERTS, :]
    e_idx = lax.broadcasted_iota(jnp.int32, lg.shape, 0).astype(F32)
    neg_inf = -jnp.inf
    m1 = jnp.max(lg, axis=0, keepdims=True)
    i1 = jnp.min(jnp.where(lg == m1, e_idx, float(N_EXPERTS)), axis=0, keepdims=True)
    lg2 = jnp.where(e_idx == i1, neg_inf, lg)
    m2 = jnp.max(lg2, axis=0, keepdims=True)
    i2 = jnp.min(jnp.where(lg2 == m2, e_idx, float(N_EXPERTS)), axis=0, keepdims=True)
    t = jnp.exp(m2 - m1)
    g1 = 1.0 / (1.0 + t)
    g2 = t * g1

    memb = jnp.where(jnp.logical_or(e_idx == i1, e_idx == i2), 1.0, 0.0)
    tr = lax.broadcasted_iota(jnp.int32, (t_blk, t_blk), 0)
    tc = lax.broadcasted_iota(jnp.int32, (t_blk, t_blk), 1)
    earlier = jnp.where(tr < tc, 1.0, 0.0).astype(BF16)
    rank = jnp.dot(memb.astype(BF16), earlier, preferred_element_type=F32)
    loff_col = jnp.zeros((N_EXPERTS, 1), F32)
    e_col = lax.broadcasted_iota(jnp.int32, (N_EXPERTS, 1), 0)
    lo = jnp.int32(0)
    for e in range(N_EXPERTS):
        cnt = jnp.sum(memb[e:e + 1, :]).astype(jnp.int32)
        pce = (cnt + (MOE_GRAN - 1)) // MOE_GRAN * MOE_GRAN
        meta_ref[b, TAB_LOFF + e] = lo
        meta_ref[b, TAB_PC + e] = pce
        loff_col = jnp.where(e_col == e, lo.astype(F32), loff_col)
        lo = lo + pce
    dest = loff_col + rank
    d1 = jnp.sum(jnp.where(e_idx == i1, dest, 0.0), axis=0, keepdims=True)
    d2 = jnp.sum(jnp.where(e_idx == i2, dest, 0.0), axis=0, keepdims=True)
    r_idx = lax.broadcasted_iota(jnp.int32, lg.shape, 0)
    route_ref[...] = jnp.where(r_idx == ROUTE_D1, d1,
                               jnp.where(r_idx == ROUTE_D2, d2,
                                         jnp.where(r_idx == ROUTE_G1, g1, jnp.where(r_idx == ROUTE_G2, g2, 0.0))))


def _moe_route(x2d, g, w_router):
    return pl.pallas_call(
        _moe_route_kernel,
        grid=(N_MOE_BLOCKS,),
        in_specs=[
            pl.BlockSpec((MOE_BLOCK, D_MODEL), lambda b: (b, 0)),
            pl.BlockSpec((1, D_MODEL), lambda b: (0, 0)),
            pl.BlockSpec((D_MODEL, LANES), lambda b: (0, 0)),
        ],
        out_specs=[
            pl.BlockSpec((ROUTE_ROWS, MOE_BLOCK), lambda b: (0, b)),
            pl.BlockSpec(memory_space=pltpu.SMEM),
        ],
        out_shape=[
            jax.ShapeDtypeStruct((ROUTE_ROWS, N_TOK), F32),
            jax.ShapeDtypeStruct((N_MOE_BLOCKS, 2 * N_EXPERTS), jnp.int32),
        ],
        compiler_params=pltpu.CompilerParams(
            dimension_semantics=("arbitrary",), vmem_limit_bytes=VMEM_LIMIT),
        name="moe_route",
    )(x2d, g, w_router)


def _moe_plan(meta):
    pc = meta[:, TAB_PC:TAB_PC + N_EXPERTS]
    totals = jnp.sum(pc, axis=0)
    per_expert = (totals + (MOE_TILE - 1)) // MOE_TILE
    e_lt = jnp.arange(N_EXPERTS)[:, None] < jnp.arange(N_EXPERTS)[None, :]
    first_tile = jnp.sum(jnp.where(e_lt, per_expert[:, None], 0), axis=0)
    ends = first_tile + per_expert
    seg_start = first_tile * MOE_TILE
    b_lt = jnp.arange(N_MOE_BLOCKS)[:, None] < jnp.arange(N_MOE_BLOCKS)[None, :]
    goff = seg_start[None, :] + jnp.sum(jnp.where(b_lt[:, :, None], pc[:, None, :], 0), axis=0)
    block_tab = jnp.concatenate([meta, goff], axis=1).astype(jnp.int32)
    n_tiles = jnp.sum(per_expert)
    t = jnp.minimum(jnp.arange(MOE_MAX_TILES, dtype=jnp.int32), n_tiles - 1)
    mine = jnp.logical_and(t[:, None] >= first_tile[None, :], t[:, None] < ends[None, :])
    expert = jnp.sum(jnp.where(mine, jnp.arange(N_EXPERTS)[None, :], 0), axis=1)
    left = totals[None, :] - (t[:, None] - first_tile[None, :]) * MOE_TILE
    used = jnp.sum(jnp.where(mine, jnp.clip(left, 0, MOE_TILE), 0), axis=1)
    tile_tab = jnp.stack([expert, jnp.full_like(t, n_tiles), used]).astype(jnp.int32)
    fill_tab = jnp.stack([seg_start + totals, ends * MOE_TILE]).astype(jnp.int32)
    return block_tab, tile_tab, fill_tab


def _moe_dispatch_kernel(tab_ref, tile_ref, fill_ref, x_ref, g_ref, route_ref, hs_hbm, hloc_ref, zero_ref, sem):
    b = pl.program_id(0)
    slot = b % 2
    hloc = hloc_ref.at[slot]
    hb = _rmsnorm(x_ref[...], g_ref[...]).astype(BF16)

    route_t = route_ref[...]
    r_iota = lax.broadcasted_iota(jnp.int32, (MOE_LOCAL_ROWS, MOE_BLOCK), 0).astype(F32)
    hit = jnp.logical_or(r_iota == route_t[ROUTE_D1:ROUTE_D1 + 1, :], r_iota == route_t[ROUTE_D2:ROUTE_D2 + 1, :])
    perm = jnp.where(hit, 1.0, 0.0).astype(BF16)
    hloc_ref[slot] = jnp.dot(perm, hb, preferred_element_type=F32).astype(BF16)

    def granule(ref, row):
        return ref.at[pl.ds(pl.multiple_of(row, MOE_GRAN), MOE_GRAN)]

    for e in range(N_EXPERTS):
        lo = tab_ref[b, TAB_LOFF + e]
        goff = tab_ref[b, TAB_GOFF + e]

        def start(gi, _, lo=lo, goff=goff):
            pltpu.make_async_copy(granule(hloc, lo + gi * MOE_GRAN),
                                  granule(hs_hbm, goff + gi * MOE_GRAN), sem.at[slot]).start()
            return 0

        lax.fori_loop(0, tab_ref[b, TAB_PC + e] // MOE_GRAN, start, 0)

    def wait_block(blk, slot_):
        n = 0
        for e in range(N_EXPERTS):
            n = n + tab_ref[blk, TAB_PC + e] // MOE_GRAN

        def wait_one(gi, _):
            pltpu.make_async_copy(granule(hloc_ref.at[slot_], 0), granule(hs_hbm, 0), sem.at[slot_]).wait()
            return 0

        lax.fori_loop(0, n, wait_one, 0)

    @pl.when(b > 0)
    def _():
        wait_block(jnp.maximum(b - 1, 0), 1 - slot)

    @pl.when(b == pl.num_programs(0) - 1)
    def _():
        wait_block(b, slot)
        zero_ref[...] = jnp.zeros_like(zero_ref)
        n_fill = 0
        for e in range(N_EXPERTS):
            first = fill_ref[0, e]
            n_e = (fill_ref[1, e] - first) // MOE_GRAN

            def fill(gi, _, first=first):
                pltpu.make_async_copy(granule(zero_ref, 0), granule(hs_hbm, first + gi * MOE_GRAN),
                                      sem.at[slot]).start()
                return 0

            lax.fori_loop(0, n_e, fill, 0)
            n_fill = n_fill + n_e

        def wait_fill(gi, _):
            pltpu.make_async_copy(granule(zero_ref, 0), granule(hs_hbm, 0), sem.at[slot]).wait()
            return 0

        lax.fori_loop(0, n_fill, wait_fill, 0)

        def tile_at(ti):
            return hs_hbm.at[pl.ds(pl.multiple_of(ti * MOE_TILE, MOE_TILE), MOE_TILE)]

        n_tiles = tile_ref[1, 0]

        def fill_tile(ti, _):
            pltpu.make_async_copy(zero_ref, tile_at(ti), sem.at[slot]).start()
            return 0

        def wait_tile(ti, _):
            pltpu.make_async_copy(zero_ref, tile_at(0), sem.at[slot]).wait()
            return 0

        lax.fori_loop(n_tiles, MOE_MAX_TILES, fill_tile, 0)
        lax.fori_loop(n_tiles, MOE_MAX_TILES, wait_tile, 0)


def _moe_dispatch(block_tab, tile_tab, fill_tab, x2d, g, route):
    grid_spec = pltpu.PrefetchScalarGridSpec(
        num_scalar_prefetch=3,
        grid=(N_MOE_BLOCKS,),
        in_specs=[
            pl.BlockSpec((MOE_BLOCK, D_MODEL), lambda b, *_: (b, 0)),
            pl.BlockSpec((1, D_MODEL), lambda b, *_: (0, 0)),
            pl.BlockSpec((ROUTE_ROWS, MOE_BLOCK), lambda b, *_: (0, b)),
        ],
        out_specs=pl.BlockSpec(memory_space=pl.ANY),
        scratch_shapes=[
            pltpu.VMEM((2, MOE_LOCAL_ROWS, D_MODEL), BF16),
            pltpu.VMEM((MOE_TILE, D_MODEL), BF16),
            pltpu.SemaphoreType.DMA((2,)),
        ],
    )
    return pl.pallas_call(
        _moe_dispatch_kernel,
        grid_spec=grid_spec,
        out_shape=jax.ShapeDtypeStruct((MOE_ROWS, D_MODEL), BF16),
        compiler_params=pltpu.CompilerParams(
            dimension_semantics=("arbitrary",), vmem_limit_bytes=VMEM_LIMIT),
        name="moe_dispatch",
    )(block_tab, tile_tab, fill_tab, x2d, g, route)


def _moe_expert_kernel(tile_ref, hs_ref, wg_ref, wu_ref, wd_ref, y_ref, acc_ref):
    t = pl.program_id(0)
    c = pl.program_id(1)
    active = t < tile_ref[1, 0]

    @pl.when(jnp.logical_and(jnp.logical_not(active), c == 0))
    def _():
        y_ref[...] = jnp.zeros_like(y_ref)

    @pl.when(active)
    def _():
        @pl.when(c == 0)
        def _():
            acc_ref[...] = jnp.zeros_like(acc_ref)

        def swiglu_rows(n_rows):
            h = hs_ref[0:n_rows, :]
            gate = jnp.dot(h, wg_ref[0].astype(BF16), preferred_element_type=F32)
            up = jnp.dot(h, wu_ref[0].astype(BF16), preferred_element_type=F32)
            a = (gate * jax.nn.sigmoid(gate) * up).astype(BF16)
            acc_ref[0:n_rows, :] += jnp.dot(a, wd_ref[0].astype(BF16), preferred_element_type=F32)

        half_only = tile_ref[2, t] <= MOE_TILE // 2

        @pl.when(half_only)
        def _():
            swiglu_rows(MOE_TILE // 2)

        @pl.when(jnp.logical_not(half_only))
        def _():
            swiglu_rows(MOE_TILE)

        @pl.when(c == pl.num_programs(1) - 1)
        def _():
            y_ref[...] = acc_ref[...].astype(BF16)


def _moe_experts(tile_tab, hs, w_gu, w_down, fc=512):
    nc = FFN_DIM // fc

    def tile_of(t, tr):
        return jnp.maximum(jnp.minimum(t, tr[1, 0] - 1), 0)

    def chunk_of(t, c, tr):
        return jnp.where(t < tr[1, 0], c, nc - 1)

    grid_spec = pltpu.PrefetchScalarGridSpec(
        num_scalar_prefetch=1,
        grid=(MOE_MAX_TILES, nc),
        in_specs=[
            pl.BlockSpec((MOE_TILE, D_MODEL), lambda t, c, tr: (tile_of(t, tr), 0)),
            pl.BlockSpec((1, D_MODEL, fc), lambda t, c, tr: (tr[0, t], 0, chunk_of(t, c, tr))),
            pl.BlockSpec((1, D_MODEL, fc), lambda t, c, tr: (tr[0, t], 0, chunk_of(t, c, tr) + nc)),
            pl.BlockSpec((1, fc, D_MODEL), lambda t, c, tr: (tr[0, t], chunk_of(t, c, tr), 0)),
        ],
        out_specs=pl.BlockSpec((MOE_TILE, D_MODEL), lambda t, c, tr: (t, 0)),
        scratch_shapes=[pltpu.VMEM((MOE_TILE, D_MODEL), F32)],
    )
    return pl.pallas_call(
        _moe_expert_kernel,
        grid_spec=grid_spec,
        out_shape=jax.ShapeDtypeStruct((MOE_ROWS, D_MODEL), BF16),
        compiler_params=pltpu.CompilerParams(
            dimension_semantics=("arbitrary", "arbitrary"), vmem_limit_bytes=VMEM_LIMIT),
        name="moe_experts",
    )(tile_tab, hs, w_gu, w_gu, w_down)


def _moe_combine_kernel(meta_ref, x_ref, route_ref, y_hbm, gf_ref, out_ref, yloc_ref, sem):
    b = pl.program_id(0)
    slot = b % 2

    def granule(ref, row):
        return ref.at[pl.ds(pl.multiple_of(row, MOE_GRAN), MOE_GRAN)]

    def start_block(blk, slot_):
        for e in range(N_EXPERTS):
            lo = meta_ref[blk, TAB_LOFF + e]
            goff = meta_ref[blk, TAB_GOFF + e]

            def start(gi, _, lo=lo, goff=goff):
                pltpu.make_async_copy(granule(y_hbm, goff + gi * MOE_GRAN),
                                      granule(yloc_ref.at[slot_], lo + gi * MOE_GRAN), sem.at[slot_]).start()
                return 0

            lax.fori_loop(0, meta_ref[blk, TAB_PC + e] // MOE_GRAN, start, 0)

    @pl.when(b == 0)
    def _():
        yloc_ref[...] = jnp.zeros_like(yloc_ref)
        start_block(0, 0)

    @pl.when(b + 1 < pl.num_programs(0))
    def _():
        start_block(jnp.minimum(b + 1, pl.num_programs(0) - 1), 1 - slot)

    n_mine = 0
    for e in range(N_EXPERTS):
        n_mine = n_mine + meta_ref[b, TAB_PC + e] // MOE_GRAN

    def wait_one(gi, _):
        pltpu.make_async_copy(granule(y_hbm, 0), granule(yloc_ref.at[slot], 0), sem.at[slot]).wait()
        return 0

    lax.fori_loop(0, n_mine, wait_one, 0)

    route_t = route_ref[...]
    route = route_t.T
    col = lax.broadcasted_iota(jnp.int32, (MOE_BLOCK, MOE_LOCAL_ROWS), 1).astype(F32)
    pick1 = jnp.where(col == route[:, ROUTE_D1:ROUTE_D1 + 1], 1.0, 0.0).astype(BF16)
    pick2 = jnp.where(col == route[:, ROUTE_D2:ROUTE_D2 + 1], 1.0, 0.0).astype(BF16)
    yl = yloc_ref[slot]
    moe = (route[:, ROUTE_G1:ROUTE_G1 + 1] * jnp.dot(pick1, yl, preferred_element_type=F32)
           + route[:, ROUTE_G2:ROUTE_G2 + 1] * jnp.dot(pick2, yl, preferred_element_type=F32))
    out_ref[...] = _rmsnorm(x_ref[...] + moe, gf_ref[...])


def _moe_combine(meta, x2d, route, y, g_final):
    grid_spec = pltpu.PrefetchScalarGridSpec(
        num_scalar_prefetch=1,
        grid=(N_MOE_BLOCKS,),
        in_specs=[
            pl.BlockSpec((MOE_BLOCK, D_MODEL), lambda b, m: (b, 0)),
            pl.BlockSpec((ROUTE_ROWS, MOE_BLOCK), lambda b, m: (0, b)),
            pl.BlockSpec(memory_space=pl.ANY),
            pl.BlockSpec((1, D_MODEL), lambda b, m: (0, 0)),
        ],
        out_specs=pl.BlockSpec((MOE_BLOCK, D_MODEL), lambda b, m: (b, 0)),
        scratch_shapes=[
            pltpu.VMEM((2, MOE_LOCAL_ROWS, D_MODEL), BF16),
            pltpu.SemaphoreType.DMA((2,)),
        ],
    )
    return pl.pallas_call(
        _moe_combine_kernel,
        grid_spec=grid_spec,
        out_shape=jax.ShapeDtypeStruct((N_TOK, D_MODEL), F32),
        compiler_params=pltpu.CompilerParams(
            dimension_semantics=("arbitrary",), vmem_limit_bytes=VMEM_LIMIT),
        name="moe_combine",
    )(meta, x2d, route, y, g_final)


def _moe(x2d, g, w_router, w_gu, w_down, g_final):
    route, meta = _moe_route(x2d, g, w_router)
    block_tab, tile_tab, fill_tab = _moe_plan(meta)
    hs = _moe_dispatch(block_tab, tile_tab, fill_tab, x2d, g, route)
    y = _moe_experts(tile_tab, hs, w_gu, w_down)
    return _moe_combine(block_tab, x2d, route, y, g_final)


def _rope_tables():
    def tables(dim):
        inv_freq = ROPE_THETA ** (-jnp.arange(0, dim, 2, dtype=F32) / dim)
        ang = jnp.arange(SEQ, dtype=F32)[:, None] * inv_freq[None, :]
        return jnp.cos(ang), jnp.sin(ang)

    c, s = tables(HEAD_DIM)
    cq = jnp.concatenate([c, c], axis=1)
    sq = jnp.concatenate([-s, s], axis=1)
    c, s = tables(IDX_ROPE_DIM)
    ones = jnp.ones((SEQ, IDX_DIM - IDX_ROPE_DIM), F32)
    zeros = jnp.zeros_like(ones)
    z16 = jnp.zeros_like(s)
    ci = jnp.concatenate([c, c, ones], axis=1)
    sia = jnp.concatenate([-s, z16, zeros], axis=1)
    sib = jnp.concatenate([z16, s, zeros], axis=1)
    rep = LANES // IDX_DIM
    return cq, sq, jnp.tile(ci, (1, rep)), jnp.tile(sia, (1, rep)), jnp.tile(sib, (1, rep))


def kernel(x, norm_mix, norm_ffn, dsa_w_in, dsa_idx_k_gain, dsa_w_out, ffn_w_gu, ffn_w_down, gmlp_w_in,
           gmlp_ln_gain, gmlp_ln_bias, gmlp_w_spatial, gmlp_b_spatial, gmlp_w_out, moe_w_router, moe_w_gu,
           moe_w_down, final_norm):
    x2d = x.reshape(N_TOK, D_MODEL)
    row = lambda v: v.reshape(1, -1)

    w_in = dsa_w_in[0]
    w_all = jnp.pad(w_in, ((0, 0), (0, DSA_COLS - w_in.shape[1])))
    wwi_t = w_in[:, COL_KI + IDX_DIM:COL_KI + IDX_DIM + IDX_HEADS].T
    kig = jnp.pad(dsa_idx_k_gain[0], (0, LANES - IDX_DIM)).reshape(1, LANES)
    q3, k, vt, kia, kib, qi, wt = _dsa_proj(x2d, row(norm_mix[0]), w_all, wwi_t, kig, _rope_tables())
    o3 = _dsa_attn(q3, k, vt, kia, kib, qi, wt)
    x2d = _resid_proj(x2d, o3, dsa_w_out[0])
    x2d = _ffn(x2d, row(norm_ffn[0]), ffn_w_gu[0], ffn_w_down[0])

    gw = GMLP_WIDTH // GMLP_GROUPS
    b_sp_full = jnp.repeat(gmlp_b_spatial[0].T, gw, axis=1)
    x2d = _gmlp(x2d, row(norm_mix[1]), gmlp_w_in[0], row(gmlp_ln_gain[0]), row(gmlp_ln_bias[0]),
                gmlp_w_spatial[0], b_sp_full, gmlp_w_out[0])
    w_router = jnp.pad(moe_w_router[0], ((0, 0), (0, LANES - N_EXPERTS)))
    out = _moe(x2d, row(norm_ffn[1]), w_router, moe_w_gu[0], moe_w_down[0], row(final_norm))
    return out.reshape(BATCH, SEQ, D_MODEL)
```

```python
import functools

import numpy as np
import jax
import jax.numpy as jnp
from jax import lax
from jax.experimental import pallas as pl
from jax.experimental.pallas import tpu as pltpu

F32 = jnp.float32
BF16 = jnp.bfloat16

D_MODEL = 1024
BATCH = 8
SEQ = 2048
N_TOK = BATCH * SEQ
N_HEADS = 8
HEAD_DIM = 128
IDX_HEADS = 8
IDX_DIM = 64
IDX_ROPE_DIM = 32
INDEX_TOPK = 256
Q_BLOCK = 128
KEY_SUPER = 512
ROPE_THETA = 10000.0
GMLP_WIDTH = 1024
GMLP_GROUPS = 8
GMLP_CHUNK = 128
FFN_DIM = 3584
N_EXPERTS = 8
NORM_EPS = 1e-6

LANES = 128
SUBLANES = 8
VMEM_LIMIT = 56 * 1024 * 1024

DSA_COLS = 1920
COL_K = N_HEADS * HEAD_DIM
COL_V = COL_K + HEAD_DIM
COL_QI = COL_V + HEAD_DIM
COL_KI = COL_QI + IDX_HEADS * IDX_DIM

NT_DIMS = (((1,), (1,)), ((), ()))
MASK_BIAS = -1e30
SCORE_MASKED = -3.0e38


def _rmsnorm(x, g):
    ms = jnp.mean(x * x, axis=-1, keepdims=True)
    return x * lax.rsqrt(ms + NORM_EPS) * g


def _cast_once(src_ref, dst_ref):
    @pl.when(pl.program_id(0) == 0)
    def _():
        dst_ref[...] = src_ref[...].astype(BF16)


def _resident(shape):
    return pl.BlockSpec(shape, lambda i: (0,) * len(shape), pipeline_mode=pl.Buffered(1))


def _dsa_proj_kernel(x_ref, g_ref, w_ref, wwi_ref, kig_ref, cq_ref, sq_ref, ci_ref, sia_ref, sib_ref,
                     q_ref, k_ref, vt_ref, kia_ref, kib_ref, qi_ref, wt_ref, wbf_ref):
    _cast_once(w_ref, wbf_ref)
    h = _rmsnorm(x_ref[...], g_ref[...]).astype(BF16)
    y = jnp.dot(h, wbf_ref[...], preferred_element_type=F32)
    cq = cq_ref[...]
    sq = sq_ref[...]
    ci = ci_ref[...]
    sia = sia_ref[...]
    sib = sib_ref[...]

    def rope_head(t):
        return t * cq + pltpu.roll(t, HEAD_DIM // 2, 1) * sq

    def rope_idx(t):
        half = IDX_ROPE_DIM // 2
        return t * ci + pltpu.roll(t, LANES - half, 1) * sia + pltpu.roll(t, half, 1) * sib

    q_scale = HEAD_DIM ** -0.5 * np.log2(np.e)
    for hh in range(N_HEADS):
        q_ref[hh] = (rope_head(y[:, hh * HEAD_DIM:(hh + 1) * HEAD_DIM]) * q_scale).astype(BF16)
    k_ref[...] = rope_head(y[:, COL_K:COL_V]).astype(BF16)
    vt_ref[0] = y[:, COL_V:COL_QI].T.astype(BF16)
    for p in range(IDX_HEADS * IDX_DIM // LANES):
        qi_ref[:, p * LANES:(p + 1) * LANES] = rope_idx(
            y[:, COL_QI + p * LANES:COL_QI + (p + 1) * LANES]).astype(BF16)

    last = y[:, COL_KI:COL_KI + LANES]
    lane = lax.broadcasted_iota(jnp.int32, last.shape, 1)
    kraw = jnp.where(lane < IDX_DIM, last, 0.0)
    kms = jnp.sum(kraw * kraw, axis=-1, keepdims=True) * (1.0 / IDX_DIM)
    kn = kraw * lax.rsqrt(kms + NORM_EPS) * kig_ref[...]
    kr = rope_idx(kn)
    kia_ref[...] = kr.astype(BF16)
    kib_ref[...] = pltpu.roll(kr, IDX_DIM, 1).astype(BF16)

    w_scale = IDX_HEADS ** -0.5 * IDX_DIM ** -0.5
    wt_ref[...] = lax.dot_general(wwi_ref[...].astype(BF16), h, NT_DIMS, preferred_element_type=F32) * w_scale


def _dsa_proj(x2d, g, w_all, wwi_t, kig, tabs, tm=KEY_SUPER):
    n = x2d.shape[0]
    nblk_seq = SEQ // tm
    tab_spec = pl.BlockSpec((tm, LANES), lambda i: (i % nblk_seq, 0))
    full = lambda shape: pl.BlockSpec(shape, lambda i: (0,) * len(shape))
    return pl.pallas_call(
        _dsa_proj_kernel,
        grid=(n // tm,),
        in_specs=[
            pl.BlockSpec((tm, D_MODEL), lambda i: (i, 0)),
            full((1, D_MODEL)),
            _resident((D_MODEL, DSA_COLS)),
            full((IDX_HEADS, D_MODEL)),
            full((1, LANES)),
            tab_spec, tab_spec, tab_spec, tab_spec, tab_spec,
        ],
        out_specs=[
            pl.BlockSpec((N_HEADS, tm, HEAD_DIM), lambda i: (0, i, 0)),
            pl.BlockSpec((tm, HEAD_DIM), lambda i: (i, 0)),
            pl.BlockSpec((1, HEAD_DIM, tm), lambda i: (i, 0, 0)),
            pl.BlockSpec((tm, LANES), lambda i: (i, 0)),
            pl.BlockSpec((tm, LANES), lambda i: (i, 0)),
            pl.BlockSpec((tm, IDX_HEADS * IDX_DIM), lambda i: (i, 0)),
            pl.BlockSpec((IDX_HEADS, tm), lambda i: (0, i)),
        ],
        out_shape=[
            jax.ShapeDtypeStruct((N_HEADS, n, HEAD_DIM), BF16),
            jax.ShapeDtypeStruct((n, HEAD_DIM), BF16),
            jax.ShapeDtypeStruct((n // tm, HEAD_DIM, tm), BF16),
            jax.ShapeDtypeStruct((n, LANES), BF16),
            jax.ShapeDtypeStruct((n, LANES), BF16),
            jax.ShapeDtypeStruct((n, IDX_HEADS * IDX_DIM), BF16),
            jax.ShapeDtypeStruct((IDX_HEADS, n), F32),
        ],
        scratch_shapes=[pltpu.VMEM((D_MODEL, DSA_COLS), BF16)],
        compiler_params=pltpu.CompilerParams(
            dimension_semantics=("arbitrary",), vmem_limit_bytes=VMEM_LIMIT),
        name="dsa_proj",
    )(x2d, g, w_all, wwi_t, kig, *tabs)


def _ordered_bits_to_float(u):
    key = u ^ jnp.int32(-2 ** 31)
    key = jnp.maximum(key, jnp.int32(-2139095041))
    bits = jnp.where(key >= 0, key, key ^ jnp.int32(0x7FFFFFFF))
    return lax.bitcast_convert_type(bits, F32)


def _dsa_attn_kernel(q_ref, k_ref, vt_ref, kia_ref, kib_ref, qi_ref, wt_ref, o_ref,
                     s_ref, sel_ref, l_ref, p_ref, acc_ref):
    j = pl.program_id(1)
    per_super = KEY_SUPER // Q_BLOCK
    n_super = j // per_super + 1
    hq = N_HEADS * Q_BLOCK

    qpos = j * Q_BLOCK + lax.broadcasted_iota(jnp.int32, (Q_BLOCK, Q_BLOCK), 1)
    krow = lax.broadcasted_iota(jnp.int32, (Q_BLOCK, Q_BLOCK), 0)

    def rows_of(sc, c4):
        return pl.ds(pl.multiple_of(sc * KEY_SUPER, KEY_SUPER) + c4 * Q_BLOCK, Q_BLOCK)

    def super_rows(sc):
        return pl.ds(pl.multiple_of(sc * KEY_SUPER, KEY_SUPER), KEY_SUPER)

    def causal(sc, c4):
        return (krow + (sc * KEY_SUPER + c4 * Q_BLOCK)) <= qpos

    def fold8(t):
        return t.reshape(t.shape[0] // SUBLANES, SUBLANES, t.shape[1])

    def score_body(sc, _):
        srows = super_rows(sc)
        ka = kia_ref[srows, :]
        kb = kib_ref[srows, :]
        for half in range(IDX_HEADS // 4):
            p0 = 2 * half
            qpp = jnp.concatenate(
                [qi_ref[:, p0 * LANES:(p0 + 1) * LANES], qi_ref[:, (p0 + 1) * LANES:(p0 + 2) * LANES]], axis=0)
            ra = lax.dot_general(ka, qpp, NT_DIMS, preferred_element_type=F32)
            rb = lax.dot_general(kb, qpp, NT_DIMS, preferred_element_type=F32)
            term = None
            for sub in range(2):
                p = p0 + sub
                cols = slice(sub * Q_BLOCK, (sub + 1) * Q_BLOCK)
                t2 = (jnp.maximum(ra[:, cols], 0.0) * wt_ref[2 * p:2 * p + 1, :]
                      + jnp.maximum(rb[:, cols], 0.0) * wt_ref[2 * p + 1:2 * p + 2, :])
                term = t2 if term is None else term + t2
            if half == 0:
                s_ref[srows, :] = term
            else:
                kpos = sc * KEY_SUPER + lax.broadcasted_iota(jnp.int32, (KEY_SUPER, Q_BLOCK), 0)
                qp_s = j * Q_BLOCK + lax.broadcasted_iota(jnp.int32, (KEY_SUPER, Q_BLOCK), 1)
                s_ref[srows, :] = jnp.where(kpos <= qp_s, s_ref[srows, :] + term, SCORE_MASKED)
        return 0

    lax.fori_loop(0, n_super, score_body, 0)

    def search(n_chunks):
        def count_ge(cand):
            accs = [jnp.zeros((SUBLANES, LANES), F32), jnp.zeros((SUBLANES, LANES), F32)]
            for c in range(n_chunks):
                ind = jnp.where(s_ref[pl.ds(c * Q_BLOCK, Q_BLOCK), :] >= cand, 1.0, 0.0)
                accs[c % 2] = accs[c % 2] + fold8(ind).sum(axis=0)
            return jnp.sum(accs[0] + accs[1], axis=0, keepdims=True)

        def bit_body(i, carry):
            prefix, cbest = carry
            cu = prefix | lax.shift_left(jnp.int32(1), 31 - i)
            cnt = count_ge(_ordered_bits_to_float(cu))
            ok = cnt >= float(INDEX_TOPK)
            return jnp.where(ok, cu, prefix), jnp.where(ok, cnt, cbest)

        prefix, cbest = lax.fori_loop(
            0, 32, bit_body,
            (jnp.zeros((1, LANES), jnp.int32), jnp.full((1, LANES), float(n_chunks * Q_BLOCK), F32)))
        sel_ref[0:1, :] = _ordered_bits_to_float(prefix)
        sel_ref[1:2, :] = cbest

    for n_chunks in range(1, SEQ // Q_BLOCK + 1):
        pl.when(j == n_chunks - 1)(functools.partial(search, n_chunks))
    tau = sel_ref[0:1, :]
    drop = sel_ref[1:2, :] - float(INDEX_TOPK)

    upper = jnp.where(krow < lax.broadcasted_iota(jnp.int32, (Q_BLOCK, Q_BLOCK), 1), 1.0, 0.0).astype(BF16)
    pair_w = 2 * Q_BLOCK

    def select_logit_body(i, carry):
        later_ties, m_acc = carry
        sc = n_super - 1 - i
        biases = [None] * per_super
        for c4 in reversed(range(per_super)):
            s = s_ref[rows_of(sc, c4), :]
            eq = jnp.where(s == tau, 1.0, 0.0)
            ties_after = jnp.dot(upper, eq.astype(BF16), preferred_element_type=F32) + later_ties
            later_ties = later_ties + jnp.sum(eq, axis=0, keepdims=True)
            tie_bias = jnp.where(ties_after >= drop, 0.0, MASK_BIAS)
            bias = jnp.where(s > tau, 0.0, jnp.where(s == tau, tie_bias, MASK_BIAS))
            biases[c4] = jnp.where(causal(sc, c4), bias, MASK_BIAS)
        bias = jnp.concatenate(biases, axis=0)
        bias2 = jnp.concatenate([bias, bias], axis=1)
        srows = super_rows(sc)
        kc = k_ref[srows, :]
        parts = []
        for pr in range(N_HEADS // 2):
            qpair = q_ref[2 * pr:2 * pr + 2].reshape(pair_w, HEAD_DIM)
            lg = lax.dot_general(kc, qpair, NT_DIMS, preferred_element_type=F32) + bias2
            l_ref[srows, pr * pair_w:(pr + 1) * pair_w] = lg
            parts.append(fold8(lg).max(axis=0))
        return later_ties, jnp.maximum(m_acc, jnp.concatenate(parts, axis=1))

    _, m_acc = lax.fori_loop(
        0, n_super, select_logit_body,
        (jnp.zeros((1, LANES), F32), jnp.full((SUBLANES, hq), MASK_BIAS, F32)))
    m = jnp.max(m_acc, axis=0, keepdims=True)

    def prob_chunk(sc, d_acc):
        for c4 in range(per_super):
            rows = rows_of(sc, c4)
            parts = []
            for pr in range(N_HEADS // 2):
                cols = slice(pr * pair_w, (pr + 1) * pair_w)
                p = jnp.exp2(l_ref[rows, cols] - m[:, cols])
                p_ref[rows, cols] = p.astype(BF16)
                parts.append(fold8(p).sum(axis=0))
            d_acc = d_acc + jnp.concatenate(parts, axis=1)
        return d_acc

    def pv_chunk(sc):
        acc_ref[...] += jnp.dot(vt_ref[sc], p_ref[super_rows(sc), :], preferred_element_type=F32)

    def prob_pv_body(sc, d_acc):
        pv_chunk(sc - 1)
        return prob_chunk(sc, d_acc)

    acc_ref[...] = jnp.zeros_like(acc_ref)
    d_acc = prob_chunk(0, jnp.zeros((SUBLANES, hq), F32))
    d_acc = lax.fori_loop(1, n_super, prob_pv_body, d_acc)
    pv_chunk(n_super - 1)
    inv = 1.0 / jnp.sum(d_acc, axis=0, keepdims=True)
    for h in range(N_HEADS):
        cols = slice(h * Q_BLOCK, (h + 1) * Q_BLOCK)
        o_ref[h] = (acc_ref[:, cols] * inv[:, cols]).T.astype(BF16)


def _dsa_attn(q3, k, vt, kia, kib, qi, wt):
    nqb = SEQ // Q_BLOCK
    n_sup = SEQ // KEY_SUPER
    hq = N_HEADS * Q_BLOCK
    return pl.pallas_call(
        _dsa_attn_kernel,
        grid=(BATCH, nqb),
        in_specs=[
            pl.BlockSpec((N_HEADS, Q_BLOCK, HEAD_DIM), lambda b, j: (0, b * nqb + j, 0)),
            pl.BlockSpec((SEQ, HEAD_DIM), lambda b, j: (b, 0)),
            pl.BlockSpec((n_sup, HEAD_DIM, KEY_SUPER), lambda b, j: (b, 0, 0)),
            pl.BlockSpec((SEQ, LANES), lambda b, j: (b, 0)),
            pl.BlockSpec((SEQ, LANES), lambda b, j: (b, 0)),
            pl.BlockSpec((Q_BLOCK, IDX_HEADS * IDX_DIM), lambda b, j: (b * nqb + j, 0)),
            pl.BlockSpec((IDX_HEADS, Q_BLOCK), lambda b, j: (0, b * nqb + j)),
        ],
        out_specs=pl.BlockSpec((N_HEADS, Q_BLOCK, HEAD_DIM), lambda b, j: (0, b * nqb + j, 0)),
        out_shape=jax.ShapeDtypeStruct((N_HEADS, N_TOK, HEAD_DIM), BF16),
        scratch_shapes=[
            pltpu.VMEM((SEQ, Q_BLOCK), F32),
            pltpu.VMEM((SUBLANES, Q_BLOCK), F32),
            pltpu.VMEM((SEQ, hq), F32),
            pltpu.VMEM((SEQ, hq), BF16),
            pltpu.VMEM((HEAD_DIM, hq), F32),
        ],
        compiler_params=pltpu.CompilerParams(
            dimension_semantics=("parallel", "arbitrary"), vmem_limit_bytes=VMEM_LIMIT),
        name="dsa_attn",
    )(q3, k, vt, kia, kib, qi, wt)


def _resid_proj_kernel(x_ref, o_ref, w_ref, out_ref, wbf_ref):
    _cast_once(w_ref, wbf_ref)
    o = jnp.concatenate([o_ref[h] for h in range(N_HEADS)], axis=1)
    out_ref[...] = x_ref[...] + jnp.dot(o, wbf_ref[...], preferred_element_type=F32)


def _resid_proj(x2d, o3, w_out, tm=1024):
    n = x2d.shape[0]
    return pl.pallas_call(
        _resid_proj_kernel,
        grid=(n // tm,),
        in_specs=[
            pl.BlockSpec((tm, D_MODEL), lambda i: (i, 0)),
            pl.BlockSpec((N_HEADS, tm, HEAD_DIM), lambda i: (0, i, 0)),
            _resident((D_MODEL, D_MODEL)),
        ],
        out_specs=pl.BlockSpec((tm, D_MODEL), lambda i: (i, 0)),
        out_shape=jax.ShapeDtypeStruct((n, D_MODEL), F32),
        scratch_shapes=[pltpu.VMEM((D_MODEL, D_MODEL), BF16)],
        compiler_params=pltpu.CompilerParams(
            dimension_semantics=("arbitrary",), vmem_limit_bytes=VMEM_LIMIT),
        name="dsa_out_proj",
    )(x2d, o3, w_out)


def _ffn_kernel(x_ref, g_ref, wg_ref, wu_ref, wd_ref, out_ref, h_ref, acc_ref):
    c = pl.program_id(1)

    @pl.when(c == 0)
    def _():
        h_ref[...] = _rmsnorm(x_ref[...], g_ref[...]).astype(BF16)
        acc_ref[...] = jnp.zeros_like(acc_ref)

    h = h_ref[...]
    gate = jnp.dot(h, wg_ref[...].astype(BF16), preferred_element_type=F32)
    up = jnp.dot(h, wu_ref[...].astype(BF16), preferred_element_type=F32)
    a = (gate * jax.nn.sigmoid(gate) * up).astype(BF16)
    acc_ref[...] += jnp.dot(a, wd_ref[...].astype(BF16), preferred_element_type=F32)

    @pl.when(c == pl.num_programs(1) - 1)
    def _():
        out_ref[...] = x_ref[...] + acc_ref[...]


def _ffn(x2d, g, w_gu, w_down, tm=1024, fc=512):
    n = x2d.shape[0]
    nc = FFN_DIM // fc
    return pl.pallas_call(
        _ffn_kernel,
        grid=(n // tm, nc),
        in_specs=[
            pl.BlockSpec((tm, D_MODEL), lambda i, c: (i, 0)),
            pl.BlockSpec((1, D_MODEL), lambda i, c: (0, 0)),
            pl.BlockSpec((D_MODEL, fc), lambda i, c: (0, c)),
            pl.BlockSpec((D_MODEL, fc), lambda i, c: (0, c + nc)),
            pl.BlockSpec((fc, D_MODEL), lambda i, c: (c, 0)),
        ],
        out_specs=pl.BlockSpec((tm, D_MODEL), lambda i, c: (i, 0)),
        out_shape=jax.ShapeDtypeStruct((n, D_MODEL), F32),
        scratch_shapes=[pltpu.VMEM((tm, D_MODEL), BF16), pltpu.VMEM((tm, D_MODEL), F32)],
        compiler_params=pltpu.CompilerParams(
            dimension_semantics=("parallel", "arbitrary"), vmem_limit_bytes=VMEM_LIMIT),
        name="swiglu_ffn",
    )(x2d, g, w_gu, w_gu, w_down)


def _gelu_tanh(x):
    return 0.5 * x * (1.0 + jnp.tanh(np.sqrt(2.0 / np.pi) * (x + 0.044715 * (x * x * x))))


def _gmlp_kernel(x_ref, g_ref, win_ref, lng_ref, lnb_ref, ws_ref, bs_ref, wout_ref, out_ref,
                 winbf_ref, woutbf_ref):
    _cast_once(win_ref, winbf_ref)
    _cast_once(wout_ref, woutbf_ref)
    x = x_ref[...]
    tm = x.shape[0]
    h = _rmsnorm(x, g_ref[...]).astype(BF16)
    y = _gelu_tanh(jnp.dot(h, winbf_ref[...], preferred_element_type=F32))
    u = y[:, :GMLP_WIDTH]
    v = y[:, GMLP_WIDTH:]
    mu = jnp.mean(v, axis=-1, keepdims=True)
    vc = v - mu
    var = jnp.mean(vc * vc, axis=-1, keepdims=True)
    vb = (vc * lax.rsqrt(var + NORM_EPS) * lng_ref[...] + lnb_ref[...]).astype(BF16)

    gw = GMLP_WIDTH // GMLP_GROUPS
    row = lax.broadcasted_iota(jnp.int32, (GMLP_CHUNK, GMLP_CHUNK), 0)
    col = lax.broadcasted_iota(jnp.int32, (GMLP_CHUNK, GMLP_CHUNK), 1)
    w_tril = [jnp.where(row >= col, ws_ref[gi], 0.0).astype(BF16) for gi in range(GMLP_GROUPS)]
    rows = []
    for c in range(tm // GMLP_CHUNK):
        parts = [
            jnp.dot(w_tril[gi], vb[c * GMLP_CHUNK:(c + 1) * GMLP_CHUNK, gi * gw:(gi + 1) * gw],
                    preferred_element_type=F32)
            for gi in range(GMLP_GROUPS)
        ]
        rows.append(jnp.concatenate(parts, axis=1) + bs_ref[...])
    s = jnp.concatenate(rows, axis=0)
    z = (u * s).astype(BF16)
    out_ref[...] = x + jnp.dot(z, woutbf_ref[...], preferred_element_type=F32)


def _gmlp(x2d, g, w_in, ln_g, ln_b, w_sp, b_sp_full, w_out, tm=512):
    n = x2d.shape[0]
    full = lambda shape: pl.BlockSpec(shape, lambda i: (0,) * len(shape))
    return pl.pallas_call(
        _gmlp_kernel,
        grid=(n // tm,),
        in_specs=[
            pl.BlockSpec((tm, D_MODEL), lambda i: (i, 0)),
            full((1, D_MODEL)),
            _resident((D_MODEL, 2 * GMLP_WIDTH)),
            full((1, GMLP_WIDTH)),
            full((1, GMLP_WIDTH)),
            full((GMLP_GROUPS, GMLP_CHUNK, GMLP_CHUNK)),
            full((GMLP_CHUNK, GMLP_WIDTH)),
            _resident((GMLP_WIDTH, D_MODEL)),
        ],
        out_specs=pl.BlockSpec((tm, D_MODEL), lambda i: (i, 0)),
        out_shape=jax.ShapeDtypeStruct((n, D_MODEL), F32),
        scratch_shapes=[pltpu.VMEM((D_MODEL, 2 * GMLP_WIDTH), BF16), pltpu.VMEM((GMLP_WIDTH, D_MODEL), BF16)],
        compiler_params=pltpu.CompilerParams(
            dimension_semantics=("arbitrary",), vmem_limit_bytes=VMEM_LIMIT),
        name="gmlp",
    )(x2d, g, w_in, ln_g, ln_b, w_sp, b_sp_full, w_out)


MOE_BLOCK = 512
MOE_GRAN = 16
MOE_TILE = 1024
MOE_LOCAL_ROWS = 2 * MOE_BLOCK + N_EXPERTS * MOE_GRAN
N_MOE_BLOCKS = N_TOK // MOE_BLOCK
MOE_MAX_TILES = (2 * N_TOK + N_MOE_BLOCKS * N_EXPERTS * (MOE_GRAN - 1)) // MOE_TILE + N_EXPERTS
MOE_ROWS = MOE_MAX_TILES * MOE_TILE
ROUTE_D1, ROUTE_D2, ROUTE_G1, ROUTE_G2 = 0, 1, 2, 3
ROUTE_ROWS = N_EXPERTS
TAB_LOFF, TAB_PC, TAB_GOFF = 0, N_EXPERTS, 2 * N_EXPERTS


def _moe_route_kernel(x_ref, g_ref, wr_ref, route_ref, meta_ref):
    b = pl.program_id(0)
    t_blk = MOE_BLOCK
    hf = _rmsnorm(x_ref[...], g_ref[...])
    wr = wr_ref[...]
    h_hi = hf.astype(BF16)
    h_lo = (hf - h_hi.astype(F32)).astype(BF16)
    w_hi = wr.astype(BF16)
    w_lo = (wr - w_hi.astype(F32)).astype(BF16)
    logits = (jnp.dot(h_hi, w_hi, preferred_element_type=F32) + jnp.dot(h_hi, w_lo, preferred_element_type=F32)
              + jnp.dot(h_lo, w_hi, preferred_element_type=F32))

    lg = logits.T[0:N_EXPERTS, :]
    e_idx = lax.broadcasted_iota(jnp.int32, lg.shape, 0).astype(F32)
    neg_inf = -jnp.inf
    m1 = jnp.max(lg, axis=0, keepdims=True)
    i1 = jnp.min(jnp.where(lg == m1, e_idx, float(N_EXPERTS)), axis=0, keepdims=True)
    lg2 = jnp.where(e_idx == i1, neg_inf, lg)
    m2 = jnp.max(lg2, axis=0, keepdims=True)
    i2 = jnp.min(jnp.where(lg2 == m2, e_idx, float(N_EXPERTS)), axis=0, keepdims=True)
    t = jnp.exp(m2 - m1)
    g1 = 1.0 / (1.0 + t)
    g2 = t * g1

    memb = jnp.where(jnp.logical_or(e_idx == i1, e_idx == i2), 1.0, 0.0)
    tr = lax.broadcasted_iota(jnp.int32, (t_blk, t_blk), 0)
    tc = lax.broadcasted_iota(jnp.int32, (t_blk, t_blk), 1)
    earlier = jnp.where(tr < tc, 1.0, 0.0).astype(BF16)
    rank = jnp.dot(memb.astype(BF16), earlier, preferred_element_type=F32)
    loff_col = jnp.zeros((N_EXPERTS, 1), F32)
    e_col = lax.broadcasted_iota(jnp.int32, (N_EXPERTS, 1), 0)
    lo = jnp.int32(0)
    for e in range(N_EXPERTS):
        cnt = jnp.sum(memb[e:e + 1, :]).astype(jnp.int32)
        pce = (cnt + (MOE_GRAN - 1)) // MOE_GRAN * MOE_GRAN
        meta_ref[b, TAB_LOFF + e] = lo
        meta_ref[b, TAB_PC + e] = pce
        loff_col = jnp.where(e_col == e, lo.astype(F32), loff_col)
        lo = lo + pce
    dest = loff_col + rank
    d1 = jnp.sum(jnp.where(e_idx == i1, dest, 0.0), axis=0, keepdims=True)
    d2 = jnp.sum(jnp.where(e_idx == i2, dest, 0.0), axis=0, keepdims=True)
    r_idx = lax.broadcasted_iota(jnp.int32, lg.shape, 0)
    route_ref[...] = jnp.where(r_idx == ROUTE_D1, d1,
                               jnp.where(r_idx == ROUTE_D2, d2,
                                         jnp.where(r_idx == ROUTE_G1, g1, jnp.where(r_idx == ROUTE_G2, g2, 0.0))))


def _moe_route(x2d, g, w_router):
    return pl.pallas_call(
        _moe_route_kernel,
        grid=(N_MOE_BLOCKS,),
        in_specs=[
            pl.BlockSpec((MOE_BLOCK, D_MODEL), lambda b: (b, 0)),
            pl.BlockSpec((1, D_MODEL), lambda b: (0, 0)),
            pl.BlockSpec((D_MODEL, LANES), lambda b: (0, 0)),
        ],
        out_specs=[
            pl.BlockSpec((ROUTE_ROWS, MOE_BLOCK), lambda b: (0, b)),
            pl.BlockSpec(memory_space=pltpu.SMEM),
        ],
        out_shape=[
            jax.ShapeDtypeStruct((ROUTE_ROWS, N_TOK), F32),
            jax.ShapeDtypeStruct((N_MOE_BLOCKS, 2 * N_EXPERTS), jnp.int32),
        ],
        compiler_params=pltpu.CompilerParams(
            dimension_semantics=("arbitrary",), vmem_limit_bytes=VMEM_LIMIT),
        name="moe_route",
    )(x2d, g, w_router)


def _moe_plan(meta):
    pc = meta[:, TAB_PC:TAB_PC + N_EXPERTS]
    totals = jnp.sum(pc, axis=0)
    per_expert = (totals + (MOE_TILE - 1)) // MOE_TILE
    e_lt = jnp.arange(N_EXPERTS)[:, None] < jnp.arange(N_EXPERTS)[None, :]
    first_tile = jnp.sum(jnp.where(e_lt, per_expert[:, None], 0), axis=0)
    ends = first_tile + per_expert
    seg_start = first_tile * MOE_TILE
    b_lt = jnp.arange(N_MOE_BLOCKS)[:, None] < jnp.arange(N_MOE_BLOCKS)[None, :]
    goff = seg_start[None, :] + jnp.sum(jnp.where(b_lt[:, :, None], pc[:, None, :], 0), axis=0)
    block_tab = jnp.concatenate([meta, goff], axis=1).astype(jnp.int32)
    n_tiles = jnp.sum(per_expert)
    t = jnp.minimum(jnp.arange(MOE_MAX_TILES, dtype=jnp.int32), n_tiles - 1)
    mine = jnp.logical_and(t[:, None] >= first_tile[None, :], t[:, None] < ends[None, :])
    expert = jnp.sum(jnp.where(mine, jnp.arange(N_EXPERTS)[None, :], 0), axis=1)
    left = totals[None, :] - (t[:, None] - first_tile[None, :]) * MOE_TILE
    used = jnp.sum(jnp.where(mine, jnp.clip(left, 0, MOE_TILE), 0), axis=1)
    tile_tab = jnp.stack([expert, jnp.full_like(t, n_tiles), used]).astype(jnp.int32)
    fill_tab = jnp.stack([seg_start + totals, ends * MOE_TILE]).astype(jnp.int32)
    return block_tab, tile_tab, fill_tab


def _moe_dispatch_kernel(tab_ref, tile_ref, fill_ref, x_ref, g_ref, route_ref, hs_hbm, hloc_ref, zero_ref, sem):
    b = pl.program_id(0)
    slot = b % 2
    hloc = hloc_ref.at[slot]
    hb = _rmsnorm(x_ref[...], g_ref[...]).astype(BF16)

    route_t = route_ref[...]
    r_iota = lax.broadcasted_iota(jnp.int32, (MOE_LOCAL_ROWS, MOE_BLOCK), 0).astype(F32)
    hit = jnp.logical_or(r_iota == route_t[ROUTE_D1:ROUTE_D1 + 1, :], r_iota == route_t[ROUTE_D2:ROUTE_D2 + 1, :])
    perm = jnp.where(hit, 1.0, 0.0).astype(BF16)
    hloc_ref[slot] = jnp.dot(perm, hb, preferred_element_type=F32).astype(BF16)

    def granule(ref, row):
        return ref.at[pl.ds(pl.multiple_of(row, MOE_GRAN), MOE_GRAN)]

    for e in range(N_EXPERTS):
        lo = tab_ref[b, TAB_LOFF + e]
        goff = tab_ref[b, TAB_GOFF + e]

        def start(gi, _, lo=lo, goff=goff):
            pltpu.make_async_copy(granule(hloc, lo + gi * MOE_GRAN),
                                  granule(hs_hbm, goff + gi * MOE_GRAN), sem.at[slot]).start()
            return 0

        lax.fori_loop(0, tab_ref[b, TAB_PC + e] // MOE_GRAN, start, 0)

    def wait_block(blk, slot_):
        n = 0
        for e in range(N_EXPERTS):
            n = n + tab_ref[blk, TAB_PC + e] // MOE_GRAN

        def wait_one(gi, _):
            pltpu.make_async_copy(granule(hloc_ref.at[slot_], 0), granule(hs_hbm, 0), sem.at[slot_]).wait()
            return 0

        lax.fori_loop(0, n, wait_one, 0)

    @pl.when(b > 0)
    def _():
        wait_block(jnp.maximum(b - 1, 0), 1 - slot)

    @pl.when(b == pl.num_programs(0) - 1)
    def _():
        wait_block(b, slot)
        zero_ref[...] = jnp.zeros_like(zero_ref)
        n_fill = 0
        for e in range(N_EXPERTS):
            first = fill_ref[0, e]
            n_e = (fill_ref[1, e] - first) // MOE_GRAN

            def fill(gi, _, first=first):
                pltpu.make_async_copy(granule(zero_ref, 0), granule(hs_hbm, first + gi * MOE_GRAN),
                                      sem.at[slot]).start()
                return 0

            lax.fori_loop(0, n_e, fill, 0)
            n_fill = n_fill + n_e

        def wait_fill(gi, _):
            pltpu.make_async_copy(granule(zero_ref, 0), granule(hs_hbm, 0), sem.at[slot]).wait()
            return 0

        lax.fori_loop(0, n_fill, wait_fill, 0)

        def tile_at(ti):
            return hs_hbm.at[pl.ds(pl.multiple_of(ti * MOE_TILE, MOE_TILE), MOE_TILE)]

        n_tiles = tile_ref[1, 0]

        def fill_tile(ti, _):
            pltpu.make_async_copy(zero_ref, tile_at(ti), sem.at[slot]).start()
            return 0

        def wait_tile(ti, _):
            pltpu.make_async_copy(zero_ref, tile_at(0), sem.at[slot]).wait()
            return 0

        lax.fori_loop(n_tiles, MOE_MAX_TILES, fill_tile, 0)
        lax.fori_loop(n_tiles, MOE_MAX_TILES, wait_tile, 0)


def _moe_dispatch(block_tab, tile_tab, fill_tab, x2d, g, route):
    grid_spec = pltpu.PrefetchScalarGridSpec(
        num_scalar_prefetch=3,
        grid=(N_MOE_BLOCKS,),
        in_specs=[
            pl.BlockSpec((MOE_BLOCK, D_MODEL), lambda b, *_: (b, 0)),
            pl.BlockSpec((1, D_MODEL), lambda b, *_: (0, 0)),
            pl.BlockSpec((ROUTE_ROWS, MOE_BLOCK), lambda b, *_: (0, b)),
        ],
        out_specs=pl.BlockSpec(memory_space=pl.ANY),
        scratch_shapes=[
            pltpu.VMEM((2, MOE_LOCAL_ROWS, D_MODEL), BF16),
            pltpu.VMEM((MOE_TILE, D_MODEL), BF16),
            pltpu.SemaphoreType.DMA((2,)),
        ],
    )
    return pl.pallas_call(
        _moe_dispatch_kernel,
        grid_spec=grid_spec,
        out_shape=jax.ShapeDtypeStruct((MOE_ROWS, D_MODEL), BF16),
        compiler_params=pltpu.CompilerParams(
            dimension_semantics=("arbitrary",), vmem_limit_bytes=VMEM_LIMIT),
        name="moe_dispatch",
    )(block_tab, tile_tab, fill_tab, x2d, g, route)


def _moe_expert_kernel(tile_ref, hs_ref, wg_ref, wu_ref, wd_ref, y_ref, acc_ref):
    t = pl.program_id(0)
    c = pl.program_id(1)
    active = t < tile_ref[1, 0]

    @pl.when(jnp.logical_and(jnp.logical_not(active), c == 0))
    def _():
        y_ref[...] = jnp.zeros_like(y_ref)

    @pl.when(active)
    def _():
        @pl.when(c == 0)
        def _():
            acc_ref[...] = jnp.zeros_like(acc_ref)

        def swiglu_rows(n_rows):
            h = hs_ref[0:n_rows, :]
            gate = jnp.dot(h, wg_ref[0].astype(BF16), preferred_element_type=F32)
            up = jnp.dot(h, wu_ref[0].astype(BF16), preferred_element_type=F32)
            a = (gate * jax.nn.sigmoid(gate) * up).astype(BF16)
            acc_ref[0:n_rows, :] += jnp.dot(a, wd_ref[0].astype(BF16), preferred_element_type=F32)

        half_only = tile_ref[2, t] <= MOE_TILE // 2

        @pl.when(half_only)
        def _():
            swiglu_rows(MOE_TILE // 2)

        @pl.when(jnp.logical_not(half_only))
        def _():
            swiglu_rows(MOE_TILE)

        @pl.when(c == pl.num_programs(1) - 1)
        def _():
            y_ref[...] = acc_ref[...].astype(BF16)


def _moe_experts(tile_tab, hs, w_gu, w_down, fc=512):
    nc = FFN_DIM // fc

    def tile_of(t, tr):
        return jnp.maximum(jnp.minimum(t, tr[1, 0] - 1), 0)

    def chunk_of(t, c, tr):
        return jnp.where(t < tr[1, 0], c, nc - 1)

    grid_spec = pltpu.PrefetchScalarGridSpec(
        num_scalar_prefetch=1,
        grid=(MOE_MAX_TILES, nc),
        in_specs=[
            pl.BlockSpec((MOE_TILE, D_MODEL), lambda t, c, tr: (tile_of(t, tr), 0)),
            pl.BlockSpec((1, D_MODEL, fc), lambda t, c, tr: (tr[0, t], 0, chunk_of(t, c, tr))),
            pl.BlockSpec((1, D_MODEL, fc), lambda t, c, tr: (tr[0, t], 0, chunk_of(t, c, tr) + nc)),
            pl.BlockSpec((1, fc, D_MODEL), lambda t, c, tr: (tr[0, t], chunk_of(t, c, tr), 0)),
        ],
        out_specs=pl.BlockSpec((MOE_TILE, D_MODEL), lambda t, c, tr: (t, 0)),
        scratch_shapes=[pltpu.VMEM((MOE_TILE, D_MODEL), F32)],
    )
    return pl.pallas_call(
        _moe_expert_kernel,
        grid_spec=grid_spec,
        out_shape=jax.ShapeDtypeStruct((MOE_ROWS, D_MODEL), BF16),
        compiler_params=pltpu.CompilerParams(
            dimension_semantics=("arbitrary", "arbitrary"), vmem_limit_bytes=VMEM_LIMIT),
        name="moe_experts",
    )(tile_tab, hs, w_gu, w_gu, w_down)


def _moe_combine_kernel(meta_ref, x_ref, route_ref, y_hbm, gf_ref, out_ref, yloc_ref, sem):
    b = pl.program_id(0)
    slot = b % 2

    def granule(ref, row):
        return ref.at[pl.ds(pl.multiple_of(row, MOE_GRAN), MOE_GRAN)]

    def start_block(blk, slot_):
        for e in range(N_EXPERTS):
            lo = meta_ref[blk, TAB_LOFF + e]
            goff = meta_ref[blk, TAB_GOFF + e]

            def start(gi, _, lo=lo, goff=goff):
                pltpu.make_async_copy(granule(y_hbm, goff + gi * MOE_GRAN),
                                      granule(yloc_ref.at[slot_], lo + gi * MOE_GRAN), sem.at[slot_]).start()
                return 0

            lax.fori_loop(0, meta_ref[blk, TAB_PC + e] // MOE_GRAN, start, 0)

    @pl.when(b == 0)
    def _():
        yloc_ref[...] = jnp.zeros_like(yloc_ref)
        start_block(0, 0)

    @pl.when(b + 1 < pl.num_programs(0))
    def _():
        start_block(jnp.minimum(b + 1, pl.num_programs(0) - 1), 1 - slot)

    n_mine = 0
    for e in range(N_EXPERTS):
        n_mine = n_mine + meta_ref[b, TAB_PC + e] // MOE_GRAN

    def wait_one(gi, _):
        pltpu.make_async_copy(granule(y_hbm, 0), granule(yloc_ref.at[slot], 0), sem.at[slot]).wait()
        return 0

    lax.fori_loop(0, n_mine, wait_one, 0)

    route_t = route_ref[...]
    route = route_t.T
    col = lax.broadcasted_iota(jnp.int32, (MOE_BLOCK, MOE_LOCAL_ROWS), 1).astype(F32)
    pick1 = jnp.where(col == route[:, ROUTE_D1:ROUTE_D1 + 1], 1.0, 0.0).astype(BF16)
    pick2 = jnp.where(col == route[:, ROUTE_D2:ROUTE_D2 + 1], 1.0, 0.0).astype(BF16)
    yl = yloc_ref[slot]
    moe = (route[:, ROUTE_G1:ROUTE_G1 + 1] * jnp.dot(pick1, yl, preferred_element_type=F32)
           + route[:, ROUTE_G2:ROUTE_G2 + 1] * jnp.dot(pick2, yl, preferred_element_type=F32))
    out_ref[...] = _rmsnorm(x_ref[...] + moe, gf_ref[...])


def _moe_combine(meta, x2d, route, y, g_final):
    grid_spec = pltpu.PrefetchScalarGridSpec(
        num_scalar_prefetch=1,
        grid=(N_MOE_BLOCKS,),
        in_specs=[
            pl.BlockSpec((MOE_BLOCK, D_MODEL), lambda b, m: (b, 0)),
            pl.BlockSpec((ROUTE_ROWS, MOE_BLOCK), lambda b, m: (0, b)),
            pl.BlockSpec(memory_space=pl.ANY),
            pl.BlockSpec((1, D_MODEL), lambda b, m: (0, 0)),
        ],
        out_specs=pl.BlockSpec((MOE_BLOCK, D_MODEL), lambda b, m: (b, 0)),
        scratch_shapes=[
            pltpu.VMEM((2, MOE_LOCAL_ROWS, D_MODEL), BF16),
            pltpu.SemaphoreType.DMA((2,)),
        ],
    )
    return pl.pallas_call(
        _moe_combine_kernel,
        grid_spec=grid_spec,
        out_shape=jax.ShapeDtypeStruct((N_TOK, D_MODEL), F32),
        compiler_params=pltpu.CompilerParams(
            dimension_semantics=("arbitrary",), vmem_limit_bytes=VMEM_LIMIT),
        name="moe_combine",
    )(meta, x2d, route, y, g_final)


def _moe(x2d, g, w_router, w_gu, w_down, g_final):
    route, meta = _moe_route(x2d, g, w_router)
    block_tab, tile_tab, fill_tab = _moe_plan(meta)
    hs = _moe_dispatch(block_tab, tile_tab, fill_tab, x2d, g, route)
    y = _moe_experts(tile_tab, hs, w_gu, w_down)
    return _moe_combine(block_tab, x2d, route, y, g_final)


def _rope_tables():
    def tables(dim):
        inv_freq = ROPE_THETA ** (-jnp.arange(0, dim, 2, dtype=F32) / dim)
        ang = jnp.arange(SEQ, dtype=F32)[:, None] * inv_freq[None, :]
        return jnp.cos(ang), jnp.sin(ang)

    c, s = tables(HEAD_DIM)
    cq = jnp.concatenate([c, c], axis=1)
    sq = jnp.concatenate([-s, s], axis=1)
    c, s = tables(IDX_ROPE_DIM)
    ones = jnp.ones((SEQ, IDX_DIM - IDX_ROPE_DIM), F32)
    zeros = jnp.zeros_like(ones)
    z16 = jnp.zeros_like(s)
    ci = jnp.concatenate([c, c, ones], axis=1)
    sia = jnp.concatenate([-s, z16, zeros], axis=1)
    sib = jnp.concatenate([z16, s, zeros], axis=1)
    rep = LANES // IDX_DIM
    return cq, sq, jnp.tile(ci, (1, rep)), jnp.tile(sia, (1, rep)), jnp.tile(sib, (1, rep))


def kernel(x, norm_mix, norm_ffn, dsa_w_in, dsa_idx_k_gain, dsa_w_out, ffn_w_gu, ffn_w_down, gmlp_w_in,
           gmlp_ln_gain, gmlp_ln_bias, gmlp_w_spatial, gmlp_b_spatial, gmlp_w_out, moe_w_router, moe_w_gu,
           moe_w_down, final_norm):
    x2d = x.reshape(N_TOK, D_MODEL)
    row = lambda v: v.reshape(1, -1)

    w_in = dsa_w_in[0]
    w_all = jnp.pad(w_in, ((0, 0), (0, DSA_COLS - w_in.shape[1])))
    wwi_t = w_in[:, COL_KI + IDX_DIM:COL_KI + IDX_DIM + IDX_HEADS].T
    kig = jnp.pad(dsa_idx_k_gain[0], (0, LANES - IDX_DIM)).reshape(1, LANES)
    q3, k, vt, kia, kib, qi, wt = _dsa_proj(x2d, row(norm_mix[0]), w_all, wwi_t, kig, _rope_tables())
    o3 = _dsa_attn(q3, k, vt, kia, kib, qi, wt)
    x2d = _resid_proj(x2d, o3, dsa_w_out[0])
    x2d = _ffn(x2d, row(norm_ffn[0]), ffn_w_gu[0], ffn_w_down[0])

    gw = GMLP_WIDTH // GMLP_GROUPS
    b_sp_full = jnp.repeat(gmlp_b_spatial[0].T, gw, axis=1)
    x2d = _gmlp(x2d, row(norm_mix[1]), gmlp_w_in[0], row(gmlp_ln_gain[0]), row(gmlp_ln_bias[0]),
                gmlp_w_spatial[0], b_sp_full, gmlp_w_out[0])
    w_router = jnp.pad(moe_w_router[0], ((0, 0), (0, LANES - N_EXPERTS)))
    out = _moe(x2d, row(norm_ffn[1]), w_router, moe_w_gu[0], moe_w_down[0], row(final_norm))
    return out.reshape(BATCH, SEQ, D_MODEL)
```

```python
import functools

import numpy as np
import jax
import jax.numpy as jnp
from jax import lax
from jax.experimental import pallas as pl
from jax.experimental.pallas import tpu as pltpu

F32 = jnp.float32
BF16 = jnp.bfloat16

D_MODEL = 1024
BATCH = 8
SEQ = 2048
N_TOK = BATCH * SEQ
N_HEADS = 8
HEAD_DIM = 128
IDX_HEADS = 8
IDX_DIM = 64
IDX_ROPE_DIM = 32
INDEX_TOPK = 256
Q_BLOCK = 128
KEY_SUPER = 512
ROPE_THETA = 10000.0
GMLP_WIDTH = 1024
GMLP_GROUPS = 8
GMLP_CHUNK = 128
FFN_DIM = 3584
N_EXPERTS = 8
NORM_EPS = 1e-6

LANES = 128
SUBLANES = 8
VMEM_LIMIT = 56 * 1024 * 1024

DSA_COLS = 1920
COL_K = N_HEADS * HEAD_DIM
COL_V = COL_K + HEAD_DIM
COL_QI = COL_V + HEAD_DIM
COL_KI = COL_QI + IDX_HEADS * IDX_DIM

NT_DIMS = (((1,), (1,)), ((), ()))
MASK_BIAS = -1e30
SCORE_MASKED = -3.0e38


def _rmsnorm(x, g):
    ms = jnp.mean(x * x, axis=-1, keepdims=True)
    return x * lax.rsqrt(ms + NORM_EPS) * g


def _cast_once(src_ref, dst_ref):
    @pl.when(pl.program_id(0) == 0)
    def _():
        dst_ref[...] = src_ref[...].astype(BF16)


def _resident(shape):
    return pl.BlockSpec(shape, lambda i: (0,) * len(shape), pipeline_mode=pl.Buffered(1))


def _dsa_proj_kernel(x_ref, g_ref, w_ref, wwi_ref, kig_ref, cq_ref, sq_ref, ci_ref, sia_ref, sib_ref,
                     q_ref, k_ref, vt_ref, kia_ref, kib_ref, qi_ref, wt_ref, wbf_ref):
    _cast_once(w_ref, wbf_ref)
    h = _rmsnorm(x_ref[...], g_ref[...]).astype(BF16)
    y = jnp.dot(h, wbf_ref[...], preferred_element_type=F32)
    cq = cq_ref[...]
    sq = sq_ref[...]
    ci = ci_ref[...]
    sia = sia_ref[...]
    sib = sib_ref[...]

    def rope_head(t):
        return t * cq + pltpu.roll(t, HEAD_DIM // 2, 1) * sq

    def rope_idx(t):
        half = IDX_ROPE_DIM // 2
        return t * ci + pltpu.roll(t, LANES - half, 1) * sia + pltpu.roll(t, half, 1) * sib

    q_scale = HEAD_DIM ** -0.5 * np.log2(np.e)
    for hh in range(N_HEADS):
        q_ref[hh] = (rope_head(y[:, hh * HEAD_DIM:(hh + 1) * HEAD_DIM]) * q_scale).astype(BF16)
    k_ref[...] = rope_head(y[:, COL_K:COL_V]).astype(BF16)
    vt_ref[0] = y[:, COL_V:COL_QI].T.astype(BF16)
    for p in range(IDX_HEADS * IDX_DIM // LANES):
        qi_ref[:, p * LANES:(p + 1) * LANES] = rope_idx(
            y[:, COL_QI + p * LANES:COL_QI + (p + 1) * LANES]).astype(BF16)

    last = y[:, COL_KI:COL_KI + LANES]
    lane = lax.broadcasted_iota(jnp.int32, last.shape, 1)
    kraw = jnp.where(lane < IDX_DIM, last, 0.0)
    kms = jnp.sum(kraw * kraw, axis=-1, keepdims=True) * (1.0 / IDX_DIM)
    kn = kraw * lax.rsqrt(kms + NORM_EPS) * kig_ref[...]
    kr = rope_idx(kn)
    kia_ref[...] = kr.astype(BF16)
    kib_ref[...] = pltpu.roll(kr, IDX_DIM, 1).astype(BF16)

    w_scale = IDX_HEADS ** -0.5 * IDX_DIM ** -0.5
    wt_ref[...] = lax.dot_general(wwi_ref[...].astype(BF16), h, NT_DIMS, preferred_element_type=F32) * w_scale


def _dsa_proj(x2d, g, w_all, wwi_t, kig, tabs, tm=KEY_SUPER):
    n = x2d.shape[0]
    nblk_seq = SEQ // tm
    tab_spec = pl.BlockSpec((tm, LANES), lambda i: (i % nblk_seq, 0))
    full = lambda shape: pl.BlockSpec(shape, lambda i: (0,) * len(shape))
    return pl.pallas_call(
        _dsa_proj_kernel,
        grid=(n // tm,),
        in_specs=[
            pl.BlockSpec((tm, D_MODEL), lambda i: (i, 0)),
            full((1, D_MODEL)),
            _resident((D_MODEL, DSA_COLS)),
            full((IDX_HEADS, D_MODEL)),
            full((1, LANES)),
            tab_spec, tab_spec, tab_spec, tab_spec, tab_spec,
        ],
        out_specs=[
            pl.BlockSpec((N_HEADS, tm, HEAD_DIM), lambda i: (0, i, 0)),
            pl.BlockSpec((tm, HEAD_DIM), lambda i: (i, 0)),
            pl.BlockSpec((1, HEAD_DIM, tm), lambda i: (i, 0, 0)),
            pl.BlockSpec((tm, LANES), lambda i: (i, 0)),
            pl.BlockSpec((tm, LANES), lambda i: (i, 0)),
            pl.BlockSpec((tm, IDX_HEADS * IDX_DIM), lambda i: (i, 0)),
            pl.BlockSpec((IDX_HEADS, tm), lambda i: (0, i)),
        ],
        out_shape=[
            jax.ShapeDtypeStruct((N_HEADS, n, HEAD_DIM), BF16),
            jax.ShapeDtypeStruct((n, HEAD_DIM), BF16),
            jax.ShapeDtypeStruct((n // tm, HEAD_DIM, tm), BF16),
            jax.ShapeDtypeStruct((n, LANES), BF16),
            jax.ShapeDtypeStruct((n, LANES), BF16),
            jax.ShapeDtypeStruct((n, IDX_HEADS * IDX_DIM), BF16),
            jax.ShapeDtypeStruct((IDX_HEADS, n), F32),
        ],
        scratch_shapes=[pltpu.VMEM((D_MODEL, DSA_COLS), BF16)],
        compiler_params=pltpu.CompilerParams(
            dimension_semantics=("arbitrary",), vmem_limit_bytes=VMEM_LIMIT),
        name="dsa_proj",
    )(x2d, g, w_all, wwi_t, kig, *tabs)


def _ordered_bits_to_float(u):
    key = u ^ jnp.int32(-2 ** 31)
    key = jnp.maximum(key, jnp.int32(-2139095041))
    bits = jnp.where(key >= 0, key, key ^ jnp.int32(0x7FFFFFFF))
    return lax.bitcast_convert_type(bits, F32)


def _dsa_attn_kernel(q_ref, k_ref, vt_ref, kia_ref, kib_ref, qi_ref, wt_ref, o_ref,
                     s_ref, sel_ref, l_ref, p_ref, acc_ref):
    j = pl.program_id(1)
    per_super = KEY_SUPER // Q_BLOCK
    n_super = j // per_super + 1
    hq = N_HEADS * Q_BLOCK

    qpos = j * Q_BLOCK + lax.broadcasted_iota(jnp.int32, (Q_BLOCK, Q_BLOCK), 1)
    krow = lax.broadcasted_iota(jnp.int32, (Q_BLOCK, Q_BLOCK), 0)

    def rows_of(sc, c4):
        return pl.ds(pl.multiple_of(sc * KEY_SUPER, KEY_SUPER) + c4 * Q_BLOCK, Q_BLOCK)

    def super_rows(sc):
        return pl.ds(pl.multiple_of(sc * KEY_SUPER, KEY_SUPER), KEY_SUPER)

    def causal(sc, c4):
        return (krow + (sc * KEY_SUPER + c4 * Q_BLOCK)) <= qpos

    def fold8(t):
        return t.reshape(t.shape[0] // SUBLANES, SUBLANES, t.shape[1])

    def score_body(sc, _):
        srows = super_rows(sc)
        ka = kia_ref[srows, :]
        kb = kib_ref[srows, :]
        for half in range(IDX_HEADS // 4):
            p0 = 2 * half
            qpp = jnp.concatenate(
                [qi_ref[:, p0 * LANES:(p0 + 1) * LANES], qi_ref[:, (p0 + 1) * LANES:(p0 + 2) * LANES]], axis=0)
            ra = lax.dot_general(ka, qpp, NT_DIMS, preferred_element_type=F32)
            rb = lax.dot_general(kb, qpp, NT_DIMS, preferred_element_type=F32)
            term = None
            for sub in range(2):
                p = p0 + sub
                cols = slice(sub * Q_BLOCK, (sub + 1) * Q_BLOCK)
                t2 = (jnp.maximum(ra[:, cols], 0.0) * wt_ref[2 * p:2 * p + 1, :]
                      + jnp.maximum(rb[:, cols], 0.0) * wt_ref[2 * p + 1:2 * p + 2, :])
                term = t2 if term is None else term + t2
            if half == 0:
                s_ref[srows, :] = term
            else:
                kpos = sc * KEY_SUPER + lax.broadcasted_iota(jnp.int32, (KEY_SUPER, Q_BLOCK), 0)
                qp_s = j * Q_BLOCK + lax.broadcasted_iota(jnp.int32, (KEY_SUPER, Q_BLOCK), 1)
                s_ref[srows, :] = jnp.where(kpos <= qp_s, s_ref[srows, :] + term, SCORE_MASKED)
        return 0

    lax.fori_loop(0, n_super, score_body, 0)

    def search(n_chunks):
        def count_ge(cand):
            accs = [jnp.zeros((SUBLANES, LANES), F32), jnp.zeros((SUBLANES, LANES), F32)]
            for c in range(n_chunks):
                ind = jnp.where(s_ref[pl.ds(c * Q_BLOCK, Q_BLOCK), :] >= cand, 1.0, 0.0)
                accs[c % 2] = accs[c % 2] + fold8(ind).sum(axis=0)
            return jnp.sum(accs[0] + accs[1], axis=0, keepdims=True)

        def bit_body(i, carry):
            prefix, cbest = carry
            cu = prefix | lax.shift_left(jnp.int32(1), 31 - i)
            cnt = count_ge(_ordered_bits_to_float(cu))
            ok = cnt >= float(INDEX_TOPK)
            return jnp.where(ok, cu, prefix), jnp.where(ok, cnt, cbest)

        prefix, cbest = lax.fori_loop(
            0, 32, bit_body,
            (jnp.zeros((1, LANES), jnp.int32), jnp.full((1, LANES), float(n_chunks * Q_BLOCK), F32)))
        sel_ref[0:1, :] = _ordered_bits_to_float(prefix)
        sel_ref[1:2, :] = cbest

    for n_chunks in range(1, SEQ // Q_BLOCK + 1):
        pl.when(j == n_chunks - 1)(functools.partial(search, n_chunks))
    tau = sel_ref[0:1, :]
    drop = sel_ref[1:2, :] - float(INDEX_TOPK)

    upper = jnp.where(krow < lax.broadcasted_iota(jnp.int32, (Q_BLOCK, Q_BLOCK), 1), 1.0, 0.0).astype(BF16)
    pair_w = 2 * Q_BLOCK

    def select_logit_body(i, carry):
        later_ties, m_acc = carry
        sc = n_super - 1 - i
        biases = [None] * per_super
        for c4 in reversed(range(per_super)):
            s = s_ref[rows_of(sc, c4), :]
            eq = jnp.where(s == tau, 1.0, 0.0)
            ties_after = jnp.dot(upper, eq.astype(BF16), preferred_element_type=F32) + later_ties
            later_ties = later_ties + jnp.sum(eq, axis=0, keepdims=True)
            tie_bias = jnp.where(ties_after >= drop, 0.0, MASK_BIAS)
            bias = jnp.where(s > tau, 0.0, jnp.where(s == tau, tie_bias, MASK_BIAS))
            biases[c4] = jnp.where(causal(sc, c4), bias, MASK_BIAS)
        bias = jnp.concatenate(biases, axis=0)
        bias2 = jnp.concatenate([bias, bias], axis=1)
        srows = super_rows(sc)
        kc = k_ref[srows, :]
        parts = []
        for pr in range(N_HEADS // 2):
            qpair = q_ref[2 * pr:2 * pr + 2].reshape(pair_w, HEAD_DIM)
            lg = lax.dot_general(kc, qpair, NT_DIMS, preferred_element_type=F32) + bias2
            l_ref[srows, pr * pair_w:(pr + 1) * pair_w] = lg
            parts.append(fold8(lg).max(axis=0))
        return later_ties, jnp.maximum(m_acc, jnp.concatenate(parts, axis=1))

    _, m_acc = lax.fori_loop(
        0, n_super, select_logit_body,
        (jnp.zeros((1, LANES), F32), jnp.full((SUBLANES, hq), MASK_BIAS, F32)))
    m = jnp.max(m_acc, axis=0, keepdims=True)

    def prob_chunk(sc, d_acc):
        for c4 in range(per_super):
            rows = rows_of(sc, c4)
            parts = []
            for pr in range(N_HEADS // 2):
                cols = slice(pr * pair_w, (pr + 1) * pair_w)
                p = jnp.exp2(l_ref[rows, cols] - m[:, cols])
                p_ref[rows, cols] = p.astype(BF16)
                parts.append(fold8(p).sum(axis=0))
            d_acc = d_acc + jnp.concatenate(parts, axis=1)
        return d_acc

    def pv_chunk(sc):
        acc_ref[...] += jnp.dot(vt_ref[sc], p_ref[super_rows(sc), :], preferred_element_type=F32)

    def prob_pv_body(sc, d_acc):
        pv_chunk(sc - 1)
        return prob_chunk(sc, d_acc)

    acc_ref[...] = jnp.zeros_like(acc_ref)
    d_acc = prob_chunk(0, jnp.zeros((SUBLANES, hq), F32))
    d_acc = lax.fori_loop(1, n_super, prob_pv_body, d_acc)
    pv_chunk(n_super - 1)
    inv = 1.0 / jnp.sum(d_acc, axis=0, keepdims=True)
    for h in range(N_HEADS):
        cols = slice(h * Q_BLOCK, (h + 1) * Q_BLOCK)
        o_ref[h] = (acc_ref[:, cols] * inv[:, cols]).T.astype(BF16)


def _dsa_attn(q3, k, vt, kia, kib, qi, wt):
    nqb = SEQ // Q_BLOCK
    n_sup = SEQ // KEY_SUPER
    hq = N_HEADS * Q_BLOCK
    return pl.pallas_call(
        _dsa_attn_kernel,
        grid=(BATCH, nqb),
        in_specs=[
            pl.BlockSpec((N_HEADS, Q_BLOCK, HEAD_DIM), lambda b, j: (0, b * nqb + j, 0)),
            pl.BlockSpec((SEQ, HEAD_DIM), lambda b, j: (b, 0)),
            pl.BlockSpec((n_sup, HEAD_DIM, KEY_SUPER), lambda b, j: (b, 0, 0)),
            pl.BlockSpec((SEQ, LANES), lambda b, j: (b, 0)),
            pl.BlockSpec((SEQ, LANES), lambda b, j: (b, 0)),
            pl.BlockSpec((Q_BLOCK, IDX_HEADS * IDX_DIM), lambda b, j: (b * nqb + j, 0)),
            pl.BlockSpec((IDX_HEADS, Q_BLOCK), lambda b, j: (0, b * nqb + j)),
        ],
        out_specs=pl.BlockSpec((N_HEADS, Q_BLOCK, HEAD_DIM), lambda b, j: (0, b * nqb + j, 0)),
        out_shape=jax.ShapeDtypeStruct((N_HEADS, N_TOK, HEAD_DIM), BF16),
        scratch_shapes=[
            pltpu.VMEM((SEQ, Q_BLOCK), F32),
            pltpu.VMEM((SUBLANES, Q_BLOCK), F32),
            pltpu.VMEM((SEQ, hq), F32),
            pltpu.VMEM((SEQ, hq), BF16),
            pltpu.VMEM((HEAD_DIM, hq), F32),
        ],
        compiler_params=pltpu.CompilerParams(
            dimension_semantics=("parallel", "arbitrary"), vmem_limit_bytes=VMEM_LIMIT),
        name="dsa_attn",
    )(q3, k, vt, kia, kib, qi, wt)


def _resid_proj_kernel(x_ref, o_ref, w_ref, out_ref, wbf_ref):
    _cast_once(w_ref, wbf_ref)
    o = jnp.concatenate([o_ref[h] for h in range(N_HEADS)], axis=1)
    out_ref[...] = x_ref[...] + jnp.dot(o, wbf_ref[...], preferred_element_type=F32)


def _resid_proj(x2d, o3, w_out, tm=1024):
    n = x2d.shape[0]
    return pl.pallas_call(
        _resid_proj_kernel,
        grid=(n // tm,),
        in_specs=[
            pl.BlockSpec((tm, D_MODEL), lambda i: (i, 0)),
            pl.BlockSpec((N_HEADS, tm, HEAD_DIM), lambda i: (0, i, 0)),
            _resident((D_MODEL, D_MODEL)),
        ],
        out_specs=pl.BlockSpec((tm, D_MODEL), lambda i: (i, 0)),
        out_shape=jax.ShapeDtypeStruct((n, D_MODEL), F32),
        scratch_shapes=[pltpu.VMEM((D_MODEL, D_MODEL), BF16)],
        compiler_params=pltpu.CompilerParams(
            dimension_semantics=("arbitrary",), vmem_limit_bytes=VMEM_LIMIT),
        name="dsa_out_proj",
    )(x2d, o3, w_out)


def _ffn_kernel(x_ref, g_ref, wg_ref, wu_ref, wd_ref, out_ref, h_ref, acc_ref):
    c = pl.program_id(1)

    @pl.when(c == 0)
    def _():
        h_ref[...] = _rmsnorm(x_ref[...], g_ref[...]).astype(BF16)
        acc_ref[...] = jnp.zeros_like(acc_ref)

    h = h_ref[...]
    gate = jnp.dot(h, wg_ref[...].astype(BF16), preferred_element_type=F32)
    up = jnp.dot(h, wu_ref[...].astype(BF16), preferred_element_type=F32)
    a = (gate * jax.nn.sigmoid(gate) * up).astype(BF16)
    acc_ref[...] += jnp.dot(a, wd_ref[...].astype(BF16), preferred_element_type=F32)

    @pl.when(c == pl.num_programs(1) - 1)
    def _():
        out_ref[...] = x_ref[...] + acc_ref[...]


def _ffn(x2d, g, w_gu, w_down, tm=1024, fc=512):
    n = x2d.shape[0]
    nc = FFN_DIM // fc
    return pl.pallas_call(
        _ffn_kernel,
        grid=(n // tm, nc),
        in_specs=[
            pl.BlockSpec((tm, D_MODEL), lambda i, c: (i, 0)),
            pl.BlockSpec((1, D_MODEL), lambda i, c: (0, 0)),
            pl.BlockSpec((D_MODEL, fc), lambda i, c: (0, c)),
            pl.BlockSpec((D_MODEL, fc), lambda i, c: (0, c + nc)),
            pl.BlockSpec((fc, D_MODEL), lambda i, c: (c, 0)),
        ],
        out_specs=pl.BlockSpec((tm, D_MODEL), lambda i, c: (i, 0)),
        out_shape=jax.ShapeDtypeStruct((n, D_MODEL), F32),
        scratch_shapes=[pltpu.VMEM((tm, D_MODEL), BF16), pltpu.VMEM((tm, D_MODEL), F32)],
        compiler_params=pltpu.CompilerParams(
            dimension_semantics=("parallel", "arbitrary"), vmem_limit_bytes=VMEM_LIMIT),
        name="swiglu_ffn",
    )(x2d, g, w_gu, w_gu, w_down)


def _gelu_tanh(x):
    return 0.5 * x * (1.0 + jnp.tanh(np.sqrt(2.0 / np.pi) * (x + 0.044715 * (x * x * x))))


def _gmlp_kernel(x_ref, g_ref, win_ref, lng_ref, lnb_ref, ws_ref, bs_ref, wout_ref, out_ref,
                 winbf_ref, woutbf_ref):
    _cast_once(win_ref, winbf_ref)
    _cast_once(wout_ref, woutbf_ref)
    x = x_ref[...]
    tm = x.shape[0]
    h = _rmsnorm(x, g_ref[...]).astype(BF16)
    y = _gelu_tanh(jnp.dot(h, winbf_ref[...], preferred_element_type=F32))
    u = y[:, :GMLP_WIDTH]
    v = y[:, GMLP_WIDTH:]
    mu = jnp.mean(v, axis=-1, keepdims=True)
    vc = v - mu
    var = jnp.mean(vc * vc, axis=-1, keepdims=True)
    vb = (vc * lax.rsqrt(var + NORM_EPS) * lng_ref[...] + lnb_ref[...]).astype(BF16)

    gw = GMLP_WIDTH // GMLP_GROUPS
    row = lax.broadcasted_iota(jnp.int32, (GMLP_CHUNK, GMLP_CHUNK), 0)
    col = lax.broadcasted_iota(jnp.int32, (GMLP_CHUNK, GMLP_CHUNK), 1)
    w_tril = [jnp.where(row >= col, ws_ref[gi], 0.0).astype(BF16) for gi in range(GMLP_GROUPS)]
    rows = []
    for c in range(tm // GMLP_CHUNK):
        parts = [
            jnp.dot(w_tril[gi], vb[c * GMLP_CHUNK:(c + 1) * GMLP_CHUNK, gi * gw:(gi + 1) * gw],
                    preferred_element_type=F32)
            for gi in range(GMLP_GROUPS)
        ]
        rows.append(jnp.concatenate(parts, axis=1) + bs_ref[...])
    s = jnp.concatenate(rows, axis=0)
    z = (u * s).astype(BF16)
    out_ref[...] = x + jnp.dot(z, woutbf_ref[...], preferred_element_type=F32)


def _gmlp(x2d, g, w_in, ln_g, ln_b, w_sp, b_sp_full, w_out, tm=512):
    n = x2d.shape[0]
    full = lambda shape: pl.BlockSpec(shape, lambda i: (0,) * len(shape))
    return pl.pallas_call(
        _gmlp_kernel,
        grid=(n // tm,),
        in_specs=[
            pl.BlockSpec((tm, D_MODEL), lambda i: (i, 0)),
            full((1, D_MODEL)),
            _resident((D_MODEL, 2 * GMLP_WIDTH)),
            full((1, GMLP_WIDTH)),
            full((1, GMLP_WIDTH)),
            full((GMLP_GROUPS, GMLP_CHUNK, GMLP_CHUNK)),
            full((GMLP_CHUNK, GMLP_WIDTH)),
            _resident((GMLP_WIDTH, D_MODEL)),
        ],
        out_specs=pl.BlockSpec((tm, D_MODEL), lambda i: (i, 0)),
        out_shape=jax.ShapeDtypeStruct((n, D_MODEL), F32),
        scratch_shapes=[pltpu.VMEM((D_MODEL, 2 * GMLP_WIDTH), BF16), pltpu.VMEM((GMLP_WIDTH, D_MODEL), BF16)],
        compiler_params=pltpu.CompilerParams(
            dimension_semantics=("arbitrary",), vmem_limit_bytes=VMEM_LIMIT),
        name="gmlp",
    )(x2d, g, w_in, ln_g, ln_b, w_sp, b_sp_full, w_out)


MOE_BLOCK = 512
MOE_GRAN = 16
MOE_TILE = 1152
MOE_LOCAL_ROWS = 2 * MOE_BLOCK + N_EXPERTS * MOE_GRAN
N_MOE_BLOCKS = N_TOK // MOE_BLOCK
MOE_MAX_TILES = (2 * N_TOK + N_MOE_BLOCKS * N_EXPERTS * (MOE_GRAN - 1)) // MOE_TILE + N_EXPERTS
MOE_ROWS = MOE_MAX_TILES * MOE_TILE
ROUTE_D1, ROUTE_D2, ROUTE_G1, ROUTE_G2 = 0, 1, 2, 3
ROUTE_ROWS = N_EXPERTS
TAB_LOFF, TAB_PC, TAB_GOFF = 0, N_EXPERTS, 2 * N_EXPERTS


def _moe_route_kernel(x_ref, g_ref, wr_ref, route_ref, meta_ref):
    b = pl.program_id(0)
    t_blk = MOE_BLOCK
    hf = _rmsnorm(x_ref[...], g_ref[...])
    wr = wr_ref[...]
    h_hi = hf.astype(BF16)
    h_lo = (hf - h_hi.astype(F32)).astype(BF16)
    w_hi = wr.astype(BF16)
    w_lo = (wr - w_hi.astype(F32)).astype(BF16)
    logits = (jnp.dot(h_hi, w_hi, preferred_element_type=F32) + jnp.dot(h_hi, w_lo, preferred_element_type=F32)
              + jnp.dot(h_lo, w_hi, preferred_element_type=F32))

    lg = logits.T[0:N_EXPERTS, :]
    e_idx = lax.broadcasted_iota(jnp.int32, lg.shape, 0).astype(F32)
    neg_inf = -jnp.inf
    m1 = jnp.max(lg, axis=0, keepdims=True)
    i1 = jnp.min(jnp.where(lg == m1, e_idx, float(N_EXPERTS)), axis=0, keepdims=True)
    lg2 = jnp.where(e_idx == i1, neg_inf, lg)
    m2 = jnp.max(lg2, axis=0, keepdims=True)
    i2 = jnp.min(jnp.where(lg2 == m2, e_idx, float(N_EXPERTS)), axis=0, keepdims=True)
    t = jnp.exp(m2 - m1)
    g1 = 1.0 / (1.0 + t)
    g2 = t * g1

    memb = jnp.where(jnp.logical_or(e_idx == i1, e_idx == i2), 1.0, 0.0)
    tr = lax.broadcasted_iota(jnp.int32, (t_blk, t_blk), 0)
    tc = lax.broadcasted_iota(jnp.int32, (t_blk, t_blk), 1)
    earlier = jnp.where(tr < tc, 1.0, 0.0).astype(BF16)
    rank = jnp.dot(memb.astype(BF16), earlier, preferred_element_type=F32)
    loff_col = jnp.zeros((N_EXPERTS, 1), F32)
    e_col = lax.broadcasted_iota(jnp.int32, (N_EXPERTS, 1), 0)
    lo = jnp.int32(0)
    for e in range(N_EXPERTS):
        cnt = jnp.sum(memb[e:e + 1, :]).astype(jnp.int32)
        pce = (cnt + (MOE_GRAN - 1)) // MOE_GRAN * MOE_GRAN
        meta_ref[b, TAB_LOFF + e] = lo
        meta_ref[b, TAB_PC + e] = pce
        loff_col = jnp.where(e_col == e, lo.astype(F32), loff_col)
        lo = lo + pce
    dest = loff_col + rank
    d1 = jnp.sum(jnp.where(e_idx == i1, dest, 0.0), axis=0, keepdims=True)
    d2 = jnp.sum(jnp.where(e_idx == i2, dest, 0.0), axis=0, keepdims=True)
    r_idx = lax.broadcasted_iota(jnp.int32, lg.shape, 0)
    route_ref[...] = jnp.where(r_idx == ROUTE_D1, d1,
                               jnp.where(r_idx == ROUTE_D2, d2,
                                         jnp.where(r_idx == ROUTE_G1, g1, jnp.where(r_idx == ROUTE_G2, g2, 0.0))))


def _moe_route(x2d, g, w_router):
    return pl.pallas_call(
        _moe_route_kernel,
        grid=(N_MOE_BLOCKS,),
        in_specs=[
            pl.BlockSpec((MOE_BLOCK, D_MODEL), lambda b: (b, 0)),
            pl.BlockSpec((1, D_MODEL), lambda b: (0, 0)),
            pl.BlockSpec((D_MODEL, LANES), lambda b: (0, 0)),
        ],
        out_specs=[
            pl.BlockSpec((ROUTE_ROWS, MOE_BLOCK), lambda b: (0, b)),
            pl.BlockSpec(memory_space=pltpu.SMEM),
        ],
        out_shape=[
            jax.ShapeDtypeStruct((ROUTE_ROWS, N_TOK), F32),
            jax.ShapeDtypeStruct((N_MOE_BLOCKS, 2 * N_EXPERTS), jnp.int32),
        ],
        compiler_params=pltpu.CompilerParams(
            dimension_semantics=("arbitrary",), vmem_limit_bytes=VMEM_LIMIT),
        name="moe_route",
    )(x2d, g, w_router)


def _moe_plan(meta):
    pc = meta[:, TAB_PC:TAB_PC + N_EXPERTS]
    totals = jnp.sum(pc, axis=0)
    per_expert = (totals + (MOE_TILE - 1)) // MOE_TILE
    e_lt = jnp.arange(N_EXPERTS)[:, None] < jnp.arange(N_EXPERTS)[None, :]
    first_tile = jnp.sum(jnp.where(e_lt, per_expert[:, None], 0), axis=0)
    ends = first_tile + per_expert
    seg_start = first_tile * MOE_TILE
    b_lt = jnp.arange(N_MOE_BLOCKS)[:, None] < jnp.arange(N_MOE_BLOCKS)[None, :]
    goff = seg_start[None, :] + jnp.sum(jnp.where(b_lt[:, :, None], pc[:, None, :], 0), axis=0)
    block_tab = jnp.concatenate([meta, goff], axis=1).astype(jnp.int32)
    n_tiles = jnp.sum(per_expert)
    t = jnp.minimum(jnp.arange(MOE_MAX_TILES, dtype=jnp.int32), n_tiles - 1)
    mine = jnp.logical_and(t[:, None] >= first_tile[None, :], t[:, None] < ends[None, :])
    expert = jnp.sum(jnp.where(mine, jnp.arange(N_EXPERTS)[None, :], 0), axis=1)
    left = totals[None, :] - (t[:, None] - first_tile[None, :]) * MOE_TILE
    used = jnp.sum(jnp.where(mine, jnp.clip(left, 0, MOE_TILE), 0), axis=1)
    tile_tab = jnp.stack([expert, jnp.full_like(t, n_tiles), used]).astype(jnp.int32)
    fill_tab = jnp.stack([seg_start + totals, ends * MOE_TILE]).astype(jnp.int32)
    return block_tab, tile_tab, fill_tab


def _moe_dispatch_kernel(tab_ref, tile_ref, fill_ref, x_ref, g_ref, route_ref, hs_hbm, hloc_ref, zero_ref, sem):
    b = pl.program_id(0)
    slot = b % 2
    hloc = hloc_ref.at[slot]
    hb = _rmsnorm(x_ref[...], g_ref[...]).astype(BF16)

    route_t = route_ref[...]
    r_iota = lax.broadcasted_iota(jnp.int32, (MOE_LOCAL_ROWS, MOE_BLOCK), 0).astype(F32)
    hit = jnp.logical_or(r_iota == route_t[ROUTE_D1:ROUTE_D1 + 1, :], r_iota == route_t[ROUTE_D2:ROUTE_D2 + 1, :])
    perm = jnp.where(hit, 1.0, 0.0).astype(BF16)
    hloc_ref[slot] = jnp.dot(perm, hb, preferred_element_type=F32).astype(BF16)

    def granule(ref, row):
        return ref.at[pl.ds(pl.multiple_of(row, MOE_GRAN), MOE_GRAN)]

    for e in range(N_EXPERTS):
        lo = tab_ref[b, TAB_LOFF + e]
        goff = tab_ref[b, TAB_GOFF + e]

        def start(gi, _, lo=lo, goff=goff):
            pltpu.make_async_copy(granule(hloc, lo + gi * MOE_GRAN),
                                  granule(hs_hbm, goff + gi * MOE_GRAN), sem.at[slot]).start()
            return 0

        lax.fori_loop(0, tab_ref[b, TAB_PC + e] // MOE_GRAN, start, 0)

    def wait_block(blk, slot_):
        n = 0
        for e in range(N_EXPERTS):
            n = n + tab_ref[blk, TAB_PC + e] // MOE_GRAN

        def wait_one(gi, _):
            pltpu.make_async_copy(granule(hloc_ref.at[slot_], 0), granule(hs_hbm, 0), sem.at[slot_]).wait()
            return 0

        lax.fori_loop(0, n, wait_one, 0)

    @pl.when(b > 0)
    def _():
        wait_block(jnp.maximum(b - 1, 0), 1 - slot)

    @pl.when(b == pl.num_programs(0) - 1)
    def _():
        wait_block(b, slot)
        zero_ref[...] = jnp.zeros_like(zero_ref)
        n_fill = 0
        for e in range(N_EXPERTS):
            first = fill_ref[0, e]
            n_e = (fill_ref[1, e] - first) // MOE_GRAN

            def fill(gi, _, first=first):
                pltpu.make_async_copy(granule(zero_ref, 0), granule(hs_hbm, first + gi * MOE_GRAN),
                                      sem.at[slot]).start()
                return 0

            lax.fori_loop(0, n_e, fill, 0)
            n_fill = n_fill + n_e

        def wait_fill(gi, _):
            pltpu.make_async_copy(granule(zero_ref, 0), granule(hs_hbm, 0), sem.at[slot]).wait()
            return 0

        lax.fori_loop(0, n_fill, wait_fill, 0)

        def tile_at(ti):
            return hs_hbm.at[pl.ds(pl.multiple_of(ti * MOE_TILE, MOE_TILE), MOE_TILE)]

        n_tiles = tile_ref[1, 0]

        def fill_tile(ti, _):
            pltpu.make_async_copy(zero_ref, tile_at(ti), sem.at[slot]).start()
            return 0

        def wait_tile(ti, _):
            pltpu.make_async_copy(zero_ref, tile_at(0), sem.at[slot]).wait()
            return 0

        lax.fori_loop(n_tiles, MOE_MAX_TILES, fill_tile, 0)
        lax.fori_loop(n_tiles, MOE_MAX_TILES, wait_tile, 0)


def _moe_dispatch(block_tab, tile_tab, fill_tab, x2d, g, route):
    grid_spec = pltpu.PrefetchScalarGridSpec(
        num_scalar_prefetch=3,
        grid=(N_MOE_BLOCKS,),
        in_specs=[
            pl.BlockSpec((MOE_BLOCK, D_MODEL), lambda b, *_: (b, 0)),
            pl.BlockSpec((1, D_MODEL), lambda b, *_: (0, 0)),
            pl.BlockSpec((ROUTE_ROWS, MOE_BLOCK), lambda b, *_: (0, b)),
        ],
        out_specs=pl.BlockSpec(memory_space=pl.ANY),
        scratch_shapes=[
            pltpu.VMEM((2, MOE_LOCAL_ROWS, D_MODEL), BF16),
            pltpu.VMEM((MOE_TILE, D_MODEL), BF16),
            pltpu.SemaphoreType.DMA((2,)),
        ],
    )
    return pl.pallas_call(
        _moe_dispatch_kernel,
        grid_spec=grid_spec,
        out_shape=jax.ShapeDtypeStruct((MOE_ROWS, D_MODEL), BF16),
        compiler_params=pltpu.CompilerParams(
            dimension_semantics=("arbitrary",), vmem_limit_bytes=VMEM_LIMIT),
        name="moe_dispatch",
    )(block_tab, tile_tab, fill_tab, x2d, g, route)


def _moe_expert_kernel(tile_ref, hs_ref, wg_ref, wu_ref, wd_ref, y_ref, acc_ref):
    t = pl.program_id(0)
    c = pl.program_id(1)
    active = t < tile_ref[1, 0]

    @pl.when(jnp.logical_and(jnp.logical_not(active), c == 0))
    def _():
        y_ref[...] = jnp.zeros_like(y_ref)

    @pl.when(active)
    def _():
        @pl.when(c == 0)
        def _():
            acc_ref[...] = jnp.zeros_like(acc_ref)

        def swiglu_rows(n_rows):
            h = hs_ref[0:n_rows, :]
            gate = jnp.dot(h, wg_ref[0].astype(BF16), preferred_element_type=F32)
            up = jnp.dot(h, wu_ref[0].astype(BF16), preferred_element_type=F32)
            a = (gate * jax.nn.sigmoid(gate) * up).astype(BF16)
            acc_ref[0:n_rows, :] += jnp.dot(a, wd_ref[0].astype(BF16), preferred_element_type=F32)

        half_only = tile_ref[2, t] <= MOE_TILE // 2

        @pl.when(half_only)
        def _():
            swiglu_rows(MOE_TILE // 2)

        @pl.when(jnp.logical_not(half_only))
        def _():
            swiglu_rows(MOE_TILE)

        @pl.when(c == pl.num_programs(1) - 1)
        def _():
            y_ref[...] = acc_ref[...].astype(BF16)


def _moe_experts(tile_tab, hs, w_gu, w_down, fc=512):
    nc = FFN_DIM // fc

    def tile_of(t, tr):
        return jnp.maximum(jnp.minimum(t, tr[1, 0] - 1), 0)

    def chunk_of(t, c, tr):
        return jnp.where(t < tr[1, 0], c, nc - 1)

    grid_spec = pltpu.PrefetchScalarGridSpec(
        num_scalar_prefetch=1,
        grid=(MOE_MAX_TILES, nc),
        in_specs=[
            pl.BlockSpec((MOE_TILE, D_MODEL), lambda t, c, tr: (tile_of(t, tr), 0)),
            pl.BlockSpec((1, D_MODEL, fc), lambda t, c, tr: (tr[0, t], 0, chunk_of(t, c, tr))),
            pl.BlockSpec((1, D_MODEL, fc), lambda t, c, tr: (tr[0, t], 0, chunk_of(t, c, tr) + nc)),
            pl.BlockSpec((1, fc, D_MODEL), lambda t, c, tr: (tr[0, t], chunk_of(t, c, tr), 0)),
        ],
        out_specs=pl.BlockSpec((MOE_TILE, D_MODEL), lambda t, c, tr: (t, 0)),
        scratch_shapes=[pltpu.VMEM((MOE_TILE, D_MODEL), F32)],
    )
    return pl.pallas_call(
        _moe_expert_kernel,
        grid_spec=grid_spec,
        out_shape=jax.ShapeDtypeStruct((MOE_ROWS, D_MODEL), BF16),
        compiler_params=pltpu.CompilerParams(
            dimension_semantics=("arbitrary", "arbitrary"), vmem_limit_bytes=VMEM_LIMIT),
        name="moe_experts",
    )(tile_tab, hs, w_gu, w_gu, w_down)


def _moe_combine_kernel(meta_ref, x_ref, route_ref, y_hbm, gf_ref, out_ref, yloc_ref, sem):
    b = pl.program_id(0)
    slot = b % 2

    def granule(ref, row):
        return ref.at[pl.ds(pl.multiple_of(row, MOE_GRAN), MOE_GRAN)]

    def start_block(blk, slot_):
        for e in range(N_EXPERTS):
            lo = meta_ref[blk, TAB_LOFF + e]
            goff = meta_ref[blk, TAB_GOFF + e]

            def start(gi, _, lo=lo, goff=goff):
                pltpu.make_async_copy(granule(y_hbm, goff + gi * MOE_GRAN),
                                      granule(yloc_ref.at[slot_], lo + gi * MOE_GRAN), sem.at[slot_]).start()
                return 0

            lax.fori_loop(0, meta_ref[blk, TAB_PC + e] // MOE_GRAN, start, 0)

    @pl.when(b == 0)
    def _():
        yloc_ref[...] = jnp.zeros_like(yloc_ref)
        start_block(0, 0)

    @pl.when(b + 1 < pl.num_programs(0))
    def _():
        start_block(jnp.minimum(b + 1, pl.num_programs(0) - 1), 1 - slot)

    n_mine = 0
    for e in range(N_EXPERTS):
        n_mine = n_mine + meta_ref[b, TAB_PC + e] // MOE_GRAN

    def wait_one(gi, _):
        pltpu.make_async_copy(granule(y_hbm, 0), granule(yloc_ref.at[slot], 0), sem.at[slot]).wait()
        return 0

    lax.fori_loop(0, n_mine, wait_one, 0)

    route_t = route_ref[...]
    route = route_t.T
    col = lax.broadcasted_iota(jnp.int32, (MOE_BLOCK, MOE_LOCAL_ROWS), 1).astype(F32)
    pick1 = jnp.where(col == route[:, ROUTE_D1:ROUTE_D1 + 1], 1.0, 0.0).astype(BF16)
    pick2 = jnp.where(col == route[:, ROUTE_D2:ROUTE_D2 + 1], 1.0, 0.0).astype(BF16)
    yl = yloc_ref[slot]
    moe = (route[:, ROUTE_G1:ROUTE_G1 + 1] * jnp.dot(pick1, yl, preferred_element_type=F32)
           + route[:, ROUTE_G2:ROUTE_G2 + 1] * jnp.dot(pick2, yl, preferred_element_type=F32))
    out_ref[...] = _rmsnorm(x_ref[...] + moe, gf_ref[...])


def _moe_combine(meta, x2d, route, y, g_final):
    grid_spec = pltpu.PrefetchScalarGridSpec(
        num_scalar_prefetch=1,
        grid=(N_MOE_BLOCKS,),
        in_specs=[
            pl.BlockSpec((MOE_BLOCK, D_MODEL), lambda b, m: (b, 0)),
            pl.BlockSpec((ROUTE_ROWS, MOE_BLOCK), lambda b, m: (0, b)),
            pl.BlockSpec(memory_space=pl.ANY),
            pl.BlockSpec((1, D_MODEL), lambda b, m: (0, 0)),
        ],
        out_specs=pl.BlockSpec((MOE_BLOCK, D_MODEL), lambda b, m: (b, 0)),
        scratch_shapes=[
            pltpu.VMEM((2, MOE_LOCAL_ROWS, D_MODEL), BF16),
            pltpu.SemaphoreType.DMA((2,)),
        ],
    )
    return pl.pallas_call(
        _moe_combine_kernel,
        grid_spec=grid_spec,
        out_shape=jax.ShapeDtypeStruct((N_TOK, D_MODEL), F32),
        compiler_params=pltpu.CompilerParams(
            dimension_semantics=("arbitrary",), vmem_limit_bytes=VMEM_LIMIT),
        name="moe_combine",
    )(meta, x2d, route, y, g_final)


def _moe(x2d, g, w_router, w_gu, w_down, g_final):
    route, meta = _moe_route(x2d, g, w_router)
    block_tab, tile_tab, fill_tab = _moe_plan(meta)
    hs = _moe_dispatch(block_tab, tile_tab, fill_tab, x2d, g, route)
    y = _moe_experts(tile_tab, hs, w_gu, w_down)
    return _moe_combine(block_tab, x2d, route, y, g_final)


def _rope_tables():
    def tables(dim):
        inv_freq = ROPE_THETA ** (-jnp.arange(0, dim, 2, dtype=F32) / dim)
        ang = jnp.arange(SEQ, dtype=F32)[:, None] * inv_freq[None, :]
        return jnp.cos(ang), jnp.sin(ang)

    c, s = tables(HEAD_DIM)
    cq = jnp.concatenate([c, c], axis=1)
    sq = jnp.concatenate([-s, s], axis=1)
    c, s = tables(IDX_ROPE_DIM)
    ones = jnp.ones((SEQ, IDX_DIM - IDX_ROPE_DIM), F32)
    zeros = jnp.zeros_like(ones)
    z16 = jnp.zeros_like(s)
    ci = jnp.concatenate([c, c, ones], axis=1)
    sia = jnp.concatenate([-s, z16, zeros], axis=1)
    sib = jnp.concatenate([z16, s, zeros], axis=1)
    rep = LANES // IDX_DIM
    return cq, sq, jnp.tile(ci, (1, rep)), jnp.tile(sia, (1, rep)), jnp.tile(sib, (1, rep))


def kernel(x, norm_mix, norm_ffn, dsa_w_in, dsa_idx_k_gain, dsa_w_out, ffn_w_gu, ffn_w_down, gmlp_w_in,
           gmlp_ln_gain, gmlp_ln_bias, gmlp_w_spatial, gmlp_b_spatial, gmlp_w_out, moe_w_router, moe_w_gu,
           moe_w_down, final_norm):
    x2d = x.reshape(N_TOK, D_MODEL)
    row = lambda v: v.reshape(1, -1)

    w_in = dsa_w_in[0]
    w_all = jnp.pad(w_in, ((0, 0), (0, DSA_COLS - w_in.shape[1])))
    wwi_t = w_in[:, COL_KI + IDX_DIM:COL_KI + IDX_DIM + IDX_HEADS].T
    kig = jnp.pad(dsa_idx_k_gain[0], (0, LANES - IDX_DIM)).reshape(1, LANES)
    q3, k, vt, kia, kib, qi, wt = _dsa_proj(x2d, row(norm_mix[0]), w_all, wwi_t, kig, _rope_tables())
    o3 = _dsa_attn(q3, k, vt, kia, kib, qi, wt)
    x2d = _resid_proj(x2d, o3, dsa_w_out[0])
    x2d = _ffn(x2d, row(norm_ffn[0]), ffn_w_gu[0], ffn_w_down[0])

    gw = GMLP_WIDTH // GMLP_GROUPS
    b_sp_full = jnp.repeat(gmlp_b_spatial[0].T, gw, axis=1)
    x2d = _gmlp(x2d, row(norm_mix[1]), gmlp_w_in[0], row(gmlp_ln_gain[0]), row(gmlp_ln_bias[0]),
                gmlp_w_spatial[0], b_sp_full, gmlp_w_out[0])
    w_router = jnp.pad(moe_w_router[0], ((0, 0), (0, LANES - N_EXPERTS)))
    out = _moe(x2d, row(norm_ffn[1]), w_router, moe_w_gu[0], moe_w_down[0], row(final_norm))
    return out.reshape(BATCH, SEQ, D_MODEL)
```

```python
import functools

import numpy as np
import jax
import jax.numpy as jnp
from jax import lax
from jax.experimental import pallas as pl
from jax.experimental.pallas import tpu as pltpu

F32 = jnp.float32
BF16 = jnp.bfloat16

D_MODEL = 1024
BATCH = 8
SEQ = 2048
N_TOK = BATCH * SEQ
N_HEADS = 8
HEAD_DIM = 128
IDX_HEADS = 8
IDX_DIM = 64
IDX_ROPE_DIM = 32
INDEX_TOPK = 256
Q_BLOCK = 128
KEY_SUPER = 512
ROPE_THETA = 10000.0
GMLP_WIDTH = 1024
GMLP_GROUPS = 8
GMLP_CHUNK = 128
FFN_DIM = 3584
N_EXPERTS = 8
NORM_EPS = 1e-6

LANES = 128
SUBLANES = 8
VMEM_LIMIT = 56 * 1024 * 1024

DSA_COLS = 1920
COL_K = N_HEADS * HEAD_DIM
COL_V = COL_K + HEAD_DIM
COL_QI = COL_V + HEAD_DIM
COL_KI = COL_QI + IDX_HEADS * IDX_DIM

NT_DIMS = (((1,), (1,)), ((), ()))
MASK_BIAS = -1e30
SCORE_MASKED = -3.0e38


def _rmsnorm(x, g):
    ms = jnp.mean(x * x, axis=-1, keepdims=True)
    return x * lax.rsqrt(ms + NORM_EPS) * g


def _cast_once(src_ref, dst_ref):
    @pl.when(pl.program_id(0) == 0)
    def _():
        dst_ref[...] = src_ref[...].astype(BF16)


def _resident(shape):
    return pl.BlockSpec(shape, lambda i: (0,) * len(shape), pipeline_mode=pl.Buffered(1))


def _dsa_proj_kernel(x_ref, g_ref, w_ref, wwi_ref, kig_ref, cq_ref, sq_ref, ci_ref, sia_ref, sib_ref,
                     q_ref, k_ref, vt_ref, kia_ref, kib_ref, qi_ref, wt_ref, wbf_ref):
    _cast_once(w_ref, wbf_ref)
    h = _rmsnorm(x_ref[...], g_ref[...]).astype(BF16)
    yr = jnp.dot(h, wbf_ref[:, COL_K:], preferred_element_type=F32)
    yq = jnp.dot(h, wbf_ref[:, :COL_K], preferred_element_type=F32)
    cq = cq_ref[...]
    sq = sq_ref[...]
    ci = ci_ref[...]
    sia = sia_ref[...]
    sib = sib_ref[...]

    def rope_head(t):
        return t * cq + pltpu.roll(t, HEAD_DIM // 2, 1) * sq

    def rope_idx(t):
        half = IDX_ROPE_DIM // 2
        return t * ci + pltpu.roll(t, LANES - half, 1) * sia + pltpu.roll(t, half, 1) * sib

    q_scale = HEAD_DIM ** -0.5 * np.log2(np.e)
    k_ref[...] = rope_head(yr[:, 0:COL_V - COL_K]).astype(BF16)
    vt_ref[0] = yr[:, COL_V - COL_K:COL_QI - COL_K].T.astype(BF16)
    for p in range(IDX_HEADS * IDX_DIM // LANES):
        c0 = COL_QI - COL_K + p * LANES
        qi_ref[:, p * LANES:(p + 1) * LANES] = rope_idx(yr[:, c0:c0 + LANES]).astype(BF16)
    for hh in range(N_HEADS):
        q_ref[hh] = (rope_head(yq[:, hh * HEAD_DIM:(hh + 1) * HEAD_DIM]) * q_scale).astype(BF16)

    last = yr[:, COL_KI - COL_K:COL_KI - COL_K + LANES]
    lane = lax.broadcasted_iota(jnp.int32, last.shape, 1)
    kraw = jnp.where(lane < IDX_DIM, last, 0.0)
    kms = jnp.sum(kraw * kraw, axis=-1, keepdims=True) * (1.0 / IDX_DIM)
    kn = kraw * lax.rsqrt(kms + NORM_EPS) * kig_ref[...]
    kr = rope_idx(kn)
    kia_ref[...] = kr.astype(BF16)
    kib_ref[...] = pltpu.roll(kr, IDX_DIM, 1).astype(BF16)

    w_scale = IDX_HEADS ** -0.5 * IDX_DIM ** -0.5
    wt_ref[...] = lax.dot_general(wwi_ref[...].astype(BF16), h, NT_DIMS, preferred_element_type=F32) * w_scale


def _dsa_proj(x2d, g, w_all, wwi_t, kig, tabs, tm=KEY_SUPER):
    n = x2d.shape[0]
    nblk_seq = SEQ // tm
    tab_spec = pl.BlockSpec((tm, LANES), lambda i: (i % nblk_seq, 0))
    full = lambda shape: pl.BlockSpec(shape, lambda i: (0,) * len(shape))
    return pl.pallas_call(
        _dsa_proj_kernel,
        grid=(n // tm,),
        in_specs=[
            pl.BlockSpec((tm, D_MODEL), lambda i: (i, 0)),
            full((1, D_MODEL)),
            _resident((D_MODEL, DSA_COLS)),
            full((IDX_HEADS, D_MODEL)),
            full((1, LANES)),
            tab_spec, tab_spec, tab_spec, tab_spec, tab_spec,
        ],
        out_specs=[
            pl.BlockSpec((N_HEADS, tm, HEAD_DIM), lambda i: (0, i, 0)),
            pl.BlockSpec((tm, HEAD_DIM), lambda i: (i, 0)),
            pl.BlockSpec((1, HEAD_DIM, tm), lambda i: (i, 0, 0)),
            pl.BlockSpec((tm, LANES), lambda i: (i, 0)),
            pl.BlockSpec((tm, LANES), lambda i: (i, 0)),
            pl.BlockSpec((tm, IDX_HEADS * IDX_DIM), lambda i: (i, 0)),
            pl.BlockSpec((IDX_HEADS, tm), lambda i: (0, i)),
        ],
        out_shape=[
            jax.ShapeDtypeStruct((N_HEADS, n, HEAD_DIM), BF16),
            jax.ShapeDtypeStruct((n, HEAD_DIM), BF16),
            jax.ShapeDtypeStruct((n // tm, HEAD_DIM, tm), BF16),
            jax.ShapeDtypeStruct((n, LANES), BF16),
            jax.ShapeDtypeStruct((n, LANES), BF16),
            jax.ShapeDtypeStruct((n, IDX_HEADS * IDX_DIM), BF16),
            jax.ShapeDtypeStruct((IDX_HEADS, n), F32),
        ],
        scratch_shapes=[pltpu.VMEM((D_MODEL, DSA_COLS), BF16)],
        compiler_params=pltpu.CompilerParams(
            dimension_semantics=("arbitrary",), vmem_limit_bytes=VMEM_LIMIT),
        name="dsa_proj",
    )(x2d, g, w_all, wwi_t, kig, *tabs)


def _ordered_bits_to_float(u):
    key = u ^ jnp.int32(-2 ** 31)
    key = jnp.maximum(key, jnp.int32(-2139095041))
    bits = jnp.where(key >= 0, key, key ^ jnp.int32(0x7FFFFFFF))
    return lax.bitcast_convert_type(bits, F32)


def _dsa_attn_kernel(q_ref, k_ref, vt_ref, kia_ref, kib_ref, qi_ref, wt_ref, o_ref,
                     s_ref, sel_ref, l_ref, p_ref, acc_ref):
    j = pl.program_id(1)
    per_super = KEY_SUPER // Q_BLOCK
    n_super = j // per_super + 1
    hq = N_HEADS * Q_BLOCK

    qpos = j * Q_BLOCK + lax.broadcasted_iota(jnp.int32, (Q_BLOCK, Q_BLOCK), 1)
    krow = lax.broadcasted_iota(jnp.int32, (Q_BLOCK, Q_BLOCK), 0)

    def rows_of(sc, c4):
        return pl.ds(pl.multiple_of(sc * KEY_SUPER, KEY_SUPER) + c4 * Q_BLOCK, Q_BLOCK)

    def super_rows(sc):
        return pl.ds(pl.multiple_of(sc * KEY_SUPER, KEY_SUPER), KEY_SUPER)

    def causal(sc, c4):
        return (krow + (sc * KEY_SUPER + c4 * Q_BLOCK)) <= qpos

    def fold8(t):
        return t.reshape(t.shape[0] // SUBLANES, SUBLANES, t.shape[1])

    def score_body(sc, _):
        srows = super_rows(sc)
        ka = kia_ref[srows, :]
        kb = kib_ref[srows, :]
        for half in range(IDX_HEADS // 4):
            p0 = 2 * half
            qpp = jnp.concatenate(
                [qi_ref[:, p0 * LANES:(p0 + 1) * LANES], qi_ref[:, (p0 + 1) * LANES:(p0 + 2) * LANES]], axis=0)
            ra = lax.dot_general(ka, qpp, NT_DIMS, preferred_element_type=F32)
            rb = lax.dot_general(kb, qpp, NT_DIMS, preferred_element_type=F32)
            term = None
            for sub in range(2):
                p = p0 + sub
                cols = slice(sub * Q_BLOCK, (sub + 1) * Q_BLOCK)
                t2 = (jnp.maximum(ra[:, cols], 0.0) * wt_ref[2 * p:2 * p + 1, :]
                      + jnp.maximum(rb[:, cols], 0.0) * wt_ref[2 * p + 1:2 * p + 2, :])
                term = t2 if term is None else term + t2
            if half == 0:
                s_ref[srows, :] = term
            else:
                kpos = sc * KEY_SUPER + lax.broadcasted_iota(jnp.int32, (KEY_SUPER, Q_BLOCK), 0)
                qp_s = j * Q_BLOCK + lax.broadcasted_iota(jnp.int32, (KEY_SUPER, Q_BLOCK), 1)
                s_ref[srows, :] = jnp.where(kpos <= qp_s, s_ref[srows, :] + term, SCORE_MASKED)
        return 0

    lax.fori_loop(0, n_super, score_body, 0)

    def search(n_chunks):
        def count_ge(cand):
            accs = [jnp.zeros((SUBLANES, LANES), F32), jnp.zeros((SUBLANES, LANES), F32)]
            for c in range(n_chunks):
                ind = jnp.where(s_ref[pl.ds(c * Q_BLOCK, Q_BLOCK), :] >= cand, 1.0, 0.0)
                accs[c % 2] = accs[c % 2] + fold8(ind).sum(axis=0)
            return jnp.sum(accs[0] + accs[1], axis=0, keepdims=True)

        def bit_body(i, carry):
            prefix, cbest = carry
            cu = prefix | lax.shift_left(jnp.int32(1), 31 - i)
            cnt = count_ge(_ordered_bits_to_float(cu))
            ok = cnt >= float(INDEX_TOPK)
            return jnp.where(ok, cu, prefix), jnp.where(ok, cnt, cbest)

        prefix, cbest = lax.fori_loop(
            0, 32, bit_body,
            (jnp.zeros((1, LANES), jnp.int32), jnp.full((1, LANES), float(n_chunks * Q_BLOCK), F32)))
        sel_ref[0:1, :] = _ordered_bits_to_float(prefix)
        sel_ref[1:2, :] = cbest

    for n_chunks in range(1, SEQ // Q_BLOCK + 1):
        pl.when(j == n_chunks - 1)(functools.partial(search, n_chunks))
    tau = sel_ref[0:1, :]
    drop = sel_ref[1:2, :] - float(INDEX_TOPK)

    upper = jnp.where(krow < lax.broadcasted_iota(jnp.int32, (Q_BLOCK, Q_BLOCK), 1), 1.0, 0.0).astype(BF16)
    pair_w = 2 * Q_BLOCK

    def select_logit_body(i, carry):
        later_ties, m_acc = carry
        sc = n_super - 1 - i
        biases = [None] * per_super
        for c4 in reversed(range(per_super)):
            s = s_ref[rows_of(sc, c4), :]
            eq = jnp.where(s == tau, 1.0, 0.0)
            ties_after = jnp.dot(upper, eq.astype(BF16), preferred_element_type=F32) + later_ties
            later_ties = later_ties + jnp.sum(eq, axis=0, keepdims=True)
            tie_bias = jnp.where(ties_after >= drop, 0.0, MASK_BIAS)
            bias = jnp.where(s > tau, 0.0, jnp.where(s == tau, tie_bias, MASK_BIAS))
            biases[c4] = jnp.where(causal(sc, c4), bias, MASK_BIAS)
        bias = jnp.concatenate(biases, axis=0)
        bias2 = jnp.concatenate([bias, bias], axis=1)
        srows = super_rows(sc)
        kc = k_ref[srows, :]
        parts = []
        for pr in range(N_HEADS // 2):
            qpair = q_ref[2 * pr:2 * pr + 2].reshape(pair_w, HEAD_DIM)
            lg = lax.dot_general(kc, qpair, NT_DIMS, preferred_element_type=F32) + bias2
            l_ref[srows, pr * pair_w:(pr + 1) * pair_w] = lg
            parts.append(fold8(lg).max(axis=0))
        return later_ties, jnp.maximum(m_acc, jnp.concatenate(parts, axis=1))

    _, m_acc = lax.fori_loop(
        0, n_super, select_logit_body,
        (jnp.zeros((1, LANES), F32), jnp.full((SUBLANES, hq), MASK_BIAS, F32)))
    m = jnp.max(m_acc, axis=0, keepdims=True)

    def prob_chunk(sc, d_acc):
        for c4 in range(per_super):
            rows = rows_of(sc, c4)
            parts = []
            for pr in range(N_HEADS // 2):
                cols = slice(pr * pair_w, (pr + 1) * pair_w)
                p = jnp.exp2(l_ref[rows, cols] - m[:, cols])
                p_ref[rows, cols] = p.astype(BF16)
                parts.append(fold8(p).sum(axis=0))
            d_acc = d_acc + jnp.concatenate(parts, axis=1)
        return d_acc

    def pv_chunk(sc):
        acc_ref[...] += jnp.dot(vt_ref[sc], p_ref[super_rows(sc), :], preferred_element_type=F32)

    def prob_pv_body(sc, d_acc):
        pv_chunk(sc - 1)
        return prob_chunk(sc, d_acc)

    acc_ref[...] = jnp.zeros_like(acc_ref)
    d_acc = prob_chunk(0, jnp.zeros((SUBLANES, hq), F32))
    d_acc = lax.fori_loop(1, n_super, prob_pv_body, d_acc)
    pv_chunk(n_super - 1)
    inv = 1.0 / jnp.sum(d_acc, axis=0, keepdims=True)
    for h in range(N_HEADS):
        cols = slice(h * Q_BLOCK, (h + 1) * Q_BLOCK)
        o_ref[h] = (acc_ref[:, cols] * inv[:, cols]).T.astype(BF16)


def _dsa_attn(q3, k, vt, kia, kib, qi, wt):
    nqb = SEQ // Q_BLOCK
    n_sup = SEQ // KEY_SUPER
    hq = N_HEADS * Q_BLOCK
    return pl.pallas_call(
        _dsa_attn_kernel,
        grid=(BATCH, nqb),
        in_specs=[
            pl.BlockSpec((N_HEADS, Q_BLOCK, HEAD_DIM), lambda b, j: (0, b * nqb + j, 0)),
            pl.BlockSpec((SEQ, HEAD_DIM), lambda b, j: (b, 0)),
            pl.BlockSpec((n_sup, HEAD_DIM, KEY_SUPER), lambda b, j: (b, 0, 0)),
            pl.BlockSpec((SEQ, LANES), lambda b, j: (b, 0)),
            pl.BlockSpec((SEQ, LANES), lambda b, j: (b, 0)),
            pl.BlockSpec((Q_BLOCK, IDX_HEADS * IDX_DIM), lambda b, j: (b * nqb + j, 0)),
            pl.BlockSpec((IDX_HEADS, Q_BLOCK), lambda b, j: (0, b * nqb + j)),
        ],
        out_specs=pl.BlockSpec((N_HEADS, Q_BLOCK, HEAD_DIM), lambda b, j: (0, b * nqb + j, 0)),
        out_shape=jax.ShapeDtypeStruct((N_HEADS, N_TOK, HEAD_DIM), BF16),
        scratch_shapes=[
            pltpu.VMEM((SEQ, Q_BLOCK), F32),
            pltpu.VMEM((SUBLANES, Q_BLOCK), F32),
            pltpu.VMEM((SEQ, hq), F32),
            pltpu.VMEM((SEQ, hq), BF16),
            pltpu.VMEM((HEAD_DIM, hq), F32),
        ],
        compiler_params=pltpu.CompilerParams(
            dimension_semantics=("parallel", "arbitrary"), vmem_limit_bytes=VMEM_LIMIT),
        name="dsa_attn",
    )(q3, k, vt, kia, kib, qi, wt)


def _resid_proj_kernel(x_ref, o_ref, w_ref, out_ref, wbf_ref):
    _cast_once(w_ref, wbf_ref)
    o = jnp.concatenate([o_ref[h] for h in range(N_HEADS)], axis=1)
    out_ref[...] = x_ref[...] + jnp.dot(o, wbf_ref[...], preferred_element_type=F32)


def _resid_proj(x2d, o3, w_out, tm=1024):
    n = x2d.shape[0]
    return pl.pallas_call(
        _resid_proj_kernel,
        grid=(n // tm,),
        in_specs=[
            pl.BlockSpec((tm, D_MODEL), lambda i: (i, 0)),
            pl.BlockSpec((N_HEADS, tm, HEAD_DIM), lambda i: (0, i, 0)),
            _resident((D_MODEL, D_MODEL)),
        ],
        out_specs=pl.BlockSpec((tm, D_MODEL), lambda i: (i, 0)),
        out_shape=jax.ShapeDtypeStruct((n, D_MODEL), F32),
        scratch_shapes=[pltpu.VMEM((D_MODEL, D_MODEL), BF16)],
        compiler_params=pltpu.CompilerParams(
            dimension_semantics=("arbitrary",), vmem_limit_bytes=VMEM_LIMIT),
        name="dsa_out_proj",
    )(x2d, o3, w_out)


def _ffn_kernel(x_ref, g_ref, wg_ref, wu_ref, wd_ref, out_ref, h_ref, acc_ref):
    c = pl.program_id(1)

    @pl.when(c == 0)
    def _():
        h_ref[...] = _rmsnorm(x_ref[...], g_ref[...]).astype(BF16)
        acc_ref[...] = jnp.zeros_like(acc_ref)

    h = h_ref[...]
    gate = jnp.dot(h, wg_ref[...].astype(BF16), preferred_element_type=F32)
    up = jnp.dot(h, wu_ref[...].astype(BF16), preferred_element_type=F32)
    a = (gate * jax.nn.sigmoid(gate) * up).astype(BF16)
    acc_ref[...] += jnp.dot(a, wd_ref[...].astype(BF16), preferred_element_type=F32)

    @pl.when(c == pl.num_programs(1) - 1)
    def _():
        out_ref[...] = x_ref[...] + acc_ref[...]


def _ffn(x2d, g, w_gu, w_down, tm=1024, fc=512):
    n = x2d.shape[0]
    nc = FFN_DIM // fc
    return pl.pallas_call(
        _ffn_kernel,
        grid=(n // tm, nc),
        in_specs=[
            pl.BlockSpec((tm, D_MODEL), lambda i, c: (i, 0)),
            pl.BlockSpec((1, D_MODEL), lambda i, c: (0, 0)),
            pl.BlockSpec((D_MODEL, fc), lambda i, c: (0, c)),
            pl.BlockSpec((D_MODEL, fc), lambda i, c: (0, c + nc)),
            pl.BlockSpec((fc, D_MODEL), lambda i, c: (c, 0)),
        ],
        out_specs=pl.BlockSpec((tm, D_MODEL), lambda i, c: (i, 0)),
        out_shape=jax.ShapeDtypeStruct((n, D_MODEL), F32),
        scratch_shapes=[pltpu.VMEM((tm, D_MODEL), BF16), pltpu.VMEM((tm, D_MODEL), F32)],
        compiler_params=pltpu.CompilerParams(
            dimension_semantics=("parallel", "arbitrary"), vmem_limit_bytes=VMEM_LIMIT),
        name="swiglu_ffn",
    )(x2d, g, w_gu, w_gu, w_down)


def _gelu_tanh(x):
    return 0.5 * x * (1.0 + jnp.tanh(np.sqrt(2.0 / np.pi) * (x + 0.044715 * (x * x * x))))


def _gmlp_kernel(x_ref, g_ref, win_ref, lng_ref, lnb_ref, ws_ref, bs_ref, wout_ref, out_ref,
                 winbf_ref, woutbf_ref):
    _cast_once(win_ref, winbf_ref)
    _cast_once(wout_ref, woutbf_ref)
    x = x_ref[...]
    tm = x.shape[0]
    h = _rmsnorm(x, g_ref[...]).astype(BF16)
    v = _gelu_tanh(jnp.dot(h, winbf_ref[:, GMLP_WIDTH:], preferred_element_type=F32))
    u = _gelu_tanh(jnp.dot(h, winbf_ref[:, :GMLP_WIDTH], preferred_element_type=F32))
    mu = jnp.mean(v, axis=-1, keepdims=True)
    vc = v - mu
    var = jnp.mean(vc * vc, axis=-1, keepdims=True)
    vb = (vc * lax.rsqrt(var + NORM_EPS) * lng_ref[...] + lnb_ref[...]).astype(BF16)

    gw = GMLP_WIDTH // GMLP_GROUPS
    row = lax.broadcasted_iota(jnp.int32, (GMLP_CHUNK, GMLP_CHUNK), 0)
    col = lax.broadcasted_iota(jnp.int32, (GMLP_CHUNK, GMLP_CHUNK), 1)
    w_tril = [jnp.where(row >= col, ws_ref[gi], 0.0).astype(BF16) for gi in range(GMLP_GROUPS)]
    rows = []
    for c in range(tm // GMLP_CHUNK):
        parts = [
            jnp.dot(w_tril[gi], vb[c * GMLP_CHUNK:(c + 1) * GMLP_CHUNK, gi * gw:(gi + 1) * gw],
                    preferred_element_type=F32)
            for gi in range(GMLP_GROUPS)
        ]
        rows.append(jnp.concatenate(parts, axis=1) + bs_ref[...])
    s = jnp.concatenate(rows, axis=0)
    z = (u * s).astype(BF16)
    out_ref[...] = x + jnp.dot(z, woutbf_ref[...], preferred_element_type=F32)


def _gmlp(x2d, g, w_in, ln_g, ln_b, w_sp, b_sp_full, w_out, tm=512):
    n = x2d.shape[0]
    full = lambda shape: pl.BlockSpec(shape, lambda i: (0,) * len(shape))
    return pl.pallas_call(
        _gmlp_kernel,
        grid=(n // tm,),
        in_specs=[
            pl.BlockSpec((tm, D_MODEL), lambda i: (i, 0)),
            full((1, D_MODEL)),
            _resident((D_MODEL, 2 * GMLP_WIDTH)),
            full((1, GMLP_WIDTH)),
            full((1, GMLP_WIDTH)),
            full((GMLP_GROUPS, GMLP_CHUNK, GMLP_CHUNK)),
            full((GMLP_CHUNK, GMLP_WIDTH)),
            _resident((GMLP_WIDTH, D_MODEL)),
        ],
        out_specs=pl.BlockSpec((tm, D_MODEL), lambda i: (i, 0)),
        out_shape=jax.ShapeDtypeStruct((n, D_MODEL), F32),
        scratch_shapes=[pltpu.VMEM((D_MODEL, 2 * GMLP_WIDTH), BF16), pltpu.VMEM((GMLP_WIDTH, D_MODEL), BF16)],
        compiler_params=pltpu.CompilerParams(
            dimension_semantics=("arbitrary",), vmem_limit_bytes=VMEM_LIMIT),
        name="gmlp",
    )(x2d, g, w_in, ln_g, ln_b, w_sp, b_sp_full, w_out)


MOE_BLOCK = 512
MOE_GRAN = 16
MOE_TILE = 1024
MOE_LOCAL_ROWS = 2 * MOE_BLOCK + N_EXPERTS * MOE_GRAN
N_MOE_BLOCKS = N_TOK // MOE_BLOCK
MOE_MAX_TILES = (2 * N_TOK + N_MOE_BLOCKS * N_EXPERTS * (MOE_GRAN - 1)) // MOE_TILE + N_EXPERTS
MOE_ROWS = MOE_MAX_TILES * MOE_TILE
ROUTE_D1, ROUTE_D2, ROUTE_G1, ROUTE_G2 = 0, 1, 2, 3
ROUTE_ROWS = N_EXPERTS
TAB_LOFF, TAB_PC, TAB_GOFF = 0, N_EXPERTS, 2 * N_EXPERTS


def _moe_route_kernel(x_ref, g_ref, wr_ref, route_ref, meta_ref):
    b = pl.program_id(0)
    t_blk = MOE_BLOCK
    hf = _rmsnorm(x_ref[...], g_ref[...])
    wr = wr_ref[...]
    h_hi = hf.astype(BF16)
    h_lo = (hf - h_hi.astype(F32)).astype(BF16)
    w_hi = wr.astype(BF16)
    w_lo = (wr - w_hi.astype(F32)).astype(BF16)
    logits = (jnp.dot(h_hi, w_hi, preferred_element_type=F32) + jnp.dot(h_hi, w_lo, preferred_element_type=F32)
              + jnp.dot(h_lo, w_hi, preferred_element_type=F32))

    lg = logits.T[0:N_EXPERTS, :]
    e_idx = lax.broadcasted_iota(jnp.int32, lg.shape, 0).astype(F32)
    neg_inf = -jnp.inf
    m1 = jnp.max(lg, axis=0, keepdims=True)
    i1 = jnp.min(jnp.where(lg == m1, e_idx, float(N_EXPERTS)), axis=0, keepdims=True)
    lg2 = jnp.where(e_idx == i1, neg_inf, lg)
    m2 = jnp.max(lg2, axis=0, keepdims=True)
    i2 = jnp.min(jnp.where(lg2 == m2, e_idx, float(N_EXPERTS)), axis=0, keepdims=True)
    t = jnp.exp(m2 - m1)
    g1 = 1.0 / (1.0 + t)
    g2 = t * g1

    memb = jnp.where(jnp.logical_or(e_idx == i1, e_idx == i2), 1.0, 0.0)
    tr = lax.broadcasted_iota(jnp.int32, (t_blk, t_blk), 0)
    tc = lax.broadcasted_iota(jnp.int32, (t_blk, t_blk), 1)
    earlier = jnp.where(tr < tc, 1.0, 0.0).astype(BF16)
    rank = jnp.dot(memb.astype(BF16), earlier, preferred_element_type=F32)
    loff_col = jnp.zeros((N_EXPERTS, 1), F32)
    e_col = lax.broadcasted_iota(jnp.int32, (N_EXPERTS, 1), 0)
    lo = jnp.int32(0)
    for e in range(N_EXPERTS):
        cnt = jnp.sum(memb[e:e + 1, :]).astype(jnp.int32)
        pce = (cnt + (MOE_GRAN - 1)) // MOE_GRAN * MOE_GRAN
        meta_ref[b, TAB_LOFF + e] = lo
        meta_ref[b, TAB_PC + e] = pce
        loff_col = jnp.where(e_col == e, lo.astype(F32), loff_col)
        lo = lo + pce
    dest = loff_col + rank
    d1 = jnp.sum(jnp.where(e_idx == i1, dest, 0.0), axis=0, keepdims=True)
    d2 = jnp.sum(jnp.where(e_idx == i2, dest, 0.0), axis=0, keepdims=True)
    r_idx = lax.broadcasted_iota(jnp.int32, lg.shape, 0)
    route_ref[...] = jnp.where(r_idx == ROUTE_D1, d1,
                               jnp.where(r_idx == ROUTE_D2, d2,
                                         jnp.where(r_idx == ROUTE_G1, g1, jnp.where(r_idx == ROUTE_G2, g2, 0.0))))


def _moe_route(x2d, g, w_router):
    return pl.pallas_call(
        _moe_route_kernel,
        grid=(N_MOE_BLOCKS,),
        in_specs=[
            pl.BlockSpec((MOE_BLOCK, D_MODEL), lambda b: (b, 0)),
            pl.BlockSpec((1, D_MODEL), lambda b: (0, 0)),
            pl.BlockSpec((D_MODEL, LANES), lambda b: (0, 0)),
        ],
        out_specs=[
            pl.BlockSpec((ROUTE_ROWS, MOE_BLOCK), lambda b: (0, b)),
            pl.BlockSpec(memory_space=pltpu.SMEM),
        ],
        out_shape=[
            jax.ShapeDtypeStruct((ROUTE_ROWS, N_TOK), F32),
            jax.ShapeDtypeStruct((N_MOE_BLOCKS, 2 * N_EXPERTS), jnp.int32),
        ],
        compiler_params=pltpu.CompilerParams(
            dimension_semantics=("arbitrary",), vmem_limit_bytes=VMEM_LIMIT),
        name="moe_route",
    )(x2d, g, w_router)


def _moe_plan(meta):
    pc = meta[:, TAB_PC:TAB_PC + N_EXPERTS]
    totals = jnp.sum(pc, axis=0)
    per_expert = (totals + (MOE_TILE - 1)) // MOE_TILE
    e_lt = jnp.arange(N_EXPERTS)[:, None] < jnp.arange(N_EXPERTS)[None, :]
    first_tile = jnp.sum(jnp.where(e_lt, per_expert[:, None], 0), axis=0)
    ends = first_tile + per_expert
    seg_start = first_tile * MOE_TILE
    b_lt = jnp.arange(N_MOE_BLOCKS)[:, None] < jnp.arange(N_MOE_BLOCKS)[None, :]
    goff = seg_start[None, :] + jnp.sum(jnp.where(b_lt[:, :, None], pc[:, None, :], 0), axis=0)
    block_tab = jnp.concatenate([meta, goff], axis=1).astype(jnp.int32)
    n_tiles = jnp.sum(per_expert)
    t = jnp.minimum(jnp.arange(MOE_MAX_TILES, dtype=jnp.int32), n_tiles - 1)
    mine = jnp.logical_and(t[:, None] >= first_tile[None, :], t[:, None] < ends[None, :])
    expert = jnp.sum(jnp.where(mine, jnp.arange(N_EXPERTS)[None, :], 0), axis=1)
    left = totals[None, :] - (t[:, None] - first_tile[None, :]) * MOE_TILE
    used = jnp.sum(jnp.where(mine, jnp.clip(left, 0, MOE_TILE), 0), axis=1)
    tile_tab = jnp.stack([expert, jnp.full_like(t, n_tiles), used]).astype(jnp.int32)
    fill_tab = jnp.stack([seg_start + totals, ends * MOE_TILE]).astype(jnp.int32)
    return block_tab, tile_tab, fill_tab


def _moe_dispatch_kernel(tab_ref, tile_ref, fill_ref, x_ref, g_ref, route_ref, hs_hbm, hloc_ref, zero_ref, sem):
    b = pl.program_id(0)
    slot = b % 2
    hloc = hloc_ref.at[slot]
    hb = _rmsnorm(x_ref[...], g_ref[...]).astype(BF16)

    route_t = route_ref[...]
    r_iota = lax.broadcasted_iota(jnp.int32, (MOE_LOCAL_ROWS, MOE_BLOCK), 0).astype(F32)
    hit = jnp.logical_or(r_iota == route_t[ROUTE_D1:ROUTE_D1 + 1, :], r_iota == route_t[ROUTE_D2:ROUTE_D2 + 1, :])
    perm = jnp.where(hit, 1.0, 0.0).astype(BF16)
    hloc_ref[slot] = jnp.dot(perm, hb, preferred_element_type=F32).astype(BF16)

    def granule(ref, row):
        return ref.at[pl.ds(pl.multiple_of(row, MOE_GRAN), MOE_GRAN)]

    for e in range(N_EXPERTS):
        lo = tab_ref[b, TAB_LOFF + e]
        goff = tab_ref[b, TAB_GOFF + e]

        def start(gi, _, lo=lo, goff=goff):
            pltpu.make_async_copy(granule(hloc, lo + gi * MOE_GRAN),
                                  granule(hs_hbm, goff + gi * MOE_GRAN), sem.at[slot]).start()
            return 0

        lax.fori_loop(0, tab_ref[b, TAB_PC + e] // MOE_GRAN, start, 0)

    def wait_block(blk, slot_):
        n = 0
        for e in range(N_EXPERTS):
            n = n + tab_ref[blk, TAB_PC + e] // MOE_GRAN

        def wait_one(gi, _):
            pltpu.make_async_copy(granule(hloc_ref.at[slot_], 0), granule(hs_hbm, 0), sem.at[slot_]).wait()
            return 0

        lax.fori_loop(0, n, wait_one, 0)

    @pl.when(b > 0)
    def _():
        wait_block(jnp.maximum(b - 1, 0), 1 - slot)

    @pl.when(b == pl.num_programs(0) - 1)
    def _():
        wait_block(b, slot)
        zero_ref[...] = jnp.zeros_like(zero_ref)
        n_fill = 0
        for e in range(N_EXPERTS):
            first = fill_ref[0, e]
            n_e = (fill_ref[1, e] - first) // MOE_GRAN

            def fill(gi, _, first=first):
                pltpu.make_async_copy(granule(zero_ref, 0), granule(hs_hbm, first + gi * MOE_GRAN),
                                      sem.at[slot]).start()
                return 0

            lax.fori_loop(0, n_e, fill, 0)
            n_fill = n_fill + n_e

        def wait_fill(gi, _):
            pltpu.make_async_copy(granule(zero_ref, 0), granule(hs_hbm, 0), sem.at[slot]).wait()
            return 0

        lax.fori_loop(0, n_fill, wait_fill, 0)

        def tile_at(ti):
            return hs_hbm.at[pl.ds(pl.multiple_of(ti * MOE_TILE, MOE_TILE), MOE_TILE)]

        n_tiles = tile_ref[1, 0]

        def fill_tile(ti, _):
            pltpu.make_async_copy(zero_ref, tile_at(ti), sem.at[slot]).start()
            return 0

        def wait_tile(ti, _):
            pltpu.make_async_copy(zero_ref, tile_at(0), sem.at[slot]).wait()
            return 0

        lax.fori_loop(n_tiles, MOE_MAX_TILES, fill_tile, 0)
        lax.fori_loop(n_tiles, MOE_MAX_TILES, wait_tile, 0)


def _moe_dispatch(block_tab, tile_tab, fill_tab, x2d, g, route):
    grid_spec = pltpu.PrefetchScalarGridSpec(
        num_scalar_prefetch=3,
        grid=(N_MOE_BLOCKS,),
        in_specs=[
            pl.BlockSpec((MOE_BLOCK, D_MODEL), lambda b, *_: (b, 0)),
            pl.BlockSpec((1, D_MODEL), lambda b, *_: (0, 0)),
            pl.BlockSpec((ROUTE_ROWS, MOE_BLOCK), lambda b, *_: (0, b)),
        ],
        out_specs=pl.BlockSpec(memory_space=pl.ANY),
        scratch_shapes=[
            pltpu.VMEM((2, MOE_LOCAL_ROWS, D_MODEL), BF16),
            pltpu.VMEM((MOE_TILE, D_MODEL), BF16),
            pltpu.SemaphoreType.DMA((2,)),
        ],
    )
    return pl.pallas_call(
        _moe_dispatch_kernel,
        grid_spec=grid_spec,
        out_shape=jax.ShapeDtypeStruct((MOE_ROWS, D_MODEL), BF16),
        compiler_params=pltpu.CompilerParams(
            dimension_semantics=("arbitrary",), vmem_limit_bytes=VMEM_LIMIT),
        name="moe_dispatch",
    )(block_tab, tile_tab, fill_tab, x2d, g, route)


def _moe_expert_kernel(tile_ref, hs_ref, wg_ref, wu_ref, wd_ref, y_ref, acc_ref):
    t = pl.program_id(0)
    c = pl.program_id(1)
    active = t < tile_ref[1, 0]

    @pl.when(jnp.logical_and(jnp.logical_not(active), c == 0))
    def _():
        y_ref[...] = jnp.zeros_like(y_ref)

    @pl.when(active)
    def _():
        @pl.when(c == 0)
        def _():
            acc_ref[...] = jnp.zeros_like(acc_ref)

        def swiglu_rows(n_rows):
            h = hs_ref[0:n_rows, :]
            gate = jnp.dot(h, wg_ref[0].astype(BF16), preferred_element_type=F32)
            up = jnp.dot(h, wu_ref[0].astype(BF16), preferred_element_type=F32)
            a = (gate * jax.nn.sigmoid(gate) * up).astype(BF16)
            acc_ref[0:n_rows, :] += jnp.dot(a, wd_ref[0].astype(BF16), preferred_element_type=F32)

        half_only = tile_ref[2, t] <= MOE_TILE // 2

        @pl.when(half_only)
        def _():
            swiglu_rows(MOE_TILE // 2)

        @pl.when(jnp.logical_not(half_only))
        def _():
            swiglu_rows(MOE_TILE)

        @pl.when(c == pl.num_programs(1) - 1)
        def _():
            y_ref[...] = acc_ref[...].astype(BF16)


def _moe_experts(tile_tab, hs, w_gu, w_down, fc=512):
    nc = FFN_DIM // fc

    def tile_of(t, tr):
        return jnp.maximum(jnp.minimum(t, tr[1, 0] - 1), 0)

    def chunk_of(t, c, tr):
        return jnp.where(t < tr[1, 0], c, nc - 1)

    grid_spec = pltpu.PrefetchScalarGridSpec(
        num_scalar_prefetch=1,
        grid=(MOE_MAX_TILES, nc),
        in_specs=[
            pl.BlockSpec((MOE_TILE, D_MODEL), lambda t, c, tr: (tile_of(t, tr), 0)),
            pl.BlockSpec((1, D_MODEL, fc), lambda t, c, tr: (tr[0, t], 0, chunk_of(t, c, tr))),
            pl.BlockSpec((1, D_MODEL, fc), lambda t, c, tr: (tr[0, t], 0, chunk_of(t, c, tr) + nc)),
            pl.BlockSpec((1, fc, D_MODEL), lambda t, c, tr: (tr[0, t], chunk_of(t, c, tr), 0)),
        ],
        out_specs=pl.BlockSpec((MOE_TILE, D_MODEL), lambda t, c, tr: (t, 0)),
        scratch_shapes=[pltpu.VMEM((MOE_TILE, D_MODEL), F32)],
    )
    return pl.pallas_call(
        _moe_expert_kernel,
        grid_spec=grid_spec,
        out_shape=jax.ShapeDtypeStruct((MOE_ROWS, D_MODEL), BF16),
        compiler_params=pltpu.CompilerParams(
            dimension_semantics=("arbitrary", "arbitrary"), vmem_limit_bytes=VMEM_LIMIT),
        name="moe_experts",
    )(tile_tab, hs, w_gu, w_gu, w_down)


def _moe_combine_kernel(meta_ref, x_ref, route_ref, y_hbm, gf_ref, out_ref, yloc_ref, sem):
    b = pl.program_id(0)
    slot = b % 2

    def granule(ref, row):
        return ref.at[pl.ds(pl.multiple_of(row, MOE_GRAN), MOE_GRAN)]

    def start_block(blk, slot_):
        for e in range(N_EXPERTS):
            lo = meta_ref[blk, TAB_LOFF + e]
            goff = meta_ref[blk, TAB_GOFF + e]

            def start(gi, _, lo=lo, goff=goff):
                pltpu.make_async_copy(granule(y_hbm, goff + gi * MOE_GRAN),
                                      granule(yloc_ref.at[slot_], lo + gi * MOE_GRAN), sem.at[slot_]).start()
                return 0

            lax.fori_loop(0, meta_ref[blk, TAB_PC + e] // MOE_GRAN, start, 0)

    @pl.when(b == 0)
    def _():
        yloc_ref[...] = jnp.zeros_like(yloc_ref)
        start_block(0, 0)

    @pl.when(b + 1 < pl.num_programs(0))
    def _():
        start_block(jnp.minimum(b + 1, pl.num_programs(0) - 1), 1 - slot)

    n_mine = 0
    for e in range(N_EXPERTS):
        n_mine = n_mine + meta_ref[b, TAB_PC + e] // MOE_GRAN

    def wait_one(gi, _):
        pltpu.make_async_copy(granule(y_hbm, 0), granule(yloc_ref.at[slot], 0), sem.at[slot]).wait()
        return 0

    lax.fori_loop(0, n_mine, wait_one, 0)

    route_t = route_ref[...]
    route = route_t.T
    col = lax.broadcasted_iota(jnp.int32, (MOE_BLOCK, MOE_LOCAL_ROWS), 1).astype(F32)
    pick1 = jnp.where(col == route[:, ROUTE_D1:ROUTE_D1 + 1], 1.0, 0.0).astype(BF16)
    pick2 = jnp.where(col == route[:, ROUTE_D2:ROUTE_D2 + 1], 1.0, 0.0).astype(BF16)
    yl = yloc_ref[slot]
    moe = (route[:, ROUTE_G1:ROUTE_G1 + 1] * jnp.dot(pick1, yl, preferred_element_type=F32)
           + route[:, ROUTE_G2:ROUTE_G2 + 1] * jnp.dot(pick2, yl, preferred_element_type=F32))
    out_ref[...] = _rmsnorm(x_ref[...] + moe, gf_ref[...])


def _moe_combine(meta, x2d, route, y, g_final):
    grid_spec = pltpu.PrefetchScalarGridSpec(
        num_scalar_prefetch=1,
        grid=(N_MOE_BLOCKS,),
        in_specs=[
            pl.BlockSpec((MOE_BLOCK, D_MODEL), lambda b, m: (b, 0)),
            pl.BlockSpec((ROUTE_ROWS, MOE_BLOCK), lambda b, m: (0, b)),
            pl.BlockSpec(memory_space=pl.ANY),
            pl.BlockSpec((1, D_MODEL), lambda b, m: (0, 0)),
        ],
        out_specs=pl.BlockSpec((MOE_BLOCK, D_MODEL), lambda b, m: (b, 0)),
        scratch_shapes=[
            pltpu.VMEM((2, MOE_LOCAL_ROWS, D_MODEL), BF16),
            pltpu.SemaphoreType.DMA((2,)),
        ],
    )
    return pl.pallas_call(
        _moe_combine_kernel,
        grid_spec=grid_spec,
        out_shape=jax.ShapeDtypeStruct((N_TOK, D_MODEL), F32),
        compiler_params=pltpu.CompilerParams(
            dimension_semantics=("arbitrary",), vmem_limit_bytes=VMEM_LIMIT),
        name="moe_combine",
    )(meta, x2d, route, y, g_final)


def _moe(x2d, g, w_router, w_gu, w_down, g_final):
    route, meta = _moe_route(x2d, g, w_router)
    block_tab, tile_tab, fill_tab = _moe_plan(meta)
    hs = _moe_dispatch(block_tab, tile_tab, fill_tab, x2d, g, route)
    y = _moe_experts(tile_tab, hs, w_gu, w_down)
    return _moe_combine(block_tab, x2d, route, y, g_final)


def _rope_tables():
    def tables(dim):
        inv_freq = ROPE_THETA ** (-jnp.arange(0, dim, 2, dtype=F32) / dim)
        ang = jnp.arange(SEQ, dtype=F32)[:, None] * inv_freq[None, :]
        return jnp.cos(ang), jnp.sin(ang)

    c, s = tables(HEAD_DIM)
    cq = jnp.concatenate([c, c], axis=1)
    sq = jnp.concatenate([-s, s], axis=1)
    c, s = tables(IDX_ROPE_DIM)
    ones = jnp.ones((SEQ, IDX_DIM - IDX_ROPE_DIM), F32)
    zeros = jnp.zeros_like(ones)
    z16 = jnp.zeros_like(s)
    ci = jnp.concatenate([c, c, ones], axis=1)
    sia = jnp.concatenate([-s, z16, zeros], axis=1)
    sib = jnp.concatenate([z16, s, zeros], axis=1)
    rep = LANES // IDX_DIM
    return cq, sq, jnp.tile(ci, (1, rep)), jnp.tile(sia, (1, rep)), jnp.tile(sib, (1, rep))


def kernel(x, norm_mix, norm_ffn, dsa_w_in, dsa_idx_k_gain, dsa_w_out, ffn_w_gu, ffn_w_down, gmlp_w_in,
           gmlp_ln_gain, gmlp_ln_bias, gmlp_w_spatial, gmlp_b_spatial, gmlp_w_out, moe_w_router, moe_w_gu,
           moe_w_down, final_norm):
    x2d = x.reshape(N_TOK, D_MODEL)
    row = lambda v: v.reshape(1, -1)

    w_in = dsa_w_in[0]
    w_all = jnp.pad(w_in, ((0, 0), (0, DSA_COLS - w_in.shape[1])))
    wwi_t = w_in[:, COL_KI + IDX_DIM:COL_KI + IDX_DIM + IDX_HEADS].T
    kig = jnp.pad(dsa_idx_k_gain[0], (0, LANES - IDX_DIM)).reshape(1, LANES)
    q3, k, vt, kia, kib, qi, wt = _dsa_proj(x2d, row(norm_mix[0]), w_all, wwi_t, kig, _rope_tables())
    o3 = _dsa_attn(q3, k, vt, kia, kib, qi, wt)
    x2d = _resid_proj(x2d, o3, dsa_w_out[0])
    x2d = _ffn(x2d, row(norm_ffn[0]), ffn_w_gu[0], ffn_w_down[0])

    gw = GMLP_WIDTH // GMLP_GROUPS
    b_sp_full = jnp.repeat(gmlp_b_spatial[0].T, gw, axis=1)
    x2d = _gmlp(x2d, row(norm_mix[1]), gmlp_w_in[0], row(gmlp_ln_gain[0]), row(gmlp_ln_bias[0]),
                gmlp_w_spatial[0], b_sp_full, gmlp_w_out[0])
    w_router = jnp.pad(moe_w_router[0], ((0, 0), (0, LANES - N_EXPERTS)))
    out = _moe(x2d, row(norm_ffn[1]), w_router, moe_w_gu[0], moe_w_down[0], row(final_norm))
    return out.reshape(BATCH, SEQ, D_MODEL)
```

```python
import functools

import numpy as np
import jax
import jax.numpy as jnp
from jax import lax
from jax.experimental import pallas as pl
from jax.experimental.pallas import tpu as pltpu

F32 = jnp.float32
BF16 = jnp.bfloat16

D_MODEL = 1024
BATCH = 8
SEQ = 2048
N_TOK = BATCH * SEQ
N_HEADS = 8
HEAD_DIM = 128
IDX_HEADS = 8
IDX_DIM = 64
IDX_ROPE_DIM = 32
INDEX_TOPK = 256
Q_BLOCK = 128
KEY_SUPER = 512
ROPE_THETA = 10000.0
GMLP_WIDTH = 1024
GMLP_GROUPS = 8
GMLP_CHUNK = 128
FFN_DIM = 3584
N_EXPERTS = 8
NORM_EPS = 1e-6

LANES = 128
SUBLANES = 8
VMEM_LIMIT = 56 * 1024 * 1024

DSA_COLS = 1920
COL_K = N_HEADS * HEAD_DIM
COL_V = COL_K + HEAD_DIM
COL_QI = COL_V + HEAD_DIM
COL_KI = COL_QI + IDX_HEADS * IDX_DIM

NT_DIMS = (((1,), (1,)), ((), ()))
MASK_BIAS = -1e30
SCORE_MASKED = -3.0e38


def _rmsnorm(x, g):
    ms = jnp.mean(x * x, axis=-1, keepdims=True)
    return x * lax.rsqrt(ms + NORM_EPS) * g


def _cast_once(src_ref, dst_ref):
    @pl.when(pl.program_id(0) == 0)
    def _():
        dst_ref[...] = src_ref[...].astype(BF16)


def _resident(shape):
    return pl.BlockSpec(shape, lambda i: (0,) * len(shape), pipeline_mode=pl.Buffered(1))


def _dsa_proj_kernel(x_ref, g_ref, w_ref, wwi_ref, kig_ref, cq_ref, sq_ref, ci_ref, sia_ref, sib_ref,
                     q_ref, k_ref, vt_ref, kia_ref, kib_ref, qi_ref, wt_ref, wbf_ref):
    _cast_once(w_ref, wbf_ref)
    h = _rmsnorm(x_ref[...], g_ref[...]).astype(BF16)
    yr = jnp.dot(h, wbf_ref[:, COL_K:], preferred_element_type=F32)
    yq = jnp.dot(h, wbf_ref[:, :COL_K], preferred_element_type=F32)
    cq = cq_ref[...]
    sq = sq_ref[...]
    ci = ci_ref[...]
    sia = sia_ref[...]
    sib = sib_ref[...]

    def rope_head(t):
        return t * cq + pltpu.roll(t, HEAD_DIM // 2, 1) * sq

    def rope_idx(t):
        half = IDX_ROPE_DIM // 2
        return t * ci + pltpu.roll(t, LANES - half, 1) * sia + pltpu.roll(t, half, 1) * sib

    q_scale = HEAD_DIM ** -0.5 * np.log2(np.e)
    k_ref[...] = rope_head(yr[:, 0:COL_V - COL_K]).astype(BF16)
    vt_ref[0] = yr[:, COL_V - COL_K:COL_QI - COL_K].T.astype(BF16)
    for p in range(IDX_HEADS * IDX_DIM // LANES):
        c0 = COL_QI - COL_K + p * LANES
        qi_ref[:, p * LANES:(p + 1) * LANES] = rope_idx(yr[:, c0:c0 + LANES]).astype(BF16)
    for hh in range(N_HEADS):
        q_ref[hh] = (rope_head(yq[:, hh * HEAD_DIM:(hh + 1) * HEAD_DIM]) * q_scale).astype(BF16)

    last = yr[:, COL_KI - COL_K:COL_KI - COL_K + LANES]
    lane = lax.broadcasted_iota(jnp.int32, last.shape, 1)
    kraw = jnp.where(lane < IDX_DIM, last, 0.0)
    kms = jnp.sum(kraw * kraw, axis=-1, keepdims=True) * (1.0 / IDX_DIM)
    kn = kraw * lax.rsqrt(kms + NORM_EPS) * kig_ref[...]
    kr = rope_idx(kn)
    kia_ref[...] = kr.astype(BF16)
    kib_ref[...] = pltpu.roll(kr, IDX_DIM, 1).astype(BF16)

    w_scale = IDX_HEADS ** -0.5 * IDX_DIM ** -0.5
    wt_ref[...] = lax.dot_general(wwi_ref[...].astype(BF16), h, NT_DIMS, preferred_element_type=F32) * w_scale


def _dsa_proj(x2d, g, w_all, wwi_t, kig, tabs, tm=KEY_SUPER):
    n = x2d.shape[0]
    nblk_seq = SEQ // tm
    tab_spec = pl.BlockSpec((tm, LANES), lambda i: (i % nblk_seq, 0))
    full = lambda shape: pl.BlockSpec(shape, lambda i: (0,) * len(shape))
    return pl.pallas_call(
        _dsa_proj_kernel,
        grid=(n // tm,),
        in_specs=[
            pl.BlockSpec((tm, D_MODEL), lambda i: (i, 0)),
            full((1, D_MODEL)),
            _resident((D_MODEL, DSA_COLS)),
            full((IDX_HEADS, D_MODEL)),
            full((1, LANES)),
            tab_spec, tab_spec, tab_spec, tab_spec, tab_spec,
        ],
        out_specs=[
            pl.BlockSpec((N_HEADS, tm, HEAD_DIM), lambda i: (0, i, 0)),
            pl.BlockSpec((tm, HEAD_DIM), lambda i: (i, 0)),
            pl.BlockSpec((1, HEAD_DIM, tm), lambda i: (i, 0, 0)),
            pl.BlockSpec((tm, LANES), lambda i: (i, 0)),
            pl.BlockSpec((tm, LANES), lambda i: (i, 0)),
            pl.BlockSpec((tm, IDX_HEADS * IDX_DIM), lambda i: (i, 0)),
            pl.BlockSpec((IDX_HEADS, tm), lambda i: (0, i)),
        ],
        out_shape=[
            jax.ShapeDtypeStruct((N_HEADS, n, HEAD_DIM), BF16),
            jax.ShapeDtypeStruct((n, HEAD_DIM), BF16),
            jax.ShapeDtypeStruct((n // tm, HEAD_DIM, tm), BF16),
            jax.ShapeDtypeStruct((n, LANES), BF16),
            jax.ShapeDtypeStruct((n, LANES), BF16),
            jax.ShapeDtypeStruct((n, IDX_HEADS * IDX_DIM), BF16),
            jax.ShapeDtypeStruct((IDX_HEADS, n), F32),
        ],
        scratch_shapes=[pltpu.VMEM((D_MODEL, DSA_COLS), BF16)],
        compiler_params=pltpu.CompilerParams(
            dimension_semantics=("arbitrary",), vmem_limit_bytes=VMEM_LIMIT),
        name="dsa_proj",
    )(x2d, g, w_all, wwi_t, kig, *tabs)


def _ordered_bits_to_float(u):
    key = u ^ jnp.int32(-2 ** 31)
    key = jnp.maximum(key, jnp.int32(-2139095041))
    bits = jnp.where(key >= 0, key, key ^ jnp.int32(0x7FFFFFFF))
    return lax.bitcast_convert_type(bits, F32)


def _dsa_attn_kernel(q_ref, k_ref, vt_ref, kia_ref, kib_ref, qi_ref, wt_ref, o_ref,
                     s_ref, sel_ref, l_ref, p_ref, acc_ref):
    j = pl.program_id(1)
    per_super = KEY_SUPER // Q_BLOCK
    n_super = j // per_super + 1
    hq = N_HEADS * Q_BLOCK

    qpos = j * Q_BLOCK + lax.broadcasted_iota(jnp.int32, (Q_BLOCK, Q_BLOCK), 1)
    krow = lax.broadcasted_iota(jnp.int32, (Q_BLOCK, Q_BLOCK), 0)

    def super_start(sc):
        return sc * KEY_SUPER if isinstance(sc, int) else pl.multiple_of(sc * KEY_SUPER, KEY_SUPER)

    def rows_of(sc, c4):
        return pl.ds(super_start(sc) + c4 * Q_BLOCK, Q_BLOCK)

    def super_rows(sc):
        return pl.ds(super_start(sc), KEY_SUPER)

    def causal(sc, c4):
        return (krow + (sc * KEY_SUPER + c4 * Q_BLOCK)) <= qpos

    def fold8(t):
        return t.reshape(t.shape[0] // SUBLANES, SUBLANES, t.shape[1])

    def score_body(sc, _):
        qpps = [
            jnp.concatenate([qi_ref[:, (2 * half) * LANES:(2 * half + 1) * LANES],
                             qi_ref[:, (2 * half + 1) * LANES:(2 * half + 2) * LANES]], axis=0)
            for half in range(IDX_HEADS // 4)
        ]
        group = 2 * Q_BLOCK
        for kg in range(KEY_SUPER // group):
            rows = pl.ds(super_start(sc) + kg * group, group)
            ka = kia_ref[rows, :]
            kb = kib_ref[rows, :]
            acc = None
            for half in range(IDX_HEADS // 4):
                ra = lax.dot_general(ka, qpps[half], NT_DIMS, preferred_element_type=F32)
                rb = lax.dot_general(kb, qpps[half], NT_DIMS, preferred_element_type=F32)
                for sub in range(2):
                    p = 2 * half + sub
                    cols = slice(sub * Q_BLOCK, (sub + 1) * Q_BLOCK)
                    t2 = (jnp.maximum(ra[:, cols], 0.0) * wt_ref[2 * p:2 * p + 1, :]
                          + jnp.maximum(rb[:, cols], 0.0) * wt_ref[2 * p + 1:2 * p + 2, :])
                    acc = t2 if acc is None else acc + t2
            kpos = sc * KEY_SUPER + kg * group + lax.broadcasted_iota(jnp.int32, (group, Q_BLOCK), 0)
            qp_s = j * Q_BLOCK + lax.broadcasted_iota(jnp.int32, (group, Q_BLOCK), 1)
            s_ref[rows, :] = jnp.where(kpos <= qp_s, acc, SCORE_MASKED)
        return 0

    def scores(n_static):
        for sc in range(n_static):
            score_body(sc, 0)

    for n_static in range(1, SEQ // KEY_SUPER + 1):
        pl.when(n_super == n_static)(functools.partial(scores, n_static))

    def search(n_chunks):
        def count_ge(cand):
            accs = [jnp.zeros((SUBLANES, LANES), F32), jnp.zeros((SUBLANES, LANES), F32)]
            for c in range(n_chunks):
                ind = jnp.where(s_ref[pl.ds(c * Q_BLOCK, Q_BLOCK), :] >= cand, 1.0, 0.0)
                accs[c % 2] = accs[c % 2] + fold8(ind).sum(axis=0)
            return jnp.sum(accs[0] + accs[1], axis=0, keepdims=True)

        def bit_body(i, carry):
            prefix, cbest = carry
            cu = prefix | lax.shift_left(jnp.int32(1), 31 - i)
            cnt = count_ge(_ordered_bits_to_float(cu))
            ok = cnt >= float(INDEX_TOPK)
            return jnp.where(ok, cu, prefix), jnp.where(ok, cnt, cbest)

        prefix, cbest = lax.fori_loop(
            0, 32, bit_body,
            (jnp.zeros((1, LANES), jnp.int32), jnp.full((1, LANES), float(n_chunks * Q_BLOCK), F32)))
        sel_ref[0:1, :] = _ordered_bits_to_float(prefix)
        sel_ref[1:2, :] = cbest

    for n_chunks in range(1, SEQ // Q_BLOCK + 1):
        pl.when(j == n_chunks - 1)(functools.partial(search, n_chunks))
    tau = sel_ref[0:1, :]
    drop = sel_ref[1:2, :] - float(INDEX_TOPK)

    upper = jnp.where(krow < lax.broadcasted_iota(jnp.int32, (Q_BLOCK, Q_BLOCK), 1), 1.0, 0.0).astype(BF16)
    pair_w = 2 * Q_BLOCK

    def select_logit_chunk(sc, carry):
        later_ties, m_acc = carry
        biases = [None] * per_super
        for c4 in reversed(range(per_super)):
            s = s_ref[rows_of(sc, c4), :]
            eq = jnp.where(s == tau, 1.0, 0.0)
            ties_after = jnp.dot(upper, eq.astype(BF16), preferred_element_type=F32) + later_ties
            later_ties = later_ties + jnp.sum(eq, axis=0, keepdims=True)
            tie_bias = jnp.where(ties_after >= drop, 0.0, MASK_BIAS)
            bias = jnp.where(s > tau, 0.0, jnp.where(s == tau, tie_bias, MASK_BIAS))
            biases[c4] = jnp.where(causal(sc, c4), bias, MASK_BIAS)
        bias = jnp.concatenate(biases, axis=0)
        bias2 = jnp.concatenate([bias, bias], axis=1)
        srows = super_rows(sc)
        kc = k_ref[srows, :]
        parts = []
        for pr in range(N_HEADS // 2):
            qpair = q_ref[2 * pr:2 * pr + 2].reshape(pair_w, HEAD_DIM)
            lg = lax.dot_general(kc, qpair, NT_DIMS, preferred_element_type=F32) + bias2
            l_ref[srows, pr * pair_w:(pr + 1) * pair_w] = lg
            parts.append(fold8(lg).max(axis=0))
        return later_ties, jnp.maximum(m_acc, jnp.concatenate(parts, axis=1))

    _, m_acc = lax.fori_loop(
        0, n_super, lambda i, carry: select_logit_chunk(n_super - 1 - i, carry),
        (jnp.zeros((1, LANES), F32), jnp.full((SUBLANES, hq), MASK_BIAS, F32)))
    m = jnp.max(m_acc, axis=0, keepdims=True)

    def prob_chunk(sc, d_acc):
        for c4 in range(per_super):
            rows = rows_of(sc, c4)
            parts = []
            for pr in range(N_HEADS // 2):
                cols = slice(pr * pair_w, (pr + 1) * pair_w)
                p = jnp.exp2(l_ref[rows, cols] - m[:, cols])
                p_ref[rows, cols] = p.astype(BF16)
                parts.append(fold8(p).sum(axis=0))
            d_acc = d_acc + jnp.concatenate(parts, axis=1)
        return d_acc

    def pv_chunk(sc):
        acc_ref[...] += jnp.dot(vt_ref[sc], p_ref[super_rows(sc), :], preferred_element_type=F32)

    def prob_pv_body(sc, d_acc):
        pv_chunk(sc - 1)
        return prob_chunk(sc, d_acc)

    acc_ref[...] = jnp.zeros_like(acc_ref)
    d_acc = prob_chunk(0, jnp.zeros((SUBLANES, hq), F32))
    d_acc = lax.fori_loop(1, n_super, prob_pv_body, d_acc)
    pv_chunk(n_super - 1)
    inv = 1.0 / jnp.sum(d_acc, axis=0, keepdims=True)
    for h in range(N_HEADS):
        cols = slice(h * Q_BLOCK, (h + 1) * Q_BLOCK)
        o_ref[h] = (acc_ref[:, cols] * inv[:, cols]).T.astype(BF16)


def _dsa_attn(q3, k, vt, kia, kib, qi, wt):
    nqb = SEQ // Q_BLOCK
    n_sup = SEQ // KEY_SUPER
    hq = N_HEADS * Q_BLOCK
    return pl.pallas_call(
        _dsa_attn_kernel,
        grid=(BATCH, nqb),
        in_specs=[
            pl.BlockSpec((N_HEADS, Q_BLOCK, HEAD_DIM), lambda b, j: (0, b * nqb + j, 0)),
            pl.BlockSpec((SEQ, HEAD_DIM), lambda b, j: (b, 0)),
            pl.BlockSpec((n_sup, HEAD_DIM, KEY_SUPER), lambda b, j: (b, 0, 0)),
            pl.BlockSpec((SEQ, LANES), lambda b, j: (b, 0)),
            pl.BlockSpec((SEQ, LANES), lambda b, j: (b, 0)),
            pl.BlockSpec((Q_BLOCK, IDX_HEADS * IDX_DIM), lambda b, j: (b * nqb + j, 0)),
            pl.BlockSpec((IDX_HEADS, Q_BLOCK), lambda b, j: (0, b * nqb + j)),
        ],
        out_specs=pl.BlockSpec((N_HEADS, Q_BLOCK, HEAD_DIM), lambda b, j: (0, b * nqb + j, 0)),
        out_shape=jax.ShapeDtypeStruct((N_HEADS, N_TOK, HEAD_DIM), BF16),
        scratch_shapes=[
            pltpu.VMEM((SEQ, Q_BLOCK), F32),
            pltpu.VMEM((SUBLANES, Q_BLOCK), F32),
            pltpu.VMEM((SEQ, hq), F32),
            pltpu.VMEM((SEQ, hq), BF16),
            pltpu.VMEM((HEAD_DIM, hq), F32),
        ],
        compiler_params=pltpu.CompilerParams(
            dimension_semantics=("parallel", "arbitrary"), vmem_limit_bytes=VMEM_LIMIT),
        name="dsa_attn",
    )(q3, k, vt, kia, kib, qi, wt)


def _resid_proj_kernel(x_ref, o_ref, w_ref, out_ref, wbf_ref):
    _cast_once(w_ref, wbf_ref)
    o = jnp.concatenate([o_ref[h] for h in range(N_HEADS)], axis=1)
    out_ref[...] = x_ref[...] + jnp.dot(o, wbf_ref[...], preferred_element_type=F32)


def _resid_proj(x2d, o3, w_out, tm=1024):
    n = x2d.shape[0]
    return pl.pallas_call(
        _resid_proj_kernel,
        grid=(n // tm,),
        in_specs=[
            pl.BlockSpec((tm, D_MODEL), lambda i: (i, 0)),
            pl.BlockSpec((N_HEADS, tm, HEAD_DIM), lambda i: (0, i, 0)),
            _resident((D_MODEL, D_MODEL)),
        ],
        out_specs=pl.BlockSpec((tm, D_MODEL), lambda i: (i, 0)),
        out_shape=jax.ShapeDtypeStruct((n, D_MODEL), F32),
        scratch_shapes=[pltpu.VMEM((D_MODEL, D_MODEL), BF16)],
        compiler_params=pltpu.CompilerParams(
            dimension_semantics=("arbitrary",), vmem_limit_bytes=VMEM_LIMIT),
        name="dsa_out_proj",
    )(x2d, o3, w_out)


def _ffn_kernel(x_ref, g_ref, wg_ref, wu_ref, wd_ref, out_ref, h_ref, acc_ref):
    c = pl.program_id(1)

    @pl.when(c == 0)
    def _():
        h_ref[...] = _rmsnorm(x_ref[...], g_ref[...]).astype(BF16)
        acc_ref[...] = jnp.zeros_like(acc_ref)

    h = h_ref[...]
    gate = jnp.dot(h, wg_ref[...].astype(BF16), preferred_element_type=F32)
    up = jnp.dot(h, wu_ref[...].astype(BF16), preferred_element_type=F32)
    a = (gate * jax.nn.sigmoid(gate) * up).astype(BF16)
    acc_ref[...] += jnp.dot(a, wd_ref[...].astype(BF16), preferred_element_type=F32)

    @pl.when(c == pl.num_programs(1) - 1)
    def _():
        out_ref[...] = x_ref[...] + acc_ref[...]


def _ffn(x2d, g, w_gu, w_down, tm=1024, fc=512):
    n = x2d.shape[0]
    nc = FFN_DIM // fc
    return pl.pallas_call(
        _ffn_kernel,
        grid=(n // tm, nc),
        in_specs=[
            pl.BlockSpec((tm, D_MODEL), lambda i, c: (i, 0)),
            pl.BlockSpec((1, D_MODEL), lambda i, c: (0, 0)),
            pl.BlockSpec((D_MODEL, fc), lambda i, c: (0, c)),
            pl.BlockSpec((D_MODEL, fc), lambda i, c: (0, c + nc)),
            pl.BlockSpec((fc, D_MODEL), lambda i, c: (c, 0)),
        ],
        out_specs=pl.BlockSpec((tm, D_MODEL), lambda i, c: (i, 0)),
        out_shape=jax.ShapeDtypeStruct((n, D_MODEL), F32),
        scratch_shapes=[pltpu.VMEM((tm, D_MODEL), BF16), pltpu.VMEM((tm, D_MODEL), F32)],
        compiler_params=pltpu.CompilerParams(
            dimension_semantics=("parallel", "arbitrary"), vmem_limit_bytes=VMEM_LIMIT),
        name="swiglu_ffn",
    )(x2d, g, w_gu, w_gu, w_down)


def _gelu_tanh(x):
    return 0.5 * x * (1.0 + jnp.tanh(np.sqrt(2.0 / np.pi) * (x + 0.044715 * (x * x * x))))


def _gmlp_kernel(x_ref, g_ref, win_ref, lng_ref, lnb_ref, ws_ref, bs_ref, wout_ref, out_ref,
                 winbf_ref, woutbf_ref):
    _cast_once(win_ref, winbf_ref)
    _cast_once(wout_ref, woutbf_ref)
    x = x_ref[...]
    tm = x.shape[0]
    h = _rmsnorm(x, g_ref[...]).astype(BF16)
    v = _gelu_tanh(jnp.dot(h, winbf_ref[:, GMLP_WIDTH:], preferred_element_type=F32))
    u = _gelu_tanh(jnp.dot(h, winbf_ref[:, :GMLP_WIDTH], preferred_element_type=F32))
    mu = jnp.mean(v, axis=-1, keepdims=True)
    vc = v - mu
    var = jnp.mean(vc * vc, axis=-1, keepdims=True)
    vb = (vc * lax.rsqrt(var + NORM_EPS) * lng_ref[...] + lnb_ref[...]).astype(BF16)

    gw = GMLP_WIDTH // GMLP_GROUPS
    row = lax.broadcasted_iota(jnp.int32, (GMLP_CHUNK, GMLP_CHUNK), 0)
    col = lax.broadcasted_iota(jnp.int32, (GMLP_CHUNK, GMLP_CHUNK), 1)
    w_tril = [jnp.where(row >= col, ws_ref[gi], 0.0).astype(BF16) for gi in range(GMLP_GROUPS)]
    rows = []
    for c in range(tm // GMLP_CHUNK):
        parts = [
            jnp.dot(w_tril[gi], vb[c * GMLP_CHUNK:(c + 1) * GMLP_CHUNK, gi * gw:(gi + 1) * gw],
                    preferred_element_type=F32)
            for gi in range(GMLP_GROUPS)
        ]
        rows.append(jnp.concatenate(parts, axis=1) + bs_ref[...])
    s = jnp.concatenate(rows, axis=0)
    z = (u * s).astype(BF16)
    out_ref[...] = x + jnp.dot(z, woutbf_ref[...], preferred_element_type=F32)


def _gmlp(x2d, g, w_in, ln_g, ln_b, w_sp, b_sp_full, w_out, tm=512):
    n = x2d.shape[0]
    full = lambda shape: pl.BlockSpec(shape, lambda i: (0,) * len(shape))
    return pl.pallas_call(
        _gmlp_kernel,
        grid=(n // tm,),
        in_specs=[
            pl.BlockSpec((tm, D_MODEL), lambda i: (i, 0)),
            full((1, D_MODEL)),
            _resident((D_MODEL, 2 * GMLP_WIDTH)),
            full((1, GMLP_WIDTH)),
            full((1, GMLP_WIDTH)),
            full((GMLP_GROUPS, GMLP_CHUNK, GMLP_CHUNK)),
            full((GMLP_CHUNK, GMLP_WIDTH)),
            _resident((GMLP_WIDTH, D_MODEL)),
        ],
        out_specs=pl.BlockSpec((tm, D_MODEL), lambda i: (i, 0)),
        out_shape=jax.ShapeDtypeStruct((n, D_MODEL), F32),
        scratch_shapes=[pltpu.VMEM((D_MODEL, 2 * GMLP_WIDTH), BF16), pltpu.VMEM((GMLP_WIDTH, D_MODEL), BF16)],
        compiler_params=pltpu.CompilerParams(
            dimension_semantics=("arbitrary",), vmem_limit_bytes=VMEM_LIMIT),
        name="gmlp",
    )(x2d, g, w_in, ln_g, ln_b, w_sp, b_sp_full, w_out)


MOE_BLOCK = 512
MOE_GRAN = 16
MOE_TILE = 1024
MOE_LOCAL_ROWS = 2 * MOE_BLOCK + N_EXPERTS * MOE_GRAN
N_MOE_BLOCKS = N_TOK // MOE_BLOCK
MOE_MAX_TILES = (2 * N_TOK + N_MOE_BLOCKS * N_EXPERTS * (MOE_GRAN - 1)) // MOE_TILE + N_EXPERTS
MOE_ROWS = MOE_MAX_TILES * MOE_TILE
ROUTE_D1, ROUTE_D2, ROUTE_G1, ROUTE_G2 = 0, 1, 2, 3
ROUTE_ROWS = N_EXPERTS
TAB_LOFF, TAB_PC, TAB_GOFF = 0, N_EXPERTS, 2 * N_EXPERTS


def _moe_route_kernel(x_ref, g_ref, wr_ref, route_ref, meta_ref):
    b = pl.program_id(0)
    t_blk = MOE_BLOCK
    hf = _rmsnorm(x_ref[...], g_ref[...])
    wr = wr_ref[...]
    h_hi = hf.astype(BF16)
    h_lo = (hf - h_hi.astype(F32)).astype(BF16)
    w_hi = wr.astype(BF16)
    w_lo = (wr - w_hi.astype(F32)).astype(BF16)
    logits = (jnp.dot(h_hi, w_hi, preferred_element_type=F32) + jnp.dot(h_hi, w_lo, preferred_element_type=F32)
              + jnp.dot(h_lo, w_hi, preferred_element_type=F32))

    lg = logits.T[0:N_EXPERTS, :]
    e_idx = lax.broadcasted_iota(jnp.int32, lg.shape, 0).astype(F32)
    neg_inf = -jnp.inf
    m1 = jnp.max(lg, axis=0, keepdims=True)
    i1 = jnp.min(jnp.where(lg == m1, e_idx, float(N_EXPERTS)), axis=0, keepdims=True)
    lg2 = jnp.where(e_idx == i1, neg_inf, lg)
    m2 = jnp.max(lg2, axis=0, keepdims=True)
    i2 = jnp.min(jnp.where(lg2 == m2, e_idx, float(N_EXPERTS)), axis=0, keepdims=True)
    t = jnp.exp(m2 - m1)
    g1 = 1.0 / (1.0 + t)
    g2 = t * g1

    memb = jnp.where(jnp.logical_or(e_idx == i1, e_idx == i2), 1.0, 0.0)
    tr = lax.broadcasted_iota(jnp.int32, (t_blk, t_blk), 0)
    tc = lax.broadcasted_iota(jnp.int32, (t_blk, t_blk), 1)
    earlier = jnp.where(tr < tc, 1.0, 0.0).astype(BF16)
    rank = jnp.dot(memb.astype(BF16), earlier, preferred_element_type=F32)
    loff_col = jnp.zeros((N_EXPERTS, 1), F32)
    e_col = lax.broadcasted_iota(jnp.int32, (N_EXPERTS, 1), 0)
    lo = jnp.int32(0)
    for e in range(N_EXPERTS):
        cnt = jnp.sum(memb[e:e + 1, :]).astype(jnp.int32)
        pce = (cnt + (MOE_GRAN - 1)) // MOE_GRAN * MOE_GRAN
        meta_ref[b, TAB_LOFF + e] = lo
        meta_ref[b, TAB_PC + e] = pce
        loff_col = jnp.where(e_col == e, lo.astype(F32), loff_col)
        lo = lo + pce
    dest = loff_col + rank
    d1 = jnp.sum(jnp.where(e_idx == i1, dest, 0.0), axis=0, keepdims=True)
    d2 = jnp.sum(jnp.where(e_idx == i2, dest, 0.0), axis=0, keepdims=True)
    r_idx = lax.broadcasted_iota(jnp.int32, lg.shape, 0)
    route_ref[...] = jnp.where(r_idx == ROUTE_D1, d1,
                               jnp.where(r_idx == ROUTE_D2, d2,
                                         jnp.where(r_idx == ROUTE_G1, g1, jnp.where(r_idx == ROUTE_G2, g2, 0.0))))


def _moe_route(x2d, g, w_router):
    return pl.pallas_call(
        _moe_route_kernel,
        grid=(N_MOE_BLOCKS,),
        in_specs=[
            pl.BlockSpec((MOE_BLOCK, D_MODEL), lambda b: (b, 0)),
            pl.BlockSpec((1, D_MODEL), lambda b: (0, 0)),
            pl.BlockSpec((D_MODEL, LANES), lambda b: (0, 0)),
        ],
        out_specs=[
            pl.BlockSpec((ROUTE_ROWS, MOE_BLOCK), lambda b: (0, b)),
            pl.BlockSpec(memory_space=pltpu.SMEM),
        ],
        out_shape=[
            jax.ShapeDtypeStruct((ROUTE_ROWS, N_TOK), F32),
            jax.ShapeDtypeStruct((N_MOE_BLOCKS, 2 * N_EXPERTS), jnp.int32),
        ],
        compiler_params=pltpu.CompilerParams(
            dimension_semantics=("arbitrary",), vmem_limit_bytes=VMEM_LIMIT),
        name="moe_route",
    )(x2d, g, w_router)


def _moe_plan(meta):
    pc = meta[:, TAB_PC:TAB_PC + N_EXPERTS]
    totals = jnp.sum(pc, axis=0)
    per_expert = (totals + (MOE_TILE - 1)) // MOE_TILE
    e_lt = jnp.arange(N_EXPERTS)[:, None] < jnp.arange(N_EXPERTS)[None, :]
    first_tile = jnp.sum(jnp.where(e_lt, per_expert[:, None], 0), axis=0)
    ends = first_tile + per_expert
    seg_start = first_tile * MOE_TILE
    b_lt = jnp.arange(N_MOE_BLOCKS)[:, None] < jnp.arange(N_MOE_BLOCKS)[None, :]
    goff = seg_start[None, :] + jnp.sum(jnp.where(b_lt[:, :, None], pc[:, None, :], 0), axis=0)
    block_tab = jnp.concatenate([meta, goff], axis=1).astype(jnp.int32)
    n_tiles = jnp.sum(per_expert)
    t = jnp.minimum(jnp.arange(MOE_MAX_TILES, dtype=jnp.int32), n_tiles - 1)
    mine = jnp.logical_and(t[:, None] >= first_tile[None, :], t[:, None] < ends[None, :])
    expert = jnp.sum(jnp.where(mine, jnp.arange(N_EXPERTS)[None, :], 0), axis=1)
    left = totals[None, :] - (t[:, None] - first_tile[None, :]) * MOE_TILE
    used = jnp.sum(jnp.where(mine, jnp.clip(left, 0, MOE_TILE), 0), axis=1)
    tile_tab = jnp.stack([expert, jnp.full_like(t, n_tiles), used]).astype(jnp.int32)
    fill_tab = jnp.stack([seg_start + totals, ends * MOE_TILE]).astype(jnp.int32)
    return block_tab, tile_tab, fill_tab


def _moe_dispatch_kernel(tab_ref, tile_ref, fill_ref, x_ref, g_ref, route_ref, hs_hbm, hloc_ref, zero_ref, sem):
    b = pl.program_id(0)
    slot = b % 2
    hloc = hloc_ref.at[slot]
    hb = _rmsnorm(x_ref[...], g_ref[...]).astype(BF16)

    route_t = route_ref[...]
    r_iota = lax.broadcasted_iota(jnp.int32, (MOE_LOCAL_ROWS, MOE_BLOCK), 0).astype(F32)
    hit = jnp.logical_or(r_iota == route_t[ROUTE_D1:ROUTE_D1 + 1, :], r_iota == route_t[ROUTE_D2:ROUTE_D2 + 1, :])
    perm = jnp.where(hit, 1.0, 0.0).astype(BF16)
    hloc_ref[slot] = jnp.dot(perm, hb, preferred_element_type=F32).astype(BF16)

    def granule(ref, row):
        return ref.at[pl.ds(pl.multiple_of(row, MOE_GRAN), MOE_GRAN)]

    for e in range(N_EXPERTS):
        lo = tab_ref[b, TAB_LOFF + e]
        goff = tab_ref[b, TAB_GOFF + e]

        def start(gi, _, lo=lo, goff=goff):
            pltpu.make_async_copy(granule(hloc, lo + gi * MOE_GRAN),
                                  granule(hs_hbm, goff + gi * MOE_GRAN), sem.at[slot]).start()
            return 0

        lax.fori_loop(0, tab_ref[b, TAB_PC + e] // MOE_GRAN, start, 0)

    def wait_block(blk, slot_):
        n = 0
        for e in range(N_EXPERTS):
            n = n + tab_ref[blk, TAB_PC + e] // MOE_GRAN

        def wait_one(gi, _):
            pltpu.make_async_copy(granule(hloc_ref.at[slot_], 0), granule(hs_hbm, 0), sem.at[slot_]).wait()
            return 0

        lax.fori_loop(0, n, wait_one, 0)

    @pl.when(b > 0)
    def _():
        wait_block(jnp.maximum(b - 1, 0), 1 - slot)

    @pl.when(b == pl.num_programs(0) - 1)
    def _():
        wait_block(b, slot)
        zero_ref[...] = jnp.zeros_like(zero_ref)
        n_fill = 0
        for e in range(N_EXPERTS):
            first = fill_ref[0, e]
            n_e = (fill_ref[1, e] - first) // MOE_GRAN

            def fill(gi, _, first=first):
                pltpu.make_async_copy(granule(zero_ref, 0), granule(hs_hbm, first + gi * MOE_GRAN),
                                      sem.at[slot]).start()
                return 0

            lax.fori_loop(0, n_e, fill, 0)
            n_fill = n_fill + n_e

        def wait_fill(gi, _):
            pltpu.make_async_copy(granule(zero_ref, 0), granule(hs_hbm, 0), sem.at[slot]).wait()
            return 0

        lax.fori_loop(0, n_fill, wait_fill, 0)

        def tile_at(ti):
            return hs_hbm.at[pl.ds(pl.multiple_of(ti * MOE_TILE, MOE_TILE), MOE_TILE)]

        n_tiles = tile_ref[1, 0]

        def fill_tile(ti, _):
            pltpu.make_async_copy(zero_ref, tile_at(ti), sem.at[slot]).start()
            return 0

        def wait_tile(ti, _):
            pltpu.make_async_copy(zero_ref, tile_at(0), sem.at[slot]).wait()
            return 0

        lax.fori_loop(n_tiles, MOE_MAX_TILES, fill_tile, 0)
        lax.fori_loop(n_tiles, MOE_MAX_TILES, wait_tile, 0)


def _moe_dispatch(block_tab, tile_tab, fill_tab, x2d, g, route):
    grid_spec = pltpu.PrefetchScalarGridSpec(
        num_scalar_prefetch=3,
        grid=(N_MOE_BLOCKS,),
        in_specs=[
            pl.BlockSpec((MOE_BLOCK, D_MODEL), lambda b, *_: (b, 0)),
            pl.BlockSpec((1, D_MODEL), lambda b, *_: (0, 0)),
            pl.BlockSpec((ROUTE_ROWS, MOE_BLOCK), lambda b, *_: (0, b)),
        ],
        out_specs=pl.BlockSpec(memory_space=pl.ANY),
        scratch_shapes=[
            pltpu.VMEM((2, MOE_LOCAL_ROWS, D_MODEL), BF16),
            pltpu.VMEM((MOE_TILE, D_MODEL), BF16),
            pltpu.SemaphoreType.DMA((2,)),
        ],
    )
    return pl.pallas_call(
        _moe_dispatch_kernel,
        grid_spec=grid_spec,
        out_shape=jax.ShapeDtypeStruct((MOE_ROWS, D_MODEL), BF16),
        compiler_params=pltpu.CompilerParams(
            dimension_semantics=("arbitrary",), vmem_limit_bytes=VMEM_LIMIT),
        name="moe_dispatch",
    )(block_tab, tile_tab, fill_tab, x2d, g, route)


def _moe_expert_kernel(tile_ref, hs_ref, wg_ref, wu_ref, wd_ref, y_ref, acc_ref):
    t = pl.program_id(0)
    c = pl.program_id(1)
    active = t < tile_ref[1, 0]

    @pl.when(jnp.logical_and(jnp.logical_not(active), c == 0))
    def _():
        y_ref[...] = jnp.zeros_like(y_ref)

    @pl.when(active)
    def _():
        @pl.when(c == 0)
        def _():
            acc_ref[...] = jnp.zeros_like(acc_ref)

        def swiglu_rows(n_rows):
            h = hs_ref[0:n_rows, :]
            gate = jnp.dot(h, wg_ref[0].astype(BF16), preferred_element_type=F32)
            up = jnp.dot(h, wu_ref[0].astype(BF16), preferred_element_type=F32)
            a = (gate * jax.nn.sigmoid(gate) * up).astype(BF16)
            acc_ref[0:n_rows, :] += jnp.dot(a, wd_ref[0].astype(BF16), preferred_element_type=F32)

        half_only = tile_ref[2, t] <= MOE_TILE // 2

        @pl.when(half_only)
        def _():
            swiglu_rows(MOE_TILE // 2)

        @pl.when(jnp.logical_not(half_only))
        def _():
            swiglu_rows(MOE_TILE)

        @pl.when(c == pl.num_programs(1) - 1)
        def _():
            y_ref[...] = acc_ref[...].astype(BF16)


def _moe_experts(tile_tab, hs, w_gu, w_down, fc=512):
    nc = FFN_DIM // fc

    def tile_of(t, tr):
        return jnp.maximum(jnp.minimum(t, tr[1, 0] - 1), 0)

    def chunk_of(t, c, tr):
        return jnp.where(t < tr[1, 0], c, nc - 1)

    grid_spec = pltpu.PrefetchScalarGridSpec(
        num_scalar_prefetch=1,
        grid=(MOE_MAX_TILES, nc),
        in_specs=[
            pl.BlockSpec((MOE_TILE, D_MODEL), lambda t, c, tr: (tile_of(t, tr), 0)),
            pl.BlockSpec((1, D_MODEL, fc), lambda t, c, tr: (tr[0, t], 0, chunk_of(t, c, tr))),
            pl.BlockSpec((1, D_MODEL, fc), lambda t, c, tr: (tr[0, t], 0, chunk_of(t, c, tr) + nc)),
            pl.BlockSpec((1, fc, D_MODEL), lambda t, c, tr: (tr[0, t], chunk_of(t, c, tr), 0)),
        ],
        out_specs=pl.BlockSpec((MOE_TILE, D_MODEL), lambda t, c, tr: (t, 0)),
        scratch_shapes=[pltpu.VMEM((MOE_TILE, D_MODEL), F32)],
    )
    return pl.pallas_call(
        _moe_expert_kernel,
        grid_spec=grid_spec,
        out_shape=jax.ShapeDtypeStruct((MOE_ROWS, D_MODEL), BF16),
        compiler_params=pltpu.CompilerParams(
            dimension_semantics=("arbitrary", "arbitrary"), vmem_limit_bytes=VMEM_LIMIT),
        name="moe_experts",
    )(tile_tab, hs, w_gu, w_gu, w_down)


def _moe_combine_kernel(meta_ref, x_ref, route_ref, y_hbm, gf_ref, out_ref, yloc_ref, sem):
    b = pl.program_id(0)
    slot = b % 2

    def granule(ref, row):
        return ref.at[pl.ds(pl.multiple_of(row, MOE_GRAN), MOE_GRAN)]

    def start_block(blk, slot_):
        for e in range(N_EXPERTS):
            lo = meta_ref[blk, TAB_LOFF + e]
            goff = meta_ref[blk, TAB_GOFF + e]

            def start(gi, _, lo=lo, goff=goff):
                pltpu.make_async_copy(granule(y_hbm, goff + gi * MOE_GRAN),
                                      granule(yloc_ref.at[slot_], lo + gi * MOE_GRAN), sem.at[slot_]).start()
                return 0

            lax.fori_loop(0, meta_ref[blk, TAB_PC + e] // MOE_GRAN, start, 0)

    @pl.when(b == 0)
    def _():
        yloc_ref[...] = jnp.zeros_like(yloc_ref)
        start_block(0, 0)

    @pl.when(b + 1 < pl.num_programs(0))
    def _():
        start_block(jnp.minimum(b + 1, pl.num_programs(0) - 1), 1 - slot)

    n_mine = 0
    for e in range(N_EXPERTS):
        n_mine = n_mine + meta_ref[b, TAB_PC + e] // MOE_GRAN

    def wait_one(gi, _):
        pltpu.make_async_copy(granule(y_hbm, 0), granule(yloc_ref.at[slot], 0), sem.at[slot]).wait()
        return 0

    lax.fori_loop(0, n_mine, wait_one, 0)

    route_t = route_ref[...]
    route = route_t.T
    col = lax.broadcasted_iota(jnp.int32, (MOE_BLOCK, MOE_LOCAL_ROWS), 1).astype(F32)
    pick1 = jnp.where(col == route[:, ROUTE_D1:ROUTE_D1 + 1], 1.0, 0.0).astype(BF16)
    pick2 = jnp.where(col == route[:, ROUTE_D2:ROUTE_D2 + 1], 1.0, 0.0).astype(BF16)
    yl = yloc_ref[slot]
    moe = (route[:, ROUTE_G1:ROUTE_G1 + 1] * jnp.dot(pick1, yl, preferred_element_type=F32)
           + route[:, ROUTE_G2:ROUTE_G2 + 1] * jnp.dot(pick2, yl, preferred_element_type=F32))
    out_ref[...] = _rmsnorm(x_ref[...] + moe, gf_ref[...])


def _moe_combine(meta, x2d, route, y, g_final):
    grid_spec = pltpu.PrefetchScalarGridSpec(
        num_scalar_prefetch=1,
        grid=(N_MOE_BLOCKS,),
        in_specs=[
            pl.BlockSpec((MOE_BLOCK, D_MODEL), lambda b, m: (b, 0)),
            pl.BlockSpec((ROUTE_ROWS, MOE_BLOCK), lambda b, m: (0, b)),
            pl.BlockSpec(memory_space=pl.ANY),
            pl.BlockSpec((1, D_MODEL), lambda b, m: (0, 0)),
        ],
        out_specs=pl.BlockSpec((MOE_BLOCK, D_MODEL), lambda b, m: (b, 0)),
        scratch_shapes=[
            pltpu.VMEM((2, MOE_LOCAL_ROWS, D_MODEL), BF16),
            pltpu.SemaphoreType.DMA((2,)),
        ],
    )
    return pl.pallas_call(
        _moe_combine_kernel,
        grid_spec=grid_spec,
        out_shape=jax.ShapeDtypeStruct((N_TOK, D_MODEL), F32),
        compiler_params=pltpu.CompilerParams(
            dimension_semantics=("arbitrary",), vmem_limit_bytes=VMEM_LIMIT),
        name="moe_combine",
    )(meta, x2d, route, y, g_final)


def _moe(x2d, g, w_router, w_gu, w_down, g_final):
    route, meta = _moe_route(x2d, g, w_router)
    block_tab, tile_tab, fill_tab = _moe_plan(meta)
    hs = _moe_dispatch(block_tab, tile_tab, fill_tab, x2d, g, route)
    y = _moe_experts(tile_tab, hs, w_gu, w_down)
    return _moe_combine(block_tab, x2d, route, y, g_final)


def _rope_tables():
    def tables(dim):
        inv_freq = ROPE_THETA ** (-jnp.arange(0, dim, 2, dtype=F32) / dim)
        ang = jnp.arange(SEQ, dtype=F32)[:, None] * inv_freq[None, :]
        return jnp.cos(ang), jnp.sin(ang)

    c, s = tables(HEAD_DIM)
    cq = jnp.concatenate([c, c], axis=1)
    sq = jnp.concatenate([-s, s], axis=1)
    c, s = tables(IDX_ROPE_DIM)
    ones = jnp.ones((SEQ, IDX_DIM - IDX_ROPE_DIM), F32)
    zeros = jnp.zeros_like(ones)
    z16 = jnp.zeros_like(s)
    ci = jnp.concatenate([c, c, ones], axis=1)
    sia = jnp.concatenate([-s, z16, zeros], axis=1)
    sib = jnp.concatenate([z16, s, zeros], axis=1)
    rep = LANES // IDX_DIM
    return cq, sq, jnp.tile(ci, (1, rep)), jnp.tile(sia, (1, rep)), jnp.tile(sib, (1, rep))


def kernel(x, norm_mix, norm_ffn, dsa_w_in, dsa_idx_k_gain, dsa_w_out, ffn_w_gu, ffn_w_down, gmlp_w_in,
           gmlp_ln_gain, gmlp_ln_bias, gmlp_w_spatial, gmlp_b_spatial, gmlp_w_out, moe_w_router, moe_w_gu,
           moe_w_down, final_norm):
    x2d = x.reshape(N_TOK, D_MODEL)
    row = lambda v: v.reshape(1, -1)

    w_in = dsa_w_in[0]
    w_all = jnp.pad(w_in, ((0, 0), (0, DSA_COLS - w_in.shape[1])))
    wwi_t = w_in[:, COL_KI + IDX_DIM:COL_KI + IDX_DIM + IDX_HEADS].T
    kig = jnp.pad(dsa_idx_k_gain[0], (0, LANES - IDX_DIM)).reshape(1, LANES)
    q3, k, vt, kia, kib, qi, wt = _dsa_proj(x2d, row(norm_mix[0]), w_all, wwi_t, kig, _rope_tables())
    o3 = _dsa_attn(q3, k, vt, kia, kib, qi, wt)
    x2d = _resid_proj(x2d, o3, dsa_w_out[0])
    x2d = _ffn(x2d, row(norm_ffn[0]), ffn_w_gu[0], ffn_w_down[0])

    gw = GMLP_WIDTH // GMLP_GROUPS
    b_sp_full = jnp.repeat(gmlp_b_spatial[0].T, gw, axis=1)
    x2d = _gmlp(x2d, row(norm_mix[1]), gmlp_w_in[0], row(gmlp_ln_gain[0]), row(gmlp_ln_bias[0]),
                gmlp_w_spatial[0], b_sp_full, gmlp_w_out[0])
    w_router = jnp.pad(moe_w_router[0], ((0, 0), (0, LANES - N_EXPERTS)))
    out = _moe(x2d, row(norm_ffn[1]), w_router, moe_w_gu[0], moe_w_down[0], row(final_norm))
    return out.reshape(BATCH, SEQ, D_MODEL)
```

```python
import functools

import numpy as np
import jax
import jax.numpy as jnp
from jax import lax
from jax.experimental import pallas as pl
from jax.experimental.pallas import tpu as pltpu

F32 = jnp.float32
BF16 = jnp.bfloat16

D_MODEL = 1024
BATCH = 8
SEQ = 2048
N_TOK = BATCH * SEQ
N_HEADS = 8
HEAD_DIM = 128
IDX_HEADS = 8
IDX_DIM = 64
IDX_ROPE_DIM = 32
INDEX_TOPK = 256
Q_BLOCK = 128
KEY_SUPER = 512
ROPE_THETA = 10000.0
GMLP_WIDTH = 1024
GMLP_GROUPS = 8
GMLP_CHUNK = 128
FFN_DIM = 3584
N_EXPERTS = 8
NORM_EPS = 1e-6

LANES = 128
SUBLANES = 8
VMEM_LIMIT = 56 * 1024 * 1024

DSA_COLS = 1920
COL_K = N_HEADS * HEAD_DIM
COL_V = COL_K + HEAD_DIM
COL_QI = COL_V + HEAD_DIM
COL_KI = COL_QI + IDX_HEADS * IDX_DIM

NT_DIMS = (((1,), (1,)), ((), ()))
MASK_BIAS = -1e30
SCORE_MASKED = -3.0e38


def _rmsnorm(x, g):
    ms = jnp.mean(x * x, axis=-1, keepdims=True)
    return x * lax.rsqrt(ms + NORM_EPS) * g


def _cast_once(src_ref, dst_ref):
    @pl.when(pl.program_id(0) == 0)
    def _():
        dst_ref[...] = src_ref[...].astype(BF16)


def _resident(shape):
    return pl.BlockSpec(shape, lambda i: (0,) * len(shape), pipeline_mode=pl.Buffered(1))


def _dsa_proj_kernel(x_ref, g_ref, w_ref, wwi_ref, kig_ref, cq_ref, sq_ref, ci_ref, sia_ref, sib_ref,
                     q_ref, k_ref, vt_ref, kia_ref, kib_ref, qi_ref, wt_ref, wbf_ref):
    _cast_once(w_ref, wbf_ref)
    h = _rmsnorm(x_ref[...], g_ref[...]).astype(BF16)
    yr = jnp.dot(h, wbf_ref[:, COL_K:], preferred_element_type=F32)
    yq = jnp.dot(h, wbf_ref[:, :COL_K], preferred_element_type=F32)
    cq = cq_ref[...]
    sq = sq_ref[...]
    ci = ci_ref[...]
    sia = sia_ref[...]
    sib = sib_ref[...]

    def rope_head(t):
        return t * cq + pltpu.roll(t, HEAD_DIM // 2, 1) * sq

    def rope_idx(t):
        half = IDX_ROPE_DIM // 2
        return t * ci + pltpu.roll(t, LANES - half, 1) * sia + pltpu.roll(t, half, 1) * sib

    q_scale = HEAD_DIM ** -0.5 * np.log2(np.e)
    k_ref[...] = rope_head(yr[:, 0:COL_V - COL_K]).astype(BF16)
    vt_ref[0] = yr[:, COL_V - COL_K:COL_QI - COL_K].T.astype(BF16)
    for p in range(IDX_HEADS * IDX_DIM // LANES):
        c0 = COL_QI - COL_K + p * LANES
        qi_ref[:, p * LANES:(p + 1) * LANES] = rope_idx(yr[:, c0:c0 + LANES]).astype(BF16)
    for hh in range(N_HEADS):
        q_ref[hh] = (rope_head(yq[:, hh * HEAD_DIM:(hh + 1) * HEAD_DIM]) * q_scale).astype(BF16)

    last = yr[:, COL_KI - COL_K:COL_KI - COL_K + LANES]
    lane = lax.broadcasted_iota(jnp.int32, last.shape, 1)
    kraw = jnp.where(lane < IDX_DIM, last, 0.0)
    kms = jnp.sum(kraw * kraw, axis=-1, keepdims=True) * (1.0 / IDX_DIM)
    kn = kraw * lax.rsqrt(kms + NORM_EPS) * kig_ref[...]
    kr = rope_idx(kn)
    kia_ref[...] = kr.astype(BF16)
    kib_ref[...] = pltpu.roll(kr, IDX_DIM, 1).astype(BF16)

    w_scale = IDX_HEADS ** -0.5 * IDX_DIM ** -0.5
    wt_ref[...] = lax.dot_general(wwi_ref[...].astype(BF16), h, NT_DIMS, preferred_element_type=F32) * w_scale


def _dsa_proj(x2d, g, w_all, wwi_t, kig, tabs, tm=KEY_SUPER):
    n = x2d.shape[0]
    nblk_seq = SEQ // tm
    tab_spec = pl.BlockSpec((tm, LANES), lambda i: (i % nblk_seq, 0))
    full = lambda shape: pl.BlockSpec(shape, lambda i: (0,) * len(shape))
    return pl.pallas_call(
        _dsa_proj_kernel,
        grid=(n // tm,),
        in_specs=[
            pl.BlockSpec((tm, D_MODEL), lambda i: (i, 0)),
            full((1, D_MODEL)),
            _resident((D_MODEL, DSA_COLS)),
            full((IDX_HEADS, D_MODEL)),
            full((1, LANES)),
            tab_spec, tab_spec, tab_spec, tab_spec, tab_spec,
        ],
        out_specs=[
            pl.BlockSpec((N_HEADS, tm, HEAD_DIM), lambda i: (0, i, 0)),
            pl.BlockSpec((tm, HEAD_DIM), lambda i: (i, 0)),
            pl.BlockSpec((1, HEAD_DIM, tm), lambda i: (i, 0, 0)),
            pl.BlockSpec((tm, LANES), lambda i: (i, 0)),
            pl.BlockSpec((tm, LANES), lambda i: (i, 0)),
            pl.BlockSpec((tm, IDX_HEADS * IDX_DIM), lambda i: (i, 0)),
            pl.BlockSpec((IDX_HEADS, tm), lambda i: (0, i)),
        ],
        out_shape=[
            jax.ShapeDtypeStruct((N_HEADS, n, HEAD_DIM), BF16),
            jax.ShapeDtypeStruct((n, HEAD_DIM), BF16),
            jax.ShapeDtypeStruct((n // tm, HEAD_DIM, tm), BF16),
            jax.ShapeDtypeStruct((n, LANES), BF16),
            jax.ShapeDtypeStruct((n, LANES), BF16),
            jax.ShapeDtypeStruct((n, IDX_HEADS * IDX_DIM), BF16),
            jax.ShapeDtypeStruct((IDX_HEADS, n), F32),
        ],
        scratch_shapes=[pltpu.VMEM((D_MODEL, DSA_COLS), BF16)],
        compiler_params=pltpu.CompilerParams(
            dimension_semantics=("arbitrary",), vmem_limit_bytes=VMEM_LIMIT),
        name="dsa_proj",
    )(x2d, g, w_all, wwi_t, kig, *tabs)


def _ordered_bits_to_float(u):
    key = u ^ jnp.int32(-2 ** 31)
    key = jnp.maximum(key, jnp.int32(-2139095041))
    bits = jnp.where(key >= 0, key, key ^ jnp.int32(0x7FFFFFFF))
    return lax.bitcast_convert_type(bits, F32)


def _dsa_attn_kernel(q_ref, k_ref, vt_ref, kia_ref, kib_ref, qi_ref, wt_ref, o_ref,
                     s_ref, sel_ref, m_ref, l_ref, p_ref, acc_ref):
    j = pl.program_id(1)
    per_super = KEY_SUPER // Q_BLOCK
    n_super = j // per_super + 1
    hq = N_HEADS * Q_BLOCK

    qpos = j * Q_BLOCK + lax.broadcasted_iota(jnp.int32, (Q_BLOCK, Q_BLOCK), 1)
    krow = lax.broadcasted_iota(jnp.int32, (Q_BLOCK, Q_BLOCK), 0)

    def super_start(sc):
        return sc * KEY_SUPER if isinstance(sc, int) else pl.multiple_of(sc * KEY_SUPER, KEY_SUPER)

    def rows_of(sc, c4):
        return pl.ds(super_start(sc) + c4 * Q_BLOCK, Q_BLOCK)

    def super_rows(sc):
        return pl.ds(super_start(sc), KEY_SUPER)

    def causal(sc, c4):
        return (krow + (sc * KEY_SUPER + c4 * Q_BLOCK)) <= qpos

    def fold8(t):
        return t.reshape(t.shape[0] // SUBLANES, SUBLANES, t.shape[1])

    def score_body(sc, _):
        qpps = [
            jnp.concatenate([qi_ref[:, (2 * half) * LANES:(2 * half + 1) * LANES],
                             qi_ref[:, (2 * half + 1) * LANES:(2 * half + 2) * LANES]], axis=0)
            for half in range(IDX_HEADS // 4)
        ]
        group = 2 * Q_BLOCK
        for kg in range(KEY_SUPER // group):
            rows = pl.ds(super_start(sc) + kg * group, group)
            ka = kia_ref[rows, :]
            kb = kib_ref[rows, :]
            acc = None
            for half in range(IDX_HEADS // 4):
                ra = lax.dot_general(ka, qpps[half], NT_DIMS, preferred_element_type=F32)
                rb = lax.dot_general(kb, qpps[half], NT_DIMS, preferred_element_type=F32)
                for sub in range(2):
                    p = 2 * half + sub
                    cols = slice(sub * Q_BLOCK, (sub + 1) * Q_BLOCK)
                    t2 = (jnp.maximum(ra[:, cols], 0.0) * wt_ref[2 * p:2 * p + 1, :]
                          + jnp.maximum(rb[:, cols], 0.0) * wt_ref[2 * p + 1:2 * p + 2, :])
                    acc = t2 if acc is None else acc + t2
            kpos = sc * KEY_SUPER + kg * group + lax.broadcasted_iota(jnp.int32, (group, Q_BLOCK), 0)
            qp_s = j * Q_BLOCK + lax.broadcasted_iota(jnp.int32, (group, Q_BLOCK), 1)
            s_ref[rows, :] = jnp.where(kpos <= qp_s, acc, SCORE_MASKED)
        return 0

    def scores(n_static):
        for sc in range(n_static):
            score_body(sc, 0)

    for n_static in range(1, SEQ // KEY_SUPER + 1):
        pl.when(n_super == n_static)(functools.partial(scores, n_static))

    def search(n_chunks):
        def count_ge(cand):
            accs = [jnp.zeros((SUBLANES, LANES), F32), jnp.zeros((SUBLANES, LANES), F32)]
            for c in range(n_chunks):
                ind = jnp.where(s_ref[pl.ds(c * Q_BLOCK, Q_BLOCK), :] >= cand, 1.0, 0.0)
                accs[c % 2] = accs[c % 2] + fold8(ind).sum(axis=0)
            return jnp.sum(accs[0] + accs[1], axis=0, keepdims=True)

        def bit_body(i, carry):
            prefix, cbest = carry
            cu = prefix | lax.shift_left(jnp.int32(1), 31 - i)
            cnt = count_ge(_ordered_bits_to_float(cu))
            ok = cnt >= float(INDEX_TOPK)
            return jnp.where(ok, cu, prefix), jnp.where(ok, cnt, cbest)

        prefix, cbest = lax.fori_loop(
            0, 32, bit_body,
            (jnp.zeros((1, LANES), jnp.int32), jnp.full((1, LANES), float(n_chunks * Q_BLOCK), F32)))
        sel_ref[0:1, :] = _ordered_bits_to_float(prefix)
        sel_ref[1:2, :] = cbest

    for n_chunks in range(1, SEQ // Q_BLOCK + 1):
        pl.when(j == n_chunks - 1)(functools.partial(search, n_chunks))
    tau = sel_ref[0:1, :]
    drop = sel_ref[1:2, :] - float(INDEX_TOPK)

    upper = jnp.where(krow < lax.broadcasted_iota(jnp.int32, (Q_BLOCK, Q_BLOCK), 1), 1.0, 0.0).astype(BF16)
    pair_w = 2 * Q_BLOCK

    def select_logit_chunk(sc, carry):
        later_ties, m_acc = carry
        biases = [None] * per_super
        for c4 in reversed(range(per_super)):
            s = s_ref[rows_of(sc, c4), :]
            eq = jnp.where(s == tau, 1.0, 0.0)
            ties_after = jnp.dot(upper, eq.astype(BF16), preferred_element_type=F32) + later_ties
            later_ties = later_ties + jnp.sum(eq, axis=0, keepdims=True)
            tie_bias = jnp.where(ties_after >= drop, 0.0, MASK_BIAS)
            bias = jnp.where(s > tau, 0.0, jnp.where(s == tau, tie_bias, MASK_BIAS))
            biases[c4] = jnp.where(causal(sc, c4), bias, MASK_BIAS)
        bias = jnp.concatenate(biases, axis=0)
        bias2 = jnp.concatenate([bias, bias], axis=1)
        srows = super_rows(sc)
        kc = k_ref[srows, :]
        parts = []
        for pr in range(N_HEADS // 2):
            qpair = q_ref[2 * pr:2 * pr + 2].reshape(pair_w, HEAD_DIM)
            lg = lax.dot_general(kc, qpair, NT_DIMS, preferred_element_type=F32) + bias2
            l_ref[srows, pr * pair_w:(pr + 1) * pair_w] = lg
            parts.append(fold8(lg).max(axis=0))
        return later_ties, jnp.maximum(m_acc, jnp.concatenate(parts, axis=1))

    def select_logits(n_static):
        carry = (jnp.zeros((1, LANES), F32), jnp.full((SUBLANES, hq), MASK_BIAS, F32))
        for sc in reversed(range(n_static)):
            carry = select_logit_chunk(sc, carry)
        m_ref[...] = carry[1]

    for n_static in range(1, SEQ // KEY_SUPER + 1):
        pl.when(n_super == n_static)(functools.partial(select_logits, n_static))
    m = jnp.max(m_ref[...], axis=0, keepdims=True)

    def prob_chunk(sc, d_acc):
        for c4 in range(per_super):
            rows = rows_of(sc, c4)
            parts = []
            for pr in range(N_HEADS // 2):
                cols = slice(pr * pair_w, (pr + 1) * pair_w)
                p = jnp.exp2(l_ref[rows, cols] - m[:, cols])
                p_ref[rows, cols] = p.astype(BF16)
                parts.append(fold8(p).sum(axis=0))
            d_acc = d_acc + jnp.concatenate(parts, axis=1)
        return d_acc

    def pv_chunk(sc):
        acc_ref[...] += jnp.dot(vt_ref[sc], p_ref[super_rows(sc), :], preferred_element_type=F32)

    def prob_pv_body(sc, d_acc):
        pv_chunk(sc - 1)
        return prob_chunk(sc, d_acc)

    acc_ref[...] = jnp.zeros_like(acc_ref)
    d_acc = prob_chunk(0, jnp.zeros((SUBLANES, hq), F32))
    d_acc = lax.fori_loop(1, n_super, prob_pv_body, d_acc)
    pv_chunk(n_super - 1)
    inv = 1.0 / jnp.sum(d_acc, axis=0, keepdims=True)
    for h in range(N_HEADS):
        cols = slice(h * Q_BLOCK, (h + 1) * Q_BLOCK)
        o_ref[h] = (acc_ref[:, cols] * inv[:, cols]).T.astype(BF16)


def _dsa_attn(q3, k, vt, kia, kib, qi, wt):
    nqb = SEQ // Q_BLOCK
    n_sup = SEQ // KEY_SUPER
    hq = N_HEADS * Q_BLOCK
    return pl.pallas_call(
        _dsa_attn_kernel,
        grid=(BATCH, nqb),
        in_specs=[
            pl.BlockSpec((N_HEADS, Q_BLOCK, HEAD_DIM), lambda b, j: (0, b * nqb + j, 0)),
            pl.BlockSpec((SEQ, HEAD_DIM), lambda b, j: (b, 0)),
            pl.BlockSpec((n_sup, HEAD_DIM, KEY_SUPER), lambda b, j: (b, 0, 0)),
            pl.BlockSpec((SEQ, LANES), lambda b, j: (b, 0)),
            pl.BlockSpec((SEQ, LANES), lambda b, j: (b, 0)),
            pl.BlockSpec((Q_BLOCK, IDX_HEADS * IDX_DIM), lambda b, j: (b * nqb + j, 0)),
            pl.BlockSpec((IDX_HEADS, Q_BLOCK), lambda b, j: (0, b * nqb + j)),
        ],
        out_specs=pl.BlockSpec((N_HEADS, Q_BLOCK, HEAD_DIM), lambda b, j: (0, b * nqb + j, 0)),
        out_shape=jax.ShapeDtypeStruct((N_HEADS, N_TOK, HEAD_DIM), BF16),
        scratch_shapes=[
            pltpu.VMEM((SEQ, Q_BLOCK), F32),
            pltpu.VMEM((SUBLANES, Q_BLOCK), F32),
            pltpu.VMEM((SUBLANES, hq), F32),
            pltpu.VMEM((SEQ, hq), F32),
            pltpu.VMEM((SEQ, hq), BF16),
            pltpu.VMEM((HEAD_DIM, hq), F32),
        ],
        compiler_params=pltpu.CompilerParams(
            dimension_semantics=("parallel", "arbitrary"), vmem_limit_bytes=VMEM_LIMIT),
        name="dsa_attn",
    )(q3, k, vt, kia, kib, qi, wt)


def _resid_proj_kernel(x_ref, o_ref, w_ref, out_ref, wbf_ref):
    _cast_once(w_ref, wbf_ref)
    o = jnp.concatenate([o_ref[h] for h in range(N_HEADS)], axis=1)
    out_ref[...] = x_ref[...] + jnp.dot(o, wbf_ref[...], preferred_element_type=F32)


def _resid_proj(x2d, o3, w_out, tm=1024):
    n = x2d.shape[0]
    return pl.pallas_call(
        _resid_proj_kernel,
        grid=(n // tm,),
        in_specs=[
            pl.BlockSpec((tm, D_MODEL), lambda i: (i, 0)),
            pl.BlockSpec((N_HEADS, tm, HEAD_DIM), lambda i: (0, i, 0)),
            _resident((D_MODEL, D_MODEL)),
        ],
        out_specs=pl.BlockSpec((tm, D_MODEL), lambda i: (i, 0)),
        out_shape=jax.ShapeDtypeStruct((n, D_MODEL), F32),
        scratch_shapes=[pltpu.VMEM((D_MODEL, D_MODEL), BF16)],
        compiler_params=pltpu.CompilerParams(
            dimension_semantics=("arbitrary",), vmem_limit_bytes=VMEM_LIMIT),
        name="dsa_out_proj",
    )(x2d, o3, w_out)


def _ffn_kernel(x_ref, g_ref, wg_ref, wu_ref, wd_ref, out_ref, h_ref, acc_ref):
    c = pl.program_id(1)

    @pl.when(c == 0)
    def _():
        h_ref[...] = _rmsnorm(x_ref[...], g_ref[...]).astype(BF16)
        acc_ref[...] = jnp.zeros_like(acc_ref)

    h = h_ref[...]
    gate = jnp.dot(h, wg_ref[...].astype(BF16), preferred_element_type=F32)
    up = jnp.dot(h, wu_ref[...].astype(BF16), preferred_element_type=F32)
    a = (gate * jax.nn.sigmoid(gate) * up).astype(BF16)
    acc_ref[...] += jnp.dot(a, wd_ref[...].astype(BF16), preferred_element_type=F32)

    @pl.when(c == pl.num_programs(1) - 1)
    def _():
        out_ref[...] = x_ref[...] + acc_ref[...]


def _ffn(x2d, g, w_gu, w_down, tm=1024, fc=512):
    n = x2d.shape[0]
    nc = FFN_DIM // fc
    return pl.pallas_call(
        _ffn_kernel,
        grid=(n // tm, nc),
        in_specs=[
            pl.BlockSpec((tm, D_MODEL), lambda i, c: (i, 0)),
            pl.BlockSpec((1, D_MODEL), lambda i, c: (0, 0)),
            pl.BlockSpec((D_MODEL, fc), lambda i, c: (0, c)),
            pl.BlockSpec((D_MODEL, fc), lambda i, c: (0, c + nc)),
            pl.BlockSpec((fc, D_MODEL), lambda i, c: (c, 0)),
        ],
        out_specs=pl.BlockSpec((tm, D_MODEL), lambda i, c: (i, 0)),
        out_shape=jax.ShapeDtypeStruct((n, D_MODEL), F32),
        scratch_shapes=[pltpu.VMEM((tm, D_MODEL), BF16), pltpu.VMEM((tm, D_MODEL), F32)],
        compiler_params=pltpu.CompilerParams(
            dimension_semantics=("parallel", "arbitrary"), vmem_limit_bytes=VMEM_LIMIT),
        name="swiglu_ffn",
    )(x2d, g, w_gu, w_gu, w_down)


def _gelu_tanh(x):
    return 0.5 * x * (1.0 + jnp.tanh(np.sqrt(2.0 / np.pi) * (x + 0.044715 * (x * x * x))))


def _gmlp_kernel(x_ref, g_ref, win_ref, lng_ref, lnb_ref, ws_ref, bs_ref, wout_ref, out_ref,
                 winbf_ref, woutbf_ref):
    _cast_once(win_ref, winbf_ref)
    _cast_once(wout_ref, woutbf_ref)
    x = x_ref[...]
    tm = x.shape[0]
    h = _rmsnorm(x, g_ref[...]).astype(BF16)
    v = _gelu_tanh(jnp.dot(h, winbf_ref[:, GMLP_WIDTH:], preferred_element_type=F32))
    u = _gelu_tanh(jnp.dot(h, winbf_ref[:, :GMLP_WIDTH], preferred_element_type=F32))
    mu = jnp.mean(v, axis=-1, keepdims=True)
    vc = v - mu
    var = jnp.mean(vc * vc, axis=-1, keepdims=True)
    vb = (vc * lax.rsqrt(var + NORM_EPS) * lng_ref[...] + lnb_ref[...]).astype(BF16)

    gw = GMLP_WIDTH // GMLP_GROUPS
    row = lax.broadcasted_iota(jnp.int32, (GMLP_CHUNK, GMLP_CHUNK), 0)
    col = lax.broadcasted_iota(jnp.int32, (GMLP_CHUNK, GMLP_CHUNK), 1)
    w_tril = [jnp.where(row >= col, ws_ref[gi], 0.0).astype(BF16) for gi in range(GMLP_GROUPS)]
    rows = []
    for c in range(tm // GMLP_CHUNK):
        parts = [
            jnp.dot(w_tril[gi], vb[c * GMLP_CHUNK:(c + 1) * GMLP_CHUNK, gi * gw:(gi + 1) * gw],
                    preferred_element_type=F32)
            for gi in range(GMLP_GROUPS)
        ]
        rows.append(jnp.concatenate(parts, axis=1) + bs_ref[...])
    s = jnp.concatenate(rows, axis=0)
    z = (u * s).astype(BF16)
    out_ref[...] = x + jnp.dot(z, woutbf_ref[...], preferred_element_type=F32)


def _gmlp(x2d, g, w_in, ln_g, ln_b, w_sp, b_sp_full, w_out, tm=512):
    n = x2d.shape[0]
    full = lambda shape: pl.BlockSpec(shape, lambda i: (0,) * len(shape))
    return pl.pallas_call(
        _gmlp_kernel,
        grid=(n // tm,),
        in_specs=[
            pl.BlockSpec((tm, D_MODEL), lambda i: (i, 0)),
            full((1, D_MODEL)),
            _resident((D_MODEL, 2 * GMLP_WIDTH)),
            full((1, GMLP_WIDTH)),
            full((1, GMLP_WIDTH)),
            full((GMLP_GROUPS, GMLP_CHUNK, GMLP_CHUNK)),
            full((GMLP_CHUNK, GMLP_WIDTH)),
            _resident((GMLP_WIDTH, D_MODEL)),
        ],
        out_specs=pl.BlockSpec((tm, D_MODEL), lambda i: (i, 0)),
        out_shape=jax.ShapeDtypeStruct((n, D_MODEL), F32),
        scratch_shapes=[pltpu.VMEM((D_MODEL, 2 * GMLP_WIDTH), BF16), pltpu.VMEM((GMLP_WIDTH, D_MODEL), BF16)],
        compiler_params=pltpu.CompilerParams(
            dimension_semantics=("arbitrary",), vmem_limit_bytes=VMEM_LIMIT),
        name="gmlp",
    )(x2d, g, w_in, ln_g, ln_b, w_sp, b_sp_full, w_out)


MOE_BLOCK = 512
MOE_GRAN = 16
MOE_TILE = 1024
MOE_LOCAL_ROWS = 2 * MOE_BLOCK + N_EXPERTS * MOE_GRAN
N_MOE_BLOCKS = N_TOK // MOE_BLOCK
MOE_MAX_TILES = (2 * N_TOK + N_MOE_BLOCKS * N_EXPERTS * (MOE_GRAN - 1)) // MOE_TILE + N_EXPERTS
MOE_ROWS = MOE_MAX_TILES * MOE_TILE
ROUTE_D1, ROUTE_D2, ROUTE_G1, ROUTE_G2 = 0, 1, 2, 3
ROUTE_ROWS = N_EXPERTS
TAB_LOFF, TAB_PC, TAB_GOFF = 0, N_EXPERTS, 2 * N_EXPERTS


def _moe_route_kernel(x_ref, g_ref, wr_ref, route_ref, meta_ref):
    b = pl.program_id(0)
    t_blk = MOE_BLOCK
    hf = _rmsnorm(x_ref[...], g_ref[...])
    wr = wr_ref[...]
    h_hi = hf.astype(BF16)
    h_lo = (hf - h_hi.astype(F32)).astype(BF16)
    w_hi = wr.astype(BF16)
    w_lo = (wr - w_hi.astype(F32)).astype(BF16)
    logits = (jnp.dot(h_hi, w_hi, preferred_element_type=F32) + jnp.dot(h_hi, w_lo, preferred_element_type=F32)
              + jnp.dot(h_lo, w_hi, preferred_element_type=F32))

    lg = logits.T[0:N_EXPERTS, :]
    e_idx = lax.broadcasted_iota(jnp.int32, lg.shape, 0).astype(F32)
    neg_inf = -jnp.inf
    m1 = jnp.max(lg, axis=0, keepdims=True)
    i1 = jnp.min(jnp.where(lg == m1, e_idx, float(N_EXPERTS)), axis=0, keepdims=True)
    lg2 = jnp.where(e_idx == i1, neg_inf, lg)
    m2 = jnp.max(lg2, axis=0, keepdims=True)
    i2 = jnp.min(jnp.where(lg2 == m2, e_idx, float(N_EXPERTS)), axis=0, keepdims=True)
    t = jnp.exp(m2 - m1)
    g1 = 1.0 / (1.0 + t)
    g2 = t * g1

    memb = jnp.where(jnp.logical_or(e_idx == i1, e_idx == i2), 1.0, 0.0)
    tr = lax.broadcasted_iota(jnp.int32, (t_blk, t_blk), 0)
    tc = lax.broadcasted_iota(jnp.int32, (t_blk, t_blk), 1)
    earlier = jnp.where(tr < tc, 1.0, 0.0).astype(BF16)
    rank = jnp.dot(memb.astype(BF16), earlier, preferred_element_type=F32)
    loff_col = jnp.zeros((N_EXPERTS, 1), F32)
    e_col = lax.broadcasted_iota(jnp.int32, (N_EXPERTS, 1), 0)
    lo = jnp.int32(0)
    for e in range(N_EXPERTS):
        cnt = jnp.sum(memb[e:e + 1, :]).astype(jnp.int32)
        pce = (cnt + (MOE_GRAN - 1)) // MOE_GRAN * MOE_GRAN
        meta_ref[b, TAB_LOFF + e] = lo
        meta_ref[b, TAB_PC + e] = pce
        loff_col = jnp.where(e_col == e, lo.astype(F32), loff_col)
        lo = lo + pce
    dest = loff_col + rank
    d1 = jnp.sum(jnp.where(e_idx == i1, dest, 0.0), axis=0, keepdims=True)
    d2 = jnp.sum(jnp.where(e_idx == i2, dest, 0.0), axis=0, keepdims=True)
    r_idx = lax.broadcasted_iota(jnp.int32, lg.shape, 0)
    route_ref[...] = jnp.where(r_idx == ROUTE_D1, d1,
                               jnp.where(r_idx == ROUTE_D2, d2,
                                         jnp.where(r_idx == ROUTE_G1, g1, jnp.where(r_idx == ROUTE_G2, g2, 0.0))))


def _moe_route(x2d, g, w_router):
    return pl.pallas_call(
        _moe_route_kernel,
        grid=(N_MOE_BLOCKS,),
        in_specs=[
            pl.BlockSpec((MOE_BLOCK, D_MODEL), lambda b: (b, 0)),
            pl.BlockSpec((1, D_MODEL), lambda b: (0, 0)),
            pl.BlockSpec((D_MODEL, LANES), lambda b: (0, 0)),
        ],
        out_specs=[
            pl.BlockSpec((ROUTE_ROWS, MOE_BLOCK), lambda b: (0, b)),
            pl.BlockSpec(memory_space=pltpu.SMEM),
        ],
        out_shape=[
            jax.ShapeDtypeStruct((ROUTE_ROWS, N_TOK), F32),
            jax.ShapeDtypeStruct((N_MOE_BLOCKS, 2 * N_EXPERTS), jnp.int32),
        ],
        compiler_params=pltpu.CompilerParams(
            dimension_semantics=("arbitrary",), vmem_limit_bytes=VMEM_LIMIT),
        name="moe_route",
    )(x2d, g, w_router)


def _moe_plan(meta):
    pc = meta[:, TAB_PC:TAB_PC + N_EXPERTS]
    totals = jnp.sum(pc, axis=0)
    per_expert = (totals + (MOE_TILE - 1)) // MOE_TILE
    e_lt = jnp.arange(N_EXPERTS)[:, None] < jnp.arange(N_EXPERTS)[None, :]
    first_tile = jnp.sum(jnp.where(e_lt, per_expert[:, None], 0), axis=0)
    ends = first_tile + per_expert
    seg_start = first_tile * MOE_TILE
    b_lt = jnp.arange(N_MOE_BLOCKS)[:, None] < jnp.arange(N_MOE_BLOCKS)[None, :]
    goff = seg_start[None, :] + jnp.sum(jnp.where(b_lt[:, :, None], pc[:, None, :], 0), axis=0)
    block_tab = jnp.concatenate([meta, goff], axis=1).astype(jnp.int32)
    n_tiles = jnp.sum(per_expert)
    t = jnp.minimum(jnp.arange(MOE_MAX_TILES, dtype=jnp.int32), n_tiles - 1)
    mine = jnp.logical_and(t[:, None] >= first_tile[None, :], t[:, None] < ends[None, :])
    expert = jnp.sum(jnp.where(mine, jnp.arange(N_EXPERTS)[None, :], 0), axis=1)
    left = totals[None, :] - (t[:, None] - first_tile[None, :]) * MOE_TILE
    used = jnp.sum(jnp.where(mine, jnp.clip(left, 0, MOE_TILE), 0), axis=1)
    tile_tab = jnp.stack([expert, jnp.full_like(t, n_tiles), used]).astype(jnp.int32)
    fill_tab = jnp.stack([seg_start + totals, ends * MOE_TILE]).astype(jnp.int32)
    return block_tab, tile_tab, fill_tab


def _moe_dispatch_kernel(tab_ref, tile_ref, fill_ref, x_ref, g_ref, route_ref, hs_hbm, hloc_ref, zero_ref, sem):
    b = pl.program_id(0)
    slot = b % 2
    hloc = hloc_ref.at[slot]
    hb = _rmsnorm(x_ref[...], g_ref[...]).astype(BF16)

    route_t = route_ref[...]
    r_iota = lax.broadcasted_iota(jnp.int32, (MOE_LOCAL_ROWS, MOE_BLOCK), 0).astype(F32)
    hit = jnp.logical_or(r_iota == route_t[ROUTE_D1:ROUTE_D1 + 1, :], r_iota == route_t[ROUTE_D2:ROUTE_D2 + 1, :])
    perm = jnp.where(hit, 1.0, 0.0).astype(BF16)
    hloc_ref[slot] = jnp.dot(perm, hb, preferred_element_type=F32).astype(BF16)

    def granule(ref, row):
        return ref.at[pl.ds(pl.multiple_of(row, MOE_GRAN), MOE_GRAN)]

    for e in range(N_EXPERTS):
        lo = tab_ref[b, TAB_LOFF + e]
        goff = tab_ref[b, TAB_GOFF + e]

        def start(gi, _, lo=lo, goff=goff):
            pltpu.make_async_copy(granule(hloc, lo + gi * MOE_GRAN),
                                  granule(hs_hbm, goff + gi * MOE_GRAN), sem.at[slot]).start()
            return 0

        lax.fori_loop(0, tab_ref[b, TAB_PC + e] // MOE_GRAN, start, 0)

    def wait_block(blk, slot_):
        n = 0
        for e in range(N_EXPERTS):
            n = n + tab_ref[blk, TAB_PC + e] // MOE_GRAN

        def wait_one(gi, _):
            pltpu.make_async_copy(granule(hloc_ref.at[slot_], 0), granule(hs_hbm, 0), sem.at[slot_]).wait()
            return 0

        lax.fori_loop(0, n, wait_one, 0)

    @pl.when(b > 0)
    def _():
        wait_block(jnp.maximum(b - 1, 0), 1 - slot)

    @pl.when(b == pl.num_programs(0) - 1)
    def _():
        wait_block(b, slot)
        zero_ref[...] = jnp.zeros_like(zero_ref)
        n_fill = 0
        for e in range(N_EXPERTS):
            first = fill_ref[0, e]
            n_e = (fill_ref[1, e] - first) // MOE_GRAN

            def fill(gi, _, first=first):
                pltpu.make_async_copy(granule(zero_ref, 0), granule(hs_hbm, first + gi * MOE_GRAN),
                                      sem.at[slot]).start()
                return 0

            lax.fori_loop(0, n_e, fill, 0)
            n_fill = n_fill + n_e

        def wait_fill(gi, _):
            pltpu.make_async_copy(granule(zero_ref, 0), granule(hs_hbm, 0), sem.at[slot]).wait()
            return 0

        lax.fori_loop(0, n_fill, wait_fill, 0)

        def tile_at(ti):
            return hs_hbm.at[pl.ds(pl.multiple_of(ti * MOE_TILE, MOE_TILE), MOE_TILE)]

        n_tiles = tile_ref[1, 0]

        def fill_tile(ti, _):
            pltpu.make_async_copy(zero_ref, tile_at(ti), sem.at[slot]).start()
            return 0

        def wait_tile(ti, _):
            pltpu.make_async_copy(zero_ref, tile_at(0), sem.at[slot]).wait()
            return 0

        lax.fori_loop(n_tiles, MOE_MAX_TILES, fill_tile, 0)
        lax.fori_loop(n_tiles, MOE_MAX_TILES, wait_tile, 0)


def _moe_dispatch(block_tab, tile_tab, fill_tab, x2d, g, route):
    grid_spec = pltpu.PrefetchScalarGridSpec(
        num_scalar_prefetch=3,
        grid=(N_MOE_BLOCKS,),
        in_specs=[
            pl.BlockSpec((MOE_BLOCK, D_MODEL), lambda b, *_: (b, 0)),
            pl.BlockSpec((1, D_MODEL), lambda b, *_: (0, 0)),
            pl.BlockSpec((ROUTE_ROWS, MOE_BLOCK), lambda b, *_: (0, b)),
        ],
        out_specs=pl.BlockSpec(memory_space=pl.ANY),
        scratch_shapes=[
            pltpu.VMEM((2, MOE_LOCAL_ROWS, D_MODEL), BF16),
            pltpu.VMEM((MOE_TILE, D_MODEL), BF16),
            pltpu.SemaphoreType.DMA((2,)),
        ],
    )
    return pl.pallas_call(
        _moe_dispatch_kernel,
        grid_spec=grid_spec,
        out_shape=jax.ShapeDtypeStruct((MOE_ROWS, D_MODEL), BF16),
        compiler_params=pltpu.CompilerParams(
            dimension_semantics=("arbitrary",), vmem_limit_bytes=VMEM_LIMIT),
        name="moe_dispatch",
    )(block_tab, tile_tab, fill_tab, x2d, g, route)


def _moe_expert_kernel(tile_ref, hs_ref, wg_ref, wu_ref, wd_ref, y_ref, acc_ref):
    t = pl.program_id(0)
    c = pl.program_id(1)
    active = t < tile_ref[1, 0]

    @pl.when(jnp.logical_and(jnp.logical_not(active), c == 0))
    def _():
        y_ref[...] = jnp.zeros_like(y_ref)

    @pl.when(active)
    def _():
        @pl.when(c == 0)
        def _():
            acc_ref[...] = jnp.zeros_like(acc_ref)

        def swiglu_rows(n_rows):
            h = hs_ref[0:n_rows, :]
            gate = jnp.dot(h, wg_ref[0].astype(BF16), preferred_element_type=F32)
            up = jnp.dot(h, wu_ref[0].astype(BF16), preferred_element_type=F32)
            a = (gate * jax.nn.sigmoid(gate) * up).astype(BF16)
            acc_ref[0:n_rows, :] += jnp.dot(a, wd_ref[0].astype(BF16), preferred_element_type=F32)

        half_only = tile_ref[2, t] <= MOE_TILE // 2

        @pl.when(half_only)
        def _():
            swiglu_rows(MOE_TILE // 2)

        @pl.when(jnp.logical_not(half_only))
        def _():
            swiglu_rows(MOE_TILE)

        @pl.when(c == pl.num_programs(1) - 1)
        def _():
            y_ref[...] = acc_ref[...].astype(BF16)


def _moe_experts(tile_tab, hs, w_gu, w_down, fc=512):
    nc = FFN_DIM // fc

    def tile_of(t, tr):
        return jnp.maximum(jnp.minimum(t, tr[1, 0] - 1), 0)

    def chunk_of(t, c, tr):
        return jnp.where(t < tr[1, 0], c, nc - 1)

    grid_spec = pltpu.PrefetchScalarGridSpec(
        num_scalar_prefetch=1,
        grid=(MOE_MAX_TILES, nc),
        in_specs=[
            pl.BlockSpec((MOE_TILE, D_MODEL), lambda t, c, tr: (tile_of(t, tr), 0)),
            pl.BlockSpec((1, D_MODEL, fc), lambda t, c, tr: (tr[0, t], 0, chunk_of(t, c, tr))),
            pl.BlockSpec((1, D_MODEL, fc), lambda t, c, tr: (tr[0, t], 0, chunk_of(t, c, tr) + nc)),
            pl.BlockSpec((1, fc, D_MODEL), lambda t, c, tr: (tr[0, t], chunk_of(t, c, tr), 0)),
        ],
        out_specs=pl.BlockSpec((MOE_TILE, D_MODEL), lambda t, c, tr: (t, 0)),
        scratch_shapes=[pltpu.VMEM((MOE_TILE, D_MODEL), F32)],
    )
    return pl.pallas_call(
        _moe_expert_kernel,
        grid_spec=grid_spec,
        out_shape=jax.ShapeDtypeStruct((MOE_ROWS, D_MODEL), BF16),
        compiler_params=pltpu.CompilerParams(
            dimension_semantics=("arbitrary", "arbitrary"), vmem_limit_bytes=VMEM_LIMIT),
        name="moe_experts",
    )(tile_tab, hs, w_gu, w_gu, w_down)


def _moe_combine_kernel(meta_ref, x_ref, route_ref, y_hbm, gf_ref, out_ref, yloc_ref, sem):
    b = pl.program_id(0)
    slot = b % 2

    def granule(ref, row):
        return ref.at[pl.ds(pl.multiple_of(row, MOE_GRAN), MOE_GRAN)]

    def start_block(blk, slot_):
        for e in range(N_EXPERTS):
            lo = meta_ref[blk, TAB_LOFF + e]
            goff = meta_ref[blk, TAB_GOFF + e]

            def start(gi, _, lo=lo, goff=goff):
                pltpu.make_async_copy(granule(y_hbm, goff + gi * MOE_GRAN),
                                      granule(yloc_ref.at[slot_], lo + gi * MOE_GRAN), sem.at[slot_]).start()
                return 0

            lax.fori_loop(0, meta_ref[blk, TAB_PC + e] // MOE_GRAN, start, 0)

    @pl.when(b == 0)
    def _():
        yloc_ref[...] = jnp.zeros_like(yloc_ref)
        start_block(0, 0)

    @pl.when(b + 1 < pl.num_programs(0))
    def _():
        start_block(jnp.minimum(b + 1, pl.num_programs(0) - 1), 1 - slot)

    n_mine = 0
    for e in range(N_EXPERTS):
        n_mine = n_mine + meta_ref[b, TAB_PC + e] // MOE_GRAN

    def wait_one(gi, _):
        pltpu.make_async_copy(granule(y_hbm, 0), granule(yloc_ref.at[slot], 0), sem.at[slot]).wait()
        return 0

    lax.fori_loop(0, n_mine, wait_one, 0)

    route_t = route_ref[...]
    route = route_t.T
    col = lax.broadcasted_iota(jnp.int32, (MOE_BLOCK, MOE_LOCAL_ROWS), 1).astype(F32)
    pick1 = jnp.where(col == route[:, ROUTE_D1:ROUTE_D1 + 1], 1.0, 0.0).astype(BF16)
    pick2 = jnp.where(col == route[:, ROUTE_D2:ROUTE_D2 + 1], 1.0, 0.0).astype(BF16)
    yl = yloc_ref[slot]
    moe = (route[:, ROUTE_G1:ROUTE_G1 + 1] * jnp.dot(pick1, yl, preferred_element_type=F32)
           + route[:, ROUTE_G2:ROUTE_G2 + 1] * jnp.dot(pick2, yl, preferred_element_type=F32))
    out_ref[...] = _rmsnorm(x_ref[...] + moe, gf_ref[...])


def _moe_combine(meta, x2d, route, y, g_final):
    grid_spec = pltpu.PrefetchScalarGridSpec(
        num_scalar_prefetch=1,
        grid=(N_MOE_BLOCKS,),
        in_specs=[
            pl.BlockSpec((MOE_BLOCK, D_MODEL), lambda b, m: (b, 0)),
            pl.BlockSpec((ROUTE_ROWS, MOE_BLOCK), lambda b, m: (0, b)),
            pl.BlockSpec(memory_space=pl.ANY),
            pl.BlockSpec((1, D_MODEL), lambda b, m: (0, 0)),
        ],
        out_specs=pl.BlockSpec((MOE_BLOCK, D_MODEL), lambda b, m: (b, 0)),
        scratch_shapes=[
            pltpu.VMEM((2, MOE_LOCAL_ROWS, D_MODEL), BF16),
            pltpu.SemaphoreType.DMA((2,)),
        ],
    )
    return pl.pallas_call(
        _moe_combine_kernel,
        grid_spec=grid_spec,
        out_shape=jax.ShapeDtypeStruct((N_TOK, D_MODEL), F32),
        compiler_params=pltpu.CompilerParams(
            dimension_semantics=("arbitrary",), vmem_limit_bytes=VMEM_LIMIT),
        name="moe_combine",
    )(meta, x2d, route, y, g_final)


def _moe(x2d, g, w_router, w_gu, w_down, g_final):
    route, meta = _moe_route(x2d, g, w_router)
    block_tab, tile_tab, fill_tab = _moe_plan(meta)
    hs = _moe_dispatch(block_tab, tile_tab, fill_tab, x2d, g, route)
    y = _moe_experts(tile_tab, hs, w_gu, w_down)
    return _moe_combine(block_tab, x2d, route, y, g_final)


def _rope_tables():
    def tables(dim):
        inv_freq = ROPE_THETA ** (-jnp.arange(0, dim, 2, dtype=F32) / dim)
        ang = jnp.arange(SEQ, dtype=F32)[:, None] * inv_freq[None, :]
        return jnp.cos(ang), jnp.sin(ang)

    c, s = tables(HEAD_DIM)
    cq = jnp.concatenate([c, c], axis=1)
    sq = jnp.concatenate([-s, s], axis=1)
    c, s = tables(IDX_ROPE_DIM)
    ones = jnp.ones((SEQ, IDX_DIM - IDX_ROPE_DIM), F32)
    zeros = jnp.zeros_like(ones)
    z16 = jnp.zeros_like(s)
    ci = jnp.concatenate([c, c, ones], axis=1)
    sia = jnp.concatenate([-s, z16, zeros], axis=1)
    sib = jnp.concatenate([z16, s, zeros], axis=1)
    rep = LANES // IDX_DIM
    return cq, sq, jnp.tile(ci, (1, rep)), jnp.tile(sia, (1, rep)), jnp.tile(sib, (1, rep))


def kernel(x, norm_mix, norm_ffn, dsa_w_in, dsa_idx_k_gain, dsa_w_out, ffn_w_gu, ffn_w_down, gmlp_w_in,
           gmlp_ln_gain, gmlp_ln_bias, gmlp_w_spatial, gmlp_b_spatial, gmlp_w_out, moe_w_router, moe_w_gu,
           moe_w_down, final_norm):
    x2d = x.reshape(N_TOK, D_MODEL)
    row = lambda v: v.reshape(1, -1)

    w_in = dsa_w_in[0]
    w_all = jnp.pad(w_in, ((0, 0), (0, DSA_COLS - w_in.shape[1])))
    wwi_t = w_in[:, COL_KI + IDX_DIM:COL_KI + IDX_DIM + IDX_HEADS].T
    kig = jnp.pad(dsa_idx_k_gain[0], (0, LANES - IDX_DIM)).reshape(1, LANES)
    q3, k, vt, kia, kib, qi, wt = _dsa_proj(x2d, row(norm_mix[0]), w_all, wwi_t, kig, _rope_tables())
    o3 = _dsa_attn(q3, k, vt, kia, kib, qi, wt)
    x2d = _resid_proj(x2d, o3, dsa_w_out[0])
    x2d = _ffn(x2d, row(norm_ffn[0]), ffn_w_gu[0], ffn_w_down[0])

    gw = GMLP_WIDTH // GMLP_GROUPS
    b_sp_full = jnp.repeat(gmlp_b_spatial[0].T, gw, axis=1)
    x2d = _gmlp(x2d, row(norm_mix[1]), gmlp_w_in[0], row(gmlp_ln_gain[0]), row(gmlp_ln_bias[0]),
                gmlp_w_spatial[0], b_sp_full, gmlp_w_out[0])
    w_router = jnp.pad(moe_w_router[0], ((0, 0), (0, LANES - N_EXPERTS)))
    out = _moe(x2d, row(norm_ffn[1]), w_router, moe_w_gu[0], moe_w_down[0], row(final_norm))
    return out.reshape(BATCH, SEQ, D_MODEL)
```

```python
import functools

import numpy as np
import jax
import jax.numpy as jnp
from jax import lax
from jax.experimental import pallas as pl
from jax.experimental.pallas import tpu as pltpu

F32 = jnp.float32
BF16 = jnp.bfloat16

D_MODEL = 1024
BATCH = 8
SEQ = 2048
N_TOK = BATCH * SEQ
N_HEADS = 8
HEAD_DIM = 128
IDX_HEADS = 8
IDX_DIM = 64
IDX_ROPE_DIM = 32
INDEX_TOPK = 256
Q_BLOCK = 128
KEY_SUPER = 512
ROPE_THETA = 10000.0
GMLP_WIDTH = 1024
GMLP_GROUPS = 8
GMLP_CHUNK = 128
FFN_DIM = 3584
N_EXPERTS = 8
NORM_EPS = 1e-6

LANES = 128
SUBLANES = 8
VMEM_LIMIT = 56 * 1024 * 1024

DSA_COLS = 1920
COL_K = N_HEADS * HEAD_DIM
COL_V = COL_K + HEAD_DIM
COL_QI = COL_V + HEAD_DIM
COL_KI = COL_QI + IDX_HEADS * IDX_DIM

NT_DIMS = (((1,), (1,)), ((), ()))
MASK_BIAS = -1e30
SCORE_MASKED = -3.0e38


def _rmsnorm(x, g):
    ms = jnp.mean(x * x, axis=-1, keepdims=True)
    return x * lax.rsqrt(ms + NORM_EPS) * g


def _cast_once(src_ref, dst_ref):
    @pl.when(pl.program_id(0) == 0)
    def _():
        dst_ref[...] = src_ref[...].astype(BF16)


def _resident(shape):
    return pl.BlockSpec(shape, lambda i: (0,) * len(shape), pipeline_mode=pl.Buffered(1))


def _dsa_proj_kernel(x_ref, g_ref, w_ref, wwi_ref, kig_ref, cq_ref, sq_ref, ci_ref, sia_ref, sib_ref,
                     q_ref, k_ref, vt_ref, kia_ref, kib_ref, qi_ref, wt_ref, wbf_ref):
    _cast_once(w_ref, wbf_ref)
    h = _rmsnorm(x_ref[...], g_ref[...]).astype(BF16)
    yr = jnp.dot(h, wbf_ref[:, COL_K:], preferred_element_type=F32)
    yq = jnp.dot(h, wbf_ref[:, :COL_K], preferred_element_type=F32)
    cq = cq_ref[...]
    sq = sq_ref[...]
    ci = ci_ref[...]
    sia = sia_ref[...]
    sib = sib_ref[...]

    def rope_head(t):
        return t * cq + pltpu.roll(t, HEAD_DIM // 2, 1) * sq

    def rope_idx(t):
        half = IDX_ROPE_DIM // 2
        return t * ci + pltpu.roll(t, LANES - half, 1) * sia + pltpu.roll(t, half, 1) * sib

    q_scale = HEAD_DIM ** -0.5 * np.log2(np.e)
    k_ref[...] = rope_head(yr[:, 0:COL_V - COL_K]).astype(BF16)
    vt_ref[0] = yr[:, COL_V - COL_K:COL_QI - COL_K].T.astype(BF16)
    for p in range(IDX_HEADS * IDX_DIM // LANES):
        c0 = COL_QI - COL_K + p * LANES
        qi_ref[:, p * LANES:(p + 1) * LANES] = rope_idx(yr[:, c0:c0 + LANES]).astype(BF16)
    for hh in range(N_HEADS):
        q_ref[hh] = (rope_head(yq[:, hh * HEAD_DIM:(hh + 1) * HEAD_DIM]) * q_scale).astype(BF16)

    last = yr[:, COL_KI - COL_K:COL_KI - COL_K + LANES]
    lane = lax.broadcasted_iota(jnp.int32, last.shape, 1)
    kraw = jnp.where(lane < IDX_DIM, last, 0.0)
    kms = jnp.sum(kraw * kraw, axis=-1, keepdims=True) * (1.0 / IDX_DIM)
    kn = kraw * lax.rsqrt(kms + NORM_EPS) * kig_ref[...]
    kr = rope_idx(kn)
    kia_ref[...] = kr.astype(BF16)
    kib_ref[...] = pltpu.roll(kr, IDX_DIM, 1).astype(BF16)

    w_scale = IDX_HEADS ** -0.5 * IDX_DIM ** -0.5
    wt_ref[...] = lax.dot_general(wwi_ref[...].astype(BF16), h, NT_DIMS, preferred_element_type=F32) * w_scale


def _dsa_proj(x2d, g, w_all, wwi_t, kig, tabs, tm=KEY_SUPER):
    n = x2d.shape[0]
    nblk_seq = SEQ // tm
    tab_spec = pl.BlockSpec((tm, LANES), lambda i: (i % nblk_seq, 0))
    full = lambda shape: pl.BlockSpec(shape, lambda i: (0,) * len(shape))
    return pl.pallas_call(
        _dsa_proj_kernel,
        grid=(n // tm,),
        in_specs=[
            pl.BlockSpec((tm, D_MODEL), lambda i: (i, 0)),
            full((1, D_MODEL)),
            _resident((D_MODEL, DSA_COLS)),
            full((IDX_HEADS, D_MODEL)),
            full((1, LANES)),
            tab_spec, tab_spec, tab_spec, tab_spec, tab_spec,
        ],
        out_specs=[
            pl.BlockSpec((N_HEADS, tm, HEAD_DIM), lambda i: (0, i, 0)),
            pl.BlockSpec((tm, HEAD_DIM), lambda i: (i, 0)),
            pl.BlockSpec((1, HEAD_DIM, tm), lambda i: (i, 0, 0)),
            pl.BlockSpec((tm, LANES), lambda i: (i, 0)),
            pl.BlockSpec((tm, LANES), lambda i: (i, 0)),
            pl.BlockSpec((tm, IDX_HEADS * IDX_DIM), lambda i: (i, 0)),
            pl.BlockSpec((IDX_HEADS, tm), lambda i: (0, i)),
        ],
        out_shape=[
            jax.ShapeDtypeStruct((N_HEADS, n, HEAD_DIM), BF16),
            jax.ShapeDtypeStruct((n, HEAD_DIM), BF16),
            jax.ShapeDtypeStruct((n // tm, HEAD_DIM, tm), BF16),
            jax.ShapeDtypeStruct((n, LANES), BF16),
            jax.ShapeDtypeStruct((n, LANES), BF16),
            jax.ShapeDtypeStruct((n, IDX_HEADS * IDX_DIM), BF16),
            jax.ShapeDtypeStruct((IDX_HEADS, n), F32),
        ],
        scratch_shapes=[pltpu.VMEM((D_MODEL, DSA_COLS), BF16)],
        compiler_params=pltpu.CompilerParams(
            dimension_semantics=("arbitrary",), vmem_limit_bytes=VMEM_LIMIT),
        name="dsa_proj",
    )(x2d, g, w_all, wwi_t, kig, *tabs)


def _ordered_bits_to_float(u):
    key = u ^ jnp.int32(-2 ** 31)
    key = jnp.maximum(key, jnp.int32(-2139095041))
    bits = jnp.where(key >= 0, key, key ^ jnp.int32(0x7FFFFFFF))
    return lax.bitcast_convert_type(bits, F32)


def _dsa_attn_kernel(q_ref, k_ref, vt_ref, kia_ref, kib_ref, qi_ref, wt_ref, o_ref,
                     s_ref, sel_ref, m_ref, l_ref, p_ref, acc_ref):
    j = pl.program_id(1)
    per_super = KEY_SUPER // Q_BLOCK
    n_super = j // per_super + 1
    hq = N_HEADS * Q_BLOCK

    qpos = j * Q_BLOCK + lax.broadcasted_iota(jnp.int32, (Q_BLOCK, Q_BLOCK), 1)
    krow = lax.broadcasted_iota(jnp.int32, (Q_BLOCK, Q_BLOCK), 0)

    def super_start(sc):
        return sc * KEY_SUPER if isinstance(sc, int) else pl.multiple_of(sc * KEY_SUPER, KEY_SUPER)

    def rows_of(sc, c4):
        return pl.ds(super_start(sc) + c4 * Q_BLOCK, Q_BLOCK)

    def super_rows(sc):
        return pl.ds(super_start(sc), KEY_SUPER)

    def causal(sc, c4):
        return (krow + (sc * KEY_SUPER + c4 * Q_BLOCK)) <= qpos

    def fold8(t):
        return t.reshape(t.shape[0] // SUBLANES, SUBLANES, t.shape[1])

    def score_body(sc, _):
        qpps = [
            jnp.concatenate([qi_ref[:, (2 * half) * LANES:(2 * half + 1) * LANES],
                             qi_ref[:, (2 * half + 1) * LANES:(2 * half + 2) * LANES]], axis=0)
            for half in range(IDX_HEADS // 4)
        ]
        group = 2 * Q_BLOCK
        for kg in range(KEY_SUPER // group):
            rows = pl.ds(super_start(sc) + kg * group, group)
            ka = kia_ref[rows, :]
            kb = kib_ref[rows, :]
            acc = None
            for half in range(IDX_HEADS // 4):
                ra = lax.dot_general(ka, qpps[half], NT_DIMS, preferred_element_type=F32)
                rb = lax.dot_general(kb, qpps[half], NT_DIMS, preferred_element_type=F32)
                for sub in range(2):
                    p = 2 * half + sub
                    cols = slice(sub * Q_BLOCK, (sub + 1) * Q_BLOCK)
                    t2 = (jnp.maximum(ra[:, cols], 0.0) * wt_ref[2 * p:2 * p + 1, :]
                          + jnp.maximum(rb[:, cols], 0.0) * wt_ref[2 * p + 1:2 * p + 2, :])
                    acc = t2 if acc is None else acc + t2
            kpos = sc * KEY_SUPER + kg * group + lax.broadcasted_iota(jnp.int32, (group, Q_BLOCK), 0)
            qp_s = j * Q_BLOCK + lax.broadcasted_iota(jnp.int32, (group, Q_BLOCK), 1)
            s_ref[rows, :] = jnp.where(kpos <= qp_s, acc, SCORE_MASKED)
        return 0

    def scores(n_static):
        for sc in range(n_static):
            score_body(sc, 0)

    for n_static in range(1, SEQ // KEY_SUPER + 1):
        pl.when(n_super == n_static)(functools.partial(scores, n_static))

    def search(n_chunks):
        def count_ge(cand):
            accs = [jnp.zeros((SUBLANES, LANES), F32), jnp.zeros((SUBLANES, LANES), F32)]
            for c in range(n_chunks):
                ind = jnp.where(s_ref[pl.ds(c * Q_BLOCK, Q_BLOCK), :] >= cand, 1.0, 0.0)
                accs[c % 2] = accs[c % 2] + fold8(ind).sum(axis=0)
            return jnp.sum(accs[0] + accs[1], axis=0, keepdims=True)

        def bit_body(i, carry):
            prefix, cbest = carry
            cu = prefix | lax.shift_left(jnp.int32(1), 31 - i)
            cnt = count_ge(_ordered_bits_to_float(cu))
            ok = cnt >= float(INDEX_TOPK)
            return jnp.where(ok, cu, prefix), jnp.where(ok, cnt, cbest)

        prefix, cbest = lax.fori_loop(
            0, 32, bit_body,
            (jnp.zeros((1, LANES), jnp.int32), jnp.full((1, LANES), float(n_chunks * Q_BLOCK), F32)))
        sel_ref[0:1, :] = _ordered_bits_to_float(prefix)
        sel_ref[1:2, :] = cbest

    for n_chunks in range(1, SEQ // Q_BLOCK + 1):
        pl.when(j == n_chunks - 1)(functools.partial(search, n_chunks))
    tau = sel_ref[0:1, :]
    drop = sel_ref[1:2, :] - float(INDEX_TOPK)

    upper = jnp.where(krow < lax.broadcasted_iota(jnp.int32, (Q_BLOCK, Q_BLOCK), 1), 1.0, 0.0).astype(BF16)
    pair_w = 2 * Q_BLOCK

    def select_logit_chunk(sc, carry):
        later_ties, m_acc = carry
        biases = [None] * per_super
        for c4 in reversed(range(per_super)):
            s = s_ref[rows_of(sc, c4), :]
            eq = jnp.where(s == tau, 1.0, 0.0)
            ties_after = jnp.dot(upper, eq.astype(BF16), preferred_element_type=F32) + later_ties
            later_ties = later_ties + jnp.sum(eq, axis=0, keepdims=True)
            tie_bias = jnp.where(ties_after >= drop, 0.0, MASK_BIAS)
            bias = jnp.where(s > tau, 0.0, jnp.where(s == tau, tie_bias, MASK_BIAS))
            biases[c4] = jnp.where(causal(sc, c4), bias, MASK_BIAS)
        bias = jnp.concatenate(biases, axis=0)
        bias2 = jnp.concatenate([bias, bias], axis=1)
        srows = super_rows(sc)
        kc = k_ref[srows, :]
        parts = []
        for pr in range(N_HEADS // 2):
            qpair = q_ref[2 * pr:2 * pr + 2].reshape(pair_w, HEAD_DIM)
            lg = lax.dot_general(kc, qpair, NT_DIMS, preferred_element_type=F32) + bias2
            l_ref[srows, pr * pair_w:(pr + 1) * pair_w] = lg
            parts.append(fold8(lg).max(axis=0))
        return later_ties, jnp.maximum(m_acc, jnp.concatenate(parts, axis=1))

    def select_logits(n_static):
        carry = (jnp.zeros((1, LANES), F32), jnp.full((SUBLANES, hq), MASK_BIAS, F32))
        for sc in reversed(range(n_static)):
            carry = select_logit_chunk(sc, carry)
        m_ref[...] = carry[1]

    for n_static in range(1, SEQ // KEY_SUPER + 1):
        pl.when(n_super == n_static)(functools.partial(select_logits, n_static))
    m = jnp.max(m_ref[...], axis=0, keepdims=True)

    def prob_chunk(sc, d_acc):
        for c4 in range(per_super):
            rows = rows_of(sc, c4)
            parts = []
            for pr in range(N_HEADS // 2):
                cols = slice(pr * pair_w, (pr + 1) * pair_w)
                p = jnp.exp2(l_ref[rows, cols] - m[:, cols])
                p_ref[rows, cols] = p.astype(BF16)
                parts.append(fold8(p).sum(axis=0))
            d_acc = d_acc + jnp.concatenate(parts, axis=1)
        return d_acc

    def pv_chunk(sc):
        acc_ref[...] += jnp.dot(vt_ref[sc], p_ref[super_rows(sc), :], preferred_element_type=F32)

    def prob_pv(n_static):
        acc_ref[...] = jnp.zeros_like(acc_ref)
        d_acc = prob_chunk(0, jnp.zeros((SUBLANES, hq), F32))
        for sc in range(1, n_static):
            pv_chunk(sc - 1)
            d_acc = prob_chunk(sc, d_acc)
        pv_chunk(n_static - 1)
        inv = 1.0 / jnp.sum(d_acc, axis=0, keepdims=True)
        for h in range(N_HEADS):
            cols = slice(h * Q_BLOCK, (h + 1) * Q_BLOCK)
            o_ref[h] = (acc_ref[:, cols] * inv[:, cols]).T.astype(BF16)

    for n_static in range(1, SEQ // KEY_SUPER + 1):
        pl.when(n_super == n_static)(functools.partial(prob_pv, n_static))


def _dsa_attn(q3, k, vt, kia, kib, qi, wt):
    nqb = SEQ // Q_BLOCK
    n_sup = SEQ // KEY_SUPER
    hq = N_HEADS * Q_BLOCK
    return pl.pallas_call(
        _dsa_attn_kernel,
        grid=(BATCH, nqb),
        in_specs=[
            pl.BlockSpec((N_HEADS, Q_BLOCK, HEAD_DIM), lambda b, j: (0, b * nqb + j, 0)),
            pl.BlockSpec((SEQ, HEAD_DIM), lambda b, j: (b, 0)),
            pl.BlockSpec((n_sup, HEAD_DIM, KEY_SUPER), lambda b, j: (b, 0, 0)),
            pl.BlockSpec((SEQ, LANES), lambda b, j: (b, 0)),
            pl.BlockSpec((SEQ, LANES), lambda b, j: (b, 0)),
            pl.BlockSpec((Q_BLOCK, IDX_HEADS * IDX_DIM), lambda b, j: (b * nqb + j, 0)),
            pl.BlockSpec((IDX_HEADS, Q_BLOCK), lambda b, j: (0, b * nqb + j)),
        ],
        out_specs=pl.BlockSpec((N_HEADS, Q_BLOCK, HEAD_DIM), lambda b, j: (0, b * nqb + j, 0)),
        out_shape=jax.ShapeDtypeStruct((N_HEADS, N_TOK, HEAD_DIM), BF16),
        scratch_shapes=[
            pltpu.VMEM((SEQ, Q_BLOCK), F32),
            pltpu.VMEM((SUBLANES, Q_BLOCK), F32),
            pltpu.VMEM((SUBLANES, hq), F32),
            pltpu.VMEM((SEQ, hq), F32),
            pltpu.VMEM((SEQ, hq), BF16),
            pltpu.VMEM((HEAD_DIM, hq), F32),
        ],
        compiler_params=pltpu.CompilerParams(
            dimension_semantics=("parallel", "arbitrary"), vmem_limit_bytes=VMEM_LIMIT),
        name="dsa_attn",
    )(q3, k, vt, kia, kib, qi, wt)


def _resid_proj_kernel(x_ref, o_ref, w_ref, out_ref, wbf_ref):
    _cast_once(w_ref, wbf_ref)
    o = jnp.concatenate([o_ref[h] for h in range(N_HEADS)], axis=1)
    out_ref[...] = x_ref[...] + jnp.dot(o, wbf_ref[...], preferred_element_type=F32)


def _resid_proj(x2d, o3, w_out, tm=1024):
    n = x2d.shape[0]
    return pl.pallas_call(
        _resid_proj_kernel,
        grid=(n // tm,),
        in_specs=[
            pl.BlockSpec((tm, D_MODEL), lambda i: (i, 0)),
            pl.BlockSpec((N_HEADS, tm, HEAD_DIM), lambda i: (0, i, 0)),
            _resident((D_MODEL, D_MODEL)),
        ],
        out_specs=pl.BlockSpec((tm, D_MODEL), lambda i: (i, 0)),
        out_shape=jax.ShapeDtypeStruct((n, D_MODEL), F32),
        scratch_shapes=[pltpu.VMEM((D_MODEL, D_MODEL), BF16)],
        compiler_params=pltpu.CompilerParams(
            dimension_semantics=("arbitrary",), vmem_limit_bytes=VMEM_LIMIT),
        name="dsa_out_proj",
    )(x2d, o3, w_out)


def _ffn_kernel(x_ref, g_ref, wg_ref, wu_ref, wd_ref, out_ref, h_ref, acc_ref):
    c = pl.program_id(1)

    @pl.when(c == 0)
    def _():
        h_ref[...] = _rmsnorm(x_ref[...], g_ref[...]).astype(BF16)
        acc_ref[...] = jnp.zeros_like(acc_ref)

    h = h_ref[...]
    gate = jnp.dot(h, wg_ref[...].astype(BF16), preferred_element_type=F32)
    up = jnp.dot(h, wu_ref[...].astype(BF16), preferred_element_type=F32)
    a = (gate * jax.nn.sigmoid(gate) * up).astype(BF16)
    acc_ref[...] += jnp.dot(a, wd_ref[...].astype(BF16), preferred_element_type=F32)

    @pl.when(c == pl.num_programs(1) - 1)
    def _():
        out_ref[...] = x_ref[...] + acc_ref[...]


def _ffn(x2d, g, w_gu, w_down, tm=1024, fc=512):
    n = x2d.shape[0]
    nc = FFN_DIM // fc
    return pl.pallas_call(
        _ffn_kernel,
        grid=(n // tm, nc),
        in_specs=[
            pl.BlockSpec((tm, D_MODEL), lambda i, c: (i, 0)),
            pl.BlockSpec((1, D_MODEL), lambda i, c: (0, 0)),
            pl.BlockSpec((D_MODEL, fc), lambda i, c: (0, c)),
            pl.BlockSpec((D_MODEL, fc), lambda i, c: (0, c + nc)),
            pl.BlockSpec((fc, D_MODEL), lambda i, c: (c, 0)),
        ],
        out_specs=pl.BlockSpec((tm, D_MODEL), lambda i, c: (i, 0)),
        out_shape=jax.ShapeDtypeStruct((n, D_MODEL), F32),
        scratch_shapes=[pltpu.VMEM((tm, D_MODEL), BF16), pltpu.VMEM((tm, D_MODEL), F32)],
        compiler_params=pltpu.CompilerParams(
            dimension_semantics=("parallel", "arbitrary"), vmem_limit_bytes=VMEM_LIMIT),
        name="swiglu_ffn",
    )(x2d, g, w_gu, w_gu, w_down)


def _gelu_tanh(x):
    return 0.5 * x * (1.0 + jnp.tanh(np.sqrt(2.0 / np.pi) * (x + 0.044715 * (x * x * x))))


def _gmlp_kernel(x_ref, g_ref, win_ref, lng_ref, lnb_ref, ws_ref, bs_ref, wout_ref, out_ref,
                 winbf_ref, woutbf_ref):
    _cast_once(win_ref, winbf_ref)
    _cast_once(wout_ref, woutbf_ref)
    x = x_ref[...]
    tm = x.shape[0]
    h = _rmsnorm(x, g_ref[...]).astype(BF16)
    v = _gelu_tanh(jnp.dot(h, winbf_ref[:, GMLP_WIDTH:], preferred_element_type=F32))
    u = _gelu_tanh(jnp.dot(h, winbf_ref[:, :GMLP_WIDTH], preferred_element_type=F32))
    mu = jnp.mean(v, axis=-1, keepdims=True)
    vc = v - mu
    var = jnp.mean(vc * vc, axis=-1, keepdims=True)
    vb = (vc * lax.rsqrt(var + NORM_EPS) * lng_ref[...] + lnb_ref[...]).astype(BF16)

    gw = GMLP_WIDTH // GMLP_GROUPS
    row = lax.broadcasted_iota(jnp.int32, (GMLP_CHUNK, GMLP_CHUNK), 0)
    col = lax.broadcasted_iota(jnp.int32, (GMLP_CHUNK, GMLP_CHUNK), 1)
    w_tril = [jnp.where(row >= col, ws_ref[gi], 0.0).astype(BF16) for gi in range(GMLP_GROUPS)]
    rows = []
    for c in range(tm // GMLP_CHUNK):
        parts = [
            jnp.dot(w_tril[gi], vb[c * GMLP_CHUNK:(c + 1) * GMLP_CHUNK, gi * gw:(gi + 1) * gw],
                    preferred_element_type=F32)
            for gi in range(GMLP_GROUPS)
        ]
        rows.append(jnp.concatenate(parts, axis=1) + bs_ref[...])
    s = jnp.concatenate(rows, axis=0)
    z = (u * s).astype(BF16)
    out_ref[...] = x + jnp.dot(z, woutbf_ref[...], preferred_element_type=F32)


def _gmlp(x2d, g, w_in, ln_g, ln_b, w_sp, b_sp_full, w_out, tm=512):
    n = x2d.shape[0]
    full = lambda shape: pl.BlockSpec(shape, lambda i: (0,) * len(shape))
    return pl.pallas_call(
        _gmlp_kernel,
        grid=(n // tm,),
        in_specs=[
            pl.BlockSpec((tm, D_MODEL), lambda i: (i, 0)),
            full((1, D_MODEL)),
            _resident((D_MODEL, 2 * GMLP_WIDTH)),
            full((1, GMLP_WIDTH)),
            full((1, GMLP_WIDTH)),
            full((GMLP_GROUPS, GMLP_CHUNK, GMLP_CHUNK)),
            full((GMLP_CHUNK, GMLP_WIDTH)),
            _resident((GMLP_WIDTH, D_MODEL)),
        ],
        out_specs=pl.BlockSpec((tm, D_MODEL), lambda i: (i, 0)),
        out_shape=jax.ShapeDtypeStruct((n, D_MODEL), F32),
        scratch_shapes=[pltpu.VMEM((D_MODEL, 2 * GMLP_WIDTH), BF16), pltpu.VMEM((GMLP_WIDTH, D_MODEL), BF16)],
        compiler_params=pltpu.CompilerParams(
            dimension_semantics=("arbitrary",), vmem_limit_bytes=VMEM_LIMIT),
        name="gmlp",
    )(x2d, g, w_in, ln_g, ln_b, w_sp, b_sp_full, w_out)


MOE_BLOCK = 512
MOE_GRAN = 16
MOE_TILE = 1024
MOE_LOCAL_ROWS = 2 * MOE_BLOCK + N_EXPERTS * MOE_GRAN
N_MOE_BLOCKS = N_TOK // MOE_BLOCK
MOE_MAX_TILES = (2 * N_TOK + N_MOE_BLOCKS * N_EXPERTS * (MOE_GRAN - 1)) // MOE_TILE + N_EXPERTS
MOE_ROWS = MOE_MAX_TILES * MOE_TILE
ROUTE_D1, ROUTE_D2, ROUTE_G1, ROUTE_G2 = 0, 1, 2, 3
ROUTE_ROWS = N_EXPERTS
TAB_LOFF, TAB_PC, TAB_GOFF = 0, N_EXPERTS, 2 * N_EXPERTS


def _moe_route_kernel(x_ref, g_ref, wr_ref, route_ref, meta_ref):
    b = pl.program_id(0)
    t_blk = MOE_BLOCK
    hf = _rmsnorm(x_ref[...], g_ref[...])
    wr = wr_ref[...]
    h_hi = hf.astype(BF16)
    h_lo = (hf - h_hi.astype(F32)).astype(BF16)
    w_hi = wr.astype(BF16)
    w_lo = (wr - w_hi.astype(F32)).astype(BF16)
    logits = (jnp.dot(h_hi, w_hi, preferred_element_type=F32) + jnp.dot(h_hi, w_lo, preferred_element_type=F32)
              + jnp.dot(h_lo, w_hi, preferred_element_type=F32))

    lg = logits.T[0:N_EXPERTS, :]
    e_idx = lax.broadcasted_iota(jnp.int32, lg.shape, 0).astype(F32)
    neg_inf = -jnp.inf
    m1 = jnp.max(lg, axis=0, keepdims=True)
    i1 = jnp.min(jnp.where(lg == m1, e_idx, float(N_EXPERTS)), axis=0, keepdims=True)
    lg2 = jnp.where(e_idx == i1, neg_inf, lg)
    m2 = jnp.max(lg2, axis=0, keepdims=True)
    i2 = jnp.min(jnp.where(lg2 == m2, e_idx, float(N_EXPERTS)), axis=0, keepdims=True)
    t = jnp.exp(m2 - m1)
    g1 = 1.0 / (1.0 + t)
    g2 = t * g1

    memb = jnp.where(jnp.logical_or(e_idx == i1, e_idx == i2), 1.0, 0.0)
    tr = lax.broadcasted_iota(jnp.int32, (t_blk, t_blk), 0)
    tc = lax.broadcasted_iota(jnp.int32, (t_blk, t_blk), 1)
    earlier = jnp.where(tr < tc, 1.0, 0.0).astype(BF16)
    rank = jnp.dot(memb.astype(BF16), earlier, preferred_element_type=F32)
    loff_col = jnp.zeros((N_EXPERTS, 1), F32)
    e_col = lax.broadcasted_iota(jnp.int32, (N_EXPERTS, 1), 0)
    lo = jnp.int32(0)
    for e in range(N_EXPERTS):
        cnt = jnp.sum(memb[e:e + 1, :]).astype(jnp.int32)
        pce = (cnt + (MOE_GRAN - 1)) // MOE_GRAN * MOE_GRAN
        meta_ref[b, TAB_LOFF + e] = lo
        meta_ref[b, TAB_PC + e] = pce
        loff_col = jnp.where(e_col == e, lo.astype(F32), loff_col)
        lo = lo + pce
    dest = loff_col + rank
    d1 = jnp.sum(jnp.where(e_idx == i1, dest, 0.0), axis=0, keepdims=True)
    d2 = jnp.sum(jnp.where(e_idx == i2, dest, 0.0), axis=0, keepdims=True)
    r_idx = lax.broadcasted_iota(jnp.int32, lg.shape, 0)
    route_ref[...] = jnp.where(r_idx == ROUTE_D1, d1,
                               jnp.where(r_idx == ROUTE_D2, d2,
                                         jnp.where(r_idx == ROUTE_G1, g1, jnp.where(r_idx == ROUTE_G2, g2, 0.0))))


def _moe_route(x2d, g, w_router):
    return pl.pallas_call(
        _moe_route_kernel,
        grid=(N_MOE_BLOCKS,),
        in_specs=[
            pl.BlockSpec((MOE_BLOCK, D_MODEL), lambda b: (b, 0)),
            pl.BlockSpec((1, D_MODEL), lambda b: (0, 0)),
            pl.BlockSpec((D_MODEL, LANES), lambda b: (0, 0)),
        ],
        out_specs=[
            pl.BlockSpec((ROUTE_ROWS, MOE_BLOCK), lambda b: (0, b)),
            pl.BlockSpec(memory_space=pltpu.SMEM),
        ],
        out_shape=[
            jax.ShapeDtypeStruct((ROUTE_ROWS, N_TOK), F32),
            jax.ShapeDtypeStruct((N_MOE_BLOCKS, 2 * N_EXPERTS), jnp.int32),
        ],
        compiler_params=pltpu.CompilerParams(
            dimension_semantics=("arbitrary",), vmem_limit_bytes=VMEM_LIMIT),
        name="moe_route",
    )(x2d, g, w_router)


def _moe_plan(meta):
    pc = meta[:, TAB_PC:TAB_PC + N_EXPERTS]
    totals = jnp.sum(pc, axis=0)
    per_expert = (totals + (MOE_TILE - 1)) // MOE_TILE
    e_lt = jnp.arange(N_EXPERTS)[:, None] < jnp.arange(N_EXPERTS)[None, :]
    first_tile = jnp.sum(jnp.where(e_lt, per_expert[:, None], 0), axis=0)
    ends = first_tile + per_expert
    seg_start = first_tile * MOE_TILE
    b_lt = jnp.arange(N_MOE_BLOCKS)[:, None] < jnp.arange(N_MOE_BLOCKS)[None, :]
    goff = seg_start[None, :] + jnp.sum(jnp.where(b_lt[:, :, None], pc[:, None, :], 0), axis=0)
    block_tab = jnp.concatenate([meta, goff], axis=1).astype(jnp.int32)
    n_tiles = jnp.sum(per_expert)
    t = jnp.minimum(jnp.arange(MOE_MAX_TILES, dtype=jnp.int32), n_tiles - 1)
    mine = jnp.logical_and(t[:, None] >= first_tile[None, :], t[:, None] < ends[None, :])
    expert = jnp.sum(jnp.where(mine, jnp.arange(N_EXPERTS)[None, :], 0), axis=1)
    left = totals[None, :] - (t[:, None] - first_tile[None, :]) * MOE_TILE
    used = jnp.sum(jnp.where(mine, jnp.clip(left, 0, MOE_TILE), 0), axis=1)
    tile_tab = jnp.stack([expert, jnp.full_like(t, n_tiles), used]).astype(jnp.int32)
    fill_tab = jnp.stack([seg_start + totals, ends * MOE_TILE]).astype(jnp.int32)
    return block_tab, tile_tab, fill_tab


def _moe_dispatch_kernel(tab_ref, tile_ref, fill_ref, x_ref, g_ref, route_ref, hs_hbm, hloc_ref, zero_ref, sem):
    b = pl.program_id(0)
    slot = b % 2
    hloc = hloc_ref.at[slot]
    hb = _rmsnorm(x_ref[...], g_ref[...]).astype(BF16)

    route_t = route_ref[...]
    r_iota = lax.broadcasted_iota(jnp.int32, (MOE_LOCAL_ROWS, MOE_BLOCK), 0).astype(F32)
    hit = jnp.logical_or(r_iota == route_t[ROUTE_D1:ROUTE_D1 + 1, :], r_iota == route_t[ROUTE_D2:ROUTE_D2 + 1, :])
    perm = jnp.where(hit, 1.0, 0.0).astype(BF16)
    hloc_ref[slot] = jnp.dot(perm, hb, preferred_element_type=F32).astype(BF16)

    def granule(ref, row):
        return ref.at[pl.ds(pl.multiple_of(row, MOE_GRAN), MOE_GRAN)]

    for e in range(N_EXPERTS):
        lo = tab_ref[b, TAB_LOFF + e]
        goff = tab_ref[b, TAB_GOFF + e]

        def start(gi, _, lo=lo, goff=goff):
            pltpu.make_async_copy(granule(hloc, lo + gi * MOE_GRAN),
                                  granule(hs_hbm, goff + gi * MOE_GRAN), sem.at[slot]).start()
            return 0

        lax.fori_loop(0, tab_ref[b, TAB_PC + e] // MOE_GRAN, start, 0)

    def wait_block(blk, slot_):
        n = 0
        for e in range(N_EXPERTS):
            n = n + tab_ref[blk, TAB_PC + e] // MOE_GRAN

        def wait_one(gi, _):
            pltpu.make_async_copy(granule(hloc_ref.at[slot_], 0), granule(hs_hbm, 0), sem.at[slot_]).wait()
            return 0

        lax.fori_loop(0, n, wait_one, 0)

    @pl.when(b > 0)
    def _():
        wait_block(jnp.maximum(b - 1, 0), 1 - slot)

    @pl.when(b == pl.num_programs(0) - 1)
    def _():
        wait_block(b, slot)
        zero_ref[...] = jnp.zeros_like(zero_ref)
        n_fill = 0
        for e in range(N_EXPERTS):
            first = fill_ref[0, e]
            n_e = (fill_ref[1, e] - first) // MOE_GRAN

            def fill(gi, _, first=first):
                pltpu.make_async_copy(granule(zero_ref, 0), granule(hs_hbm, first + gi * MOE_GRAN),
                                      sem.at[slot]).start()
                return 0

            lax.fori_loop(0, n_e, fill, 0)
            n_fill = n_fill + n_e

        def wait_fill(gi, _):
            pltpu.make_async_copy(granule(zero_ref, 0), granule(hs_hbm, 0), sem.at[slot]).wait()
            return 0

        lax.fori_loop(0, n_fill, wait_fill, 0)

        def tile_at(ti):
            return hs_hbm.at[pl.ds(pl.multiple_of(ti * MOE_TILE, MOE_TILE), MOE_TILE)]

        n_tiles = tile_ref[1, 0]

        def fill_tile(ti, _):
            pltpu.make_async_copy(zero_ref, tile_at(ti), sem.at[slot]).start()
            return 0

        def wait_tile(ti, _):
            pltpu.make_async_copy(zero_ref, tile_at(0), sem.at[slot]).wait()
            return 0

        lax.fori_loop(n_tiles, MOE_MAX_TILES, fill_tile, 0)
        lax.fori_loop(n_tiles, MOE_MAX_TILES, wait_tile, 0)


def _moe_dispatch(block_tab, tile_tab, fill_tab, x2d, g, route):
    grid_spec = pltpu.PrefetchScalarGridSpec(
        num_scalar_prefetch=3,
        grid=(N_MOE_BLOCKS,),
        in_specs=[
            pl.BlockSpec((MOE_BLOCK, D_MODEL), lambda b, *_: (b, 0)),
            pl.BlockSpec((1, D_MODEL), lambda b, *_: (0, 0)),
            pl.BlockSpec((ROUTE_ROWS, MOE_BLOCK), lambda b, *_: (0, b)),
        ],
        out_specs=pl.BlockSpec(memory_space=pl.ANY),
        scratch_shapes=[
            pltpu.VMEM((2, MOE_LOCAL_ROWS, D_MODEL), BF16),
            pltpu.VMEM((MOE_TILE, D_MODEL), BF16),
            pltpu.SemaphoreType.DMA((2,)),
        ],
    )
    return pl.pallas_call(
        _moe_dispatch_kernel,
        grid_spec=grid_spec,
        out_shape=jax.ShapeDtypeStruct((MOE_ROWS, D_MODEL), BF16),
        compiler_params=pltpu.CompilerParams(
            dimension_semantics=("arbitrary",), vmem_limit_bytes=VMEM_LIMIT),
        name="moe_dispatch",
    )(block_tab, tile_tab, fill_tab, x2d, g, route)


def _moe_expert_kernel(tile_ref, hs_ref, wg_ref, wu_ref, wd_ref, y_ref, acc_ref):
    t = pl.program_id(0)
    c = pl.program_id(1)
    active = t < tile_ref[1, 0]

    @pl.when(jnp.logical_and(jnp.logical_not(active), c == 0))
    def _():
        y_ref[...] = jnp.zeros_like(y_ref)

    @pl.when(active)
    def _():
        @pl.when(c == 0)
        def _():
            acc_ref[...] = jnp.zeros_like(acc_ref)

        def swiglu_rows(n_rows):
            h = hs_ref[0:n_rows, :]
            gate = jnp.dot(h, wg_ref[0].astype(BF16), preferred_element_type=F32)
            up = jnp.dot(h, wu_ref[0].astype(BF16), preferred_element_type=F32)
            a = (gate * jax.nn.sigmoid(gate) * up).astype(BF16)
            acc_ref[0:n_rows, :] += jnp.dot(a, wd_ref[0].astype(BF16), preferred_element_type=F32)

        half_only = tile_ref[2, t] <= MOE_TILE // 2

        @pl.when(half_only)
        def _():
            swiglu_rows(MOE_TILE // 2)

        @pl.when(jnp.logical_not(half_only))
        def _():
            swiglu_rows(MOE_TILE)

        @pl.when(c == pl.num_programs(1) - 1)
        def _():
            y_ref[...] = acc_ref[...].astype(BF16)


def _moe_experts(tile_tab, hs, w_gu, w_down, fc=512):
    nc = FFN_DIM // fc

    def tile_of(t, tr):
        return jnp.maximum(jnp.minimum(t, tr[1, 0] - 1), 0)

    def chunk_of(t, c, tr):
        return jnp.where(t < tr[1, 0], c, nc - 1)

    grid_spec = pltpu.PrefetchScalarGridSpec(
        num_scalar_prefetch=1,
        grid=(MOE_MAX_TILES, nc),
        in_specs=[
            pl.BlockSpec((MOE_TILE, D_MODEL), lambda t, c, tr: (tile_of(t, tr), 0)),
            pl.BlockSpec((1, D_MODEL, fc), lambda t, c, tr: (tr[0, t], 0, chunk_of(t, c, tr))),
            pl.BlockSpec((1, D_MODEL, fc), lambda t, c, tr: (tr[0, t], 0, chunk_of(t, c, tr) + nc)),
            pl.BlockSpec((1, fc, D_MODEL), lambda t, c, tr: (tr[0, t], chunk_of(t, c, tr), 0)),
        ],
        out_specs=pl.BlockSpec((MOE_TILE, D_MODEL), lambda t, c, tr: (t, 0)),
        scratch_shapes=[pltpu.VMEM((MOE_TILE, D_MODEL), F32)],
    )
    return pl.pallas_call(
        _moe_expert_kernel,
        grid_spec=grid_spec,
        out_shape=jax.ShapeDtypeStruct((MOE_ROWS, D_MODEL), BF16),
        compiler_params=pltpu.CompilerParams(
            dimension_semantics=("arbitrary", "arbitrary"), vmem_limit_bytes=VMEM_LIMIT),
        name="moe_experts",
    )(tile_tab, hs, w_gu, w_gu, w_down)


def _moe_combine_kernel(meta_ref, x_ref, route_ref, y_hbm, gf_ref, out_ref, yloc_ref, sem):
    b = pl.program_id(0)
    slot = b % 2

    def granule(ref, row):
        return ref.at[pl.ds(pl.multiple_of(row, MOE_GRAN), MOE_GRAN)]

    def start_block(blk, slot_):
        for e in range(N_EXPERTS):
            lo = meta_ref[blk, TAB_LOFF + e]
            goff = meta_ref[blk, TAB_GOFF + e]

            def start(gi, _, lo=lo, goff=goff):
                pltpu.make_async_copy(granule(y_hbm, goff + gi * MOE_GRAN),
                                      granule(yloc_ref.at[slot_], lo + gi * MOE_GRAN), sem.at[slot_]).start()
                return 0

            lax.fori_loop(0, meta_ref[blk, TAB_PC + e] // MOE_GRAN, start, 0)

    @pl.when(b == 0)
    def _():
        yloc_ref[...] = jnp.zeros_like(yloc_ref)
        start_block(0, 0)

    @pl.when(b + 1 < pl.num_programs(0))
    def _():
        start_block(jnp.minimum(b + 1, pl.num_programs(0) - 1), 1 - slot)

    n_mine = 0
    for e in range(N_EXPERTS):
        n_mine = n_mine + meta_ref[b, TAB_PC + e] // MOE_GRAN

    def wait_one(gi, _):
        pltpu.make_async_copy(granule(y_hbm, 0), granule(yloc_ref.at[slot], 0), sem.at[slot]).wait()
        return 0

    lax.fori_loop(0, n_mine, wait_one, 0)

    route_t = route_ref[...]
    route = route_t.T
    col = lax.broadcasted_iota(jnp.int32, (MOE_BLOCK, MOE_LOCAL_ROWS), 1).astype(F32)
    pick1 = jnp.where(col == route[:, ROUTE_D1:ROUTE_D1 + 1], 1.0, 0.0).astype(BF16)
    pick2 = jnp.where(col == route[:, ROUTE_D2:ROUTE_D2 + 1], 1.0, 0.0).astype(BF16)
    yl = yloc_ref[slot]
    moe = (route[:, ROUTE_G1:ROUTE_G1 + 1] * jnp.dot(pick1, yl, preferred_element_type=F32)
           + route[:, ROUTE_G2:ROUTE_G2 + 1] * jnp.dot(pick2, yl, preferred_element_type=F32))
    out_ref[...] = _rmsnorm(x_ref[...] + moe, gf_ref[...])


def _moe_combine(meta, x2d, route, y, g_final):
    grid_spec = pltpu.PrefetchScalarGridSpec(
        num_scalar_prefetch=1,
        grid=(N_MOE_BLOCKS,),
        in_specs=[
            pl.BlockSpec((MOE_BLOCK, D_MODEL), lambda b, m: (b, 0)),
            pl.BlockSpec((ROUTE_ROWS, MOE_BLOCK), lambda b, m: (0, b)),
            pl.BlockSpec(memory_space=pl.ANY),
            pl.BlockSpec((1, D_MODEL), lambda b, m: (0, 0)),
        ],
        out_specs=pl.BlockSpec((MOE_BLOCK, D_MODEL), lambda b, m: (b, 0)),
        scratch_shapes=[
            pltpu.VMEM((2, MOE_LOCAL_ROWS, D_MODEL), BF16),
            pltpu.SemaphoreType.DMA((2,)),
        ],
    )
    return pl.pallas_call(
        _moe_combine_kernel,
        grid_spec=grid_spec,
        out_shape=jax.ShapeDtypeStruct((N_TOK, D_MODEL), F32),
        compiler_params=pltpu.CompilerParams(
            dimension_semantics=("arbitrary",), vmem_limit_bytes=VMEM_LIMIT),
        name="moe_combine",
    )(meta, x2d, route, y, g_final)


def _moe(x2d, g, w_router, w_gu, w_down, g_final):
    route, meta = _moe_route(x2d, g, w_router)
    block_tab, tile_tab, fill_tab = _moe_plan(meta)
    hs = _moe_dispatch(block_tab, tile_tab, fill_tab, x2d, g, route)
    y = _moe_experts(tile_tab, hs, w_gu, w_down)
    return _moe_combine(block_tab, x2d, route, y, g_final)


def _rope_tables():
    def tables(dim):
        inv_freq = ROPE_THETA ** (-jnp.arange(0, dim, 2, dtype=F32) / dim)
        ang = jnp.arange(SEQ, dtype=F32)[:, None] * inv_freq[None, :]
        return jnp.cos(ang), jnp.sin(ang)

    c, s = tables(HEAD_DIM)
    cq = jnp.concatenate([c, c], axis=1)
    sq = jnp.concatenate([-s, s], axis=1)
    c, s = tables(IDX_ROPE_DIM)
    ones = jnp.ones((SEQ, IDX_DIM - IDX_ROPE_DIM), F32)
    zeros = jnp.zeros_like(ones)
    z16 = jnp.zeros_like(s)
    ci = jnp.concatenate([c, c, ones], axis=1)
    sia = jnp.concatenate([-s, z16, zeros], axis=1)
    sib = jnp.concatenate([z16, s, zeros], axis=1)
    rep = LANES // IDX_DIM
    return cq, sq, jnp.tile(ci, (1, rep)), jnp.tile(sia, (1, rep)), jnp.tile(sib, (1, rep))


def kernel(x, norm_mix, norm_ffn, dsa_w_in, dsa_idx_k_gain, dsa_w_out, ffn_w_gu, ffn_w_down, gmlp_w_in,
           gmlp_ln_gain, gmlp_ln_bias, gmlp_w_spatial, gmlp_b_spatial, gmlp_w_out, moe_w_router, moe_w_gu,
           moe_w_down, final_norm):
    x2d = x.reshape(N_TOK, D_MODEL)
    row = lambda v: v.reshape(1, -1)

    w_in = dsa_w_in[0]
    w_all = jnp.pad(w_in, ((0, 0), (0, DSA_COLS - w_in.shape[1])))
    wwi_t = w_in[:, COL_KI + IDX_DIM:COL_KI + IDX_DIM + IDX_HEADS].T
    kig = jnp.pad(dsa_idx_k_gain[0], (0, LANES - IDX_DIM)).reshape(1, LANES)
    q3, k, vt, kia, kib, qi, wt = _dsa_proj(x2d, row(norm_mix[0]), w_all, wwi_t, kig, _rope_tables())
    o3 = _dsa_attn(q3, k, vt, kia, kib, qi, wt)
    x2d = _resid_proj(x2d, o3, dsa_w_out[0])
    x2d = _ffn(x2d, row(norm_ffn[0]), ffn_w_gu[0], ffn_w_down[0])

    gw = GMLP_WIDTH // GMLP_GROUPS
    b_sp_full = jnp.repeat(gmlp_b_spatial[0].T, gw, axis=1)
    x2d = _gmlp(x2d, row(norm_mix[1]), gmlp_w_in[0], row(gmlp_ln_gain[0]), row(gmlp_ln_bias[0]),
                gmlp_w_spatial[0], b_sp_full, gmlp_w_out[0])
    w_router = jnp.pad(moe_w_router[0], ((0, 0), (0, LANES - N_EXPERTS)))
    out = _moe(x2d, row(norm_ffn[1]), w_router, moe_w_gu[0], moe_w_down[0], row(final_norm))
    return out.reshape(BATCH, SEQ, D_MODEL)
```

```python
import functools

import numpy as np
import jax
import jax.numpy as jnp
from jax import lax
from jax.experimental import pallas as pl
from jax.experimental.pallas import tpu as pltpu

F32 = jnp.float32
BF16 = jnp.bfloat16

D_MODEL = 1024
BATCH = 8
SEQ = 2048
N_TOK = BATCH * SEQ
N_HEADS = 8
HEAD_DIM = 128
IDX_HEADS = 8
IDX_DIM = 64
IDX_ROPE_DIM = 32
INDEX_TOPK = 256
Q_BLOCK = 128
KEY_SUPER = 512
ROPE_THETA = 10000.0
GMLP_WIDTH = 1024
GMLP_GROUPS = 8
GMLP_CHUNK = 128
FFN_DIM = 3584
N_EXPERTS = 8
NORM_EPS = 1e-6

LANES = 128
SUBLANES = 8
VMEM_LIMIT = 56 * 1024 * 1024

DSA_COLS = 1920
COL_K = N_HEADS * HEAD_DIM
COL_V = COL_K + HEAD_DIM
COL_QI = COL_V + HEAD_DIM
COL_KI = COL_QI + IDX_HEADS * IDX_DIM

NT_DIMS = (((1,), (1,)), ((), ()))
MASK_BIAS = -1e30
SCORE_MASKED = -3.0e38


def _rmsnorm(x, g):
    ms = jnp.mean(x * x, axis=-1, keepdims=True)
    return x * lax.rsqrt(ms + NORM_EPS) * g


def _cast_once(src_ref, dst_ref):
    @pl.when(pl.program_id(0) == 0)
    def _():
        dst_ref[...] = src_ref[...].astype(BF16)


def _resident(shape):
    return pl.BlockSpec(shape, lambda i: (0,) * len(shape), pipeline_mode=pl.Buffered(1))


def _dsa_proj_kernel(x_ref, g_ref, w_ref, wwi_ref, kig_ref, cq_ref, sq_ref, ci_ref, sia_ref, sib_ref,
                     q_ref, k_ref, vt_ref, kia_ref, kib_ref, qi_ref, wt_ref, wbf_ref):
    _cast_once(w_ref, wbf_ref)
    h = _rmsnorm(x_ref[...], g_ref[...]).astype(BF16)
    yr = jnp.dot(h, wbf_ref[:, COL_K:], preferred_element_type=F32)
    yq = jnp.dot(h, wbf_ref[:, :COL_K], preferred_element_type=F32)
    cq = cq_ref[...]
    sq = sq_ref[...]
    ci = ci_ref[...]
    sia = sia_ref[...]
    sib = sib_ref[...]

    def rope_head(t):
        return t * cq + pltpu.roll(t, HEAD_DIM // 2, 1) * sq

    def rope_idx(t):
        half = IDX_ROPE_DIM // 2
        return t * ci + pltpu.roll(t, LANES - half, 1) * sia + pltpu.roll(t, half, 1) * sib

    q_scale = HEAD_DIM ** -0.5 * np.log2(np.e)
    k_ref[...] = rope_head(yr[:, 0:COL_V - COL_K]).astype(BF16)
    vt_ref[0] = yr[:, COL_V - COL_K:COL_QI - COL_K].T.astype(BF16)
    for p in range(IDX_HEADS * IDX_DIM // LANES):
        c0 = COL_QI - COL_K + p * LANES
        qi_ref[:, p * LANES:(p + 1) * LANES] = rope_idx(yr[:, c0:c0 + LANES]).astype(BF16)
    for hh in range(N_HEADS):
        q_ref[hh] = (rope_head(yq[:, hh * HEAD_DIM:(hh + 1) * HEAD_DIM]) * q_scale).astype(BF16)

    last = yr[:, COL_KI - COL_K:COL_KI - COL_K + LANES]
    lane = lax.broadcasted_iota(jnp.int32, last.shape, 1)
    kraw = jnp.where(lane < IDX_DIM, last, 0.0)
    kms = jnp.sum(kraw * kraw, axis=-1, keepdims=True) * (1.0 / IDX_DIM)
    kn = kraw * lax.rsqrt(kms + NORM_EPS) * kig_ref[...]
    kr = rope_idx(kn)
    kia_ref[...] = kr.astype(BF16)
    kib_ref[...] = pltpu.roll(kr, IDX_DIM, 1).astype(BF16)

    w_scale = IDX_HEADS ** -0.5 * IDX_DIM ** -0.5
    wt_ref[...] = lax.dot_general(wwi_ref[...].astype(BF16), h, NT_DIMS, preferred_element_type=F32) * w_scale


def _dsa_proj(x2d, g, w_all, wwi_t, kig, tabs, tm=KEY_SUPER):
    n = x2d.shape[0]
    nblk_seq = SEQ // tm
    tab_spec = pl.BlockSpec((tm, LANES), lambda i: (i % nblk_seq, 0))
    full = lambda shape: pl.BlockSpec(shape, lambda i: (0,) * len(shape))
    return pl.pallas_call(
        _dsa_proj_kernel,
        grid=(n // tm,),
        in_specs=[
            pl.BlockSpec((tm, D_MODEL), lambda i: (i, 0)),
            full((1, D_MODEL)),
            _resident((D_MODEL, DSA_COLS)),
            full((IDX_HEADS, D_MODEL)),
            full((1, LANES)),
            tab_spec, tab_spec, tab_spec, tab_spec, tab_spec,
        ],
        out_specs=[
            pl.BlockSpec((N_HEADS, tm, HEAD_DIM), lambda i: (0, i, 0)),
            pl.BlockSpec((tm, HEAD_DIM), lambda i: (i, 0)),
            pl.BlockSpec((1, HEAD_DIM, tm), lambda i: (i, 0, 0)),
            pl.BlockSpec((tm, LANES), lambda i: (i, 0)),
            pl.BlockSpec((tm, LANES), lambda i: (i, 0)),
            pl.BlockSpec((tm, IDX_HEADS * IDX_DIM), lambda i: (i, 0)),
            pl.BlockSpec((IDX_HEADS, tm), lambda i: (0, i)),
        ],
        out_shape=[
            jax.ShapeDtypeStruct((N_HEADS, n, HEAD_DIM), BF16),
            jax.ShapeDtypeStruct((n, HEAD_DIM), BF16),
            jax.ShapeDtypeStruct((n // tm, HEAD_DIM, tm), BF16),
            jax.ShapeDtypeStruct((n, LANES), BF16),
            jax.ShapeDtypeStruct((n, LANES), BF16),
            jax.ShapeDtypeStruct((n, IDX_HEADS * IDX_DIM), BF16),
            jax.ShapeDtypeStruct((IDX_HEADS, n), F32),
        ],
        scratch_shapes=[pltpu.VMEM((D_MODEL, DSA_COLS), BF16)],
        compiler_params=pltpu.CompilerParams(
            dimension_semantics=("arbitrary",), vmem_limit_bytes=VMEM_LIMIT),
        name="dsa_proj",
    )(x2d, g, w_all, wwi_t, kig, *tabs)


def _ordered_bits_to_float(u):
    key = u ^ jnp.int32(-2 ** 31)
    key = jnp.maximum(key, jnp.int32(-2139095041))
    bits = jnp.where(key >= 0, key, key ^ jnp.int32(0x7FFFFFFF))
    return lax.bitcast_convert_type(bits, F32)


def _dsa_attn_kernel(q_ref, k_ref, vt_ref, kia_ref, kib_ref, qi_ref, wt_ref, o_ref,
                     s_ref, sel_ref, m_ref, l_ref, p_ref, acc_ref):
    j = pl.program_id(1)
    per_super = KEY_SUPER // Q_BLOCK
    n_super = j // per_super + 1
    hq = N_HEADS * Q_BLOCK

    qpos = j * Q_BLOCK + lax.broadcasted_iota(jnp.int32, (Q_BLOCK, Q_BLOCK), 1)
    krow = lax.broadcasted_iota(jnp.int32, (Q_BLOCK, Q_BLOCK), 0)

    def super_start(sc):
        return sc * KEY_SUPER if isinstance(sc, int) else pl.multiple_of(sc * KEY_SUPER, KEY_SUPER)

    def rows_of(sc, c4):
        return pl.ds(super_start(sc) + c4 * Q_BLOCK, Q_BLOCK)

    def super_rows(sc):
        return pl.ds(super_start(sc), KEY_SUPER)

    def causal(sc, c4):
        return (krow + (sc * KEY_SUPER + c4 * Q_BLOCK)) <= qpos

    def fold8(t):
        return t.reshape(t.shape[0] // SUBLANES, SUBLANES, t.shape[1])

    def score_body(sc, _):
        qpps = [
            jnp.concatenate([qi_ref[:, (2 * half) * LANES:(2 * half + 1) * LANES],
                             qi_ref[:, (2 * half + 1) * LANES:(2 * half + 2) * LANES]], axis=0)
            for half in range(IDX_HEADS // 4)
        ]
        group = 2 * Q_BLOCK
        for kg in range(KEY_SUPER // group):
            rows = pl.ds(super_start(sc) + kg * group, group)
            ka = kia_ref[rows, :]
            kb = kib_ref[rows, :]
            acc = None
            for half in range(IDX_HEADS // 4):
                ra = lax.dot_general(ka, qpps[half], NT_DIMS, preferred_element_type=F32)
                rb = lax.dot_general(kb, qpps[half], NT_DIMS, preferred_element_type=F32)
                for sub in range(2):
                    p = 2 * half + sub
                    cols = slice(sub * Q_BLOCK, (sub + 1) * Q_BLOCK)
                    t2 = (jnp.maximum(ra[:, cols], 0.0) * wt_ref[2 * p:2 * p + 1, :]
                          + jnp.maximum(rb[:, cols], 0.0) * wt_ref[2 * p + 1:2 * p + 2, :])
                    acc = t2 if acc is None else acc + t2
            kpos = sc * KEY_SUPER + kg * group + lax.broadcasted_iota(jnp.int32, (group, Q_BLOCK), 0)
            qp_s = j * Q_BLOCK + lax.broadcasted_iota(jnp.int32, (group, Q_BLOCK), 1)
            s_ref[rows, :] = jnp.where(kpos <= qp_s, acc, SCORE_MASKED)
        return 0

    def scores(n_static):
        for sc in range(n_static):
            score_body(sc, 0)

    for n_static in range(1, SEQ // KEY_SUPER + 1):
        pl.when(n_super == n_static)(functools.partial(scores, n_static))

    def search(n_chunks):
        def count_ge(cand):
            accs = [jnp.zeros((SUBLANES, LANES), F32), jnp.zeros((SUBLANES, LANES), F32)]
            for c in range(n_chunks):
                ind = jnp.where(s_ref[pl.ds(c * Q_BLOCK, Q_BLOCK), :] >= cand, 1.0, 0.0)
                accs[c % 2] = accs[c % 2] + fold8(ind).sum(axis=0)
            return jnp.sum(accs[0] + accs[1], axis=0, keepdims=True)

        def bit_body(i, carry):
            prefix, cbest = carry
            cu = prefix | lax.shift_left(jnp.int32(1), 31 - i)
            cnt = count_ge(_ordered_bits_to_float(cu))
            ok = cnt >= float(INDEX_TOPK)
            return jnp.where(ok, cu, prefix), jnp.where(ok, cnt, cbest)

        prefix, cbest = lax.fori_loop(
            0, 32, bit_body,
            (jnp.zeros((1, LANES), jnp.int32), jnp.full((1, LANES), float(n_chunks * Q_BLOCK), F32)))
        sel_ref[0:1, :] = _ordered_bits_to_float(prefix)
        sel_ref[1:2, :] = cbest

    for n_chunks in range(1, SEQ // Q_BLOCK + 1):
        pl.when(j == n_chunks - 1)(functools.partial(search, n_chunks))
    tau = sel_ref[0:1, :]
    drop = sel_ref[1:2, :] - float(INDEX_TOPK)

    upper = jnp.where(krow < lax.broadcasted_iota(jnp.int32, (Q_BLOCK, Q_BLOCK), 1), 1.0, 0.0).astype(BF16)
    pair_w = 2 * Q_BLOCK

    def select_logit_chunk(sc, carry):
        later_ties, m_acc = carry
        biases = [None] * per_super
        for c4 in reversed(range(per_super)):
            s = s_ref[rows_of(sc, c4), :]
            eq = jnp.where(s == tau, 1.0, 0.0)
            ties_after = jnp.dot(upper, eq.astype(BF16), preferred_element_type=F32) + later_ties
            later_ties = later_ties + jnp.sum(eq, axis=0, keepdims=True)
            tie_bias = jnp.where(ties_after >= drop, 0.0, MASK_BIAS)
            bias = jnp.where(s > tau, 0.0, jnp.where(s == tau, tie_bias, MASK_BIAS))
            biases[c4] = jnp.where(causal(sc, c4), bias, MASK_BIAS)
        bias = jnp.concatenate(biases, axis=0)
        bias2 = jnp.concatenate([bias, bias], axis=1)
        srows = super_rows(sc)
        kc = k_ref[srows, :]
        parts = []
        for pr in range(N_HEADS // 2):
            qpair = q_ref[2 * pr:2 * pr + 2].reshape(pair_w, HEAD_DIM)
            lg = lax.dot_general(kc, qpair, NT_DIMS, preferred_element_type=F32) + bias2
            l_ref[srows, pr * pair_w:(pr + 1) * pair_w] = lg
            parts.append(fold8(lg).max(axis=0))
        return later_ties, jnp.maximum(m_acc, jnp.concatenate(parts, axis=1))

    def select_logits(n_static):
        carry = (jnp.zeros((1, LANES), F32), jnp.full((SUBLANES, hq), MASK_BIAS, F32))
        for sc in reversed(range(n_static)):
            carry = select_logit_chunk(sc, carry)
        m_ref[...] = carry[1]

    for n_static in range(1, SEQ // KEY_SUPER + 1):
        pl.when(n_super == n_static)(functools.partial(select_logits, n_static))
    m = jnp.max(m_ref[...], axis=0, keepdims=True)

    def prob_chunk(sc, d_acc):
        for c4 in range(per_super):
            rows = rows_of(sc, c4)
            parts = []
            for pr in range(N_HEADS // 2):
                cols = slice(pr * pair_w, (pr + 1) * pair_w)
                p = jnp.exp2(l_ref[rows, cols] - m[:, cols])
                p_ref[rows, cols] = p.astype(BF16)
                parts.append(fold8(p).sum(axis=0))
            d_acc = d_acc + jnp.concatenate(parts, axis=1)
        return d_acc

    def pv_chunk(sc):
        acc_ref[...] += jnp.dot(vt_ref[sc], p_ref[super_rows(sc), :], preferred_element_type=F32)

    def prob_pv(n_static):
        acc_ref[...] = jnp.zeros_like(acc_ref)
        d_acc = prob_chunk(0, jnp.zeros((SUBLANES, hq), F32))
        for sc in range(1, n_static):
            pv_chunk(sc - 1)
            d_acc = prob_chunk(sc, d_acc)
        pv_chunk(n_static - 1)
        inv = 1.0 / jnp.sum(d_acc, axis=0, keepdims=True)
        for h in range(N_HEADS):
            cols = slice(h * Q_BLOCK, (h + 1) * Q_BLOCK)
            o_ref[h] = (acc_ref[:, cols] * inv[:, cols]).T.astype(BF16)

    for n_static in range(1, SEQ // KEY_SUPER + 1):
        pl.when(n_super == n_static)(functools.partial(prob_pv, n_static))


def _dsa_attn(q3, k, vt, kia, kib, qi, wt):
    nqb = SEQ // Q_BLOCK
    n_sup = SEQ // KEY_SUPER
    hq = N_HEADS * Q_BLOCK
    return pl.pallas_call(
        _dsa_attn_kernel,
        grid=(BATCH, nqb),
        in_specs=[
            pl.BlockSpec((N_HEADS, Q_BLOCK, HEAD_DIM), lambda b, j: (0, b * nqb + j, 0)),
            pl.BlockSpec((SEQ, HEAD_DIM), lambda b, j: (b, 0)),
            pl.BlockSpec((n_sup, HEAD_DIM, KEY_SUPER), lambda b, j: (b, 0, 0)),
            pl.BlockSpec((SEQ, LANES), lambda b, j: (b, 0)),
            pl.BlockSpec((SEQ, LANES), lambda b, j: (b, 0)),
            pl.BlockSpec((Q_BLOCK, IDX_HEADS * IDX_DIM), lambda b, j: (b * nqb + j, 0)),
            pl.BlockSpec((IDX_HEADS, Q_BLOCK), lambda b, j: (0, b * nqb + j)),
        ],
        out_specs=pl.BlockSpec((N_HEADS, Q_BLOCK, HEAD_DIM), lambda b, j: (0, b * nqb + j, 0)),
        out_shape=jax.ShapeDtypeStruct((N_HEADS, N_TOK, HEAD_DIM), BF16),
        scratch_shapes=[
            pltpu.VMEM((SEQ, Q_BLOCK), F32),
            pltpu.VMEM((SUBLANES, Q_BLOCK), F32),
            pltpu.VMEM((SUBLANES, hq), F32),
            pltpu.VMEM((SEQ, hq), F32),
            pltpu.VMEM((SEQ, hq), BF16),
            pltpu.VMEM((HEAD_DIM, hq), F32),
        ],
        compiler_params=pltpu.CompilerParams(
            dimension_semantics=("parallel", "arbitrary"), vmem_limit_bytes=VMEM_LIMIT),
        name="dsa_attn",
    )(q3, k, vt, kia, kib, qi, wt)


def _resid_proj_kernel(x_ref, o_ref, w_ref, out_ref, wbf_ref):
    _cast_once(w_ref, wbf_ref)
    o = jnp.concatenate([o_ref[h] for h in range(N_HEADS)], axis=1)
    out_ref[...] = x_ref[...] + jnp.dot(o, wbf_ref[...], preferred_element_type=F32)


def _resid_proj(x2d, o3, w_out, tm=1024):
    n = x2d.shape[0]
    return pl.pallas_call(
        _resid_proj_kernel,
        grid=(n // tm,),
        in_specs=[
            pl.BlockSpec((tm, D_MODEL), lambda i: (i, 0)),
            pl.BlockSpec((N_HEADS, tm, HEAD_DIM), lambda i: (0, i, 0)),
            _resident((D_MODEL, D_MODEL)),
        ],
        out_specs=pl.BlockSpec((tm, D_MODEL), lambda i: (i, 0)),
        out_shape=jax.ShapeDtypeStruct((n, D_MODEL), F32),
        scratch_shapes=[pltpu.VMEM((D_MODEL, D_MODEL), BF16)],
        compiler_params=pltpu.CompilerParams(
            dimension_semantics=("arbitrary",), vmem_limit_bytes=VMEM_LIMIT),
        name="dsa_out_proj",
    )(x2d, o3, w_out)


def _ffn_kernel(x_ref, g_ref, wg_ref, wu_ref, wd_ref, out_ref, h_ref, acc_ref):
    c = pl.program_id(1)

    @pl.when(c == 0)
    def _():
        h_ref[...] = _rmsnorm(x_ref[...], g_ref[...]).astype(BF16)
        acc_ref[...] = jnp.zeros_like(acc_ref)

    h = h_ref[...]
    gate = jnp.dot(h, wg_ref[...].astype(BF16), preferred_element_type=F32)
    up = jnp.dot(h, wu_ref[...].astype(BF16), preferred_element_type=F32)
    a = (gate * jax.nn.sigmoid(gate) * up).astype(BF16)
    acc_ref[...] += jnp.dot(a, wd_ref[...].astype(BF16), preferred_element_type=F32)

    @pl.when(c == pl.num_programs(1) - 1)
    def _():
        out_ref[...] = x_ref[...] + acc_ref[...]


def _ffn(x2d, g, w_gu, w_down, tm=1024, fc=512):
    n = x2d.shape[0]
    nc = FFN_DIM // fc
    return pl.pallas_call(
        _ffn_kernel,
        grid=(n // tm, nc),
        in_specs=[
            pl.BlockSpec((tm, D_MODEL), lambda i, c: (i, 0)),
            pl.BlockSpec((1, D_MODEL), lambda i, c: (0, 0)),
            pl.BlockSpec((D_MODEL, fc), lambda i, c: (0, c)),
            pl.BlockSpec((D_MODEL, fc), lambda i, c: (0, c + nc)),
            pl.BlockSpec((fc, D_MODEL), lambda i, c: (c, 0)),
        ],
        out_specs=pl.BlockSpec((tm, D_MODEL), lambda i, c: (i, 0)),
        out_shape=jax.ShapeDtypeStruct((n, D_MODEL), F32),
        scratch_shapes=[pltpu.VMEM((tm, D_MODEL), BF16), pltpu.VMEM((tm, D_MODEL), F32)],
        compiler_params=pltpu.CompilerParams(
            dimension_semantics=("parallel", "arbitrary"), vmem_limit_bytes=VMEM_LIMIT),
        name="swiglu_ffn",
    )(x2d, g, w_gu, w_gu, w_down)


def _gelu_tanh(x):
    return 0.5 * x * (1.0 + jnp.tanh(np.sqrt(2.0 / np.pi) * (x + 0.044715 * (x * x * x))))


def _gmlp_kernel(x_ref, g_ref, win_ref, lng_ref, lnb_ref, ws_ref, bs_ref, wout_ref, out_ref,
                 winbf_ref, woutbf_ref):
    _cast_once(win_ref, winbf_ref)
    _cast_once(wout_ref, woutbf_ref)
    x = x_ref[...]
    tm = x.shape[0]
    h = _rmsnorm(x, g_ref[...]).astype(BF16)
    v = _gelu_tanh(jnp.dot(h, winbf_ref[:, GMLP_WIDTH:], preferred_element_type=F32))
    u = _gelu_tanh(jnp.dot(h, winbf_ref[:, :GMLP_WIDTH], preferred_element_type=F32))
    mu = jnp.mean(v, axis=-1, keepdims=True)
    vc = v - mu
    var = jnp.mean(vc * vc, axis=-1, keepdims=True)
    vb = (vc * lax.rsqrt(var + NORM_EPS) * lng_ref[...] + lnb_ref[...]).astype(BF16)

    gw = GMLP_WIDTH // GMLP_GROUPS
    row = lax.broadcasted_iota(jnp.int32, (GMLP_CHUNK, GMLP_CHUNK), 0)
    col = lax.broadcasted_iota(jnp.int32, (GMLP_CHUNK, GMLP_CHUNK), 1)
    w_tril = [jnp.where(row >= col, ws_ref[gi], 0.0).astype(BF16) for gi in range(GMLP_GROUPS)]
    rows = []
    for c in range(tm // GMLP_CHUNK):
        parts = [
            jnp.dot(w_tril[gi], vb[c * GMLP_CHUNK:(c + 1) * GMLP_CHUNK, gi * gw:(gi + 1) * gw],
                    preferred_element_type=F32)
            for gi in range(GMLP_GROUPS)
        ]
        rows.append(jnp.concatenate(parts, axis=1) + bs_ref[...])
    s = jnp.concatenate(rows, axis=0)
    z = (u * s).astype(BF16)
    out_ref[...] = x + jnp.dot(z, woutbf_ref[...], preferred_element_type=F32)


def _gmlp(x2d, g, w_in, ln_g, ln_b, w_sp, b_sp_full, w_out, tm=512):
    n = x2d.shape[0]
    full = lambda shape: pl.BlockSpec(shape, lambda i: (0,) * len(shape))
    return pl.pallas_call(
        _gmlp_kernel,
        grid=(n // tm,),
        in_specs=[
            pl.BlockSpec((tm, D_MODEL), lambda i: (i, 0)),
            full((1, D_MODEL)),
            _resident((D_MODEL, 2 * GMLP_WIDTH)),
            full((1, GMLP_WIDTH)),
            full((1, GMLP_WIDTH)),
            full((GMLP_GROUPS, GMLP_CHUNK, GMLP_CHUNK)),
            full((GMLP_CHUNK, GMLP_WIDTH)),
            _resident((GMLP_WIDTH, D_MODEL)),
        ],
        out_specs=pl.BlockSpec((tm, D_MODEL), lambda i: (i, 0)),
        out_shape=jax.ShapeDtypeStruct((n, D_MODEL), F32),
        scratch_shapes=[pltpu.VMEM((D_MODEL, 2 * GMLP_WIDTH), BF16), pltpu.VMEM((GMLP_WIDTH, D_MODEL), BF16)],
        compiler_params=pltpu.CompilerParams(
            dimension_semantics=("arbitrary",), vmem_limit_bytes=VMEM_LIMIT),
        name="gmlp",
    )(x2d, g, w_in, ln_g, ln_b, w_sp, b_sp_full, w_out)


MOE_BLOCK = 512
MOE_GRAN = 16
MOE_TILE = 1024
MOE_LOCAL_ROWS = 2 * MOE_BLOCK + N_EXPERTS * MOE_GRAN
N_MOE_BLOCKS = N_TOK // MOE_BLOCK
MOE_MAX_TILES = (2 * N_TOK + N_MOE_BLOCKS * N_EXPERTS * (MOE_GRAN - 1)) // MOE_TILE + N_EXPERTS
MOE_ROWS = MOE_MAX_TILES * MOE_TILE
ROUTE_D1, ROUTE_D2, ROUTE_G1, ROUTE_G2 = 0, 1, 2, 3
ROUTE_ROWS = N_EXPERTS
TAB_LOFF, TAB_PC, TAB_GOFF = 0, N_EXPERTS, 2 * N_EXPERTS


def _moe_route_kernel(x_ref, g_ref, wr_ref, route_ref, meta_ref):
    b = pl.program_id(0)
    t_blk = MOE_BLOCK
    hf = _rmsnorm(x_ref[...], g_ref[...])
    wr = wr_ref[...]
    h_hi = hf.astype(BF16)
    h_lo = (hf - h_hi.astype(F32)).astype(BF16)
    w_hi = wr.astype(BF16)
    w_lo = (wr - w_hi.astype(F32)).astype(BF16)
    logits = (jnp.dot(h_hi, w_hi, preferred_element_type=F32) + jnp.dot(h_hi, w_lo, preferred_element_type=F32)
              + jnp.dot(h_lo, w_hi, preferred_element_type=F32))

    lg = logits.T[0:N_EXPERTS, :]
    e_idx = lax.broadcasted_iota(jnp.int32, lg.shape, 0).astype(F32)
    neg_inf = -jnp.inf
    m1 = jnp.max(lg, axis=0, keepdims=True)
    i1 = jnp.min(jnp.where(lg == m1, e_idx, float(N_EXPERTS)), axis=0, keepdims=True)
    lg2 = jnp.where(e_idx == i1, neg_inf, lg)
    m2 = jnp.max(lg2, axis=0, keepdims=True)
    i2 = jnp.min(jnp.where(lg2 == m2, e_idx, float(N_EXPERTS)), axis=0, keepdims=True)
    t = jnp.exp(m2 - m1)
    g1 = 1.0 / (1.0 + t)
    g2 = t * g1

    memb = jnp.where(jnp.logical_or(e_idx == i1, e_idx == i2), 1.0, 0.0)
    tr = lax.broadcasted_iota(jnp.int32, (t_blk, t_blk), 0)
    tc = lax.broadcasted_iota(jnp.int32, (t_blk, t_blk), 1)
    earlier = jnp.where(tr < tc, 1.0, 0.0).astype(BF16)
    rank = jnp.dot(memb.astype(BF16), earlier, preferred_element_type=F32)
    loff_col = jnp.zeros((N_EXPERTS, 1), F32)
    e_col = lax.broadcasted_iota(jnp.int32, (N_EXPERTS, 1), 0)
    lo = jnp.int32(0)
    for e in range(N_EXPERTS):
        cnt = jnp.sum(memb[e:e + 1, :]).astype(jnp.int32)
        pce = (cnt + (MOE_GRAN - 1)) // MOE_GRAN * MOE_GRAN
        meta_ref[b, TAB_LOFF + e] = lo
        meta_ref[b, TAB_PC + e] = pce
        loff_col = jnp.where(e_col == e, lo.astype(F32), loff_col)
        lo = lo + pce
    dest = loff_col + rank
    d1 = jnp.sum(jnp.where(e_idx == i1, dest, 0.0), axis=0, keepdims=True)
    d2 = jnp.sum(jnp.where(e_idx == i2, dest, 0.0), axis=0, keepdims=True)
    r_idx = lax.broadcasted_iota(jnp.int32, lg.shape, 0)
    route_ref[...] = jnp.where(r_idx == ROUTE_D1, d1,
                               jnp.where(r_idx == ROUTE_D2, d2,
                                         jnp.where(r_idx == ROUTE_G1, g1, jnp.where(r_idx == ROUTE_G2, g2, 0.0))))


def _moe_route(x2d, g, w_router):
    return pl.pallas_call(
        _moe_route_kernel,
        grid=(N_MOE_BLOCKS,),
        in_specs=[
            pl.BlockSpec((MOE_BLOCK, D_MODEL), lambda b: (b, 0)),
            pl.BlockSpec((1, D_MODEL), lambda b: (0, 0)),
            pl.BlockSpec((D_MODEL, LANES), lambda b: (0, 0)),
        ],
        out_specs=[
            pl.BlockSpec((ROUTE_ROWS, MOE_BLOCK), lambda b: (0, b)),
            pl.BlockSpec(memory_space=pltpu.SMEM),
        ],
        out_shape=[
            jax.ShapeDtypeStruct((ROUTE_ROWS, N_TOK), F32),
            jax.ShapeDtypeStruct((N_MOE_BLOCKS, 2 * N_EXPERTS), jnp.int32),
        ],
        compiler_params=pltpu.CompilerParams(
            dimension_semantics=("arbitrary",), vmem_limit_bytes=VMEM_LIMIT),
        name="moe_route",
    )(x2d, g, w_router)


def _moe_plan(meta):
    pc = meta[:, TAB_PC:TAB_PC + N_EXPERTS]
    totals = jnp.sum(pc, axis=0)
    per_expert = (totals + (MOE_TILE - 1)) // MOE_TILE
    e_lt = jnp.arange(N_EXPERTS)[:, None] < jnp.arange(N_EXPERTS)[None, :]
    first_tile = jnp.sum(jnp.where(e_lt, per_expert[:, None], 0), axis=0)
    ends = first_tile + per_expert
    seg_start = first_tile * MOE_TILE
    b_lt = jnp.arange(N_MOE_BLOCKS)[:, None] < jnp.arange(N_MOE_BLOCKS)[None, :]
    goff = seg_start[None, :] + jnp.sum(jnp.where(b_lt[:, :, None], pc[:, None, :], 0), axis=0)
    block_tab = jnp.concatenate([meta, goff], axis=1).astype(jnp.int32)
    n_tiles = jnp.sum(per_expert)
    t = jnp.minimum(jnp.arange(MOE_MAX_TILES, dtype=jnp.int32), n_tiles - 1)
    mine = jnp.logical_and(t[:, None] >= first_tile[None, :], t[:, None] < ends[None, :])
    expert = jnp.sum(jnp.where(mine, jnp.arange(N_EXPERTS)[None, :], 0), axis=1)
    left = totals[None, :] - (t[:, None] - first_tile[None, :]) * MOE_TILE
    used = jnp.sum(jnp.where(mine, jnp.clip(left, 0, MOE_TILE), 0), axis=1)
    tile_tab = jnp.stack([expert, jnp.full_like(t, n_tiles), used]).astype(jnp.int32)
    fill_tab = jnp.stack([seg_start + totals, ends * MOE_TILE]).astype(jnp.int32)
    return block_tab, tile_tab, fill_tab


def _moe_dispatch_kernel(tab_ref, tile_ref, fill_ref, x_ref, g_ref, route_ref, hs_hbm, hloc_ref, zero_ref, sem):
    b = pl.program_id(0)
    slot = b % 2
    hloc = hloc_ref.at[slot]
    hb = _rmsnorm(x_ref[...], g_ref[...]).astype(BF16)

    route_t = route_ref[...]
    r_iota = lax.broadcasted_iota(jnp.int32, (MOE_LOCAL_ROWS, MOE_BLOCK), 0).astype(F32)
    hit = jnp.logical_or(r_iota == route_t[ROUTE_D1:ROUTE_D1 + 1, :], r_iota == route_t[ROUTE_D2:ROUTE_D2 + 1, :])
    perm = jnp.where(hit, 1.0, 0.0).astype(BF16)
    hloc_ref[slot] = jnp.dot(perm, hb, preferred_element_type=F32).astype(BF16)

    def granule(ref, row):
        return ref.at[pl.ds(pl.multiple_of(row, MOE_GRAN), MOE_GRAN)]

    for e in range(N_EXPERTS):
        lo = tab_ref[b, TAB_LOFF + e]
        goff = tab_ref[b, TAB_GOFF + e]

        def start(gi, _, lo=lo, goff=goff, prio=e % 2):
            pltpu.make_async_copy(granule(hloc, lo + gi * MOE_GRAN),
                                  granule(hs_hbm, goff + gi * MOE_GRAN), sem.at[slot]).start(priority=prio)
            return 0

        lax.fori_loop(0, tab_ref[b, TAB_PC + e] // MOE_GRAN, start, 0)

    def wait_block(blk, slot_):
        n = 0
        for e in range(N_EXPERTS):
            n = n + tab_ref[blk, TAB_PC + e] // MOE_GRAN

        def wait_one(gi, _):
            pltpu.make_async_copy(granule(hloc_ref.at[slot_], 0), granule(hs_hbm, 0), sem.at[slot_]).wait()
            return 0

        lax.fori_loop(0, n, wait_one, 0)

    @pl.when(b > 0)
    def _():
        wait_block(jnp.maximum(b - 1, 0), 1 - slot)

    @pl.when(b == pl.num_programs(0) - 1)
    def _():
        wait_block(b, slot)
        zero_ref[...] = jnp.zeros_like(zero_ref)
        n_fill = 0
        for e in range(N_EXPERTS):
            first = fill_ref[0, e]
            n_e = (fill_ref[1, e] - first) // MOE_GRAN

            def fill(gi, _, first=first):
                pltpu.make_async_copy(granule(zero_ref, 0), granule(hs_hbm, first + gi * MOE_GRAN),
                                      sem.at[slot]).start()
                return 0

            lax.fori_loop(0, n_e, fill, 0)
            n_fill = n_fill + n_e

        def wait_fill(gi, _):
            pltpu.make_async_copy(granule(zero_ref, 0), granule(hs_hbm, 0), sem.at[slot]).wait()
            return 0

        lax.fori_loop(0, n_fill, wait_fill, 0)

        def tile_at(ti):
            return hs_hbm.at[pl.ds(pl.multiple_of(ti * MOE_TILE, MOE_TILE), MOE_TILE)]

        n_tiles = tile_ref[1, 0]

        def fill_tile(ti, _):
            pltpu.make_async_copy(zero_ref, tile_at(ti), sem.at[slot]).start()
            return 0

        def wait_tile(ti, _):
            pltpu.make_async_copy(zero_ref, tile_at(0), sem.at[slot]).wait()
            return 0

        lax.fori_loop(n_tiles, MOE_MAX_TILES, fill_tile, 0)
        lax.fori_loop(n_tiles, MOE_MAX_TILES, wait_tile, 0)


def _moe_dispatch(block_tab, tile_tab, fill_tab, x2d, g, route):
    grid_spec = pltpu.PrefetchScalarGridSpec(
        num_scalar_prefetch=3,
        grid=(N_MOE_BLOCKS,),
        in_specs=[
            pl.BlockSpec((MOE_BLOCK, D_MODEL), lambda b, *_: (b, 0)),
            pl.BlockSpec((1, D_MODEL), lambda b, *_: (0, 0)),
            pl.BlockSpec((ROUTE_ROWS, MOE_BLOCK), lambda b, *_: (0, b)),
        ],
        out_specs=pl.BlockSpec(memory_space=pl.ANY),
        scratch_shapes=[
            pltpu.VMEM((2, MOE_LOCAL_ROWS, D_MODEL), BF16),
            pltpu.VMEM((MOE_TILE, D_MODEL), BF16),
            pltpu.SemaphoreType.DMA((2,)),
        ],
    )
    return pl.pallas_call(
        _moe_dispatch_kernel,
        grid_spec=grid_spec,
        out_shape=jax.ShapeDtypeStruct((MOE_ROWS, D_MODEL), BF16),
        compiler_params=pltpu.CompilerParams(
            dimension_semantics=("arbitrary",), vmem_limit_bytes=VMEM_LIMIT),
        name="moe_dispatch",
    )(block_tab, tile_tab, fill_tab, x2d, g, route)


def _moe_expert_kernel(tile_ref, hs_ref, wg_ref, wu_ref, wd_ref, y_ref, acc_ref):
    t = pl.program_id(0)
    c = pl.program_id(1)
    active = t < tile_ref[1, 0]

    @pl.when(jnp.logical_and(jnp.logical_not(active), c == 0))
    def _():
        y_ref[...] = jnp.zeros_like(y_ref)

    @pl.when(active)
    def _():
        @pl.when(c == 0)
        def _():
            acc_ref[...] = jnp.zeros_like(acc_ref)

        def swiglu_rows(n_rows):
            h = hs_ref[0:n_rows, :]
            gate = jnp.dot(h, wg_ref[0].astype(BF16), preferred_element_type=F32)
            up = jnp.dot(h, wu_ref[0].astype(BF16), preferred_element_type=F32)
            a = (gate * jax.nn.sigmoid(gate) * up).astype(BF16)
            acc_ref[0:n_rows, :] += jnp.dot(a, wd_ref[0].astype(BF16), preferred_element_type=F32)

        half_only = tile_ref[2, t] <= MOE_TILE // 2

        @pl.when(half_only)
        def _():
            swiglu_rows(MOE_TILE // 2)

        @pl.when(jnp.logical_not(half_only))
        def _():
            swiglu_rows(MOE_TILE)

        @pl.when(c == pl.num_programs(1) - 1)
        def _():
            y_ref[...] = acc_ref[...].astype(BF16)


def _moe_experts(tile_tab, hs, w_gu, w_down, fc=512):
    nc = FFN_DIM // fc

    def tile_of(t, tr):
        return jnp.maximum(jnp.minimum(t, tr[1, 0] - 1), 0)

    def chunk_of(t, c, tr):
        return jnp.where(t < tr[1, 0], c, nc - 1)

    grid_spec = pltpu.PrefetchScalarGridSpec(
        num_scalar_prefetch=1,
        grid=(MOE_MAX_TILES, nc),
        in_specs=[
            pl.BlockSpec((MOE_TILE, D_MODEL), lambda t, c, tr: (tile_of(t, tr), 0)),
            pl.BlockSpec((1, D_MODEL, fc), lambda t, c, tr: (tr[0, t], 0, chunk_of(t, c, tr))),
            pl.BlockSpec((1, D_MODEL, fc), lambda t, c, tr: (tr[0, t], 0, chunk_of(t, c, tr) + nc)),
            pl.BlockSpec((1, fc, D_MODEL), lambda t, c, tr: (tr[0, t], chunk_of(t, c, tr), 0)),
        ],
        out_specs=pl.BlockSpec((MOE_TILE, D_MODEL), lambda t, c, tr: (t, 0)),
        scratch_shapes=[pltpu.VMEM((MOE_TILE, D_MODEL), F32)],
    )
    return pl.pallas_call(
        _moe_expert_kernel,
        grid_spec=grid_spec,
        out_shape=jax.ShapeDtypeStruct((MOE_ROWS, D_MODEL), BF16),
        compiler_params=pltpu.CompilerParams(
            dimension_semantics=("arbitrary", "arbitrary"), vmem_limit_bytes=VMEM_LIMIT),
        name="moe_experts",
    )(tile_tab, hs, w_gu, w_gu, w_down)


def _moe_combine_kernel(meta_ref, x_ref, route_ref, y_hbm, gf_ref, out_ref, yloc_ref, sem):
    b = pl.program_id(0)
    slot = b % 2

    def granule(ref, row):
        return ref.at[pl.ds(pl.multiple_of(row, MOE_GRAN), MOE_GRAN)]

    def start_block(blk, slot_):
        for e in range(N_EXPERTS):
            lo = meta_ref[blk, TAB_LOFF + e]
            goff = meta_ref[blk, TAB_GOFF + e]

            def start(gi, _, lo=lo, goff=goff, prio=e % 2):
                pltpu.make_async_copy(granule(y_hbm, goff + gi * MOE_GRAN),
                                      granule(yloc_ref.at[slot_], lo + gi * MOE_GRAN),
                                      sem.at[slot_]).start(priority=prio)
                return 0

            lax.fori_loop(0, meta_ref[blk, TAB_PC + e] // MOE_GRAN, start, 0)

    @pl.when(b == 0)
    def _():
        yloc_ref[...] = jnp.zeros_like(yloc_ref)
        start_block(0, 0)

    @pl.when(b + 1 < pl.num_programs(0))
    def _():
        start_block(jnp.minimum(b + 1, pl.num_programs(0) - 1), 1 - slot)

    n_mine = 0
    for e in range(N_EXPERTS):
        n_mine = n_mine + meta_ref[b, TAB_PC + e] // MOE_GRAN

    def wait_one(gi, _):
        pltpu.make_async_copy(granule(y_hbm, 0), granule(yloc_ref.at[slot], 0), sem.at[slot]).wait()
        return 0

    lax.fori_loop(0, n_mine, wait_one, 0)

    route_t = route_ref[...]
    route = route_t.T
    col = lax.broadcasted_iota(jnp.int32, (MOE_BLOCK, MOE_LOCAL_ROWS), 1).astype(F32)
    pick1 = jnp.where(col == route[:, ROUTE_D1:ROUTE_D1 + 1], 1.0, 0.0).astype(BF16)
    pick2 = jnp.where(col == route[:, ROUTE_D2:ROUTE_D2 + 1], 1.0, 0.0).astype(BF16)
    yl = yloc_ref[slot]
    moe = (route[:, ROUTE_G1:ROUTE_G1 + 1] * jnp.dot(pick1, yl, preferred_element_type=F32)
           + route[:, ROUTE_G2:ROUTE_G2 + 1] * jnp.dot(pick2, yl, preferred_element_type=F32))
    out_ref[...] = _rmsnorm(x_ref[...] + moe, gf_ref[...])


def _moe_combine(meta, x2d, route, y, g_final):
    grid_spec = pltpu.PrefetchScalarGridSpec(
        num_scalar_prefetch=1,
        grid=(N_MOE_BLOCKS,),
        in_specs=[
            pl.BlockSpec((MOE_BLOCK, D_MODEL), lambda b, m: (b, 0)),
            pl.BlockSpec((ROUTE_ROWS, MOE_BLOCK), lambda b, m: (0, b)),
            pl.BlockSpec(memory_space=pl.ANY),
            pl.BlockSpec((1, D_MODEL), lambda b, m: (0, 0)),
        ],
        out_specs=pl.BlockSpec((MOE_BLOCK, D_MODEL), lambda b, m: (b, 0)),
        scratch_shapes=[
            pltpu.VMEM((2, MOE_LOCAL_ROWS, D_MODEL), BF16),
            pltpu.SemaphoreType.DMA((2,)),
        ],
    )
    return pl.pallas_call(
        _moe_combine_kernel,
        grid_spec=grid_spec,
        out_shape=jax.ShapeDtypeStruct((N_TOK, D_MODEL), F32),
        compiler_params=pltpu.CompilerParams(
            dimension_semantics=("arbitrary",), vmem_limit_bytes=VMEM_LIMIT),
        name="moe_combine",
    )(meta, x2d, route, y, g_final)


def _moe(x2d, g, w_router, w_gu, w_down, g_final):
    route, meta = _moe_route(x2d, g, w_router)
    block_tab, tile_tab, fill_tab = _moe_plan(meta)
    hs = _moe_dispatch(block_tab, tile_tab, fill_tab, x2d, g, route)
    y = _moe_experts(tile_tab, hs, w_gu, w_down)
    return _moe_combine(block_tab, x2d, route, y, g_final)


def _rope_tables():
    def tables(dim):
        inv_freq = ROPE_THETA ** (-jnp.arange(0, dim, 2, dtype=F32) / dim)
        ang = jnp.arange(SEQ, dtype=F32)[:, None] * inv_freq[None, :]
        return jnp.cos(ang), jnp.sin(ang)

    c, s = tables(HEAD_DIM)
    cq = jnp.concatenate([c, c], axis=1)
    sq = jnp.concatenate([-s, s], axis=1)
    c, s = tables(IDX_ROPE_DIM)
    ones = jnp.ones((SEQ, IDX_DIM - IDX_ROPE_DIM), F32)
    zeros = jnp.zeros_like(ones)
    z16 = jnp.zeros_like(s)
    ci = jnp.concatenate([c, c, ones], axis=1)
    sia = jnp.concatenate([-s, z16, zeros], axis=1)
    sib = jnp.concatenate([z16, s, zeros], axis=1)
    rep = LANES // IDX_DIM
    return cq, sq, jnp.tile(ci, (1, rep)), jnp.tile(sia, (1, rep)), jnp.tile(sib, (1, rep))


def kernel(x, norm_mix, norm_ffn, dsa_w_in, dsa_idx_k_gain, dsa_w_out, ffn_w_gu, ffn_w_down, gmlp_w_in,
           gmlp_ln_gain, gmlp_ln_bias, gmlp_w_spatial, gmlp_b_spatial, gmlp_w_out, moe_w_router, moe_w_gu,
           moe_w_down, final_norm):
    x2d = x.reshape(N_TOK, D_MODEL)
    row = lambda v: v.reshape(1, -1)

    w_in = dsa_w_in[0]
    w_all = jnp.pad(w_in, ((0, 0), (0, DSA_COLS - w_in.shape[1])))
    wwi_t = w_in[:, COL_KI + IDX_DIM:COL_KI + IDX_DIM + IDX_HEADS].T
    kig = jnp.pad(dsa_idx_k_gain[0], (0, LANES - IDX_DIM)).reshape(1, LANES)
    q3, k, vt, kia, kib, qi, wt = _dsa_proj(x2d, row(norm_mix[0]), w_all, wwi_t, kig, _rope_tables())
    o3 = _dsa_attn(q3, k, vt, kia, kib, qi, wt)
    x2d = _resid_proj(x2d, o3, dsa_w_out[0])
    x2d = _ffn(x2d, row(norm_ffn[0]), ffn_w_gu[0], ffn_w_down[0])

    gw = GMLP_WIDTH // GMLP_GROUPS
    b_sp_full = jnp.repeat(gmlp_b_spatial[0].T, gw, axis=1)
    x2d = _gmlp(x2d, row(norm_mix[1]), gmlp_w_in[0], row(gmlp_ln_gain[0]), row(gmlp_ln_bias[0]),
                gmlp_w_spatial[0], b_sp_full, gmlp_w_out[0])
    w_router = jnp.pad(moe_w_router[0], ((0, 0), (0, LANES - N_EXPERTS)))
    out = _moe(x2d, row(norm_ffn[1]), w_router, moe_w_gu[0], moe_w_down[0], row(final_norm))
    return out.reshape(BATCH, SEQ, D_MODEL)
```
